```python
import math
import jax, jax.numpy as jnp
from jax import lax
import numpy as np

D_MODEL = 2048
BATCH = 4
SEQ = 2048
DEPTH = 1
DEC_BATCH = 128
DEC_SEQ = 1
PAST_LEN = 16384
PAGE_SIZE = 128

D_S5 = D_MODEL // 2
S5_GROUP = 16
G_S5 = D_S5 // S5_GROUP
P_S5 = 64
D_RW = D_MODEL // 2
RW_HEAD = 64
H_RW = D_RW // RW_HEAD
LORA_W = 64
LORA_A = 64

RMS_EPS = 1e-6
GN_EPS = 64e-5

OFF_S5_U = 0
OFF_S5_Z = D_S5
OFF_RW = 2 * D_S5
N_SHIFT = 3 * D_RW + LORA_W + LORA_A
OFF_RW_Z = OFF_RW + N_SHIFT
OFF_GATE_S5 = OFF_RW_Z + D_RW
OFF_GATE_RW = OFF_GATE_S5 + D_MODEL
N_IN = OFF_GATE_RW + D_MODEL

kernel_name = "hybrid_s5_rwkv7_adaln_step"


def rmsnorm(x, g):
    xf = x.astype(jnp.float32)
    y = xf * lax.rsqrt(jnp.mean(xf * xf, axis=-1, keepdims=True) + RMS_EPS)
    return (y * g.astype(jnp.float32)).astype(x.dtype)


def s5_branch(u, z, x0_re, x0_im, A_re, A_im, log_step, B_re, B_im, C_re, C_im, D_skip, w_glu, b_glu):
    bsz, T, _ = u.shape
    f32 = jnp.float32
    ug = u.astype(f32).reshape(bsz, T, G_S5, S5_GROUP)
    step = jnp.exp(log_step.astype(f32))[:, None]
    lam_re = jnp.minimum(A_re.astype(f32), -1e-4)
    lam_im = A_im.astype(f32)
    mag = jnp.exp(lam_re * step)
    ab_re = mag * jnp.cos(lam_im * step)
    ab_im = mag * jnp.sin(lam_im * step)
    den = lam_re * lam_re + lam_im * lam_im
    f_re = ((ab_re - 1.0) * lam_re + ab_im * lam_im) / den
    f_im = (ab_im * lam_re - (ab_re - 1.0) * lam_im) / den
    Br, Bi = B_re.astype(f32), B_im.astype(f32)
    bb_re = f_re[:, :, None] * Br - f_im[:, :, None] * Bi
    bb_im = f_re[:, :, None] * Bi + f_im[:, :, None] * Br
    bu_re = jnp.einsum('btgc,gpc->btgp', ug, bb_re)
    bu_im = jnp.einsum('btgc,gpc->btgp', ug, bb_im)
    a_re = jnp.broadcast_to(ab_re, bu_re.shape)
    a_im = jnp.broadcast_to(ab_im, bu_im.shape)

    def combine(e1, e2):
        a1r, a1i, b1r, b1i = e1
        a2r, a2i, b2r, b2i = e2
        return (a2r * a1r - a2i * a1i,
                a2r * a1i + a2i * a1r,
                a2r * b1r - a2i * b1i + b2r,
                a2r * b1i + a2i * b1r + b2i)

    pr, pi, sr, si = lax.associative_scan(combine, (a_re, a_im, bu_re, bu_im), axis=1)
    x0r = x0_re.astype(f32)[:, None]
    x0i = x0_im.astype(f32)[:, None]
    xr = sr + pr * x0r - pi * x0i
    xi = si + pr * x0i + pi * x0r
    y = (jnp.einsum('btgp,gcp->btgc', xr, C_re.astype(f32))
         - jnp.einsum('btgp,gcp->btgc', xi, C_im.astype(f32))
         + D_skip.astype(f32) * ug)
    y = jax.nn.gelu(y.reshape(bsz, T, D_S5)).astype(u.dtype)
    y = y * jax.nn.sigmoid(y @ w_glu + b_glu)
    out = y * jax.nn.silu(z)
    return out, xr[:, -1].astype(x0_re.dtype), xi[:, -1].astype(x0_im.dtype)


def rwkv_branch(cur, prev, z, S0, mu_rw, w0, w2, a0, a2, k_k, k_a, r_k, gn_w, gn_b):
    bsz, T, _ = cur.shape
    f32 = jnp.float32
    xs = cur + (prev - cur) * mu_rw
    r = xs[..., :D_RW]
    k = xs[..., D_RW:2 * D_RW]
    v = xs[..., 2 * D_RW:3 * D_RW]
    wd = xs[..., 3 * D_RW:3 * D_RW + LORA_W]
    ad = xs[..., 3 * D_RW + LORA_W:]
    w = -jax.nn.softplus(-(w0 + jnp.tanh(wd) @ w2)) - 0.5
    decay = jnp.exp(-jnp.exp(w.astype(f32)))
    a = jax.nn.sigmoid((a0 + ad @ a2).astype(f32))
    hs = lambda t: t.astype(f32).reshape(bsz, T, H_RW, RW_HEAD)
    r, k, v, decay, a = hs(r), hs(k), hs(v), hs(decay), hs(a)
    kk = k * k_k.astype(f32).reshape(H_RW, RW_HEAD)
    kk = kk / jnp.maximum(jnp.sqrt(jnp.sum(kk * kk, axis=-1, keepdims=True)), 1e-12)
    k = k * (1.0 + (a - 1.0) * k_a.astype(f32).reshape(H_RW, RW_HEAD))
    b = kk * a

    def step(S, inp):
        r_t, d_t, k_t, v_t, kk_t, b_t = inp
        sa = jnp.einsum('bhij,bhj->bhi', S, -kk_t)
        S = S * d_t[:, :, None, :] + sa[..., None] * b_t[:, :, None, :] + v_t[..., None] * k_t[:, :, None, :]
        o = jnp.einsum('bhij,bhj->bhi', S, r_t)
        return S, o

    tm = lambda t: jnp.moveaxis(t, 1, 0)
    S_fin, o = lax.scan(step, S0.astype(f32), (tm(r), tm(decay), tm(k), tm(v), tm(kk), tm(b)))
    o = jnp.moveaxis(o, 0, 1)
    mu = jnp.mean(o, axis=-1, keepdims=True)
    var = jnp.mean((o - mu) ** 2, axis=-1, keepdims=True)
    o = (o - mu) * lax.rsqrt(var + GN_EPS)
    o = o * gn_w.astype(f32).reshape(H_RW, RW_HEAD) + gn_b.astype(f32).reshape(H_RW, RW_HEAD)
    o = o + jnp.sum(r * k * r_k.astype(f32), axis=-1, keepdims=True) * v
    o = o.reshape(bsz, T, D_RW).astype(z.dtype)
    out = o * jax.nn.silu(z)
    return out, S_fin.astype(S0.dtype)


def hybrid_layer(x, c, shift_prev, s5_re0, s5_im0, wkv0,
                 norm_g, w_ada, b_ada, w_in, mu_rw,
                 A_re, A_im, log_step, B_re, B_im, C_re, C_im, D_skip, w_glu, b_glu,
                 w0, w2, a0, a2, k_k, k_a, r_k, gn_w, gn_b, w_out):
    mod = jax.nn.silu(c) @ w_ada + b_ada
    sh = mod[:, None, :D_MODEL]
    sc = mod[:, None, D_MODEL:2 * D_MODEL]
    gt = mod[:, None, 2 * D_MODEL:]
    h = rmsnorm(x, norm_g) * (1.0 + sc) + sh
    h_all = jnp.concatenate([shift_prev[:, None].astype(h.dtype), h], axis=1)
    p_all = h_all @ w_in
    p = p_all[:, 1:]
    p_prev = p_all[:, :-1, OFF_RW:OFF_RW_Z]
    o_s, s5_re, s5_im = s5_branch(p[..., OFF_S5_U:OFF_S5_Z], p[..., OFF_S5_Z:OFF_RW], s5_re0, s5_im0,
                                  A_re, A_im, log_step, B_re, B_im, C_re, C_im, D_skip, w_glu, b_glu)
    o_r, wkv = rwkv_branch(p[..., OFF_RW:OFF_RW_Z], p_prev, p[..., OFF_RW_Z:OFF_GATE_S5], wkv0,
                           mu_rw, w0, w2, a0, a2, k_k, k_a, r_k, gn_w, gn_b)
    mixed = (jax.nn.sigmoid(p[..., OFF_GATE_S5:OFF_GATE_RW]) * (o_s @ w_out[:D_S5])
             + jax.nn.sigmoid(p[..., OFF_GATE_RW:]) * (o_r @ w_out[D_S5:]))
    x = x + gt * mixed
    return x, h[:, -1], s5_re, s5_im, wkv


def setup_inputs(seed: int = 0) -> dict:
    key = jax.random.key(seed)
    ks = jax.random.split(key, 40)
    f32 = jnp.float32
    nrm = lambda k, s, sc: jax.random.normal(k, s, f32) * sc
    L = DEPTH
    inp = {}
    inp["x_prompt"] = nrm(ks[0], (BATCH, SEQ, D_MODEL), 1.0)
    inp["x_sample"] = nrm(ks[1], (DEC_BATCH, DEC_SEQ, D_MODEL), 1.0)
    inp["c_prompt"] = nrm(ks[2], (BATCH, D_MODEL), 1.0)
    inp["c_sample"] = nrm(ks[3], (DEC_BATCH, D_MODEL), 1.0)
    inp["state_s5_re"] = nrm(ks[4], (L, DEC_BATCH, G_S5, P_S5), 0.1)
    inp["state_s5_im"] = nrm(ks[5], (L, DEC_BATCH, G_S5, P_S5), 0.1)
    inp["state_wkv"] = nrm(ks[6], (L, DEC_BATCH, H_RW, RW_HEAD, RW_HEAD), 0.1)
    inp["state_shift"] = nrm(ks[7], (L, DEC_BATCH, D_MODEL), 1.0)
    inp["norm_g"] = 1.0 + nrm(ks[8], (L, D_MODEL), 0.01)
    inp["w_ada"] = nrm(ks[9], (L, D_MODEL, 3 * D_MODEL), 0.5 * D_MODEL ** -0.5)
    inp["b_ada"] = nrm(ks[10], (L, 3 * D_MODEL), 0.01)
    inp["w_in"] = nrm(ks[11], (L, D_MODEL, N_IN), D_MODEL ** -0.5)
    inp["mu_rw"] = jax.random.uniform(ks[12], (L, N_SHIFT), f32)
    inp["A_re"] = -0.5 + nrm(ks[13], (L, G_S5, P_S5), 0.01)
    inp["A_im"] = jnp.pi * jnp.arange(P_S5, dtype=f32) + nrm(ks[14], (L, G_S5, P_S5), 0.01)
    inp["log_step"] = jax.random.uniform(ks[15], (L, G_S5), f32, math.log(1e-3), math.log(1e-1))
    inp["B_re"] = nrm(ks[16], (L, G_S5, P_S5, S5_GROUP), (2.0 * S5_GROUP) ** -0.5)
    inp["B_im"] = nrm(ks[17], (L, G_S5, P_S5, S5_GROUP), (2.0 * S5_GROUP) ** -0.5)
    inp["C_re"] = nrm(ks[18], (L, G_S5, S5_GROUP, P_S5), (2.0 * P_S5) ** -0.5)
    inp["C_im"] = nrm(ks[19], (L, G_S5, S5_GROUP, P_S5), (2.0 * P_S5) ** -0.5)
    inp["D_skip"] = nrm(ks[20], (L, G_S5, S5_GROUP), 1.0)
    inp["w_glu"] = nrm(ks[21], (L, D_S5, D_S5), D_S5 ** -0.5)
    inp["b_glu"] = nrm(ks[22], (L, D_S5), 0.01)
    inp["w0"] = jax.random.uniform(ks[23], (L, D_RW), f32, -6.0, -1.0)
    inp["w2"] = nrm(ks[24], (L, LORA_W, D_RW), 0.5 * LORA_W ** -0.5)
    inp["a0"] = nrm(ks[25], (L, D_RW), 0.1)
    inp["a2"] = nrm(ks[26], (L, LORA_A, D_RW), 0.5 * LORA_A ** -0.5)
    inp["k_k"] = 0.85 + nrm(ks[27], (L, D_RW), 0.02)
    inp["k_a"] = 1.0 + nrm(ks[28], (L, D_RW), 0.02)
    inp["r_k"] = nrm(ks[29], (L, H_RW, RW_HEAD), 0.1)
    inp["gn_w"] = 1.0 + nrm(ks[30], (L, D_RW), 0.01)
    inp["gn_b"] = nrm(ks[31], (L, D_RW), 0.01)
    inp["w_out"] = nrm(ks[32], (L, D_S5 + D_RW, D_MODEL), (D_S5 + D_RW) ** -0.5)
    inp["final_g"] = 1.0 + nrm(ks[33], (D_MODEL,), 0.01)
    return inp


def reference(x_prompt, x_sample, c_prompt, c_sample, state_s5_re, state_s5_im, state_wkv, state_shift,
              norm_g, w_ada, b_ada, w_in, mu_rw, A_re, A_im, log_step, B_re, B_im, C_re, C_im, D_skip,
              w_glu, b_glu, w0, w2, a0, a2, k_k, k_a, r_k, gn_w, gn_b, w_out, final_g):
    xp, xs = x_prompt, x_sample
    p_re, p_im, p_wkv, p_shift = [], [], [], []
    s_re, s_im, s_wkv, s_shift = [], [], [], []
    dt = x_prompt.dtype
    for l in range(DEPTH):
        params = (norm_g[l], w_ada[l], b_ada[l], w_in[l], mu_rw[l],
                  A_re[l], A_im[l], log_step[l], B_re[l], B_im[l], C_re[l], C_im[l], D_skip[l], w_glu[l], b_glu[l],
                  w0[l], w2[l], a0[l], a2[l], k_k[l], k_a[l], r_k[l], gn_w[l], gn_b[l], w_out[l])
        xp, sh_p, re_p, im_p, wkv_p = hybrid_layer(
            xp, c_prompt,
            jnp.zeros((xp.shape[0], D_MODEL), dt),
            jnp.zeros((xp.shape[0], G_S5, P_S5), dt),
            jnp.zeros((xp.shape[0], G_S5, P_S5), dt),
            jnp.zeros((xp.shape[0], H_RW, RW_HEAD, RW_HEAD), dt),
            *params)
        xs, sh_s, re_s, im_s, wkv_s = hybrid_layer(
            xs, c_sample, state_shift[l], state_s5_re[l], state_s5_im[l], state_wkv[l], *params)
        p_re.append(re_p); p_im.append(im_p); p_wkv.append(wkv_p); p_shift.append(sh_p)
        s_re.append(re_s); s_im.append(im_s); s_wkv.append(wkv_s); s_shift.append(sh_s)
    y_prompt = rmsnorm(xp, final_g)
    y_sample = rmsnorm(xs, final_g)
    return (y_prompt, y_sample,
            jnp.stack(p_re), jnp.stack(p_im), jnp.stack(p_wkv), jnp.stack(p_shift),
            jnp.stack(s_re), jnp.stack(s_im), jnp.stack(s_wkv), jnp.stack(s_shift))
```

```python
import functools

import jax
import jax.numpy as jnp
from jax import lax
from jax.experimental import pallas as pl
from jax.experimental.pallas import tpu as pltpu

F32 = jnp.float32
BF16 = jnp.bfloat16

RMS_EPS = 1e-6
GN_EPS = 64e-5
S5_GROUP = 16
P_S5 = 64
RW_HEAD = 64
LORA = 64
LANES = 128
GROUPS_PER_BLOCK = LANES // S5_GROUP
STATES_PER_BLOCK = GROUPS_PER_BLOCK * P_S5
S5_SEG = 16
RW_CHUNK = 64
INV_BASE = 8

_NT = (((1,), (1,)), ((), ()))
_TN = (((0,), (0,)), ((), ()))


def _dot(a, b):
    return jnp.dot(a.astype(BF16), b.astype(BF16), preferred_element_type=F32)


def _dot_nt(a, b):
    return lax.dot_general(a.astype(BF16), b.astype(BF16), _NT, preferred_element_type=F32)


def _dot_tn(a, b):
    return lax.dot_general(a.astype(BF16), b.astype(BF16), _TN, preferred_element_type=F32)


def _sigmoid(x):
    return 1.0 / (1.0 + jnp.exp(-x))


def _silu(x):
    return x * _sigmoid(x)


def _params(*sem):
    return pltpu.CompilerParams(dimension_semantics=sem, vmem_limit_bytes=56 * 1024 * 1024)


def _mod_kernel(c_ref, w_ref, b_ref, o_ref):
    o_ref[...] = _dot(_silu(c_ref[...]), w_ref[...]) + b_ref[...]


def _mod(c, w_bf, b):
    rows, d = c.shape
    n = w_bf.shape[1]
    tn = 1024
    return pl.pallas_call(
        _mod_kernel,
        grid=(n // tn,),
        in_specs=[pl.BlockSpec((rows, d), lambda j: (0, 0)),
                  pl.BlockSpec((d, tn), lambda j: (0, j)),
                  pl.BlockSpec((1, tn), lambda j: (0, j))],
        out_specs=pl.BlockSpec((rows, tn), lambda j: (0, j)),
        out_shape=jax.ShapeDtypeStruct((rows, n), F32),
        compiler_params=_params("parallel"),
        name="adaln_mod",
    )(c, w_bf, b.reshape(1, n))


def _h_kernel(x_ref, g_ref, sh_ref, sc_ref, h_ref):
    x = x_ref[...]
    y = x * lax.rsqrt(jnp.mean(x * x, axis=-1, keepdims=True) + RMS_EPS) * g_ref[...]
    h_ref[...] = (y * (1.0 + sc_ref[...]) + sh_ref[...]).astype(h_ref.dtype)


def _modulated_norm(x, g, mod, out_dtype, tt):
    b, t, d = x.shape
    tt = min(tt, t)
    tm = 1 if mod.shape[1] == 1 else tt
    mod_map = (lambda i, j: (i, 0, 0)) if tm == 1 else (lambda i, j: (i, j, 0))
    mod_map1 = (lambda i, j: (i, 0, 1)) if tm == 1 else (lambda i, j: (i, j, 1))
    return pl.pallas_call(
        _h_kernel,
        grid=(b, t // tt),
        in_specs=[pl.BlockSpec((None, tt, d), lambda i, j: (i, j, 0)),
                  pl.BlockSpec((1, d), lambda i, j: (0, 0)),
                  pl.BlockSpec((None, tm, d), mod_map),
                  pl.BlockSpec((None, tm, d), mod_map1)],
        out_specs=pl.BlockSpec((None, tt, d), lambda i, j: (i, j, 0)),
        out_shape=jax.ShapeDtypeStruct((b, t, d), out_dtype),
        compiler_params=_params("parallel", "parallel"),
        name="modulated_norm",
    )(x, g.reshape(1, d), mod, mod)


def _mm_kernel(a_ref, w_ref, o_ref):
    o_ref[...] = jnp.dot(a_ref[...], w_ref[...], preferred_element_type=F32)


def _matmul(a, w, tm, tn):
    m, k = a.shape
    n = w.shape[1]
    tm = min(tm, m)
    return pl.pallas_call(
        _mm_kernel,
        grid=(m // tm, n // tn),
        in_specs=[pl.BlockSpec((tm, k), lambda i, j: (i, 0)),
                  pl.BlockSpec((k, tn), lambda i, j: (0, j))],
        out_specs=pl.BlockSpec((tm, tn), lambda i, j: (i, j)),
        out_shape=jax.ShapeDtypeStruct((m, n), F32),
        compiler_params=_params("parallel", "parallel"),
        name="in_proj",
    )(a, w)


def _s5_disc_kernel(are_ref, aim_ref, ls_ref, bre_ref, bim_ref,
                    abr_ref, abi_ref, asr_ref, asi_ref, bbr_ref, bbi_ref):
    step = jnp.exp(ls_ref[...])
    lam_re = jnp.minimum(are_ref[...], -1e-4)
    lam_im = aim_ref[...]
    mag = jnp.exp(lam_re * step)
    ab_re = mag * jnp.cos(lam_im * step)
    ab_im = mag * jnp.sin(lam_im * step)
    den = lam_re * lam_re + lam_im * lam_im
    f_re = ((ab_re - 1.0) * lam_re + ab_im * lam_im) / den
    f_im = (ab_im * lam_re - (ab_re - 1.0) * lam_im) / den
    br, bi = bre_ref[...], bim_ref[...]
    bbr_ref[...] = f_re * br - f_im * bi
    bbi_ref[...] = f_re * bi + f_im * br
    abr_ref[...] = ab_re
    abi_ref[...] = ab_im
    pr, pi = ab_re, ab_im
    n = 1
    while n < S5_SEG:
        pr, pi = pr * pr - pi * pi, 2.0 * pr * pi
        n *= 2
    asr_ref[...] = pr
    asi_ref[...] = pi


def _s5_discretise(a_re, a_im, log_step, b_re, b_im):
    g, p = a_re.shape
    c = b_re.shape[-1]
    gp = jax.ShapeDtypeStruct((g, 1, p), F32)
    gcp = jax.ShapeDtypeStruct((g, c, p), F32)
    return pl.pallas_call(
        _s5_disc_kernel,
        out_shape=(gp, gp, gp, gp, gcp, gcp),
        name="s5_discretise",
    )(a_re.reshape(g, 1, p), a_im.reshape(g, 1, p), log_step.reshape(g, 1, 1),
      jnp.swapaxes(b_re, 1, 2), jnp.swapaxes(b_im, 1, 2))


def _s5_block_weights(bbt_re, bbt_im, c_re, c_im):
    g, c, p = bbt_re.shape
    nb = g // GROUPS_PER_BLOCK
    eye = jnp.eye(GROUPS_PER_BLOCK, dtype=F32)

    def b_blk(x):
        x = x.reshape(nb, GROUPS_PER_BLOCK, c, p)
        return jnp.einsum("jacp,ab->jacbp", x, eye).reshape(nb, LANES, STATES_PER_BLOCK)

    def c_blk(x):
        x = x.reshape(nb, GROUPS_PER_BLOCK, c, p)
        return jnp.einsum("jbcp,ab->japbc", x, eye).reshape(nb, STATES_PER_BLOCK, LANES)

    b_w = jnp.concatenate([b_blk(bbt_re), b_blk(bbt_im)], axis=2).astype(BF16)
    c_w = jnp.concatenate([c_blk(c_re), -c_blk(c_im)], axis=1).astype(BF16)
    return b_w, c_w


def _s5_seq_kernel(u_ref, bw_ref, cw_ref, abr_ref, abi_ref, asr_ref, asi_ref, d_ref,
                   y_ref, xre_ref, xim_ref, bu_ref, e_ref, cin_ref, *, t_len):
    ns = STATES_PER_BLOCK
    n_lt = ns // LANES
    nseg = t_len // S5_SEG
    u = u_ref[...]
    bu = _dot(u, bw_ref[...])
    for kt in range(2 * n_lt):
        bu_ref[kt] = bu[:, kt * LANES:(kt + 1) * LANES]

    def seg_rows(tl):
        return pl.ds(tl, nseg, stride=S5_SEG)

    def scan_tiles(init_from_carry, store_states):
        for lt in range(n_lt):
            re_l = slice(lt * LANES, (lt + 1) * LANES)
            im_l = slice(ns + lt * LANES, ns + (lt + 1) * LANES)
            bre, bim = bu_ref.at[lt], bu_ref.at[n_lt + lt]
            ar = abr_ref[:, re_l]
            ai = abi_ref[:, re_l]
            if init_from_carry:
                xr = cin_ref[:, re_l]
                xi = cin_ref[:, im_l]
            else:
                xr = jnp.zeros((nseg, LANES), F32)
                xi = jnp.zeros((nseg, LANES), F32)
            for tl in range(S5_SEG):
                br = bre[seg_rows(tl), :]
                bi = bim[seg_rows(tl), :]
                xr, xi = ar * xr - ai * xi + br, ar * xi + ai * xr + bi
                if store_states:
                    bre[seg_rows(tl), :] = xr
                    bim[seg_rows(tl), :] = xi
            if not store_states:
                e_ref[:, re_l] = xr
                e_ref[:, im_l] = xi

    scan_tiles(False, False)

    asr = asr_ref[...]
    asi = asi_ref[...]

    def seg_step(s, carry):
        cr, ci = carry
        cin_ref[pl.ds(s, 1), :] = jnp.concatenate([cr, ci], axis=1)
        e = e_ref[pl.ds(s, 1), :]
        er, ei = e[:, :ns], e[:, ns:]
        return asr * cr - asi * ci + er, asr * ci + asi * cr + ei

    zero = jnp.zeros((1, ns), F32)
    fr, fi = lax.fori_loop(0, nseg, seg_step, (zero, zero))
    xre_ref[...] = fr
    xim_ref[...] = fi

    scan_tiles(True, True)

    y = d_ref[...] * u
    for kt in range(2 * n_lt):
        y = y + _dot(bu_ref[kt], cw_ref[kt * LANES:(kt + 1) * LANES, :])
    y_ref[...] = y


def _s5_sequence(p_main, b_w, c_w, ab_re, ab_im, as_re, as_im, d_skip):
    b, t, _ = p_main.shape
    nb = b_w.shape[0]
    ns = STATES_PER_BLOCK
    vec = lambda: pl.BlockSpec((1, ns), lambda i, j: (0, j))
    y, xre, xim = pl.pallas_call(
        functools.partial(_s5_seq_kernel, t_len=t),
        grid=(b, nb),
        in_specs=[pl.BlockSpec((None, t, LANES), lambda i, j: (i, 0, j)),
                  pl.BlockSpec((None, LANES, 2 * ns), lambda i, j: (j, 0, 0)),
                  pl.BlockSpec((None, 2 * ns, LANES), lambda i, j: (j, 0, 0)),
                  vec(), vec(), vec(), vec(),
                  pl.BlockSpec((1, LANES), lambda i, j: (0, j))],
        out_specs=[pl.BlockSpec((None, t, LANES), lambda i, j: (i, 0, j)),
                   pl.BlockSpec((None, 1, ns), lambda i, j: (i, 0, j)),
                   pl.BlockSpec((None, 1, ns), lambda i, j: (i, 0, j))],
        out_shape=(jax.ShapeDtypeStruct((b, t, nb * LANES), F32),
                   jax.ShapeDtypeStruct((b, 1, nb * ns), F32),
                   jax.ShapeDtypeStruct((b, 1, nb * ns), F32)),
        scratch_shapes=[pltpu.VMEM((2 * ns // LANES, t, LANES), F32),
                        pltpu.VMEM((t // S5_SEG, 2 * ns), F32),
                        pltpu.VMEM((t // S5_SEG, 2 * ns), F32)],
        compiler_params=_params("parallel", "parallel"),
        name="s5_sequence",
    )(p_main, b_w, c_w, ab_re, ab_im, as_re, as_im, d_skip)
    return y, xre, xim


def _s5_step_kernel(u_ref, bw_ref, cw_ref, abr_ref, abi_ref, d_ref, x0r_ref, x0i_ref,
                    y_ref, x1r_ref, x1i_ref):
    ns = STATES_PER_BLOCK
    u = u_ref[...]
    bu = _dot(u, bw_ref[...])
    ar, ai = abr_ref[...], abi_ref[...]
    x0r, x0i = x0r_ref[...], x0i_ref[...]
    xr = ar * x0r - ai * x0i + bu[:, :ns]
    xi = ar * x0i + ai * x0r + bu[:, ns:]
    x1r_ref[...] = xr
    x1i_ref[...] = xi
    y_ref[...] = _dot(jnp.concatenate([xr, xi], axis=1), cw_ref[...]) + d_ref[...] * u


def _s5_step(p_rows, b_w, c_w, ab_re, ab_im, d_skip, x0_re, x0_im):
    rows = p_rows.shape[0]
    nb = b_w.shape[0]
    ns = STATES_PER_BLOCK
    vec = lambda: pl.BlockSpec((1, ns), lambda j: (0, j))
    st = lambda: pl.BlockSpec((rows, ns), lambda j: (0, j))
    return pl.pallas_call(
        _s5_step_kernel,
        grid=(nb,),
        in_specs=[pl.BlockSpec((rows, LANES), lambda j: (0, j)),
                  pl.BlockSpec((None, LANES, 2 * ns), lambda j: (j, 0, 0)),
                  pl.BlockSpec((None, 2 * ns, LANES), lambda j: (j, 0, 0)),
                  vec(), vec(),
                  pl.BlockSpec((1, LANES), lambda j: (0, j)),
                  st(), st()],
        out_specs=[pl.BlockSpec((rows, LANES), lambda j: (0, j)), st(), st()],
        out_shape=(jax.ShapeDtypeStruct((rows, nb * LANES), F32),
                   jax.ShapeDtypeStruct((rows, nb * ns), F32),
                   jax.ShapeDtypeStruct((rows, nb * ns), F32)),
        compiler_params=_params("parallel"),
        name="s5_step",
    )(p_rows, b_w, c_w, ab_re, ab_im, d_skip, x0_re, x0_im)


def _glu_kernel(y_ref, z_ref, w_ref, b_ref, o_ref):
    y = jax.nn.gelu(y_ref[...], approximate=True)
    gate = _sigmoid(_dot(y, w_ref[...]) + b_ref[...])
    o_ref[...] = (y * gate * _silu(z_ref[...])).astype(o_ref.dtype)


def _glu(y, p_rows, z_block, w_bf, b, tm):
    rows, d = y.shape
    tm = min(tm, rows)
    return pl.pallas_call(
        _glu_kernel,
        grid=(rows // tm,),
        in_specs=[pl.BlockSpec((tm, d), lambda i: (i, 0)),
                  pl.BlockSpec((tm, d), lambda i: (i, z_block)),
                  pl.BlockSpec((d, d), lambda i: (0, 0)),
                  pl.BlockSpec((1, d), lambda i: (0, 0))],
        out_specs=pl.BlockSpec((tm, d), lambda i: (i, 0)),
        out_shape=jax.ShapeDtypeStruct((rows, d), BF16),
        compiler_params=_params("parallel"),
        name="s5_glu",
    )(y, p_rows, w_bf, b.reshape(1, d))


def _softplus(x):
    return jnp.maximum(x, 0.0) + jnp.log(1.0 + jnp.exp(-jnp.abs(x)))


def _rwkv_token_terms(r, k, lo, w0, w2p, a0, a2p, k_a):
    w = -_softplus(-(w0 + _dot(jnp.tanh(lo), w2p))) - 0.5
    logd = -jnp.exp(w)
    a = _sigmoid(a0 + _dot(lo, a2p))
    k2 = k * (1.0 + (a - 1.0) * k_a)
    return logd, a, k2


def _unit_rows(x):
    n = jnp.sqrt(jnp.sum(x * x, axis=-1, keepdims=True))
    return x / jnp.maximum(n, 1e-12)


def _group_norm_bonus(o, r, k2, v, rk, gw, gb):
    mu = jnp.mean(o, axis=-1, keepdims=True)
    var = jnp.mean((o - mu) ** 2, axis=-1, keepdims=True)
    o = (o - mu) * lax.rsqrt(var + GN_EPS) * gw + gb
    return o + jnp.sum(r * k2 * rk, axis=-1, keepdims=True) * v


def _unit_lower_inverse(nm, blk_masks, eye):
    d = jnp.where(blk_masks[0], nm, 0.0)
    x = eye - d
    pw = _dot(d, d)
    s = 2
    while s < INV_BASE:
        x = x + _dot(x, pw)
        s *= 2
        if s < INV_BASE:
            pw = _dot(pw, pw)
    for lvl in range(1, len(blk_masks)):
        c = jnp.where(blk_masks[lvl] & ~blk_masks[lvl - 1], nm, 0.0)
        x = x - _dot(_dot(x, c), x)
    return x


def _rwkv_chunk_kernel(r_ref, k_ref, v_ref, lo_ref, z_ref,
                       mur_ref, muk_ref, muv_ref, mulo_ref,
                       w0_ref, w2p_ref, a0_ref, a2p_ref, kk_ref, ka_ref, rk_ref, gw_ref, gb_ref,
                       o_ref, hs_ref,
                       h_scr, pr_scr, pk_scr, pv_scr, plo_scr, *, n_heads):
    L = RW_CHUNK
    c = pl.program_id(1)

    @pl.when(c == 0)
    def _():
        h_scr[...] = jnp.zeros_like(h_scr)
        pr_scr[...] = jnp.zeros_like(pr_scr)
        pk_scr[...] = jnp.zeros_like(pk_scr)
        pv_scr[...] = jnp.zeros_like(pv_scr)
        plo_scr[...] = jnp.zeros_like(plo_scr)

    row1 = lax.broadcasted_iota(jnp.int32, (L, 1), 0)

    def token_shift(cur_ref, prev_scr, mu_ref):
        cur = cur_ref[...]
        prev = jnp.where(row1 == 0, prev_scr[...], pltpu.roll(cur, 1, 0))
        prev_scr[...] = cur[L - 1:L, :]
        return cur + (prev - cur) * mu_ref[...]

    r = token_shift(r_ref, pr_scr, mur_ref)
    k = token_shift(k_ref, pk_scr, muk_ref)
    v = token_shift(v_ref, pv_scr, muv_ref)
    lo = token_shift(lo_ref, plo_scr, mulo_ref)
    logd, a, k2 = _rwkv_token_terms(r, k, lo, w0_ref[...], w2p_ref[...], a0_ref[...], a2p_ref[...],
                                    ka_ref[...])
    kk = k * kk_ref[...]

    ri = lax.broadcasted_iota(jnp.int32, (L, L), 0)
    ci = lax.broadcasted_iota(jnp.int32, (L, L), 1)
    strict = ri > ci
    incl = ri >= ci
    eye = (ri == ci).astype(F32)
    blk_masks = []
    s = INV_BASE
    while s <= L:
        blk_masks.append((ri // s) == (ci // s))
        s *= 2

    tri = incl.astype(BF16)
    hi = logd.astype(BF16)
    rem = logd - hi.astype(F32)
    mid = rem.astype(BF16)
    low = (rem - mid.astype(F32)).astype(BF16)
    lp = (jnp.dot(tri, hi, preferred_element_type=F32)
          + jnp.dot(tri, mid, preferred_element_type=F32)
          + jnp.dot(tri, low, preferred_element_type=F32))
    p_inc = jnp.exp(lp)
    p_exc = jnp.exp(lp - logd)
    p_inv = jnp.exp(-lp)

    outs = []
    for h in range(n_heads):
        sl = slice(h * RW_HEAD, (h + 1) * RW_HEAD)
        r_h, k2_h, v_h, a_h = r[:, sl], k2[:, sl], v[:, sl], a[:, sl]
        kk_h = _unit_rows(kk[:, sl])
        b_h = kk_h * a_h
        pinc_h, pinv_h = p_inc[:, sl], p_inv[:, sl]
        kkp = kk_h * p_exc[:, sl]
        rp = r_h * pinc_h
        kd = k2_h * pinv_h
        bd = b_h * pinv_h
        lhs = jnp.concatenate([kkp, rp], axis=0)
        nb = _dot_nt(lhs, bd)
        mk = _dot_nt(lhs, kd)
        nm = jnp.where(strict, nb[:L], 0.0)
        arb = jnp.where(incl, nb[L:], 0.0)
        mm = jnp.where(strict, mk[:L], 0.0)
        ark = jnp.where(incl, mk[L:], 0.0)
        tinv = _unit_lower_inverse(nm, blk_masks, eye)
        hs = h_scr[h]
        lh = _dot(lhs, hs)
        u = _dot(tinv, lh[:L] + _dot(mm, v_h))
        o = lh[L:] + _dot(ark, v_h) - _dot(arb, u)
        p_end = pinc_h[L - 1:L, :]
        p_end_col = jnp.sum(eye * p_end, axis=1, keepdims=True)
        h_scr[h] = p_end_col * hs + _dot_tn(kd * p_end, v_h) - _dot_tn(bd * p_end, u)
        outs.append(_group_norm_bonus(o, r_h, k2_h, v_h, rk_ref[:, sl], gw_ref[:, sl], gb_ref[:, sl]))

    o_all = jnp.concatenate(outs, axis=1)
    o_ref[...] = (o_all * _silu(z_ref[...])).astype(o_ref.dtype)

    @pl.when(c == pl.num_programs(1) - 1)
    def _():
        hs_ref[...] = h_scr[...]


def _rwkv_sequence(p_main, p_lora, col, mu, w0, w2p, a0, a2p, k_k, k_a, r_k, gn_w, gn_b):
    b, t, _ = p_main.shape
    d = w0.shape[-1]
    n_heads = d // RW_HEAD
    L = RW_CHUNK
    blk = lambda cb: pl.BlockSpec((None, L, d), lambda i, j, cb=cb: (i, j, cb))
    vec = lambda n: pl.BlockSpec((1, n), lambda i, j: (0, 0))
    full = lambda shp: pl.BlockSpec(shp, lambda i, j: (0,) * len(shp))
    mu_r, mu_k, mu_v, mu_lo = mu
    o, hs = pl.pallas_call(
        functools.partial(_rwkv_chunk_kernel, n_heads=n_heads),
        grid=(b, t // L),
        in_specs=[blk(col["r"]), blk(col["k"]), blk(col["v"]),
                  pl.BlockSpec((None, L, LANES), lambda i, j: (i, j, 0)),
                  blk(col["z_rw"]),
                  vec(d), vec(d), vec(d), vec(LANES),
                  vec(d), full((LANES, d)), vec(d), full((LANES, d)),
                  vec(d), vec(d), vec(d), vec(d), vec(d)],
        out_specs=[pl.BlockSpec((None, L, d), lambda i, j: (i, j, 0)),
                   pl.BlockSpec((None, n_heads, RW_HEAD, RW_HEAD), lambda i, j: (i, 0, 0, 0))],
        out_shape=(jax.ShapeDtypeStruct((b, t, d), BF16),
                   jax.ShapeDtypeStruct((b, n_heads, RW_HEAD, RW_HEAD), F32)),
        scratch_shapes=[pltpu.VMEM((n_heads, RW_HEAD, RW_HEAD), F32),
                        pltpu.VMEM((1, d), F32), pltpu.VMEM((1, d), F32), pltpu.VMEM((1, d), F32),
                        pltpu.VMEM((1, LANES), F32)],
        compiler_params=_params("parallel", "arbitrary"),
        name="rwkv_sequence",
    )(p_main, p_main, p_main, p_lora, p_main,
      mu_r, mu_k, mu_v, mu_lo, w0, w2p, a0, a2p, k_k, k_a, r_k, gn_w, gn_b)
    return o, hs


def _rwkv_step_prep_kernel(cr_ref, ck_ref, cv_ref, clo_ref, pr_ref, pk_ref, pv_ref, plo_ref,
                           mur_ref, muk_ref, muv_ref, mulo_ref,
                           w0_ref, w2p_ref, a0_ref, a2p_ref, kk_ref, ka_ref,
                           r_o, k2_o, v_o, kk_o, a_o, d_o):
    def lerp(c_ref, p_ref, mu_ref):
        cur = c_ref[...]
        return cur + (p_ref[...] - cur) * mu_ref[...]

    r = lerp(cr_ref, pr_ref, mur_ref)
    k = lerp(ck_ref, pk_ref, muk_ref)
    v = lerp(cv_ref, pv_ref, muv_ref)
    lo = lerp(clo_ref, plo_ref, mulo_ref)
    logd, a, k2 = _rwkv_token_terms(r, k, lo, w0_ref[...], w2p_ref[...], a0_ref[...], a2p_ref[...],
                                    ka_ref[...])
    r_o[...] = r
    k2_o[...] = k2
    v_o[...] = v
    kk_o[...] = k * kk_ref[...]
    a_o[...] = a
    d_o[...] = jnp.exp(logd)


def _rwkv_step_kernel(s_ref, r_ref, k2_ref, v_ref, kk_ref, a_ref, d_ref, z_ref,
                      rk_ref, gw_ref, gb_ref, o_ref, s1_ref, *, n_heads):
    n = RW_HEAD
    eye = (lax.broadcasted_iota(jnp.int32, (n, n), 0)
           == lax.broadcasted_iota(jnp.int32, (n, n), 1)).astype(F32)
    for h in range(n_heads):
        row = lambda ref: ref[pl.ds(h, 1), :]
        r_h, k2_h, v_h, a_h, d_h = row(r_ref), row(k2_ref), row(v_ref), row(a_ref), row(d_ref)
        kk_h = _unit_rows(row(kk_ref))
        b_h = kk_h * a_h
        s = s_ref[h]
        sa = jnp.sum(s * kk_h, axis=1, keepdims=True)
        v_col = jnp.sum(eye * v_h, axis=1, keepdims=True)
        s1 = s * d_h - sa * b_h + v_col * k2_h
        s1_ref[h] = s1
        o_col = jnp.sum(s1 * r_h, axis=1, keepdims=True)
        o = jnp.sum(eye * o_col, axis=0, keepdims=True)
        o = _group_norm_bonus(o, r_h, k2_h, v_h, row(rk_ref), row(gw_ref), row(gb_ref))
        o_ref[pl.ds(h, 1), :] = (o * _silu(row(z_ref))).astype(o_ref.dtype)


def _rwkv_step(p_rows, p_lora_rows, col, mu, w0, w2p, a0, a2p, k_k, k_a, r_k, gn_w, gn_b, s0):
    rows = s0.shape[0]
    d = w0.shape[-1]
    n_heads = d // RW_HEAD
    cur = lambda cb: pl.BlockSpec((rows, d), lambda i, cb=cb: (0, cb))
    prv = lambda cb: pl.BlockSpec((rows, d), lambda i, cb=cb: (1, cb))
    vec = lambda n: pl.BlockSpec((1, n), lambda i: (0, 0))
    mu_r, mu_k, mu_v, mu_lo = mu
    out = jax.ShapeDtypeStruct((rows, d), F32)
    terms = pl.pallas_call(
        _rwkv_step_prep_kernel,
        grid=(1,),
        in_specs=[cur(col["r"]), cur(col["k"]), cur(col["v"]),
                  pl.BlockSpec((rows, LANES), lambda i: (0, 0)),
                  prv(col["r"]), prv(col["k"]), prv(col["v"]),
                  pl.BlockSpec((rows, LANES), lambda i: (1, 0)),
                  vec(d), vec(d), vec(d), vec(LANES),
                  vec(d), pl.BlockSpec((LANES, d), lambda i: (0, 0)),
                  vec(d), pl.BlockSpec((LANES, d), lambda i: (0, 0)),
                  vec(d), vec(d)],
        out_specs=[pl.BlockSpec((rows, d), lambda i: (0, 0))] * 6,
        out_shape=(out,) * 6,
        compiler_params=_params("arbitrary"),
        name="rwkv_step_prep",
    )(p_rows, p_rows, p_rows, p_lora_rows, p_rows, p_rows, p_rows, p_lora_rows,
      mu_r, mu_k, mu_v, mu_lo, w0, w2p, a0, a2p, k_k, k_a)
    heads = lambda x: x.reshape(x.shape[0], n_heads, RW_HEAD)
    z = p_rows[:rows, col["z_rw"] * d:(col["z_rw"] + 1) * d]
    per_b = lambda: pl.BlockSpec((None, n_heads, RW_HEAD), lambda i: (i, 0, 0))
    par = lambda: pl.BlockSpec((n_heads, RW_HEAD), lambda i: (0, 0))
    st = lambda: pl.BlockSpec((None, n_heads, RW_HEAD, RW_HEAD), lambda i: (i, 0, 0, 0))
    o, s1 = pl.pallas_call(
        functools.partial(_rwkv_step_kernel, n_heads=n_heads),
        grid=(rows,),
        in_specs=[st()] + [per_b()] * 7 + [par()] * 3,
        out_specs=[per_b(), st()],
        out_shape=(jax.ShapeDtypeStruct((rows, n_heads, RW_HEAD), BF16),
                   jax.ShapeDtypeStruct(s0.shape, F32)),
        compiler_params=_params("parallel"),
        name="rwkv_step",
    )(s0, *[heads(x) for x in terms], heads(z),
      r_k.reshape(n_heads, RW_HEAD), gn_w.reshape(n_heads, RW_HEAD), gn_b.reshape(n_heads, RW_HEAD))
    return o.reshape(rows, d), s1


def _out_kernel(os_ref, or_ref, gs_ref, gr_ref, x_ref, gt_ref, w1_ref, w2_ref, fg_ref, y_ref):
    mixed = (_sigmoid(gs_ref[...]) * jnp.dot(os_ref[...], w1_ref[...], preferred_element_type=F32)
             + _sigmoid(gr_ref[...]) * jnp.dot(or_ref[...], w2_ref[...], preferred_element_type=F32))
    x = x_ref[...] + gt_ref[...] * mixed
    y_ref[...] = x * lax.rsqrt(jnp.mean(x * x, axis=-1, keepdims=True) + RMS_EPS) * fg_ref[...]


def _out_proj(o_s, o_r, p_main, col, x, mod, w1, w2, final_g, tt):
    b, t, d = x.shape
    dh = o_s.shape[-1]
    tt = min(tt, t)
    tm = 1 if mod.shape[1] == 1 else tt
    gt_map = (lambda i, j: (i, 0, 2)) if tm == 1 else (lambda i, j: (i, j, 2))
    return pl.pallas_call(
        _out_kernel,
        grid=(b, t // tt),
        in_specs=[pl.BlockSpec((None, tt, dh), lambda i, j: (i, j, 0)),
                  pl.BlockSpec((None, tt, dh), lambda i, j: (i, j, 0)),
                  pl.BlockSpec((None, tt, d), lambda i, j: (i, j, col["g_s5"])),
                  pl.BlockSpec((None, tt, d), lambda i, j: (i, j, col["g_rw"])),
                  pl.BlockSpec((None, tt, d), lambda i, j: (i, j, 0)),
                  pl.BlockSpec((None, tm, d), gt_map),
                  pl.BlockSpec((dh, d), lambda i, j: (0, 0)),
                  pl.BlockSpec((dh, d), lambda i, j: (0, 0)),
                  pl.BlockSpec((1, d), lambda i, j: (0, 0))],
        out_specs=pl.BlockSpec((None, tt, d), lambda i, j: (i, j, 0)),
        out_shape=jax.ShapeDtypeStruct((b, t, d), F32),
        compiler_params=_params("parallel", "parallel"),
        name="out_proj",
    )(o_s, o_r, p_main, p_main, x, mod, w1, w2, final_g.reshape(1, d))


def kernel(x_prompt, x_sample, c_prompt, c_sample, state_s5_re, state_s5_im, state_wkv, state_shift, norm_g, w_ada, b_ada, w_in, mu_rw, A_re, A_im, log_step, B_re, B_im, C_re, C_im, D_skip, w_glu, b_glu, w0, w2, a0, a2, k_k, k_a, r_k, gn_w, gn_b, w_out, final_g):
    depth = norm_g.shape[0]
    assert depth == 1
    bp, tp, d = x_prompt.shape
    bs = x_sample.shape[0]
    assert x_sample.shape[1] == 1
    dh = d // 2
    l = 0

    off_rw = 2 * dh
    off_lora = off_rw + 3 * dh
    off_rwz = off_lora + 2 * LORA
    off_gate = off_rwz + dh
    w = w_in[l]
    w_main = jnp.concatenate([w[:, :off_lora], w[:, off_rwz:]], axis=1).astype(BF16)
    w_lora = w[:, off_lora:off_rwz].astype(BF16)
    col = {"u": 0, "z_s5": 1, "r": 2, "k": 3, "v": 4, "z_rw": 5, "g_s5": 3, "g_rw": 4}
    mu = mu_rw[l]
    mu_parts = (mu[None, :dh], mu[None, dh:2 * dh], mu[None, 2 * dh:3 * dh], mu[None, 3 * dh:])
    zpad = jnp.zeros((LORA, dh), F32)
    w2p = jnp.concatenate([w2[l], zpad], axis=0).astype(BF16)
    a2p = jnp.concatenate([zpad, a2[l]], axis=0).astype(BF16)
    row = lambda x: x.reshape(1, -1)
    rw_params = (row(w0[l]), w2p, row(a0[l]), a2p, row(k_k[l]), row(k_a[l]), row(r_k[l]),
                 row(gn_w[l]), row(gn_b[l]))
    w_out_bf = w_out[l].astype(BF16)
    wo1, wo2 = w_out_bf[:dh], w_out_bf[dh:]
    w_glu_bf = w_glu[l].astype(BF16)

    ab_re, ab_im, as_re, as_im, bbt_re, bbt_im = _s5_discretise(A_re[l], A_im[l], log_step[l], B_re[l], B_im[l])
    b_w, c_w = _s5_block_weights(bbt_re, bbt_im, C_re[l], C_im[l])
    flat = lambda x: x.reshape(1, -1)
    ab_re, ab_im, as_re, as_im = flat(ab_re), flat(ab_im), flat(as_re), flat(as_im)
    d_skip = flat(D_skip[l])

    mod = _mod(jnp.concatenate([c_prompt, c_sample], axis=0), w_ada[l].astype(BF16), b_ada[l])
    mod_p = mod[:bp].reshape(bp, 1, 3 * d)
    mod_s = mod[bp:].reshape(1, bs, 3 * d)

    h_p = _modulated_norm(x_prompt, norm_g[l], mod_p, BF16, 512)
    shift_p = _modulated_norm(x_prompt[:, tp - 1:, :], norm_g[l], mod_p, F32, 1)[:, 0]
    a_p = h_p.reshape(bp * tp, d)
    pm = _matmul(a_p, w_main, 1024, 1024).reshape(bp, tp, -1)
    plo = _matmul(a_p, w_lora, 1024, LANES).reshape(bp, tp, LANES)
    y_s5, xre_p, xim_p = _s5_sequence(pm, b_w, c_w, ab_re, ab_im, as_re, as_im, d_skip)
    o_s = _glu(y_s5.reshape(bp * tp, dh), pm.reshape(bp * tp, -1), col["z_s5"], w_glu_bf, b_glu[l], 512)
    o_r, hs_p = _rwkv_sequence(pm, plo, col, mu_parts, *rw_params)
    y_prompt = _out_proj(o_s.reshape(bp, tp, dh), o_r, pm, col, x_prompt, mod_p, wo1, wo2, final_g, 256)
    g_s5 = A_re.shape[1]
    s5_shape = (1, bp, g_s5, P_S5)
    wkv_p = jnp.swapaxes(hs_p, -1, -2)[None]

    xs = x_sample.reshape(1, bs, d)
    h_s = _modulated_norm(xs, norm_g[l], mod_s, F32, bs)[0]
    a_s = jnp.concatenate([h_s, state_shift[l]], axis=0).astype(BF16)
    ps = _matmul(a_s, w_main, 2 * bs, 1024)
    pslo = _matmul(a_s, w_lora, 2 * bs, LANES)
    y_s5s, xre_s, xim_s = _s5_step(ps[:bs], b_w, c_w, ab_re, ab_im, d_skip,
                                   state_s5_re[l].reshape(bs, -1), state_s5_im[l].reshape(bs, -1))
    o_ss = _glu(y_s5s, ps[:bs], col["z_s5"], w_glu_bf, b_glu[l], bs)
    o_rs, wkv_s = _rwkv_step(ps, pslo, col, mu_parts, *rw_params, state_wkv[l])
    y_sample = _out_proj(o_ss[None], o_rs[None], ps[:bs][None], col, xs, mod_s, wo1, wo2, final_g, bs)
    y_sample = y_sample.reshape(bs, 1, d)

    return (y_prompt, y_sample,
            xre_p.reshape(s5_shape), xim_p.reshape(s5_shape), wkv_p, shift_p[None],
            xre_s.reshape(1, bs, g_s5, P_S5), xim_s.reshape(1, bs, g_s5, P_S5), wkv_s[None], h_s[None])
```

```python
import functools

import jax
import jax.numpy as jnp
from jax import lax
from jax.experimental import pallas as pl
from jax.experimental.pallas import tpu as pltpu

F32 = jnp.float32
BF16 = jnp.bfloat16

RMS_EPS = 1e-6
GN_EPS = 64e-5
S5_GROUP = 16
P_S5 = 64
RW_HEAD = 64
LORA = 64
LANES = 128
GROUPS_PER_BLOCK = LANES // S5_GROUP
STATES_PER_BLOCK = GROUPS_PER_BLOCK * P_S5
S5_SEG = 16
RW_CHUNK = 64
INV_BASE = 8

_NT = (((1,), (1,)), ((), ()))
_TN = (((0,), (0,)), ((), ()))


def _dot(a, b):
    return jnp.dot(a.astype(BF16), b.astype(BF16), preferred_element_type=F32)


def _sigmoid(x):
    return 1.0 / (1.0 + jnp.exp(-x))


def _silu(x):
    return x * _sigmoid(x)


def _params(*sem):
    return pltpu.CompilerParams(dimension_semantics=sem, vmem_limit_bytes=56 * 1024 * 1024)


def _mod_kernel(c_ref, w_ref, b_ref, o_ref):
    o_ref[...] = _dot(_silu(c_ref[...]), w_ref[...]) + b_ref[...]


def _mod(c, w_bf, b):
    rows, d = c.shape
    n = w_bf.shape[1]
    tn = 1024
    return pl.pallas_call(
        _mod_kernel,
        grid=(n // tn,),
        in_specs=[pl.BlockSpec((rows, d), lambda j: (0, 0)),
                  pl.BlockSpec((d, tn), lambda j: (0, j)),
                  pl.BlockSpec((1, tn), lambda j: (0, j))],
        out_specs=pl.BlockSpec((rows, tn), lambda j: (0, j)),
        out_shape=jax.ShapeDtypeStruct((rows, n), F32),
        compiler_params=_params("parallel"),
        name="adaln_mod",
    )(c, w_bf, b.reshape(1, n))


def _h_kernel(x_ref, g_ref, sh_ref, sc_ref, h_ref):
    x = x_ref[...]
    y = x * lax.rsqrt(jnp.mean(x * x, axis=-1, keepdims=True) + RMS_EPS) * g_ref[...]
    h_ref[...] = (y * (1.0 + sc_ref[...]) + sh_ref[...]).astype(h_ref.dtype)


def _modulated_norm(x, g, mod, out_dtype, tt):
    b, t, d = x.shape
    tt = min(tt, t)
    tm = 1 if mod.shape[1] == 1 else tt
    mod_map = (lambda i, j: (i, 0, 0)) if tm == 1 else (lambda i, j: (i, j, 0))
    mod_map1 = (lambda i, j: (i, 0, 1)) if tm == 1 else (lambda i, j: (i, j, 1))
    return pl.pallas_call(
        _h_kernel,
        grid=(b, t // tt),
        in_specs=[pl.BlockSpec((None, tt, d), lambda i, j: (i, j, 0)),
                  pl.BlockSpec((1, d), lambda i, j: (0, 0)),
                  pl.BlockSpec((None, tm, d), mod_map),
                  pl.BlockSpec((None, tm, d), mod_map1)],
        out_specs=pl.BlockSpec((None, tt, d), lambda i, j: (i, j, 0)),
        out_shape=jax.ShapeDtypeStruct((b, t, d), out_dtype),
        compiler_params=_params("parallel", "parallel"),
        name="modulated_norm",
    )(x, g.reshape(1, d), mod, mod)


def _mm_kernel(a_ref, w_ref, o_ref):
    o_ref[...] = jnp.dot(a_ref[...], w_ref[...], preferred_element_type=F32)


def _matmul(a, w, tm, tn):
    m, k = a.shape
    n = w.shape[1]
    tm = min(tm, m)
    return pl.pallas_call(
        _mm_kernel,
        grid=(m // tm, n // tn),
        in_specs=[pl.BlockSpec((tm, k), lambda i, j: (i, 0)),
                  pl.BlockSpec((k, tn), lambda i, j: (0, j))],
        out_specs=pl.BlockSpec((tm, tn), lambda i, j: (i, j)),
        out_shape=jax.ShapeDtypeStruct((m, n), F32),
        compiler_params=_params("parallel", "parallel"),
        name="in_proj",
    )(a, w)


def _s5_disc_kernel(are_ref, aim_ref, ls_ref, bre_ref, bim_ref,
                    abr_ref, abi_ref, asr_ref, asi_ref, bbr_ref, bbi_ref):
    step = jnp.exp(ls_ref[...])
    lam_re = jnp.minimum(are_ref[...], -1e-4)
    lam_im = aim_ref[...]
    mag = jnp.exp(lam_re * step)
    ab_re = mag * jnp.cos(lam_im * step)
    ab_im = mag * jnp.sin(lam_im * step)
    den = lam_re * lam_re + lam_im * lam_im
    f_re = ((ab_re - 1.0) * lam_re + ab_im * lam_im) / den
    f_im = (ab_im * lam_re - (ab_re - 1.0) * lam_im) / den
    br, bi = bre_ref[...], bim_ref[...]
    bbr_ref[...] = f_re * br - f_im * bi
    bbi_ref[...] = f_re * bi + f_im * br
    abr_ref[...] = ab_re
    abi_ref[...] = ab_im
    pr, pi = ab_re, ab_im
    n = 1
    while n < S5_SEG:
        pr, pi = pr * pr - pi * pi, 2.0 * pr * pi
        n *= 2
    asr_ref[...] = pr
    asi_ref[...] = pi


def _s5_discretise(a_re, a_im, log_step, b_re, b_im):
    g, p = a_re.shape
    c = b_re.shape[-1]
    gp = jax.ShapeDtypeStruct((g, 1, p), F32)
    gcp = jax.ShapeDtypeStruct((g, c, p), F32)
    return pl.pallas_call(
        _s5_disc_kernel,
        out_shape=(gp, gp, gp, gp, gcp, gcp),
        name="s5_discretise",
    )(a_re.reshape(g, 1, p), a_im.reshape(g, 1, p), log_step.reshape(g, 1, 1),
      jnp.swapaxes(b_re, 1, 2), jnp.swapaxes(b_im, 1, 2))


def _s5_block_weights(bbt_re, bbt_im, c_re, c_im):
    g, c, p = bbt_re.shape
    nb = g // GROUPS_PER_BLOCK
    eye = jnp.eye(GROUPS_PER_BLOCK, dtype=F32)

    def b_blk(x):
        x = x.reshape(nb, GROUPS_PER_BLOCK, c, p)
        return jnp.einsum("jacp,ab->jacbp", x, eye).reshape(nb, LANES, STATES_PER_BLOCK)

    def c_blk(x):
        x = x.reshape(nb, GROUPS_PER_BLOCK, c, p)
        return jnp.einsum("jbcp,ab->japbc", x, eye).reshape(nb, STATES_PER_BLOCK, LANES)

    b_w = jnp.concatenate([b_blk(bbt_re), b_blk(bbt_im)], axis=2).astype(BF16)
    c_w = jnp.concatenate([c_blk(c_re), -c_blk(c_im)], axis=1).astype(BF16)
    return b_w, c_w


def _s5_seq_kernel(u_ref, bw_ref, cw_ref, abr_ref, abi_ref, asr_ref, asi_ref, d_ref,
                   y_ref, xre_ref, xim_ref, up_ref, bu_ref, e_ref, cin_ref, *, t_len):
    ns = STATES_PER_BLOCK
    n_lt = ns // LANES
    nseg = t_len // S5_SEG

    def seg_rows(tl):
        return pl.ds(tl * nseg, nseg)

    for tl in range(S5_SEG):
        up_ref[seg_rows(tl), :] = u_ref[pl.ds(tl, nseg, stride=S5_SEG), :]
    u = up_ref[...]
    bu = _dot(u, bw_ref[...])
    for kt in range(2 * n_lt):
        bu_ref[kt] = bu[:, kt * LANES:(kt + 1) * LANES]

    def scan_tiles(init_from_carry, store_states):
        for lt in range(n_lt):
            re_l = slice(lt * LANES, (lt + 1) * LANES)
            im_l = slice(ns + lt * LANES, ns + (lt + 1) * LANES)
            bre, bim = bu_ref.at[lt], bu_ref.at[n_lt + lt]
            ar = abr_ref[:, re_l]
            ai = abi_ref[:, re_l]
            if init_from_carry:
                xr = cin_ref[:, re_l]
                xi = cin_ref[:, im_l]
            else:
                xr = jnp.zeros((nseg, LANES), F32)
                xi = jnp.zeros((nseg, LANES), F32)
            for tl in range(S5_SEG):
                br = bre[seg_rows(tl), :]
                bi = bim[seg_rows(tl), :]
                xr, xi = ar * xr - ai * xi + br, ar * xi + ai * xr + bi
                if store_states:
                    bre[seg_rows(tl), :] = xr
                    bim[seg_rows(tl), :] = xi
            if not store_states:
                e_ref[:, re_l] = xr
                e_ref[:, im_l] = xi

    scan_tiles(False, False)

    asr = asr_ref[...]
    asi = asi_ref[...]

    def seg_step(s, carry):
        cr, ci = carry
        cin_ref[pl.ds(s, 1), :] = jnp.concatenate([cr, ci], axis=1)
        e = e_ref[pl.ds(s, 1), :]
        er, ei = e[:, :ns], e[:, ns:]
        return asr * cr - asi * ci + er, asr * ci + asi * cr + ei

    zero = jnp.zeros((1, ns), F32)
    fr, fi = lax.fori_loop(0, nseg, seg_step, (zero, zero))
    xre_ref[...] = fr
    xim_ref[...] = fi

    scan_tiles(True, True)

    y = d_ref[...] * u
    for kt in range(2 * n_lt):
        y = y + _dot(bu_ref[kt], cw_ref[kt * LANES:(kt + 1) * LANES, :])
    up_ref[...] = y
    for tl in range(S5_SEG):
        y_ref[pl.ds(tl, nseg, stride=S5_SEG), :] = up_ref[seg_rows(tl), :]


def _s5_sequence(p_main, b_w, c_w, ab_re, ab_im, as_re, as_im, d_skip):
    b, t, _ = p_main.shape
    nb = b_w.shape[0]
    ns = STATES_PER_BLOCK
    vec = lambda: pl.BlockSpec((1, ns), lambda i, j: (0, j))
    y, xre, xim = pl.pallas_call(
        functools.partial(_s5_seq_kernel, t_len=t),
        grid=(b, nb),
        in_specs=[pl.BlockSpec((None, t, LANES), lambda i, j: (i, 0, j)),
                  pl.BlockSpec((None, LANES, 2 * ns), lambda i, j: (j, 0, 0)),
                  pl.BlockSpec((None, 2 * ns, LANES), lambda i, j: (j, 0, 0)),
                  vec(), vec(), vec(), vec(),
                  pl.BlockSpec((1, LANES), lambda i, j: (0, j))],
        out_specs=[pl.BlockSpec((None, t, LANES), lambda i, j: (i, 0, j)),
                   pl.BlockSpec((None, 1, ns), lambda i, j: (i, 0, j)),
                   pl.BlockSpec((None, 1, ns), lambda i, j: (i, 0, j))],
        out_shape=(jax.ShapeDtypeStruct((b, t, nb * LANES), F32),
                   jax.ShapeDtypeStruct((b, 1, nb * ns), F32),
                   jax.ShapeDtypeStruct((b, 1, nb * ns), F32)),
        scratch_shapes=[pltpu.VMEM((t, LANES), F32),
                        pltpu.VMEM((2 * ns // LANES, t, LANES), F32),
                        pltpu.VMEM((t // S5_SEG, 2 * ns), F32),
                        pltpu.VMEM((t // S5_SEG, 2 * ns), F32)],
        compiler_params=_params("parallel", "parallel"),
        name="s5_sequence",
    )(p_main, b_w, c_w, ab_re, ab_im, as_re, as_im, d_skip)
    return y, xre, xim


def _s5_step_kernel(u_ref, bw_ref, cw_ref, abr_ref, abi_ref, d_ref, x0r_ref, x0i_ref,
                    y_ref, x1r_ref, x1i_ref):
    ns = STATES_PER_BLOCK
    u = u_ref[...]
    bu = _dot(u, bw_ref[...])
    ar, ai = abr_ref[...], abi_ref[...]
    x0r, x0i = x0r_ref[...], x0i_ref[...]
    xr = ar * x0r - ai * x0i + bu[:, :ns]
    xi = ar * x0i + ai * x0r + bu[:, ns:]
    x1r_ref[...] = xr
    x1i_ref[...] = xi
    y_ref[...] = _dot(jnp.concatenate([xr, xi], axis=1), cw_ref[...]) + d_ref[...] * u


def _s5_step(p_rows, b_w, c_w, ab_re, ab_im, d_skip, x0_re, x0_im):
    rows = p_rows.shape[0]
    nb = b_w.shape[0]
    ns = STATES_PER_BLOCK
    vec = lambda: pl.BlockSpec((1, ns), lambda j: (0, j))
    st = lambda: pl.BlockSpec((rows, ns), lambda j: (0, j))
    return pl.pallas_call(
        _s5_step_kernel,
        grid=(nb,),
        in_specs=[pl.BlockSpec((rows, LANES), lambda j: (0, j)),
                  pl.BlockSpec((None, LANES, 2 * ns), lambda j: (j, 0, 0)),
                  pl.BlockSpec((None, 2 * ns, LANES), lambda j: (j, 0, 0)),
                  vec(), vec(),
                  pl.BlockSpec((1, LANES), lambda j: (0, j)),
                  st(), st()],
        out_specs=[pl.BlockSpec((rows, LANES), lambda j: (0, j)), st(), st()],
        out_shape=(jax.ShapeDtypeStruct((rows, nb * LANES), F32),
                   jax.ShapeDtypeStruct((rows, nb * ns), F32),
                   jax.ShapeDtypeStruct((rows, nb * ns), F32)),
        compiler_params=_params("parallel"),
        name="s5_step",
    )(p_rows, b_w, c_w, ab_re, ab_im, d_skip, x0_re, x0_im)


def _glu_kernel(y_ref, z_ref, w_ref, b_ref, o_ref):
    y = jax.nn.gelu(y_ref[...], approximate=True)
    gate = _sigmoid(_dot(y, w_ref[...]) + b_ref[...])
    o_ref[...] = (y * gate * _silu(z_ref[...])).astype(o_ref.dtype)


def _glu(y, p_rows, z_block, w_bf, b, tm):
    rows, d = y.shape
    tm = min(tm, rows)
    return pl.pallas_call(
        _glu_kernel,
        grid=(rows // tm,),
        in_specs=[pl.BlockSpec((tm, d), lambda i: (i, 0)),
                  pl.BlockSpec((tm, d), lambda i: (i, z_block)),
                  pl.BlockSpec((d, d), lambda i: (0, 0)),
                  pl.BlockSpec((1, d), lambda i: (0, 0))],
        out_specs=pl.BlockSpec((tm, d), lambda i: (i, 0)),
        out_shape=jax.ShapeDtypeStruct((rows, d), BF16),
        compiler_params=_params("parallel"),
        name="s5_glu",
    )(y, p_rows, w_bf, b.reshape(1, d))


def _softplus(x):
    return jnp.maximum(x, 0.0) + jnp.log(1.0 + jnp.exp(-jnp.abs(x)))


def _rwkv_token_terms(r, k, lo, w0, w2p, a0, a2p, k_a):
    w = -_softplus(-(w0 + _dot(jnp.tanh(lo), w2p))) - 0.5
    logd = -jnp.exp(w)
    a = _sigmoid(a0 + _dot(lo, a2p))
    k2 = k * (1.0 + (a - 1.0) * k_a)
    return logd, a, k2


def _unit_rows(x):
    n = jnp.sqrt(jnp.sum(x * x, axis=-1, keepdims=True))
    return x / jnp.maximum(n, 1e-12)


def _group_norm_bonus(o, r, k2, v, rk, gw, gb):
    mu = jnp.mean(o, axis=-1, keepdims=True)
    var = jnp.mean((o - mu) ** 2, axis=-1, keepdims=True)
    o = (o - mu) * lax.rsqrt(var + GN_EPS) * gw + gb
    return o + jnp.sum(r * k2 * rk, axis=-1, keepdims=True) * v


def _rwkv_chunk_kernel(r_ref, k_ref, v_ref, lo_ref, z_ref,
                       mur_ref, muk_ref, muv_ref, mulo_ref,
                       w0_ref, w2p_ref, a0_ref, a2p_ref, kk_ref, ka_ref, rk_ref, gw_ref, gb_ref,
                       o_ref, hs_ref,
                       h_scr, pr_scr, pk_scr, pv_scr, plo_scr, *, n_heads):
    L = RW_CHUNK
    W = 2 * RW_HEAD
    n_pairs = n_heads // 2
    c = pl.program_id(1)

    @pl.when(c == 0)
    def _():
        h_scr[...] = jnp.zeros_like(h_scr)
        pr_scr[...] = jnp.zeros_like(pr_scr)
        pk_scr[...] = jnp.zeros_like(pk_scr)
        pv_scr[...] = jnp.zeros_like(pv_scr)
        plo_scr[...] = jnp.zeros_like(plo_scr)

    row1 = lax.broadcasted_iota(jnp.int32, (L, 1), 0)

    def token_shift(cur_ref, prev_scr, mu_ref):
        cur = cur_ref[...]
        prev = jnp.where(row1 == 0, prev_scr[...], pltpu.roll(cur, 1, 0))
        prev_scr[...] = cur[L - 1:L, :]
        return cur + (prev - cur) * mu_ref[...]

    r = token_shift(r_ref, pr_scr, mur_ref)
    k = token_shift(k_ref, pk_scr, muk_ref)
    v = token_shift(v_ref, pv_scr, muv_ref)
    lo = token_shift(lo_ref, plo_scr, mulo_ref)
    logd, a, k2 = _rwkv_token_terms(r, k, lo, w0_ref[...], w2p_ref[...], a0_ref[...], a2p_ref[...],
                                    ka_ref[...])
    kk = k * kk_ref[...]

    tri = (lax.broadcasted_iota(jnp.int32, (L, L), 0)
           >= lax.broadcasted_iota(jnp.int32, (L, L), 1)).astype(BF16)
    hi = logd.astype(BF16)
    rem = logd - hi.astype(F32)
    mid = rem.astype(BF16)
    low = (rem - mid.astype(F32)).astype(BF16)
    lp = (jnp.dot(tri, hi, preferred_element_type=F32)
          + jnp.dot(tri, mid, preferred_element_type=F32)
          + jnp.dot(tri, low, preferred_element_type=F32))
    p_inc = jnp.exp(lp)
    p_exc = jnp.exp(lp - logd)
    p_inv = jnp.exp(-lp)

    first_head = lax.broadcasted_iota(jnp.int32, (L, W), 1) < RW_HEAD

    def head_sum(x):
        s0 = jnp.sum(jnp.where(first_head, x, 0.0), axis=-1, keepdims=True)
        s1 = jnp.sum(jnp.where(first_head, 0.0, x), axis=-1, keepdims=True)
        return jnp.where(first_head, s0, s1)

    def stack_heads(x):
        return jnp.concatenate([jnp.where(first_head, x, 0.0), jnp.where(first_head, 0.0, x)], axis=0)

    ri = lax.broadcasted_iota(jnp.int32, (2 * L, 2 * L), 0)
    ci = lax.broadcasted_iota(jnp.int32, (2 * L, 2 * L), 1)
    t_row, t_col = ri & (L - 1), ci & (L - 1)
    strict = t_row > t_col
    incl = t_row >= t_col
    eye = (ri == ci).astype(F32)
    blk_masks = []
    s = INV_BASE
    while s <= L:
        blk_masks.append((ri // s) == (ci // s))
        s *= 2

    pairs = range(n_pairs)
    lanes = [slice(p * W, (p + 1) * W) for p in pairs]

    lhs, nm, mm, ab, vs, kb, p_end = [], [], [], [], [], [], []
    for sl in lanes:
        kk_p = kk[:, sl]
        kkn = kk_p / jnp.maximum(jnp.sqrt(head_sum(kk_p * kk_p)), 1e-12)
        pinc, pinv = p_inc[:, sl], p_inv[:, sl]
        kd = k2[:, sl] * pinv
        bd = kkn * a[:, sl] * pinv
        pe = pinc[L - 1:L, :]
        lhs_p = jnp.concatenate([stack_heads(kkn * p_exc[:, sl]), stack_heads(r[:, sl] * pinc)],
                                axis=0).astype(BF16)
        rhs_p = jnp.concatenate([stack_heads(bd), stack_heads(kd)], axis=0).astype(BF16)
        amat = lax.dot_general(lhs_p, rhs_p, _NT, preferred_element_type=F32)
        lhs.append(lhs_p)
        nm.append(jnp.where(strict, amat[:2 * L, :2 * L], 0.0))
        mm.append(jnp.where(strict, amat[:2 * L, 2 * L:], 0.0).astype(BF16))
        ab.append(jnp.concatenate([jnp.where(incl, amat[2 * L:, 2 * L:], 0.0),
                                   -jnp.where(incl, amat[2 * L:, :2 * L], 0.0)], axis=1).astype(BF16))
        vs.append(stack_heads(v[:, sl]).astype(BF16))
        kb.append(jnp.concatenate([stack_heads(kd * pe), stack_heads(bd * pe)], axis=0).astype(BF16))
        p_end.append(pe)

    d = [jnp.where(blk_masks[0], n_p, 0.0).astype(BF16) for n_p in nm]
    x = [eye - d_p.astype(F32) for d_p in d]
    pw = [jnp.dot(d_p, d_p, preferred_element_type=F32) for d_p in d]
    s = 2
    while s < INV_BASE:
        x = [x_p + _dot(x_p, pw_p) for x_p, pw_p in zip(x, pw)]
        s *= 2
        if s < INV_BASE:
            pw = [_dot(pw_p, pw_p) for pw_p in pw]
    for lvl in range(1, len(blk_masks)):
        off = blk_masks[lvl] & ~blk_masks[lvl - 1]
        xc = [_dot(x_p, jnp.where(off, n_p, 0.0)) for x_p, n_p in zip(x, nm)]
        x = [x_p - _dot(xc_p, x_p) for x_p, xc_p in zip(x, xc)]

    hs = [h_scr[p] for p in pairs]
    lh = [jnp.dot(lhs_p, hs_p.astype(BF16), preferred_element_type=F32) for lhs_p, hs_p in zip(lhs, hs)]
    mv = [jnp.dot(mm_p, vs_p, preferred_element_type=F32) for mm_p, vs_p in zip(mm, vs)]
    u = [_dot(x_p, lh_p[:2 * L] + mv_p).astype(BF16) for x_p, lh_p, mv_p in zip(x, lh, mv)]
    o_st = [lh_p[2 * L:] + jnp.dot(ab_p, jnp.concatenate([vs_p, u_p], axis=0), preferred_element_type=F32)
            for lh_p, ab_p, vs_p, u_p in zip(lh, ab, vs, u)]
    for p in pairs:
        p_end_col = jnp.sum(eye * p_end[p], axis=1, keepdims=True)
        h_scr[p] = p_end_col * hs[p] + lax.dot_general(
            kb[p], jnp.concatenate([vs[p], -u[p]], axis=0), _TN, preferred_element_type=F32)

    for p, sl in enumerate(lanes):
        o = o_st[p][:L] + o_st[p][L:]
        mu = head_sum(o) * (1.0 / RW_HEAD)
        var = head_sum((o - mu) ** 2) * (1.0 / RW_HEAD)
        o = (o - mu) * lax.rsqrt(var + GN_EPS) * gw_ref[:, sl] + gb_ref[:, sl]
        o = o + head_sum(r[:, sl] * k2[:, sl] * rk_ref[:, sl]) * v[:, sl]
        o_ref[:, sl] = (o * _silu(z_ref[:, sl])).astype(o_ref.dtype)

    @pl.when(c == pl.num_programs(1) - 1)
    def _():
        hs_ref[...] = h_scr[...]


def _rwkv_sequence(p_main, p_lora, col, mu, w0, w2p, a0, a2p, k_k, k_a, r_k, gn_w, gn_b):
    b, t, _ = p_main.shape
    d = w0.shape[-1]
    n_heads = d // RW_HEAD
    n_pairs, pw = n_heads // 2, 2 * RW_HEAD
    L = RW_CHUNK
    blk = lambda cb: pl.BlockSpec((None, L, d), lambda i, j, cb=cb: (i, j, cb))
    vec = lambda n: pl.BlockSpec((1, n), lambda i, j: (0, 0))
    full = lambda shp: pl.BlockSpec(shp, lambda i, j: (0,) * len(shp))
    mu_r, mu_k, mu_v, mu_lo = mu
    o, hs = pl.pallas_call(
        functools.partial(_rwkv_chunk_kernel, n_heads=n_heads),
        grid=(b, t // L),
        in_specs=[blk(col["r"]), blk(col["k"]), blk(col["v"]),
                  pl.BlockSpec((None, L, LANES), lambda i, j: (i, j, 0)),
                  blk(col["z_rw"]),
                  vec(d), vec(d), vec(d), vec(LANES),
                  vec(d), full((LANES, d)), vec(d), full((LANES, d)),
                  vec(d), vec(d), vec(d), vec(d), vec(d)],
        out_specs=[pl.BlockSpec((None, L, d), lambda i, j: (i, j, 0)),
                   pl.BlockSpec((None, n_pairs, pw, pw), lambda i, j: (i, 0, 0, 0))],
        out_shape=(jax.ShapeDtypeStruct((b, t, d), BF16),
                   jax.ShapeDtypeStruct((b, n_pairs, pw, pw), F32)),
        scratch_shapes=[pltpu.VMEM((n_pairs, pw, pw), F32),
                        pltpu.VMEM((1, d), F32), pltpu.VMEM((1, d), F32), pltpu.VMEM((1, d), F32),
                        pltpu.VMEM((1, LANES), F32)],
        compiler_params=_params("parallel", "arbitrary"),
        name="rwkv_sequence",
    )(p_main, p_main, p_main, p_lora, p_main,
      mu_r, mu_k, mu_v, mu_lo, w0, w2p, a0, a2p, k_k, k_a, r_k, gn_w, gn_b)
    hs = hs.reshape(b, n_pairs, 2, RW_HEAD, 2, RW_HEAD)
    hs = jnp.stack([hs[:, :, 0, :, 0, :], hs[:, :, 1, :, 1, :]], axis=2)
    return o, jnp.swapaxes(hs.reshape(b, n_heads, RW_HEAD, RW_HEAD), -1, -2)


def _rwkv_step_prep_kernel(cr_ref, ck_ref, cv_ref, clo_ref, pr_ref, pk_ref, pv_ref, plo_ref,
                           mur_ref, muk_ref, muv_ref, mulo_ref,
                           w0_ref, w2p_ref, a0_ref, a2p_ref, kk_ref, ka_ref,
                           r_o, k2_o, v_o, kk_o, a_o, d_o):
    def lerp(c_ref, p_ref, mu_ref):
        cur = c_ref[...]
        return cur + (p_ref[...] - cur) * mu_ref[...]

    r = lerp(cr_ref, pr_ref, mur_ref)
    k = lerp(ck_ref, pk_ref, muk_ref)
    v = lerp(cv_ref, pv_ref, muv_ref)
    lo = lerp(clo_ref, plo_ref, mulo_ref)
    logd, a, k2 = _rwkv_token_terms(r, k, lo, w0_ref[...], w2p_ref[...], a0_ref[...], a2p_ref[...],
                                    ka_ref[...])
    r_o[...] = r
    k2_o[...] = k2
    v_o[...] = v
    kk_o[...] = k * kk_ref[...]
    a_o[...] = a
    d_o[...] = jnp.exp(logd)


def _rwkv_step_kernel(s_ref, r_ref, k2_ref, v_ref, kk_ref, a_ref, d_ref, z_ref,
                      rk_ref, gw_ref, gb_ref, o_ref, s1_ref, orow_scr, *, n_heads, bb):
    n = RW_HEAD
    eye = (lax.broadcasted_iota(jnp.int32, (n, n), 0)
           == lax.broadcasted_iota(jnp.int32, (n, n), 1)).astype(F32)
    heads = range(n_heads)
    for bi in range(bb):
        r, k2, v, d = r_ref[bi], k2_ref[bi], v_ref[bi], d_ref[bi]
        kkn = _unit_rows(kk_ref[bi])
        b = kkn * a_ref[bi]
        row = lambda x, h: x[h:h + 1, :]
        sa = [jnp.sum(s_ref[bi, h] * row(kkn, h), axis=1, keepdims=True) for h in heads]
        v_col = [jnp.sum(eye * row(v, h), axis=1, keepdims=True) for h in heads]
        o_col = []
        for h in heads:
            s1 = s_ref[bi, h] * row(d, h) - sa[h] * row(b, h) + v_col[h] * row(k2, h)
            s1_ref[bi, h] = s1
            o_col.append(jnp.sum(s1 * row(r, h), axis=1, keepdims=True))
        for h in heads:
            orow_scr[pl.ds(h, 1), :] = jnp.sum(eye * o_col[h], axis=0, keepdims=True)
        o = _group_norm_bonus(orow_scr[...], r, k2, v, rk_ref[...], gw_ref[...], gb_ref[...])
        o_ref[bi] = (o * _silu(z_ref[bi])).astype(o_ref.dtype)


def _rwkv_step(p_rows, p_lora_rows, col, mu, w0, w2p, a0, a2p, k_k, k_a, r_k, gn_w, gn_b, s0):
    rows = s0.shape[0]
    d = w0.shape[-1]
    n_heads = d // RW_HEAD
    cur = lambda cb: pl.BlockSpec((rows, d), lambda i, cb=cb: (0, cb))
    prv = lambda cb: pl.BlockSpec((rows, d), lambda i, cb=cb: (1, cb))
    vec = lambda n: pl.BlockSpec((1, n), lambda i: (0, 0))
    mu_r, mu_k, mu_v, mu_lo = mu
    out = jax.ShapeDtypeStruct((rows, d), F32)
    terms = pl.pallas_call(
        _rwkv_step_prep_kernel,
        grid=(1,),
        in_specs=[cur(col["r"]), cur(col["k"]), cur(col["v"]),
                  pl.BlockSpec((rows, LANES), lambda i: (0, 0)),
                  prv(col["r"]), prv(col["k"]), prv(col["v"]),
                  pl.BlockSpec((rows, LANES), lambda i: (1, 0)),
                  vec(d), vec(d), vec(d), vec(LANES),
                  vec(d), pl.BlockSpec((LANES, d), lambda i: (0, 0)),
                  vec(d), pl.BlockSpec((LANES, d), lambda i: (0, 0)),
                  vec(d), vec(d)],
        out_specs=[pl.BlockSpec((rows, d), lambda i: (0, 0))] * 6,
        out_shape=(out,) * 6,
        compiler_params=_params("arbitrary"),
        name="rwkv_step_prep",
    )(p_rows, p_rows, p_rows, p_lora_rows, p_rows, p_rows, p_rows, p_lora_rows,
      mu_r, mu_k, mu_v, mu_lo, w0, w2p, a0, a2p, k_k, k_a)
    heads = lambda x: x.reshape(x.shape[0], n_heads, RW_HEAD)
    z = p_rows[:rows, col["z_rw"] * d:(col["z_rw"] + 1) * d]
    bb = 4 if rows % 4 == 0 else 1
    per_b = lambda: pl.BlockSpec((bb, n_heads, RW_HEAD), lambda i: (i, 0, 0))
    par = lambda: pl.BlockSpec((n_heads, RW_HEAD), lambda i: (0, 0))
    st = lambda: pl.BlockSpec((bb, n_heads, RW_HEAD, RW_HEAD), lambda i: (i, 0, 0, 0))
    o, s1 = pl.pallas_call(
        functools.partial(_rwkv_step_kernel, n_heads=n_heads, bb=bb),
        grid=(rows // bb,),
        in_specs=[st()] + [per_b()] * 7 + [par()] * 3,
        out_specs=[per_b(), st()],
        out_shape=(jax.ShapeDtypeStruct((rows, n_heads, RW_HEAD), BF16),
                   jax.ShapeDtypeStruct(s0.shape, F32)),
        scratch_shapes=[pltpu.VMEM((n_heads, RW_HEAD), F32)],
        compiler_params=_params("parallel"),
        name="rwkv_step",
    )(s0, *[heads(x) for x in terms], heads(z),
      r_k.reshape(n_heads, RW_HEAD), gn_w.reshape(n_heads, RW_HEAD), gn_b.reshape(n_heads, RW_HEAD))
    return o.reshape(rows, d), s1


def _out_kernel(os_ref, or_ref, gs_ref, gr_ref, x_ref, gt_ref, w1_ref, w2_ref, fg_ref, y_ref):
    mixed = (_sigmoid(gs_ref[...]) * jnp.dot(os_ref[...], w1_ref[...], preferred_element_type=F32)
             + _sigmoid(gr_ref[...]) * jnp.dot(or_ref[...], w2_ref[...], preferred_element_type=F32))
    x = x_ref[...] + gt_ref[...] * mixed
    y_ref[...] = x * lax.rsqrt(jnp.mean(x * x, axis=-1, keepdims=True) + RMS_EPS) * fg_ref[...]


def _out_proj(o_s, o_r, p_main, col, x, mod, w1, w2, final_g, tt):
    b, t, d = x.shape
    dh = o_s.shape[-1]
    tt = min(tt, t)
    tm = 1 if mod.shape[1] == 1 else tt
    gt_map = (lambda i, j: (i, 0, 2)) if tm == 1 else (lambda i, j: (i, j, 2))
    return pl.pallas_call(
        _out_kernel,
        grid=(b, t // tt),
        in_specs=[pl.BlockSpec((None, tt, dh), lambda i, j: (i, j, 0)),
                  pl.BlockSpec((None, tt, dh), lambda i, j: (i, j, 0)),
                  pl.BlockSpec((None, tt, d), lambda i, j: (i, j, col["g_s5"])),
                  pl.BlockSpec((None, tt, d), lambda i, j: (i, j, col["g_rw"])),
                  pl.BlockSpec((None, tt, d), lambda i, j: (i, j, 0)),
                  pl.BlockSpec((None, tm, d), gt_map),
                  pl.BlockSpec((dh, d), lambda i, j: (0, 0)),
                  pl.BlockSpec((dh, d), lambda i, j: (0, 0)),
                  pl.BlockSpec((1, d), lambda i, j: (0, 0))],
        out_specs=pl.BlockSpec((None, tt, d), lambda i, j: (i, j, 0)),
        out_shape=jax.ShapeDtypeStruct((b, t, d), F32),
        compiler_params=_params("parallel", "parallel"),
        name="out_proj",
    )(o_s, o_r, p_main, p_main, x, mod, w1, w2, final_g.reshape(1, d))


def kernel(x_prompt, x_sample, c_prompt, c_sample, state_s5_re, state_s5_im, state_wkv, state_shift, norm_g, w_ada, b_ada, w_in, mu_rw, A_re, A_im, log_step, B_re, B_im, C_re, C_im, D_skip, w_glu, b_glu, w0, w2, a0, a2, k_k, k_a, r_k, gn_w, gn_b, w_out, final_g):
    depth = norm_g.shape[0]
    assert depth == 1
    bp, tp, d = x_prompt.shape
    bs = x_sample.shape[0]
    assert x_sample.shape[1] == 1
    dh = d // 2
    l = 0

    off_rw = 2 * dh
    off_lora = off_rw + 3 * dh
    off_rwz = off_lora + 2 * LORA
    off_gate = off_rwz + dh
    w = w_in[l]
    w_main = jnp.concatenate([w[:, :off_lora], w[:, off_rwz:]], axis=1).astype(BF16)
    w_lora = w[:, off_lora:off_rwz].astype(BF16)
    col = {"u": 0, "z_s5": 1, "r": 2, "k": 3, "v": 4, "z_rw": 5, "g_s5": 3, "g_rw": 4}
    mu = mu_rw[l]
    mu_parts = (mu[None, :dh], mu[None, dh:2 * dh], mu[None, 2 * dh:3 * dh], mu[None, 3 * dh:])
    zpad = jnp.zeros((LORA, dh), F32)
    w2p = jnp.concatenate([w2[l], zpad], axis=0).astype(BF16)
    a2p = jnp.concatenate([zpad, a2[l]], axis=0).astype(BF16)
    row = lambda x: x.reshape(1, -1)
    rw_params = (row(w0[l]), w2p, row(a0[l]), a2p, row(k_k[l]), row(k_a[l]), row(r_k[l]),
                 row(gn_w[l]), row(gn_b[l]))
    w_out_bf = w_out[l].astype(BF16)
    wo1, wo2 = w_out_bf[:dh], w_out_bf[dh:]
    w_glu_bf = w_glu[l].astype(BF16)

    ab_re, ab_im, as_re, as_im, bbt_re, bbt_im = _s5_discretise(A_re[l], A_im[l], log_step[l], B_re[l], B_im[l])
    b_w, c_w = _s5_block_weights(bbt_re, bbt_im, C_re[l], C_im[l])
    flat = lambda x: x.reshape(1, -1)
    ab_re, ab_im, as_re, as_im = flat(ab_re), flat(ab_im), flat(as_re), flat(as_im)
    d_skip = flat(D_skip[l])

    mod = _mod(jnp.concatenate([c_prompt, c_sample], axis=0), w_ada[l].astype(BF16), b_ada[l])
    mod_p = mod[:bp].reshape(bp, 1, 3 * d)
    mod_s = mod[bp:].reshape(1, bs, 3 * d)

    h_p = _modulated_norm(x_prompt, norm_g[l], mod_p, BF16, 512)
    shift_p = _modulated_norm(x_prompt[:, tp - 1:, :], norm_g[l], mod_p, F32, 1)[:, 0]
    a_p = h_p.reshape(bp * tp, d)
    pm = _matmul(a_p, w_main, 1024, 1024).reshape(bp, tp, -1)
    plo = _matmul(a_p, w_lora, 1024, LANES).reshape(bp, tp, LANES)
    y_s5, xre_p, xim_p = _s5_sequence(pm, b_w, c_w, ab_re, ab_im, as_re, as_im, d_skip)
    o_s = _glu(y_s5.reshape(bp * tp, dh), pm.reshape(bp * tp, -1), col["z_s5"], w_glu_bf, b_glu[l], 512)
    o_r, hs_p = _rwkv_sequence(pm, plo, col, mu_parts, *rw_params)
    y_prompt = _out_proj(o_s.reshape(bp, tp, dh), o_r, pm, col, x_prompt, mod_p, wo1, wo2, final_g, 256)
    g_s5 = A_re.shape[1]
    s5_shape = (1, bp, g_s5, P_S5)
    wkv_p = hs_p[None]

    xs = x_sample.reshape(1, bs, d)
    h_s = _modulated_norm(xs, norm_g[l], mod_s, F32, bs)[0]
    a_s = jnp.concatenate([h_s, state_shift[l]], axis=0).astype(BF16)
    ps = _matmul(a_s, w_main, 2 * bs, 1024)
    pslo = _matmul(a_s, w_lora, 2 * bs, LANES)
    y_s5s, xre_s, xim_s = _s5_step(ps[:bs], b_w, c_w, ab_re, ab_im, d_skip,
                                   state_s5_re[l].reshape(bs, -1), state_s5_im[l].reshape(bs, -1))
    o_ss = _glu(y_s5s, ps[:bs], col["z_s5"], w_glu_bf, b_glu[l], bs)
    o_rs, wkv_s = _rwkv_step(ps, pslo, col, mu_parts, *rw_params, state_wkv[l])
    y_sample = _out_proj(o_ss[None], o_rs[None], ps[:bs][None], col, xs, mod_s, wo1, wo2, final_g, bs)
    y_sample = y_sample.reshape(bs, 1, d)

    return (y_prompt, y_sample,
            xre_p.reshape(s5_shape), xim_p.reshape(s5_shape), wkv_p, shift_p[None],
            xre_s.reshape(1, bs, g_s5, P_S5), xim_s.reshape(1, bs, g_s5, P_S5), wkv_s[None], h_s[None])
```

```python
import functools

import jax
import jax.numpy as jnp
from jax import lax
from jax.experimental import pallas as pl
from jax.experimental.pallas import tpu as pltpu

F32 = jnp.float32
BF16 = jnp.bfloat16

RMS_EPS = 1e-6
GN_EPS = 64e-5
S5_GROUP = 16
P_S5 = 64
RW_HEAD = 64
LORA = 64
LANES = 128
GROUPS_PER_BLOCK = LANES // S5_GROUP
STATES_PER_BLOCK = GROUPS_PER_BLOCK * P_S5
S5_SEG = 16
RW_CHUNK = 64
INV_BASE = 8

_NT = (((1,), (1,)), ((), ()))
_TN = (((0,), (0,)), ((), ()))


def _dot(a, b):
    return jnp.dot(a.astype(BF16), b.astype(BF16), preferred_element_type=F32)


def _sigmoid(x):
    return 1.0 / (1.0 + jnp.exp(-x))


def _silu(x):
    return x * _sigmoid(x)


def _params(*sem):
    return pltpu.CompilerParams(dimension_semantics=sem, vmem_limit_bytes=56 * 1024 * 1024)


def _mod_kernel(c_ref, w_ref, b_ref, o_ref):
    o_ref[...] = _dot(_silu(c_ref[...]), w_ref[...]) + b_ref[...]


def _mod(c, w_bf, b):
    rows, d = c.shape
    n = w_bf.shape[1]
    tn = 1024
    return pl.pallas_call(
        _mod_kernel,
        grid=(n // tn,),
        in_specs=[pl.BlockSpec((rows, d), lambda j: (0, 0)),
                  pl.BlockSpec((d, tn), lambda j: (0, j)),
                  pl.BlockSpec((1, tn), lambda j: (0, j))],
        out_specs=pl.BlockSpec((rows, tn), lambda j: (0, j)),
        out_shape=jax.ShapeDtypeStruct((rows, n), F32),
        compiler_params=_params("parallel"),
        name="adaln_mod",
    )(c, w_bf, b.reshape(1, n))


def _h_kernel(x_ref, g_ref, sh_ref, sc_ref, h_ref):
    x = x_ref[...]
    y = x * lax.rsqrt(jnp.mean(x * x, axis=-1, keepdims=True) + RMS_EPS) * g_ref[...]
    h_ref[...] = (y * (1.0 + sc_ref[...]) + sh_ref[...]).astype(h_ref.dtype)


def _modulated_norm(x, g, mod, out_dtype, tt):
    b, t, d = x.shape
    tt = min(tt, t)
    tm = 1 if mod.shape[1] == 1 else tt
    mod_map = (lambda i, j: (i, 0, 0)) if tm == 1 else (lambda i, j: (i, j, 0))
    mod_map1 = (lambda i, j: (i, 0, 1)) if tm == 1 else (lambda i, j: (i, j, 1))
    return pl.pallas_call(
        _h_kernel,
        grid=(b, t // tt),
        in_specs=[pl.BlockSpec((None, tt, d), lambda i, j: (i, j, 0)),
                  pl.BlockSpec((1, d), lambda i, j: (0, 0)),
                  pl.BlockSpec((None, tm, d), mod_map),
                  pl.BlockSpec((None, tm, d), mod_map1)],
        out_specs=pl.BlockSpec((None, tt, d), lambda i, j: (i, j, 0)),
        out_shape=jax.ShapeDtypeStruct((b, t, d), out_dtype),
        compiler_params=_params("parallel", "parallel"),
        name="modulated_norm",
    )(x, g.reshape(1, d), mod, mod)


def _in_proj_kernel(a_ref, w_ref, o_ref, wbf_ref):
    @pl.when(pl.program_id(1) == 0)
    def _():
        wbf_ref[...] = w_ref[...].astype(BF16)

    o_ref[...] = jnp.dot(a_ref[...], wbf_ref[...], preferred_element_type=F32)


def _in_proj(a, w, col0, width, tn, tm):
    m, k = a.shape
    tm = min(tm, m)
    assert width % tn == 0 and m % tm == 0 and col0 % LANES == 0
    return pl.pallas_call(
        _in_proj_kernel,
        grid=(width // tn, m // tm),
        in_specs=[pl.BlockSpec((tm, k), lambda j, i: (i, 0)),
                  pl.BlockSpec((pl.Element(k), pl.Element(tn)), lambda j, i: (0, pl.multiple_of(col0 + j * tn, LANES)))],
        out_specs=pl.BlockSpec((tm, tn), lambda j, i: (i, j)),
        out_shape=jax.ShapeDtypeStruct((m, width), F32),
        scratch_shapes=[pltpu.VMEM((k, tn), BF16)],
        compiler_params=_params("parallel", "arbitrary"),
        name="in_proj",
    )(a, w)


def _s5_disc_kernel(are_ref, aim_ref, ls_ref, bre_ref, bim_ref,
                    abr_ref, abi_ref, asr_ref, asi_ref, bbr_ref, bbi_ref):
    step = jnp.exp(ls_ref[...])
    lam_re = jnp.minimum(are_ref[...], -1e-4)
    lam_im = aim_ref[...]
    mag = jnp.exp(lam_re * step)
    ab_re = mag * jnp.cos(lam_im * step)
    ab_im = mag * jnp.sin(lam_im * step)
    den = lam_re * lam_re + lam_im * lam_im
    f_re = ((ab_re - 1.0) * lam_re + ab_im * lam_im) / den
    f_im = (ab_im * lam_re - (ab_re - 1.0) * lam_im) / den
    br, bi = bre_ref[...], bim_ref[...]
    bbr_ref[...] = f_re * br - f_im * bi
    bbi_ref[...] = f_re * bi + f_im * br
    abr_ref[...] = ab_re
    abi_ref[...] = ab_im
    pr, pi = ab_re, ab_im
    n = 1
    while n < S5_SEG:
        pr, pi = pr * pr - pi * pi, 2.0 * pr * pi
        n *= 2
    asr_ref[...] = pr
    asi_ref[...] = pi


def _s5_discretise(a_re, a_im, log_step, b_re, b_im):
    g, p = a_re.shape
    c = b_re.shape[-1]
    gp = jax.ShapeDtypeStruct((g, 1, p), F32)
    gcp = jax.ShapeDtypeStruct((g, c, p), F32)
    return pl.pallas_call(
        _s5_disc_kernel,
        out_shape=(gp, gp, gp, gp, gcp, gcp),
        name="s5_discretise",
    )(a_re.reshape(g, 1, p), a_im.reshape(g, 1, p), log_step.reshape(g, 1, 1),
      jnp.swapaxes(b_re, 1, 2), jnp.swapaxes(b_im, 1, 2))


def _s5_block_weights(bbt_re, bbt_im, c_re, c_im):
    g, c, p = bbt_re.shape
    nb = g // GROUPS_PER_BLOCK
    eye = jnp.eye(GROUPS_PER_BLOCK, dtype=F32)

    def b_blk(x):
        x = x.reshape(nb, GROUPS_PER_BLOCK, c, p)
        return jnp.einsum("jacp,ab->jacbp", x, eye).reshape(nb, LANES, STATES_PER_BLOCK)

    def c_blk(x):
        x = x.reshape(nb, GROUPS_PER_BLOCK, c, p)
        return jnp.einsum("jbcp,ab->japbc", x, eye).reshape(nb, STATES_PER_BLOCK, LANES)

    b_w = jnp.concatenate([b_blk(bbt_re), b_blk(bbt_im)], axis=2).astype(BF16)
    c_w = jnp.concatenate([c_blk(c_re), -c_blk(c_im)], axis=1).astype(BF16)
    return b_w, c_w


def _s5_seq_kernel(u_ref, bw_ref, cw_ref, abr_ref, abi_ref, asr_ref, asi_ref, d_ref,
                   y_ref, xre_ref, xim_ref, up_ref, bu_ref, e_ref, cin_ref, *, t_len):
    ns = STATES_PER_BLOCK
    n_lt = ns // LANES
    nseg = t_len // S5_SEG

    def seg_rows(tl):
        return pl.ds(tl * nseg, nseg)

    for tl in range(S5_SEG):
        up_ref[seg_rows(tl), :] = u_ref[pl.ds(tl, nseg, stride=S5_SEG), :]
    u = up_ref[...]
    bu = _dot(u, bw_ref[...])
    for kt in range(2 * n_lt):
        bu_ref[kt] = bu[:, kt * LANES:(kt + 1) * LANES]

    def scan_tiles(init_from_carry, store_states):
        for lt in range(n_lt):
            re_l = slice(lt * LANES, (lt + 1) * LANES)
            im_l = slice(ns + lt * LANES, ns + (lt + 1) * LANES)
            bre, bim = bu_ref.at[lt], bu_ref.at[n_lt + lt]
            ar = abr_ref[:, re_l]
            ai = abi_ref[:, re_l]
            if init_from_carry:
                xr = cin_ref[:, re_l]
                xi = cin_ref[:, im_l]
            else:
                xr = jnp.zeros((nseg, LANES), F32)
                xi = jnp.zeros((nseg, LANES), F32)
            for tl in range(S5_SEG):
                br = bre[seg_rows(tl), :]
                bi = bim[seg_rows(tl), :]
                xr, xi = ar * xr - ai * xi + br, ar * xi + ai * xr + bi
                if store_states:
                    bre[seg_rows(tl), :] = xr
                    bim[seg_rows(tl), :] = xi
            if not store_states:
                e_ref[:, re_l] = xr
                e_ref[:, im_l] = xi

    scan_tiles(False, False)

    asr = asr_ref[...]
    asi = asi_ref[...]

    def seg_step(s, carry):
        cr, ci = carry
        cin_ref[pl.ds(s, 1), :] = jnp.concatenate([cr, ci], axis=1)
        e = e_ref[pl.ds(s, 1), :]
        er, ei = e[:, :ns], e[:, ns:]
        return asr * cr - asi * ci + er, asr * ci + asi * cr + ei

    zero = jnp.zeros((1, ns), F32)
    fr, fi = lax.fori_loop(0, nseg, seg_step, (zero, zero))
    xre_ref[...] = fr
    xim_ref[...] = fi

    scan_tiles(True, True)

    y = d_ref[...] * u
    for kt in range(2 * n_lt):
        y = y + _dot(bu_ref[kt], cw_ref[kt * LANES:(kt + 1) * LANES, :])
    up_ref[...] = y
    for tl in range(S5_SEG):
        y_ref[pl.ds(tl, nseg, stride=S5_SEG), :] = up_ref[seg_rows(tl), :]


def _s5_sequence(p_main, b_w, c_w, ab_re, ab_im, as_re, as_im, d_skip):
    b, t, _ = p_main.shape
    nb = b_w.shape[0]
    ns = STATES_PER_BLOCK
    vec = lambda: pl.BlockSpec((1, ns), lambda i, j: (0, j))
    y, xre, xim = pl.pallas_call(
        functools.partial(_s5_seq_kernel, t_len=t),
        grid=(b, nb),
        in_specs=[pl.BlockSpec((None, t, LANES), lambda i, j: (i, 0, j)),
                  pl.BlockSpec((None, LANES, 2 * ns), lambda i, j: (j, 0, 0)),
                  pl.BlockSpec((None, 2 * ns, LANES), lambda i, j: (j, 0, 0)),
                  vec(), vec(), vec(), vec(),
                  pl.BlockSpec((1, LANES), lambda i, j: (0, j))],
        out_specs=[pl.BlockSpec((None, t, LANES), lambda i, j: (i, 0, j)),
                   pl.BlockSpec((None, 1, ns), lambda i, j: (i, 0, j)),
                   pl.BlockSpec((None, 1, ns), lambda i, j: (i, 0, j))],
        out_shape=(jax.ShapeDtypeStruct((b, t, nb * LANES), F32),
                   jax.ShapeDtypeStruct((b, 1, nb * ns), F32),
                   jax.ShapeDtypeStruct((b, 1, nb * ns), F32)),
        scratch_shapes=[pltpu.VMEM((t, LANES), F32),
                        pltpu.VMEM((2 * ns // LANES, t, LANES), F32),
                        pltpu.VMEM((t // S5_SEG, 2 * ns), F32),
                        pltpu.VMEM((t // S5_SEG, 2 * ns), F32)],
        compiler_params=_params("parallel", "parallel"),
        name="s5_sequence",
    )(p_main, b_w, c_w, ab_re, ab_im, as_re, as_im, d_skip)
    return y, xre, xim


def _s5_step_kernel(u_ref, bw_ref, cw_ref, abr_ref, abi_ref, d_ref, x0r_ref, x0i_ref,
                    y_ref, x1r_ref, x1i_ref):
    ns = STATES_PER_BLOCK
    u = u_ref[...]
    bu = _dot(u, bw_ref[...])
    ar, ai = abr_ref[...], abi_ref[...]
    x0r, x0i = x0r_ref[...], x0i_ref[...]
    xr = ar * x0r - ai * x0i + bu[:, :ns]
    xi = ar * x0i + ai * x0r + bu[:, ns:]
    x1r_ref[...] = xr
    x1i_ref[...] = xi
    y_ref[...] = _dot(jnp.concatenate([xr, xi], axis=1), cw_ref[...]) + d_ref[...] * u


def _s5_step(p_rows, b_w, c_w, ab_re, ab_im, d_skip, x0_re, x0_im):
    rows = p_rows.shape[0]
    nb = b_w.shape[0]
    ns = STATES_PER_BLOCK
    vec = lambda: pl.BlockSpec((1, ns), lambda j: (0, j))
    st = lambda: pl.BlockSpec((rows, ns), lambda j: (0, j))
    return pl.pallas_call(
        _s5_step_kernel,
        grid=(nb,),
        in_specs=[pl.BlockSpec((rows, LANES), lambda j: (0, j)),
                  pl.BlockSpec((None, LANES, 2 * ns), lambda j: (j, 0, 0)),
                  pl.BlockSpec((None, 2 * ns, LANES), lambda j: (j, 0, 0)),
                  vec(), vec(),
                  pl.BlockSpec((1, LANES), lambda j: (0, j)),
                  st(), st()],
        out_specs=[pl.BlockSpec((rows, LANES), lambda j: (0, j)), st(), st()],
        out_shape=(jax.ShapeDtypeStruct((rows, nb * LANES), F32),
                   jax.ShapeDtypeStruct((rows, nb * ns), F32),
                   jax.ShapeDtypeStruct((rows, nb * ns), F32)),
        compiler_params=_params("parallel"),
        name="s5_step",
    )(p_rows, b_w, c_w, ab_re, ab_im, d_skip, x0_re, x0_im)


def _glu_kernel(y_ref, z_ref, w_ref, b_ref, o_ref):
    y = jax.nn.gelu(y_ref[...], approximate=True)
    gate = _sigmoid(_dot(y, w_ref[...]) + b_ref[...])
    o_ref[...] = (y * gate * _silu(z_ref[...])).astype(o_ref.dtype)


def _glu(y, p_rows, z_block, w_bf, b, tm):
    rows, d = y.shape
    tm = min(tm, rows)
    return pl.pallas_call(
        _glu_kernel,
        grid=(rows // tm,),
        in_specs=[pl.BlockSpec((tm, d), lambda i: (i, 0)),
                  pl.BlockSpec((tm, d), lambda i: (i, z_block)),
                  pl.BlockSpec((d, d), lambda i: (0, 0)),
                  pl.BlockSpec((1, d), lambda i: (0, 0))],
        out_specs=pl.BlockSpec((tm, d), lambda i: (i, 0)),
        out_shape=jax.ShapeDtypeStruct((rows, d), BF16),
        compiler_params=_params("parallel"),
        name="s5_glu",
    )(y, p_rows, w_bf, b.reshape(1, d))


def _softplus(x):
    return jnp.maximum(x, 0.0) + jnp.log(1.0 + jnp.exp(-jnp.abs(x)))


def _rwkv_token_terms(r, k, lo, w0, w2p, a0, a2p, k_a):
    w = -_softplus(-(w0 + _dot(jnp.tanh(lo), w2p))) - 0.5
    logd = -jnp.exp(w)
    a = _sigmoid(a0 + _dot(lo, a2p))
    k2 = k * (1.0 + (a - 1.0) * k_a)
    return logd, a, k2


def _rwkv_chunk_kernel(r_ref, k_ref, v_ref, lo_ref, z_ref,
                       mur_ref, muk_ref, muv_ref, mulo_ref,
                       w0_ref, w2p_ref, a0_ref, a2p_ref, kk_ref, ka_ref, rk_ref, gw_ref, gb_ref,
                       o_ref, hs_ref,
                       h_scr, pr_scr, pk_scr, pv_scr, plo_scr, *, n_heads):
    L = RW_CHUNK
    W = 2 * RW_HEAD
    n_pairs = n_heads // 2
    c = pl.program_id(1)

    @pl.when(c == 0)
    def _():
        h_scr[...] = jnp.zeros_like(h_scr)
        pr_scr[...] = jnp.zeros_like(pr_scr)
        pk_scr[...] = jnp.zeros_like(pk_scr)
        pv_scr[...] = jnp.zeros_like(pv_scr)
        plo_scr[...] = jnp.zeros_like(plo_scr)

    row1 = lax.broadcasted_iota(jnp.int32, (L, 1), 0)

    def token_shift(cur_ref, prev_scr, mu_ref):
        cur = cur_ref[...]
        prev = jnp.where(row1 == 0, prev_scr[...], pltpu.roll(cur, 1, 0))
        prev_scr[...] = cur[L - 1:L, :]
        return cur + (prev - cur) * mu_ref[...]

    r = token_shift(r_ref, pr_scr, mur_ref)
    k = token_shift(k_ref, pk_scr, muk_ref)
    v = token_shift(v_ref, pv_scr, muv_ref)
    lo = token_shift(lo_ref, plo_scr, mulo_ref)
    logd, a, k2 = _rwkv_token_terms(r, k, lo, w0_ref[...], w2p_ref[...], a0_ref[...], a2p_ref[...],
                                    ka_ref[...])
    kk = k * kk_ref[...]

    tri = (lax.broadcasted_iota(jnp.int32, (L, L), 0)
           >= lax.broadcasted_iota(jnp.int32, (L, L), 1)).astype(BF16)
    hi = logd.astype(BF16)
    rem = logd - hi.astype(F32)
    mid = rem.astype(BF16)
    low = (rem - mid.astype(F32)).astype(BF16)
    lp = (jnp.dot(tri, hi, preferred_element_type=F32)
          + jnp.dot(tri, mid, preferred_element_type=F32)
          + jnp.dot(tri, low, preferred_element_type=F32))
    p_inc = jnp.exp(lp)
    p_exc = jnp.exp(lp - logd)
    p_inv = jnp.exp(-lp)

    first_head = lax.broadcasted_iota(jnp.int32, (L, W), 1) < RW_HEAD

    def head_sum(x):
        s0 = jnp.sum(jnp.where(first_head, x, 0.0), axis=-1, keepdims=True)
        s1 = jnp.sum(jnp.where(first_head, 0.0, x), axis=-1, keepdims=True)
        return jnp.where(first_head, s0, s1)

    def stack_heads(x):
        return jnp.concatenate([jnp.where(first_head, x, 0.0), jnp.where(first_head, 0.0, x)], axis=0)

    ri = lax.broadcasted_iota(jnp.int32, (2 * L, 2 * L), 0)
    ci = lax.broadcasted_iota(jnp.int32, (2 * L, 2 * L), 1)
    t_row, t_col = ri & (L - 1), ci & (L - 1)
    strict = t_row > t_col
    incl = t_row >= t_col
    eye = (ri == ci).astype(F32)
    blk_masks = []
    s = INV_BASE
    while s <= L:
        blk_masks.append((ri // s) == (ci // s))
        s *= 2

    pairs = range(n_pairs)
    lanes = [slice(p * W, (p + 1) * W) for p in pairs]

    lhs, nm, mm, ab, vs, kb, p_end = [], [], [], [], [], [], []
    for sl in lanes:
        kk_p = kk[:, sl]
        kkn = kk_p / jnp.maximum(jnp.sqrt(head_sum(kk_p * kk_p)), 1e-12)
        pinc, pinv = p_inc[:, sl], p_inv[:, sl]
        kd = k2[:, sl] * pinv
        bd = kkn * a[:, sl] * pinv
        pe = pinc[L - 1:L, :]
        lhs_p = jnp.concatenate([stack_heads(kkn * p_exc[:, sl]), stack_heads(r[:, sl] * pinc)],
                                axis=0).astype(BF16)
        rhs_p = jnp.concatenate([stack_heads(bd), stack_heads(kd)], axis=0).astype(BF16)
        amat = lax.dot_general(lhs_p, rhs_p, _NT, preferred_element_type=F32)
        lhs.append(lhs_p)
        nm.append(jnp.where(strict, amat[:2 * L, :2 * L], 0.0))
        mm.append(jnp.where(strict, amat[:2 * L, 2 * L:], 0.0).astype(BF16))
        ab.append(jnp.concatenate([jnp.where(incl, amat[2 * L:, 2 * L:], 0.0),
                                   -jnp.where(incl, amat[2 * L:, :2 * L], 0.0)], axis=1).astype(BF16))
        vs.append(stack_heads(v[:, sl]).astype(BF16))
        kb.append(jnp.concatenate([stack_heads(kd * pe), stack_heads(bd * pe)], axis=0).astype(BF16))
        p_end.append(pe)

    d = [jnp.where(blk_masks[0], n_p, 0.0).astype(BF16) for n_p in nm]
    x = [eye - d_p.astype(F32) for d_p in d]
    pw = [jnp.dot(d_p, d_p, preferred_element_type=F32) for d_p in d]
    s = 2
    while s < INV_BASE:
        x = [x_p + _dot(x_p, pw_p) for x_p, pw_p in zip(x, pw)]
        s *= 2
        if s < INV_BASE:
            pw = [_dot(pw_p, pw_p) for pw_p in pw]
    for lvl in range(1, len(blk_masks)):
        off = blk_masks[lvl] & ~blk_masks[lvl - 1]
        xc = [_dot(x_p, jnp.where(off, n_p, 0.0)) for x_p, n_p in zip(x, nm)]
        x = [x_p - _dot(xc_p, x_p) for x_p, xc_p in zip(x, xc)]

    hs = [h_scr[p] for p in pairs]
    lh = [jnp.dot(lhs_p, hs_p.astype(BF16), preferred_element_type=F32) for lhs_p, hs_p in zip(lhs, hs)]
    mv = [jnp.dot(mm_p, vs_p, preferred_element_type=F32) for mm_p, vs_p in zip(mm, vs)]
    u = [_dot(x_p, lh_p[:2 * L] + mv_p).astype(BF16) for x_p, lh_p, mv_p in zip(x, lh, mv)]
    o_st = [lh_p[2 * L:] + jnp.dot(ab_p, jnp.concatenate([vs_p, u_p], axis=0), preferred_element_type=F32)
            for lh_p, ab_p, vs_p, u_p in zip(lh, ab, vs, u)]
    for p in pairs:
        p_end_col = jnp.sum(eye * p_end[p], axis=1, keepdims=True)
        h_scr[p] = p_end_col * hs[p] + lax.dot_general(
            kb[p], jnp.concatenate([vs[p], -u[p]], axis=0), _TN, preferred_element_type=F32)

    for p, sl in enumerate(lanes):
        o = o_st[p][:L] + o_st[p][L:]
        mu = head_sum(o) * (1.0 / RW_HEAD)
        var = head_sum((o - mu) ** 2) * (1.0 / RW_HEAD)
        o = (o - mu) * lax.rsqrt(var + GN_EPS) * gw_ref[:, sl] + gb_ref[:, sl]
        o = o + head_sum(r[:, sl] * k2[:, sl] * rk_ref[:, sl]) * v[:, sl]
        z = z_ref[:, LANES + p * W:LANES + (p + 1) * W]
        o_ref[:, sl] = (o * _silu(z)).astype(o_ref.dtype)

    @pl.when(c == pl.num_programs(1) - 1)
    def _():
        hs_ref[...] = h_scr[...]


def _rwkv_sequence(p_main, p_lz, col, mu, w0, w2p, a0, a2p, k_k, k_a, r_k, gn_w, gn_b):
    b, t, _ = p_main.shape
    wlz = p_lz.shape[-1]
    d = w0.shape[-1]
    n_heads = d // RW_HEAD
    n_pairs, pw = n_heads // 2, 2 * RW_HEAD
    L = RW_CHUNK
    blk = lambda cb: pl.BlockSpec((None, L, d), lambda i, j, cb=cb: (i, j, cb))
    vec = lambda n: pl.BlockSpec((1, n), lambda i, j: (0, 0))
    full = lambda shp: pl.BlockSpec(shp, lambda i, j: (0,) * len(shp))
    mu_r, mu_k, mu_v, mu_lo = mu
    o, hs = pl.pallas_call(
        functools.partial(_rwkv_chunk_kernel, n_heads=n_heads),
        grid=(b, t // L),
        in_specs=[blk(col["r"]), blk(col["k"]), blk(col["v"]),
                  pl.BlockSpec((None, L, LANES), lambda i, j: (i, j, 0)),
                  pl.BlockSpec((None, L, wlz), lambda i, j: (i, j, 0)),
                  vec(d), vec(d), vec(d), vec(LANES),
                  vec(d), full((LANES, d)), vec(d), full((LANES, d)),
                  vec(d), vec(d), vec(d), vec(d), vec(d)],
        out_specs=[pl.BlockSpec((None, L, d), lambda i, j: (i, j, 0)),
                   pl.BlockSpec((None, n_pairs, pw, pw), lambda i, j: (i, 0, 0, 0))],
        out_shape=(jax.ShapeDtypeStruct((b, t, d), BF16),
                   jax.ShapeDtypeStruct((b, n_pairs, pw, pw), F32)),
        scratch_shapes=[pltpu.VMEM((n_pairs, pw, pw), F32),
                        pltpu.VMEM((1, d), F32), pltpu.VMEM((1, d), F32), pltpu.VMEM((1, d), F32),
                        pltpu.VMEM((1, LANES), F32)],
        compiler_params=_params("parallel", "arbitrary"),
        name="rwkv_sequence",
    )(p_main, p_main, p_main, p_lz, p_lz,
      mu_r, mu_k, mu_v, mu_lo, w0, w2p, a0, a2p, k_k, k_a, r_k, gn_w, gn_b)
    hs = hs.reshape(b, n_pairs, 2, RW_HEAD, 2, RW_HEAD)
    hs = jnp.stack([hs[:, :, 0, :, 0, :], hs[:, :, 1, :, 1, :]], axis=2)
    return o, jnp.swapaxes(hs.reshape(b, n_heads, RW_HEAD, RW_HEAD), -1, -2)


def _rwkv_step_prep_kernel(cr_ref, ck_ref, cv_ref, clo_ref, cz_ref, pr_ref, pk_ref, pv_ref, plo_ref,
                           mur_ref, muk_ref, muv_ref, mulo_ref,
                           w0_ref, w2p_ref, a0_ref, a2p_ref, kk_ref, ka_ref,
                           r_o, k2_o, v_o, kk_o, a_o, d_o, z_o):
    def lerp(c_ref, p_ref, mu_ref):
        cur = c_ref[...]
        return cur + (p_ref[...] - cur) * mu_ref[...]

    r = lerp(cr_ref, pr_ref, mur_ref)
    k = lerp(ck_ref, pk_ref, muk_ref)
    v = lerp(cv_ref, pv_ref, muv_ref)
    lo = lerp(clo_ref, plo_ref, mulo_ref)
    logd, a, k2 = _rwkv_token_terms(r, k, lo, w0_ref[...], w2p_ref[...], a0_ref[...], a2p_ref[...],
                                    ka_ref[...])
    r_o[...] = r.T
    k2_o[...] = k2.T
    v_o[...] = v.T
    kk_o[...] = (k * kk_ref[...]).T
    a_o[...] = a.T
    d_o[...] = jnp.exp(logd).T
    z_o[...] = cz_ref[:, LANES:].T


def _rwkv_step_kernel(s_ref, r_ref, k2_ref, v_ref, kk_ref, a_ref, d_ref, z_ref,
                      rk_ref, gw_ref, gb_ref, o_ref, s1_ref, o_scr):
    r, k2, v, dec = r_ref[...], k2_ref[...], v_ref[...], d_ref[...]
    kk = kk_ref[...]
    kkn = kk / jnp.maximum(jnp.sqrt(jnp.sum(kk * kk, axis=0, keepdims=True)), 1e-12)
    bvec = kkn * a_ref[...]
    for i in range(RW_HEAD):
        s = s_ref[i]
        sa = jnp.sum(s * kkn, axis=0, keepdims=True)
        s1 = s * dec - sa * bvec + v[i:i + 1, :] * k2
        s1_ref[i] = s1
        o_scr[pl.ds(i, 1), :] = jnp.sum(s1 * r, axis=0, keepdims=True)
    o = o_scr[...]
    mu = jnp.mean(o, axis=0, keepdims=True)
    var = jnp.mean((o - mu) ** 2, axis=0, keepdims=True)
    o = (o - mu) * lax.rsqrt(var + GN_EPS) * gw_ref[...] + gb_ref[...]
    o = o + jnp.sum(r * k2 * rk_ref[...], axis=0, keepdims=True) * v
    o_ref[...] = o * _silu(z_ref[...])


def _rwkv_step(p_rows, p_lz_rows, col, mu, w0, w2p, a0, a2p, k_k, k_a, r_k, gn_w, gn_b, s0):
    rows = s0.shape[0]
    wlz = p_lz_rows.shape[-1]
    d = w0.shape[-1]
    n_heads = d // RW_HEAD
    cur = lambda cb: pl.BlockSpec((rows, d), lambda i, cb=cb: (0, cb))
    prv = lambda cb: pl.BlockSpec((rows, d), lambda i, cb=cb: (1, cb))
    vec = lambda n: pl.BlockSpec((1, n), lambda i: (0, 0))
    mu_r, mu_k, mu_v, mu_lo = mu
    out = jax.ShapeDtypeStruct((d, rows), F32)
    terms = pl.pallas_call(
        _rwkv_step_prep_kernel,
        grid=(1,),
        in_specs=[cur(col["r"]), cur(col["k"]), cur(col["v"]),
                  pl.BlockSpec((rows, LANES), lambda i: (0, 0)),
                  pl.BlockSpec((rows, wlz), lambda i: (0, 0)),
                  prv(col["r"]), prv(col["k"]), prv(col["v"]),
                  pl.BlockSpec((rows, LANES), lambda i: (1, 0)),
                  vec(d), vec(d), vec(d), vec(LANES),
                  vec(d), pl.BlockSpec((LANES, d), lambda i: (0, 0)),
                  vec(d), pl.BlockSpec((LANES, d), lambda i: (0, 0)),
                  vec(d), vec(d)],
        out_specs=[pl.BlockSpec((d, rows), lambda i: (0, 0))] * 7,
        out_shape=(out,) * 7,
        compiler_params=_params("arbitrary"),
        name="rwkv_step_prep",
    )(p_rows, p_rows, p_rows, p_lz_rows, p_lz_rows, p_rows, p_rows, p_rows, p_lz_rows,
      mu_r, mu_k, mu_v, mu_lo, w0, w2p, a0, a2p, k_k, k_a)
    per_h = lambda: pl.BlockSpec((RW_HEAD, rows), lambda h: (h, 0))
    par = lambda: pl.BlockSpec((RW_HEAD, 1), lambda h: (h, 0))
    st = lambda: pl.BlockSpec((None, RW_HEAD, RW_HEAD, rows), lambda h: (h, 0, 0, 0))
    o_t, s1_t = pl.pallas_call(
        _rwkv_step_kernel,
        grid=(n_heads,),
        in_specs=[st()] + [per_h()] * 7 + [par()] * 3,
        out_specs=[per_h(), st()],
        out_shape=(jax.ShapeDtypeStruct((d, rows), F32),
                   jax.ShapeDtypeStruct((n_heads, RW_HEAD, RW_HEAD, rows), F32)),
        scratch_shapes=[pltpu.VMEM((RW_HEAD, rows), F32)],
        compiler_params=_params("parallel"),
        name="rwkv_step",
    )(jnp.transpose(s0, (1, 2, 3, 0)), *terms,
      r_k.reshape(d, 1), gn_w.reshape(d, 1), gn_b.reshape(d, 1))
    return o_t.T.astype(BF16), jnp.transpose(s1_t, (3, 0, 1, 2))


def _out_kernel(os_ref, or_ref, gs_ref, gr_ref, x_ref, gt_ref, w1_ref, w2_ref, fg_ref, y_ref):
    mixed = (_sigmoid(gs_ref[...]) * jnp.dot(os_ref[...], w1_ref[...], preferred_element_type=F32)
             + _sigmoid(gr_ref[...]) * jnp.dot(or_ref[...], w2_ref[...], preferred_element_type=F32))
    x = x_ref[...] + gt_ref[...] * mixed
    y_ref[...] = x * lax.rsqrt(jnp.mean(x * x, axis=-1, keepdims=True) + RMS_EPS) * fg_ref[...]


def _out_proj(o_s, o_r, p_main, col, x, mod, w1, w2, final_g, tt):
    b, t, d = x.shape
    dh = o_s.shape[-1]
    tt = min(tt, t)
    tm = 1 if mod.shape[1] == 1 else tt
    gt_map = (lambda i, j: (i, 0, 2)) if tm == 1 else (lambda i, j: (i, j, 2))
    return pl.pallas_call(
        _out_kernel,
        grid=(b, t // tt),
        in_specs=[pl.BlockSpec((None, tt, dh), lambda i, j: (i, j, 0)),
                  pl.BlockSpec((None, tt, dh), lambda i, j: (i, j, 0)),
                  pl.BlockSpec((None, tt, d), lambda i, j: (i, j, col["g_s5"])),
                  pl.BlockSpec((None, tt, d), lambda i, j: (i, j, col["g_rw"])),
                  pl.BlockSpec((None, tt, d), lambda i, j: (i, j, 0)),
                  pl.BlockSpec((None, tm, d), gt_map),
                  pl.BlockSpec((dh, d), lambda i, j: (0, 0)),
                  pl.BlockSpec((dh, d), lambda i, j: (0, 0)),
                  pl.BlockSpec((1, d), lambda i, j: (0, 0))],
        out_specs=pl.BlockSpec((None, tt, d), lambda i, j: (i, j, 0)),
        out_shape=jax.ShapeDtypeStruct((b, t, d), F32),
        compiler_params=_params("parallel", "parallel"),
        name="out_proj",
    )(o_s, o_r, p_main, p_main, x, mod, w1, w2, final_g.reshape(1, d))


def kernel(x_prompt, x_sample, c_prompt, c_sample, state_s5_re, state_s5_im, state_wkv, state_shift, norm_g, w_ada, b_ada, w_in, mu_rw, A_re, A_im, log_step, B_re, B_im, C_re, C_im, D_skip, w_glu, b_glu, w0, w2, a0, a2, k_k, k_a, r_k, gn_w, gn_b, w_out, final_g):
    depth = norm_g.shape[0]
    assert depth == 1
    bp, tp, d = x_prompt.shape
    bs = x_sample.shape[0]
    assert x_sample.shape[1] == 1
    dh = d // 2
    l = 0

    w_main, w_lz, w_gate = 5 * dh, 2 * LORA + dh, 2 * d
    col = {"u": 0, "z_s5": 1, "r": 2, "k": 3, "v": 4, "g_s5": 0, "g_rw": 1}
    w = w_in[l]

    def project(rows, tm):
        return (_in_proj(rows, w, 0, w_main, w_main // 4, tm),
                _in_proj(rows, w, w_main, w_lz, w_lz, tm),
                _in_proj(rows, w, w_main + w_lz, w_gate, w_gate // 4, tm))

    mu = mu_rw[l]
    mu_parts = (mu[None, :dh], mu[None, dh:2 * dh], mu[None, 2 * dh:3 * dh], mu[None, 3 * dh:])
    zpad = jnp.zeros((LORA, dh), F32)
    w2p = jnp.concatenate([w2[l], zpad], axis=0).astype(BF16)
    a2p = jnp.concatenate([zpad, a2[l]], axis=0).astype(BF16)
    row = lambda x: x.reshape(1, -1)
    rw_params = (row(w0[l]), w2p, row(a0[l]), a2p, row(k_k[l]), row(k_a[l]), row(r_k[l]),
                 row(gn_w[l]), row(gn_b[l]))
    w_out_bf = w_out[l].astype(BF16)
    wo1, wo2 = w_out_bf[:dh], w_out_bf[dh:]
    w_glu_bf = w_glu[l].astype(BF16)

    ab_re, ab_im, as_re, as_im, bbt_re, bbt_im = _s5_discretise(A_re[l], A_im[l], log_step[l], B_re[l], B_im[l])
    b_w, c_w = _s5_block_weights(bbt_re, bbt_im, C_re[l], C_im[l])
    flat = lambda x: x.reshape(1, -1)
    ab_re, ab_im, as_re, as_im = flat(ab_re), flat(ab_im), flat(as_re), flat(as_im)
    d_skip = flat(D_skip[l])

    mod = _mod(jnp.concatenate([c_prompt, c_sample], axis=0), w_ada[l], b_ada[l])
    mod_p = mod[:bp].reshape(bp, 1, 3 * d)
    mod_s = mod[bp:].reshape(1, bs, 3 * d)

    h_p = _modulated_norm(x_prompt, norm_g[l], mod_p, BF16, 512)
    shift_p = _modulated_norm(x_prompt[:, tp - 1:, :], norm_g[l], mod_p, F32, 1)[:, 0]
    a_p = h_p.reshape(bp * tp, d)
    pm, plz, pg = project(a_p, 1024)
    pm3 = pm.reshape(bp, tp, -1)
    y_s5, xre_p, xim_p = _s5_sequence(pm3, b_w, c_w, ab_re, ab_im, as_re, as_im, d_skip)
    o_s = _glu(y_s5.reshape(bp * tp, dh), pm, col["z_s5"], w_glu_bf, b_glu[l], 512)
    o_r, hs_p = _rwkv_sequence(pm3, plz.reshape(bp, tp, -1), col, mu_parts, *rw_params)
    y_prompt = _out_proj(o_s.reshape(bp, tp, dh), o_r, pg.reshape(bp, tp, -1), col, x_prompt, mod_p,
                         wo1, wo2, final_g, 256)
    g_s5 = A_re.shape[1]
    s5_shape = (1, bp, g_s5, P_S5)
    wkv_p = hs_p[None]

    xs = x_sample.reshape(1, bs, d)
    h_s = _modulated_norm(xs, norm_g[l], mod_s, F32, bs)[0]
    a_s = jnp.concatenate([h_s, state_shift[l]], axis=0).astype(BF16)
    ps, pslz, psg = project(a_s, 2 * bs)
    y_s5s, xre_s, xim_s = _s5_step(ps[:bs], b_w, c_w, ab_re, ab_im, d_skip,
                                   state_s5_re[l].reshape(bs, -1), state_s5_im[l].reshape(bs, -1))
    o_ss = _glu(y_s5s, ps[:bs], col["z_s5"], w_glu_bf, b_glu[l], bs)
    o_rs, wkv_s = _rwkv_step(ps, pslz, col, mu_parts, *rw_params, state_wkv[l])
    y_sample = _out_proj(o_ss[None], o_rs[None], psg[:bs][None], col, xs, mod_s, wo1, wo2, final_g, bs)
    y_sample = y_sample.reshape(bs, 1, d)

    return (y_prompt, y_sample,
            xre_p.reshape(s5_shape), xim_p.reshape(s5_shape), wkv_p, shift_p[None],
            xre_s.reshape(1, bs, g_s5, P_S5), xim_s.reshape(1, bs, g_s5, P_S5), wkv_s[None], h_s[None])
```

```python
import functools

import jax
import jax.numpy as jnp
from jax import lax
from jax.experimental import pallas as pl
from jax.experimental.pallas import tpu as pltpu

F32 = jnp.float32
BF16 = jnp.bfloat16

RMS_EPS = 1e-6
GN_EPS = 64e-5
S5_GROUP = 16
P_S5 = 64
RW_HEAD = 64
LORA = 64
LANES = 128
GROUPS_PER_BLOCK = LANES // S5_GROUP
STATES_PER_BLOCK = GROUPS_PER_BLOCK * P_S5
S5_SEG = 16
RW_CHUNK = 64
INV_BASE = 8

_NT = (((1,), (1,)), ((), ()))
_TN = (((0,), (0,)), ((), ()))


def _dot(a, b):
    return jnp.dot(a.astype(BF16), b.astype(BF16), preferred_element_type=F32)


def _sigmoid(x):
    return 1.0 / (1.0 + jnp.exp(-x))


def _silu(x):
    return x * _sigmoid(x)


def _params(*sem):
    return pltpu.CompilerParams(dimension_semantics=sem, vmem_limit_bytes=56 * 1024 * 1024)


def _mod_kernel(c_ref, w_ref, b_ref, o_ref):
    o_ref[...] = _dot(_silu(c_ref[...]), w_ref[...]) + b_ref[...]


def _mod(c, w_bf, b):
    rows, d = c.shape
    n = w_bf.shape[1]
    tn = 1024
    return pl.pallas_call(
        _mod_kernel,
        grid=(n // tn,),
        in_specs=[pl.BlockSpec((rows, d), lambda j: (0, 0)),
                  pl.BlockSpec((d, tn), lambda j: (0, j)),
                  pl.BlockSpec((1, tn), lambda j: (0, j))],
        out_specs=pl.BlockSpec((rows, tn), lambda j: (0, j)),
        out_shape=jax.ShapeDtypeStruct((rows, n), F32),
        compiler_params=_params("parallel"),
        name="adaln_mod",
    )(c, w_bf, b.reshape(1, n))


def _h_kernel(x_ref, g_ref, sh_ref, sc_ref, h_ref):
    x = x_ref[...]
    y = x * lax.rsqrt(jnp.mean(x * x, axis=-1, keepdims=True) + RMS_EPS) * g_ref[...]
    h_ref[...] = (y * (1.0 + sc_ref[...]) + sh_ref[...]).astype(h_ref.dtype)


def _modulated_norm(x, g, mod, out_dtype, tt):
    b, t, d = x.shape
    tt = min(tt, t)
    tm = 1 if mod.shape[1] == 1 else tt
    mod_map = (lambda i, j: (i, 0, 0)) if tm == 1 else (lambda i, j: (i, j, 0))
    mod_map1 = (lambda i, j: (i, 0, 1)) if tm == 1 else (lambda i, j: (i, j, 1))
    return pl.pallas_call(
        _h_kernel,
        grid=(b, t // tt),
        in_specs=[pl.BlockSpec((None, tt, d), lambda i, j: (i, j, 0)),
                  pl.BlockSpec((1, d), lambda i, j: (0, 0)),
                  pl.BlockSpec((None, tm, d), mod_map),
                  pl.BlockSpec((None, tm, d), mod_map1)],
        out_specs=pl.BlockSpec((None, tt, d), lambda i, j: (i, j, 0)),
        out_shape=jax.ShapeDtypeStruct((b, t, d), out_dtype),
        compiler_params=_params("parallel", "parallel"),
        name="modulated_norm",
    )(x, g.reshape(1, d), mod, mod)


def _in_proj_kernel(a_ref, w_ref, o_ref, wbf_ref, *, gate):
    @pl.when(pl.program_id(1) == 0)
    def _():
        wbf_ref[...] = w_ref[...].astype(BF16)

    p = jnp.dot(a_ref[...], wbf_ref[...], preferred_element_type=F32)
    o_ref[...] = (_sigmoid(p) if gate else p).astype(o_ref.dtype)


def _in_proj(a, w, col0, width, tn, tm, gate=False):
    m, k = a.shape
    tm = min(tm, m)
    assert width % tn == 0 and m % tm == 0 and col0 % LANES == 0
    return pl.pallas_call(
        functools.partial(_in_proj_kernel, gate=gate),
        grid=(width // tn, m // tm),
        in_specs=[pl.BlockSpec((tm, k), lambda j, i: (i, 0)),
                  pl.BlockSpec((pl.Element(k), pl.Element(tn)), lambda j, i: (0, pl.multiple_of(col0 + j * tn, LANES)))],
        out_specs=pl.BlockSpec((tm, tn), lambda j, i: (i, j)),
        out_shape=jax.ShapeDtypeStruct((m, width), BF16 if gate else F32),
        scratch_shapes=[pltpu.VMEM((k, tn), BF16)],
        compiler_params=_params("parallel", "arbitrary"),
        name="in_proj",
    )(a, w)


def _s5_disc_kernel(are_ref, aim_ref, ls_ref, bre_ref, bim_ref,
                    abr_ref, abi_ref, asr_ref, asi_ref, bbr_ref, bbi_ref):
    step = jnp.exp(ls_ref[...])
    lam_re = jnp.minimum(are_ref[...], -1e-4)
    lam_im = aim_ref[...]
    mag = jnp.exp(lam_re * step)
    ab_re = mag * jnp.cos(lam_im * step)
    ab_im = mag * jnp.sin(lam_im * step)
    den = lam_re * lam_re + lam_im * lam_im
    f_re = ((ab_re - 1.0) * lam_re + ab_im * lam_im) / den
    f_im = (ab_im * lam_re - (ab_re - 1.0) * lam_im) / den
    br, bi = bre_ref[...], bim_ref[...]
    bbr_ref[...] = f_re * br - f_im * bi
    bbi_ref[...] = f_re * bi + f_im * br
    abr_ref[...] = ab_re
    abi_ref[...] = ab_im
    pr, pi = ab_re, ab_im
    n = 1
    while n < S5_SEG:
        pr, pi = pr * pr - pi * pi, 2.0 * pr * pi
        n *= 2
    asr_ref[...] = pr
    asi_ref[...] = pi


def _s5_discretise(a_re, a_im, log_step, b_re, b_im):
    g, p = a_re.shape
    c = b_re.shape[-1]
    gp = jax.ShapeDtypeStruct((g, 1, p), F32)
    gcp = jax.ShapeDtypeStruct((g, c, p), F32)
    return pl.pallas_call(
        _s5_disc_kernel,
        out_shape=(gp, gp, gp, gp, gcp, gcp),
        name="s5_discretise",
    )(a_re.reshape(g, 1, p), a_im.reshape(g, 1, p), log_step.reshape(g, 1, 1),
      jnp.swapaxes(b_re, 1, 2), jnp.swapaxes(b_im, 1, 2))


def _s5_block_weights(bbt_re, bbt_im, c_re, c_im):
    g, c, p = bbt_re.shape
    nb = g // GROUPS_PER_BLOCK
    n_lt = STATES_PER_BLOCK // LANES
    eye = jnp.eye(GROUPS_PER_BLOCK, dtype=F32)

    def b_blk(x):
        x = x.reshape(nb, GROUPS_PER_BLOCK, c, p)
        return jnp.einsum("jacp,ab->jacbp", x, eye).reshape(nb, LANES, n_lt, LANES)

    def c_blk(x):
        x = x.reshape(nb, GROUPS_PER_BLOCK, c, p)
        return jnp.einsum("jbcp,ab->japbc", x, eye).reshape(nb, n_lt, LANES, LANES)

    b_w = jnp.stack([b_blk(bbt_re), b_blk(bbt_im)], axis=3).reshape(nb, LANES, 2 * STATES_PER_BLOCK)
    c_w = jnp.stack([c_blk(c_re), -c_blk(c_im)], axis=2).reshape(nb, 2 * STATES_PER_BLOCK, LANES)
    return b_w.astype(BF16), c_w.astype(BF16)


def _s5_seq_kernel(u_ref, bw_ref, cw_ref, abr_ref, abi_ref, asr_ref, asi_ref, d_ref,
                   y_ref, xre_ref, xim_ref, up_ref, bu_ref, e_ref, cin_ref, *, t_len):
    ns = STATES_PER_BLOCK
    n_lt = ns // LANES
    nseg = t_len // S5_SEG
    re_l = lambda lt: slice(2 * lt * LANES, (2 * lt + 1) * LANES)
    im_l = lambda lt: slice((2 * lt + 1) * LANES, (2 * lt + 2) * LANES)
    both = lambda lt: slice(2 * lt * LANES, (2 * lt + 2) * LANES)
    nat = lambda lt: slice(lt * LANES, (lt + 1) * LANES)

    def seg_rows(tl):
        return pl.ds(tl * nseg, nseg)

    for tl in range(S5_SEG):
        up_ref[seg_rows(tl), :] = u_ref[pl.ds(tl, nseg, stride=S5_SEG), :]
    u = up_ref[...]
    u_bf = u.astype(BF16)

    def project_in(lt):
        bu_ref[:, both(lt)] = jnp.dot(u_bf, bw_ref[:, both(lt)], preferred_element_type=F32)

    def scan_tile(lt, init_from_carry, store_states):
        ar = abr_ref[:, nat(lt)]
        ai = abi_ref[:, nat(lt)]
        if init_from_carry:
            xr = cin_ref[:, re_l(lt)]
            xi = cin_ref[:, im_l(lt)]
        else:
            xr = jnp.zeros((nseg, LANES), F32)
            xi = jnp.zeros((nseg, LANES), F32)
        for tl in range(S5_SEG):
            br = bu_ref[seg_rows(tl), re_l(lt)]
            bi = bu_ref[seg_rows(tl), im_l(lt)]
            xr, xi = ar * xr - ai * xi + br, ar * xi + ai * xr + bi
            if store_states:
                bu_ref[seg_rows(tl), re_l(lt)] = xr
                bu_ref[seg_rows(tl), im_l(lt)] = xi
        if not store_states:
            e_ref[:, re_l(lt)] = xr
            e_ref[:, im_l(lt)] = xi

    project_in(0)
    for lt in range(n_lt):
        if lt + 1 < n_lt:
            project_in(lt + 1)
        scan_tile(lt, False, False)

    a_seg = [(asr_ref[:, nat(lt)], asi_ref[:, nat(lt)]) for lt in range(n_lt)]

    def seg_step(s, carry):
        cin_ref[pl.ds(s, 1), :] = jnp.concatenate(carry, axis=1)
        e = e_ref[pl.ds(s, 1), :]
        new = []
        for lt in range(n_lt):
            cr, ci = carry[2 * lt], carry[2 * lt + 1]
            asr, asi = a_seg[lt]
            new += [asr * cr - asi * ci + e[:, re_l(lt)], asr * ci + asi * cr + e[:, im_l(lt)]]
        return tuple(new)

    zero = jnp.zeros((1, LANES), F32)
    fin = lax.fori_loop(0, nseg, seg_step, (zero,) * (2 * n_lt))
    for lt in range(n_lt):
        xre_ref[:, nat(lt)] = fin[2 * lt]
        xim_ref[:, nat(lt)] = fin[2 * lt + 1]

    y = d_ref[...] * u
    scan_tile(0, True, True)
    for lt in range(n_lt):
        if lt + 1 < n_lt:
            scan_tile(lt + 1, True, True)
        y = y + _dot(bu_ref[:, both(lt)], cw_ref[both(lt), :])
    up_ref[...] = y
    for tl in range(S5_SEG):
        y_ref[pl.ds(tl, nseg, stride=S5_SEG), :] = up_ref[seg_rows(tl), :]


def _s5_sequence(p_main, b_w, c_w, ab_re, ab_im, as_re, as_im, d_skip):
    b, t, _ = p_main.shape
    nb = b_w.shape[0]
    ns = STATES_PER_BLOCK
    vec = lambda: pl.BlockSpec((1, ns), lambda i, j: (0, j))
    y, xre, xim = pl.pallas_call(
        functools.partial(_s5_seq_kernel, t_len=t),
        grid=(b, nb),
        in_specs=[pl.BlockSpec((None, t, LANES), lambda i, j: (i, 0, j)),
                  pl.BlockSpec((None, LANES, 2 * ns), lambda i, j: (j, 0, 0)),
                  pl.BlockSpec((None, 2 * ns, LANES), lambda i, j: (j, 0, 0)),
                  vec(), vec(), vec(), vec(),
                  pl.BlockSpec((1, LANES), lambda i, j: (0, j))],
        out_specs=[pl.BlockSpec((None, t, LANES), lambda i, j: (i, 0, j)),
                   pl.BlockSpec((None, 1, ns), lambda i, j: (i, 0, j)),
                   pl.BlockSpec((None, 1, ns), lambda i, j: (i, 0, j))],
        out_shape=(jax.ShapeDtypeStruct((b, t, nb * LANES), F32),
                   jax.ShapeDtypeStruct((b, 1, nb * ns), F32),
                   jax.ShapeDtypeStruct((b, 1, nb * ns), F32)),
        scratch_shapes=[pltpu.VMEM((t, LANES), F32),
                        pltpu.VMEM((t, 2 * ns), F32),
                        pltpu.VMEM((t // S5_SEG, 2 * ns), F32),
                        pltpu.VMEM((t // S5_SEG, 2 * ns), F32)],
        compiler_params=_params("parallel", "parallel"),
        name="s5_sequence",
    )(p_main, b_w, c_w, ab_re, ab_im, as_re, as_im, d_skip)
    return y, xre, xim


def _s5_step_kernel(u_ref, bw_ref, cw_ref, abr_ref, abi_ref, d_ref, x0r_ref, x0i_ref,
                    y_ref, x1r_ref, x1i_ref):
    n_lt = STATES_PER_BLOCK // LANES
    u = u_ref[...]
    bu = _dot(u, bw_ref[...])
    tiles = []
    for lt in range(n_lt):
        nat = slice(lt * LANES, (lt + 1) * LANES)
        ar, ai = abr_ref[:, nat], abi_ref[:, nat]
        x0r, x0i = x0r_ref[:, nat], x0i_ref[:, nat]
        xr = ar * x0r - ai * x0i + bu[:, 2 * lt * LANES:(2 * lt + 1) * LANES]
        xi = ar * x0i + ai * x0r + bu[:, (2 * lt + 1) * LANES:(2 * lt + 2) * LANES]
        x1r_ref[:, nat] = xr
        x1i_ref[:, nat] = xi
        tiles += [xr, xi]
    y_ref[...] = _dot(jnp.concatenate(tiles, axis=1), cw_ref[...]) + d_ref[...] * u


def _s5_step(p_rows, b_w, c_w, ab_re, ab_im, d_skip, x0_re, x0_im):
    rows = p_rows.shape[0]
    nb = b_w.shape[0]
    ns = STATES_PER_BLOCK
    vec = lambda: pl.BlockSpec((1, ns), lambda j: (0, j))
    st = lambda: pl.BlockSpec((rows, ns), lambda j: (0, j))
    return pl.pallas_call(
        _s5_step_kernel,
        grid=(nb,),
        in_specs=[pl.BlockSpec((rows, LANES), lambda j: (0, j)),
                  pl.BlockSpec((None, LANES, 2 * ns), lambda j: (j, 0, 0)),
                  pl.BlockSpec((None, 2 * ns, LANES), lambda j: (j, 0, 0)),
                  vec(), vec(),
                  pl.BlockSpec((1, LANES), lambda j: (0, j)),
                  st(), st()],
        out_specs=[pl.BlockSpec((rows, LANES), lambda j: (0, j)), st(), st()],
        out_shape=(jax.ShapeDtypeStruct((rows, nb * LANES), F32),
                   jax.ShapeDtypeStruct((rows, nb * ns), F32),
                   jax.ShapeDtypeStruct((rows, nb * ns), F32)),
        compiler_params=_params("parallel"),
        name="s5_step",
    )(p_rows, b_w, c_w, ab_re, ab_im, d_skip, x0_re, x0_im)


def _glu_kernel(y_ref, z_ref, w_ref, b_ref, o_ref):
    y = jax.nn.gelu(y_ref[...], approximate=True)
    gate = _sigmoid(_dot(y, w_ref[...]) + b_ref[...])
    o_ref[...] = (y * gate * _silu(z_ref[...])).astype(o_ref.dtype)


def _glu(y, p_rows, z_block, w_bf, b, tm):
    rows, d = y.shape
    tm = min(tm, rows)
    return pl.pallas_call(
        _glu_kernel,
        grid=(rows // tm,),
        in_specs=[pl.BlockSpec((tm, d), lambda i: (i, 0)),
                  pl.BlockSpec((tm, d), lambda i: (i, z_block)),
                  pl.BlockSpec((d, d), lambda i: (0, 0)),
                  pl.BlockSpec((1, d), lambda i: (0, 0))],
        out_specs=pl.BlockSpec((tm, d), lambda i: (i, 0)),
        out_shape=jax.ShapeDtypeStruct((rows, d), BF16),
        compiler_params=_params("parallel"),
        name="s5_glu",
    )(y, p_rows, w_bf, b.reshape(1, d))


def _softplus(x):
    return jnp.maximum(x, 0.0) + jnp.log(1.0 + jnp.exp(-jnp.abs(x)))


def _rwkv_token_terms(r, k, lo, w0, w2p, a0, a2p, k_a):
    w = -_softplus(-(w0 + _dot(jnp.tanh(lo), w2p))) - 0.5
    logd = -jnp.exp(w)
    a = _sigmoid(a0 + _dot(lo, a2p))
    k2 = k * (1.0 + (a - 1.0) * k_a)
    return logd, a, k2


def _rwkv_chunk_kernel(r_ref, k_ref, v_ref, lo_ref, z_ref,
                       mur_ref, muk_ref, muv_ref, mulo_ref,
                       w0_ref, w2p_ref, a0_ref, a2p_ref, kk_ref, ka_ref, rk_ref, gw_ref, gb_ref,
                       o_ref, hs_ref,
                       h_scr, pr_scr, pk_scr, pv_scr, plo_scr, *, n_heads):
    L = RW_CHUNK
    W = 2 * RW_HEAD
    n_pairs = n_heads // 2
    c = pl.program_id(1)

    @pl.when(c == 0)
    def _():
        h_scr[...] = jnp.zeros_like(h_scr)
        pr_scr[...] = jnp.zeros_like(pr_scr)
        pk_scr[...] = jnp.zeros_like(pk_scr)
        pv_scr[...] = jnp.zeros_like(pv_scr)
        plo_scr[...] = jnp.zeros_like(plo_scr)

    row1 = lax.broadcasted_iota(jnp.int32, (L, 1), 0)

    def token_shift(cur_ref, prev_scr, mu_ref):
        cur = cur_ref[...]
        prev = jnp.where(row1 == 0, prev_scr[...], pltpu.roll(cur, 1, 0))
        prev_scr[...] = cur[L - 1:L, :]
        return cur + (prev - cur) * mu_ref[...]

    r = token_shift(r_ref, pr_scr, mur_ref)
    k = token_shift(k_ref, pk_scr, muk_ref)
    v = token_shift(v_ref, pv_scr, muv_ref)
    lo = token_shift(lo_ref, plo_scr, mulo_ref)
    logd, a, k2 = _rwkv_token_terms(r, k, lo, w0_ref[...], w2p_ref[...], a0_ref[...], a2p_ref[...],
                                    ka_ref[...])
    kk = k * kk_ref[...]

    tri = (lax.broadcasted_iota(jnp.int32, (L, L), 0)
           >= lax.broadcasted_iota(jnp.int32, (L, L), 1)).astype(BF16)
    hi = logd.astype(BF16)
    rem = logd - hi.astype(F32)
    mid = rem.astype(BF16)
    low = (rem - mid.astype(F32)).astype(BF16)
    lp = (jnp.dot(tri, hi, preferred_element_type=F32)
          + jnp.dot(tri, mid, preferred_element_type=F32)
          + jnp.dot(tri, low, preferred_element_type=F32))
    p_inc = jnp.exp(lp)
    p_exc = jnp.exp(lp - logd)
    p_inv = jnp.exp(-lp)

    first_head = lax.broadcasted_iota(jnp.int32, (L, W), 1) < RW_HEAD

    def head_sum(x):
        s0 = jnp.sum(jnp.where(first_head, x, 0.0), axis=-1, keepdims=True)
        s1 = jnp.sum(jnp.where(first_head, 0.0, x), axis=-1, keepdims=True)
        return jnp.where(first_head, s0, s1)

    def stack_heads(x):
        return jnp.concatenate([jnp.where(first_head, x, 0.0), jnp.where(first_head, 0.0, x)], axis=0)

    ri = lax.broadcasted_iota(jnp.int32, (2 * L, 2 * L), 0)
    ci = lax.broadcasted_iota(jnp.int32, (2 * L, 2 * L), 1)
    t_row, t_col = ri & (L - 1), ci & (L - 1)
    strict = t_row > t_col
    incl = t_row >= t_col
    eye = (ri == ci).astype(F32)
    blk_masks = []
    s = INV_BASE
    while s <= L:
        blk_masks.append((ri // s) == (ci // s))
        s *= 2

    pairs = range(n_pairs)
    lanes = [slice(p * W, (p + 1) * W) for p in pairs]

    lhs, nm, mm, ab, vs, kb, p_end = [], [], [], [], [], [], []
    for sl in lanes:
        kk_p = kk[:, sl]
        kkn = kk_p / jnp.maximum(jnp.sqrt(head_sum(kk_p * kk_p)), 1e-12)
        pinc, pinv = p_inc[:, sl], p_inv[:, sl]
        kd = k2[:, sl] * pinv
        bd = kkn * a[:, sl] * pinv
        pe = pinc[L - 1:L, :]
        lhs_p = jnp.concatenate([stack_heads(kkn * p_exc[:, sl]), stack_heads(r[:, sl] * pinc)],
                                axis=0).astype(BF16)
        rhs_p = jnp.concatenate([stack_heads(bd), stack_heads(kd)], axis=0).astype(BF16)
        amat = lax.dot_general(lhs_p, rhs_p, _NT, preferred_element_type=F32)
        lhs.append(lhs_p)
        nm.append(jnp.where(strict, amat[:2 * L, :2 * L], 0.0))
        mm.append(jnp.where(strict, amat[:2 * L, 2 * L:], 0.0).astype(BF16))
        ab.append(jnp.concatenate([jnp.where(incl, amat[2 * L:, 2 * L:], 0.0),
                                   -jnp.where(incl, amat[2 * L:, :2 * L], 0.0)], axis=1).astype(BF16))
        vs.append(stack_heads(v[:, sl]).astype(BF16))
        kb.append(jnp.concatenate([stack_heads(kd * pe), stack_heads(bd * pe)], axis=0).astype(BF16))
        p_end.append(pe)

    d = [jnp.where(blk_masks[0], n_p, 0.0).astype(BF16) for n_p in nm]
    x = [eye - d_p.astype(F32) for d_p in d]
    pw = [jnp.dot(d_p, d_p, preferred_element_type=F32) for d_p in d]
    s = 2
    while s < INV_BASE:
        x = [x_p + _dot(x_p, pw_p) for x_p, pw_p in zip(x, pw)]
        s *= 2
        if s < INV_BASE:
            pw = [_dot(pw_p, pw_p) for pw_p in pw]
    for lvl in range(1, len(blk_masks)):
        off = blk_masks[lvl] & ~blk_masks[lvl - 1]
        xc = [_dot(x_p, jnp.where(off, n_p, 0.0)) for x_p, n_p in zip(x, nm)]
        x = [x_p - _dot(xc_p, x_p) for x_p, xc_p in zip(x, xc)]

    hs = [h_scr[p] for p in pairs]
    lh = [jnp.dot(lhs_p, hs_p.astype(BF16), preferred_element_type=F32) for lhs_p, hs_p in zip(lhs, hs)]
    mv = [jnp.dot(mm_p, vs_p, preferred_element_type=F32) for mm_p, vs_p in zip(mm, vs)]
    u = [_dot(x_p, lh_p[:2 * L] + mv_p).astype(BF16) for x_p, lh_p, mv_p in zip(x, lh, mv)]
    o_st = [lh_p[2 * L:] + jnp.dot(ab_p, jnp.concatenate([vs_p, u_p], axis=0), preferred_element_type=F32)
            for lh_p, ab_p, vs_p, u_p in zip(lh, ab, vs, u)]
    for p in pairs:
        p_end_col = jnp.sum(eye * p_end[p], axis=1, keepdims=True)
        h_scr[p] = p_end_col * hs[p] + lax.dot_general(
            kb[p], jnp.concatenate([vs[p], -u[p]], axis=0), _TN, preferred_element_type=F32)

    for p, sl in enumerate(lanes):
        o = o_st[p][:L] + o_st[p][L:]
        mu = head_sum(o) * (1.0 / RW_HEAD)
        var = head_sum((o - mu) ** 2) * (1.0 / RW_HEAD)
        o = (o - mu) * lax.rsqrt(var + GN_EPS) * gw_ref[:, sl] + gb_ref[:, sl]
        o = o + head_sum(r[:, sl] * k2[:, sl] * rk_ref[:, sl]) * v[:, sl]
        z = z_ref[:, LANES + p * W:LANES + (p + 1) * W]
        o_ref[:, sl] = (o * _silu(z)).astype(o_ref.dtype)

    @pl.when(c == pl.num_programs(1) - 1)
    def _():
        hs_ref[...] = h_scr[...]


def _rwkv_sequence(p_main, p_lz, col, mu, w0, w2p, a0, a2p, k_k, k_a, r_k, gn_w, gn_b):
    b, t, _ = p_main.shape
    wlz = p_lz.shape[-1]
    d = w0.shape[-1]
    n_heads = d // RW_HEAD
    n_pairs, pw = n_heads // 2, 2 * RW_HEAD
    L = RW_CHUNK
    blk = lambda cb: pl.BlockSpec((None, L, d), lambda i, j, cb=cb: (i, j, cb))
    vec = lambda n: pl.BlockSpec((1, n), lambda i, j: (0, 0))
    full = lambda shp: pl.BlockSpec(shp, lambda i, j: (0,) * len(shp))
    mu_r, mu_k, mu_v, mu_lo = mu
    o, hs = pl.pallas_call(
        functools.partial(_rwkv_chunk_kernel, n_heads=n_heads),
        grid=(b, t // L),
        in_specs=[blk(col["r"]), blk(col["k"]), blk(col["v"]),
                  pl.BlockSpec((None, L, LANES), lambda i, j: (i, j, 0)),
                  pl.BlockSpec((None, L, wlz), lambda i, j: (i, j, 0)),
                  vec(d), vec(d), vec(d), vec(LANES),
                  vec(d), full((LANES, d)), vec(d), full((LANES, d)),
                  vec(d), vec(d), vec(d), vec(d), vec(d)],
        out_specs=[pl.BlockSpec((None, L, d), lambda i, j: (i, j, 0)),
                   pl.BlockSpec((None, n_pairs, pw, pw), lambda i, j: (i, 0, 0, 0))],
        out_shape=(jax.ShapeDtypeStruct((b, t, d), BF16),
                   jax.ShapeDtypeStruct((b, n_pairs, pw, pw), F32)),
        scratch_shapes=[pltpu.VMEM((n_pairs, pw, pw), F32),
                        pltpu.VMEM((1, d), F32), pltpu.VMEM((1, d), F32), pltpu.VMEM((1, d), F32),
                        pltpu.VMEM((1, LANES), F32)],
        compiler_params=_params("parallel", "arbitrary"),
        name="rwkv_sequence",
    )(p_main, p_main, p_main, p_lz, p_lz,
      mu_r, mu_k, mu_v, mu_lo, w0, w2p, a0, a2p, k_k, k_a, r_k, gn_w, gn_b)
    hs = hs.reshape(b, n_pairs, 2, RW_HEAD, 2, RW_HEAD)
    hs = jnp.stack([hs[:, :, 0, :, 0, :], hs[:, :, 1, :, 1, :]], axis=2)
    return o, jnp.swapaxes(hs.reshape(b, n_heads, RW_HEAD, RW_HEAD), -1, -2)


def _rwkv_step_prep_kernel(cr_ref, ck_ref, cv_ref, clo_ref, cz_ref, pr_ref, pk_ref, pv_ref, plo_ref,
                           mur_ref, muk_ref, muv_ref, mulo_ref,
                           w0_ref, w2p_ref, a0_ref, a2p_ref, kk_ref, ka_ref,
                           r_o, k2_o, v_o, kk_o, a_o, d_o, z_o):
    def lerp(c_ref, p_ref, mu_ref):
        cur = c_ref[...]
        return cur + (p_ref[...] - cur) * mu_ref[...]

    r = lerp(cr_ref, pr_ref, mur_ref)
    k = lerp(ck_ref, pk_ref, muk_ref)
    v = lerp(cv_ref, pv_ref, muv_ref)
    lo = lerp(clo_ref, plo_ref, mulo_ref)
    logd, a, k2 = _rwkv_token_terms(r, k, lo, w0_ref[...], w2p_ref[...], a0_ref[...], a2p_ref[...],
                                    ka_ref[...])
    r_o[...] = r.T
    k2_o[...] = k2.T
    v_o[...] = v.T
    kk_o[...] = (k * kk_ref[...]).T
    a_o[...] = a.T
    d_o[...] = jnp.exp(logd).T
    z_o[...] = cz_ref[:, LANES:].T


def _rwkv_step_kernel(s_ref, r_ref, k2_ref, v_ref, kk_ref, a_ref, d_ref, z_ref,
                      rk_ref, gw_ref, gb_ref, o_ref, s1_ref, o_scr):
    r, k2, v, dec = r_ref[...], k2_ref[...], v_ref[...], d_ref[...]
    kk = kk_ref[...]
    kkn = kk / jnp.maximum(jnp.sqrt(jnp.sum(kk * kk, axis=0, keepdims=True)), 1e-12)
    bvec = kkn * a_ref[...]
    for i in range(RW_HEAD):
        s = s_ref[i]
        sa = jnp.sum(s * kkn, axis=0, keepdims=True)
        s1 = s * dec - sa * bvec + v[i:i + 1, :] * k2
        s1_ref[i] = s1
        o_scr[pl.ds(i, 1), :] = jnp.sum(s1 * r, axis=0, keepdims=True)
    o = o_scr[...]
    mu = jnp.mean(o, axis=0, keepdims=True)
    var = jnp.mean((o - mu) ** 2, axis=0, keepdims=True)
    o = (o - mu) * lax.rsqrt(var + GN_EPS) * gw_ref[...] + gb_ref[...]
    o = o + jnp.sum(r * k2 * rk_ref[...], axis=0, keepdims=True) * v
    o_ref[...] = o * _silu(z_ref[...])


def _rwkv_step(p_rows, p_lz_rows, col, mu, w0, w2p, a0, a2p, k_k, k_a, r_k, gn_w, gn_b, s0):
    rows = s0.shape[0]
    wlz = p_lz_rows.shape[-1]
    d = w0.shape[-1]
    n_heads = d // RW_HEAD
    cur = lambda cb: pl.BlockSpec((rows, d), lambda i, cb=cb: (0, cb))
    prv = lambda cb: pl.BlockSpec((rows, d), lambda i, cb=cb: (1, cb))
    vec = lambda n: pl.BlockSpec((1, n), lambda i: (0, 0))
    mu_r, mu_k, mu_v, mu_lo = mu
    out = jax.ShapeDtypeStruct((d, rows), F32)
    terms = pl.pallas_call(
        _rwkv_step_prep_kernel,
        grid=(1,),
        in_specs=[cur(col["r"]), cur(col["k"]), cur(col["v"]),
                  pl.BlockSpec((rows, LANES), lambda i: (0, 0)),
                  pl.BlockSpec((rows, wlz), lambda i: (0, 0)),
                  prv(col["r"]), prv(col["k"]), prv(col["v"]),
                  pl.BlockSpec((rows, LANES), lambda i: (1, 0)),
                  vec(d), vec(d), vec(d), vec(LANES),
                  vec(d), pl.BlockSpec((LANES, d), lambda i: (0, 0)),
                  vec(d), pl.BlockSpec((LANES, d), lambda i: (0, 0)),
                  vec(d), vec(d)],
        out_specs=[pl.BlockSpec((d, rows), lambda i: (0, 0))] * 7,
        out_shape=(out,) * 7,
        compiler_params=_params("arbitrary"),
        name="rwkv_step_prep",
    )(p_rows, p_rows, p_rows, p_lz_rows, p_lz_rows, p_rows, p_rows, p_rows, p_lz_rows,
      mu_r, mu_k, mu_v, mu_lo, w0, w2p, a0, a2p, k_k, k_a)
    per_h = lambda: pl.BlockSpec((RW_HEAD, rows), lambda h: (h, 0))
    par = lambda: pl.BlockSpec((RW_HEAD, 1), lambda h: (h, 0))
    st = lambda: pl.BlockSpec((None, RW_HEAD, RW_HEAD, rows), lambda h: (h, 0, 0, 0))
    o_t, s1_t = pl.pallas_call(
        _rwkv_step_kernel,
        grid=(n_heads,),
        in_specs=[st()] + [per_h()] * 7 + [par()] * 3,
        out_specs=[per_h(), st()],
        out_shape=(jax.ShapeDtypeStruct((d, rows), F32),
                   jax.ShapeDtypeStruct((n_heads, RW_HEAD, RW_HEAD, rows), F32)),
        scratch_shapes=[pltpu.VMEM((RW_HEAD, rows), F32)],
        compiler_params=_params("parallel"),
        name="rwkv_step",
    )(jnp.transpose(s0, (1, 2, 3, 0)), *terms,
      r_k.reshape(d, 1), gn_w.reshape(d, 1), gn_b.reshape(d, 1))
    return o_t.T.astype(BF16), jnp.transpose(s1_t, (3, 0, 1, 2))


def _out_kernel(os_ref, or_ref, gs_ref, gr_ref, x_ref, gt_ref, w1_ref, w2_ref, fg_ref, y_ref):
    mixed = (gs_ref[...].astype(F32) * jnp.dot(os_ref[...], w1_ref[...], preferred_element_type=F32)
             + gr_ref[...].astype(F32) * jnp.dot(or_ref[...], w2_ref[...], preferred_element_type=F32))
    x = x_ref[...] + gt_ref[...] * mixed
    y_ref[...] = x * lax.rsqrt(jnp.mean(x * x, axis=-1, keepdims=True) + RMS_EPS) * fg_ref[...]


def _out_proj(o_s, o_r, p_gate, col, x, mod, w1, w2, final_g, tt):
    b, t, d = x.shape
    dh = o_s.shape[-1]
    tt = min(tt, t)
    tm = 1 if mod.shape[1] == 1 else tt
    gt_map = (lambda i, j: (i, 0, 2)) if tm == 1 else (lambda i, j: (i, j, 2))
    return pl.pallas_call(
        _out_kernel,
        grid=(b, t // tt),
        in_specs=[pl.BlockSpec((None, tt, dh), lambda i, j: (i, j, 0)),
                  pl.BlockSpec((None, tt, dh), lambda i, j: (i, j, 0)),
                  pl.BlockSpec((None, tt, d), lambda i, j: (i, j, col["g_s5"])),
                  pl.BlockSpec((None, tt, d), lambda i, j: (i, j, col["g_rw"])),
                  pl.BlockSpec((None, tt, d), lambda i, j: (i, j, 0)),
                  pl.BlockSpec((None, tm, d), gt_map),
                  pl.BlockSpec((dh, d), lambda i, j: (0, 0)),
                  pl.BlockSpec((dh, d), lambda i, j: (0, 0)),
                  pl.BlockSpec((1, d), lambda i, j: (0, 0))],
        out_specs=pl.BlockSpec((None, tt, d), lambda i, j: (i, j, 0)),
        out_shape=jax.ShapeDtypeStruct((b, t, d), F32),
        compiler_params=_params("parallel", "parallel"),
        name="out_proj",
    )(o_s, o_r, p_gate, p_gate, x, mod, w1, w2, final_g.reshape(1, d))


def kernel(x_prompt, x_sample, c_prompt, c_sample, state_s5_re, state_s5_im, state_wkv, state_shift, norm_g, w_ada, b_ada, w_in, mu_rw, A_re, A_im, log_step, B_re, B_im, C_re, C_im, D_skip, w_glu, b_glu, w0, w2, a0, a2, k_k, k_a, r_k, gn_w, gn_b, w_out, final_g):
    depth = norm_g.shape[0]
    assert depth == 1
    bp, tp, d = x_prompt.shape
    bs = x_sample.shape[0]
    assert x_sample.shape[1] == 1
    dh = d // 2
    l = 0

    w_main, w_lz, w_gate = 5 * dh, 2 * LORA + dh, 2 * d
    col = {"u": 0, "z_s5": 1, "r": 2, "k": 3, "v": 4, "g_s5": 0, "g_rw": 1}
    w = w_in[l]

    def project(rows, tm):
        return (_in_proj(rows, w, 0, w_main, w_main // 4, tm),
                _in_proj(rows, w, w_main, w_lz, w_lz, tm),
                _in_proj(rows, w, w_main + w_lz, w_gate, w_gate // 4, tm, gate=True))

    mu = mu_rw[l]
    mu_parts = (mu[None, :dh], mu[None, dh:2 * dh], mu[None, 2 * dh:3 * dh], mu[None, 3 * dh:])
    zpad = jnp.zeros((LORA, dh), F32)
    w2p = jnp.concatenate([w2[l], zpad], axis=0).astype(BF16)
    a2p = jnp.concatenate([zpad, a2[l]], axis=0).astype(BF16)
    row = lambda x: x.reshape(1, -1)
    rw_params = (row(w0[l]), w2p, row(a0[l]), a2p, row(k_k[l]), row(k_a[l]), row(r_k[l]),
                 row(gn_w[l]), row(gn_b[l]))
    w_out_bf = w_out[l].astype(BF16)
    wo1, wo2 = w_out_bf[:dh], w_out_bf[dh:]
    w_glu_bf = w_glu[l].astype(BF16)

    ab_re, ab_im, as_re, as_im, bbt_re, bbt_im = _s5_discretise(A_re[l], A_im[l], log_step[l], B_re[l], B_im[l])
    b_w, c_w = _s5_block_weights(bbt_re, bbt_im, C_re[l], C_im[l])
    flat = lambda x: x.reshape(1, -1)
    ab_re, ab_im, as_re, as_im = flat(ab_re), flat(ab_im), flat(as_re), flat(as_im)
    d_skip = flat(D_skip[l])

    mod = _mod(jnp.concatenate([c_prompt, c_sample], axis=0), w_ada[l], b_ada[l])
    mod_p = mod[:bp].reshape(bp, 1, 3 * d)
    mod_s = mod[bp:].reshape(1, bs, 3 * d)

    h_p = _modulated_norm(x_prompt, norm_g[l], mod_p, BF16, 512)
    shift_p = _modulated_norm(x_prompt[:, tp - 1:, :], norm_g[l], mod_p, F32, 1)[:, 0]
    a_p = h_p.reshape(bp * tp, d)
    pm, plz, pg = project(a_p, 1024)
    pm3 = pm.reshape(bp, tp, -1)
    y_s5, xre_p, xim_p = _s5_sequence(pm3, b_w, c_w, ab_re, ab_im, as_re, as_im, d_skip)
    o_s = _glu(y_s5.reshape(bp * tp, dh), pm, col["z_s5"], w_glu_bf, b_glu[l], 512)
    o_r, hs_p = _rwkv_sequence(pm3, plz.reshape(bp, tp, -1), col, mu_parts, *rw_params)
    y_prompt = _out_proj(o_s.reshape(bp, tp, dh), o_r, pg.reshape(bp, tp, -1), col, x_prompt, mod_p,
                         wo1, wo2, final_g, 512)
    g_s5 = A_re.shape[1]
    s5_shape = (1, bp, g_s5, P_S5)
    wkv_p = hs_p[None]

    xs = x_sample.reshape(1, bs, d)
    h_s = _modulated_norm(xs, norm_g[l], mod_s, F32, bs)[0]
    a_s = jnp.concatenate([h_s, state_shift[l]], axis=0).astype(BF16)
    ps, pslz, psg = project(a_s, 2 * bs)
    y_s5s, xre_s, xim_s = _s5_step(ps[:bs], b_w, c_w, ab_re, ab_im, d_skip,
                                   state_s5_re[l].reshape(bs, -1), state_s5_im[l].reshape(bs, -1))
    o_ss = _glu(y_s5s, ps[:bs], col["z_s5"], w_glu_bf, b_glu[l], bs)
    o_rs, wkv_s = _rwkv_step(ps, pslz, col, mu_parts, *rw_params, state_wkv[l])
    y_sample = _out_proj(o_ss[None], o_rs[None], psg[:bs][None], col, xs, mod_s, wo1, wo2, final_g, bs)
    y_sample = y_sample.reshape(bs, 1, d)

    return (y_prompt, y_sample,
            xre_p.reshape(s5_shape), xim_p.reshape(s5_shape), wkv_p, shift_p[None],
            xre_s.reshape(1, bs, g_s5, P_S5), xim_s.reshape(1, bs, g_s5, P_S5), wkv_s[None], h_s[None])
```

```python
import functools

import jax
import jax.numpy as jnp
from jax import lax
from jax.experimental import pallas as pl
from jax.experimental.pallas import tpu as pltpu

F32 = jnp.float32
BF16 = jnp.bfloat16

RMS_EPS = 1e-6
GN_EPS = 64e-5
S5_GROUP = 16
P_S5 = 64
RW_HEAD = 64
LORA = 64
LANES = 128
GROUPS_PER_BLOCK = LANES // S5_GROUP
STATES_PER_BLOCK = GROUPS_PER_BLOCK * P_S5
S5_SEG = 32
RW_CHUNK = 64
INV_BASE = 8
RW_SEQS = 4
RW_SEQ_LAG = 1

_NT = (((1,), (1,)), ((), ()))
_TN = (((0,), (0,)), ((), ()))


def _dot(a, b):
    return jnp.dot(a.astype(BF16), b.astype(BF16), preferred_element_type=F32)


def _sigmoid(x):
    return 1.0 / (1.0 + jnp.exp(-x))


def _silu(x):
    return x * _sigmoid(x)


def _params(*sem):
    return pltpu.CompilerParams(dimension_semantics=sem, vmem_limit_bytes=56 * 1024 * 1024)


def _mod_kernel(c_ref, w_ref, b_ref, o_ref):
    o_ref[...] = _dot(_silu(c_ref[...]), w_ref[...]) + b_ref[...]


def _mod(c, w_bf, b):
    rows, d = c.shape
    n = w_bf.shape[1]
    tn = 1024
    return pl.pallas_call(
        _mod_kernel,
        grid=(n // tn,),
        in_specs=[pl.BlockSpec((rows, d), lambda j: (0, 0)),
                  pl.BlockSpec((d, tn), lambda j: (0, j)),
                  pl.BlockSpec((1, tn), lambda j: (0, j))],
        out_specs=pl.BlockSpec((rows, tn), lambda j: (0, j)),
        out_shape=jax.ShapeDtypeStruct((rows, n), F32),
        compiler_params=_params("parallel"),
        name="adaln_mod",
    )(c, w_bf, b.reshape(1, n))


def _h_kernel(x_ref, g_ref, sh_ref, sc_ref, h_ref):
    x = x_ref[...]
    y = x * lax.rsqrt(jnp.mean(x * x, axis=-1, keepdims=True) + RMS_EPS) * g_ref[...]
    h_ref[...] = (y * (1.0 + sc_ref[...]) + sh_ref[...]).astype(h_ref.dtype)


def _modulated_norm(x, g, mod, out_dtype, tt):
    b, t, d = x.shape
    tt = min(tt, t)
    tm = 1 if mod.shape[1] == 1 else tt
    mod_map = (lambda i, j: (i, 0, 0)) if tm == 1 else (lambda i, j: (i, j, 0))
    mod_map1 = (lambda i, j: (i, 0, 1)) if tm == 1 else (lambda i, j: (i, j, 1))
    return pl.pallas_call(
        _h_kernel,
        grid=(b, t // tt),
        in_specs=[pl.BlockSpec((None, tt, d), lambda i, j: (i, j, 0)),
                  pl.BlockSpec((1, d), lambda i, j: (0, 0)),
                  pl.BlockSpec((None, tm, d), mod_map),
                  pl.BlockSpec((None, tm, d), mod_map1)],
        out_specs=pl.BlockSpec((None, tt, d), lambda i, j: (i, j, 0)),
        out_shape=jax.ShapeDtypeStruct((b, t, d), out_dtype),
        compiler_params=_params("parallel", "parallel"),
        name="modulated_norm",
    )(x, g.reshape(1, d), mod, mod)


def _in_proj_kernel(a_ref, w_ref, o_ref, wbf_ref, *, gate):
    @pl.when(pl.program_id(1) == 0)
    def _():
        wbf_ref[...] = w_ref[...].astype(BF16)

    p = jnp.dot(a_ref[...], wbf_ref[...], preferred_element_type=F32)
    o_ref[...] = (_sigmoid(p) if gate else p).astype(o_ref.dtype)


def _in_proj(a, w, col0, width, tn, tm, gate=False):
    m, k = a.shape
    tm = min(tm, m)
    assert width % tn == 0 and m % tm == 0 and col0 % LANES == 0
    return pl.pallas_call(
        functools.partial(_in_proj_kernel, gate=gate),
        grid=(width // tn, m // tm),
        in_specs=[pl.BlockSpec((tm, k), lambda j, i: (i, 0)),
                  pl.BlockSpec((pl.Element(k), pl.Element(tn)), lambda j, i: (0, pl.multiple_of(col0 + j * tn, LANES)))],
        out_specs=pl.BlockSpec((tm, tn), lambda j, i: (i, j)),
        out_shape=jax.ShapeDtypeStruct((m, width), BF16 if gate else F32),
        scratch_shapes=[pltpu.VMEM((k, tn), BF16)],
        compiler_params=_params("parallel", "arbitrary"),
        name="in_proj",
    )(a, w)


def _s5_disc_kernel(are_ref, aim_ref, ls_ref, bre_ref, bim_ref,
                    abr_ref, abi_ref, asr_ref, asi_ref, bbr_ref, bbi_ref):
    step = jnp.exp(ls_ref[...])
    lam_re = jnp.minimum(are_ref[...], -1e-4)
    lam_im = aim_ref[...]
    mag = jnp.exp(lam_re * step)
    ab_re = mag * jnp.cos(lam_im * step)
    ab_im = mag * jnp.sin(lam_im * step)
    den = lam_re * lam_re + lam_im * lam_im
    f_re = ((ab_re - 1.0) * lam_re + ab_im * lam_im) / den
    f_im = (ab_im * lam_re - (ab_re - 1.0) * lam_im) / den
    br, bi = bre_ref[...], bim_ref[...]
    bbr_ref[...] = f_re * br - f_im * bi
    bbi_ref[...] = f_re * bi + f_im * br
    abr_ref[...] = ab_re
    abi_ref[...] = ab_im
    pr, pi = ab_re, ab_im
    n = 1
    while n < S5_SEG:
        pr, pi = pr * pr - pi * pi, 2.0 * pr * pi
        n *= 2
    asr_ref[...] = pr
    asi_ref[...] = pi


def _s5_discretise(a_re, a_im, log_step, b_re, b_im):
    g, p = a_re.shape
    c = b_re.shape[-1]
    gp = jax.ShapeDtypeStruct((g, 1, p), F32)
    gcp = jax.ShapeDtypeStruct((g, c, p), F32)
    return pl.pallas_call(
        _s5_disc_kernel,
        out_shape=(gp, gp, gp, gp, gcp, gcp),
        name="s5_discretise",
    )(a_re.reshape(g, 1, p), a_im.reshape(g, 1, p), log_step.reshape(g, 1, 1),
      jnp.swapaxes(b_re, 1, 2), jnp.swapaxes(b_im, 1, 2))


def _s5_block_weights(bbt_re, bbt_im, c_re, c_im):
    g, c, p = bbt_re.shape
    nb = g // GROUPS_PER_BLOCK
    n_lt = STATES_PER_BLOCK // LANES
    eye = jnp.eye(GROUPS_PER_BLOCK, dtype=F32)

    def b_blk(x):
        x = x.reshape(nb, GROUPS_PER_BLOCK, c, p)
        return jnp.einsum("jacp,ab->jacbp", x, eye).reshape(nb, LANES, n_lt, LANES)

    def c_blk(x):
        x = x.reshape(nb, GROUPS_PER_BLOCK, c, p)
        return jnp.einsum("jbcp,ab->japbc", x, eye).reshape(nb, n_lt, LANES, LANES)

    b_w = jnp.stack([b_blk(bbt_re), b_blk(bbt_im)], axis=3).reshape(nb, LANES, 2 * STATES_PER_BLOCK)
    c_w = jnp.stack([c_blk(c_re), -c_blk(c_im)], axis=2).reshape(nb, 2 * STATES_PER_BLOCK, LANES)
    return b_w.astype(BF16), c_w.astype(BF16)


def _s5_seq_kernel(u_ref, bw_ref, cw_ref, abr_ref, abi_ref, asr_ref, asi_ref, d_ref,
                   y_ref, xre_ref, xim_ref, up_ref, bu_ref, e_ref, cin_ref, *, t_len):
    ns = STATES_PER_BLOCK
    n_lt = ns // LANES
    nseg = t_len // S5_SEG
    re_l = lambda lt: slice(2 * lt * LANES, (2 * lt + 1) * LANES)
    im_l = lambda lt: slice((2 * lt + 1) * LANES, (2 * lt + 2) * LANES)
    both = lambda lt: slice(2 * lt * LANES, (2 * lt + 2) * LANES)
    nat = lambda lt: slice(lt * LANES, (lt + 1) * LANES)

    def seg_rows(tl):
        return pl.ds(tl * nseg, nseg)

    for tl in range(S5_SEG):
        up_ref[seg_rows(tl), :] = u_ref[pl.ds(tl, nseg, stride=S5_SEG), :]
    u = up_ref[...]
    u_bf = u.astype(BF16)

    def project_in(lt):
        bu_ref[:, both(lt)] = jnp.dot(u_bf, bw_ref[:, both(lt)], preferred_element_type=F32)

    def scan_tile(lt, init_from_carry, store_states):
        ar = abr_ref[:, nat(lt)]
        ai = abi_ref[:, nat(lt)]
        if init_from_carry:
            xr = cin_ref[:, re_l(lt)]
            xi = cin_ref[:, im_l(lt)]
        else:
            xr = jnp.zeros((nseg, LANES), F32)
            xi = jnp.zeros((nseg, LANES), F32)
        for tl in range(S5_SEG):
            br = bu_ref[seg_rows(tl), re_l(lt)]
            bi = bu_ref[seg_rows(tl), im_l(lt)]
            xr, xi = ar * xr - ai * xi + br, ar * xi + ai * xr + bi
            if store_states:
                bu_ref[seg_rows(tl), re_l(lt)] = xr
                bu_ref[seg_rows(tl), im_l(lt)] = xi
        if not store_states:
            e_ref[:, re_l(lt)] = xr
            e_ref[:, im_l(lt)] = xi

    project_in(0)
    for lt in range(n_lt):
        if lt + 1 < n_lt:
            project_in(lt + 1)
        scan_tile(lt, False, False)

    a_seg = [(asr_ref[:, nat(lt)], asi_ref[:, nat(lt)]) for lt in range(n_lt)]

    def seg_step(s, carry):
        cin_ref[pl.ds(s, 1), :] = jnp.concatenate(carry, axis=1)
        e = e_ref[pl.ds(s, 1), :]
        new = []
        for lt in range(n_lt):
            cr, ci = carry[2 * lt], carry[2 * lt + 1]
            asr, asi = a_seg[lt]
            new += [asr * cr - asi * ci + e[:, re_l(lt)], asr * ci + asi * cr + e[:, im_l(lt)]]
        return tuple(new)

    zero = jnp.zeros((1, LANES), F32)
    fin = lax.fori_loop(0, nseg, seg_step, (zero,) * (2 * n_lt))
    for lt in range(n_lt):
        xre_ref[:, nat(lt)] = fin[2 * lt]
        xim_ref[:, nat(lt)] = fin[2 * lt + 1]

    y = d_ref[...] * u
    scan_tile(0, True, True)
    for lt in range(n_lt):
        if lt + 1 < n_lt:
            scan_tile(lt + 1, True, True)
        y = y + _dot(bu_ref[:, both(lt)], cw_ref[both(lt), :])
    up_ref[...] = y
    for tl in range(S5_SEG):
        y_ref[pl.ds(tl, nseg, stride=S5_SEG), :] = up_ref[seg_rows(tl), :]


def _s5_sequence(p_main, b_w, c_w, ab_re, ab_im, as_re, as_im, d_skip):
    b, t, _ = p_main.shape
    nb = b_w.shape[0]
    ns = STATES_PER_BLOCK
    vec = lambda: pl.BlockSpec((1, ns), lambda i, j: (0, j))
    y, xre, xim = pl.pallas_call(
        functools.partial(_s5_seq_kernel, t_len=t),
        grid=(b, nb),
        in_specs=[pl.BlockSpec((None, t, LANES), lambda i, j: (i, 0, j)),
                  pl.BlockSpec((None, LANES, 2 * ns), lambda i, j: (j, 0, 0)),
                  pl.BlockSpec((None, 2 * ns, LANES), lambda i, j: (j, 0, 0)),
                  vec(), vec(), vec(), vec(),
                  pl.BlockSpec((1, LANES), lambda i, j: (0, j))],
        out_specs=[pl.BlockSpec((None, t, LANES), lambda i, j: (i, 0, j)),
                   pl.BlockSpec((None, 1, ns), lambda i, j: (i, 0, j)),
                   pl.BlockSpec((None, 1, ns), lambda i, j: (i, 0, j))],
        out_shape=(jax.ShapeDtypeStruct((b, t, nb * LANES), F32),
                   jax.ShapeDtypeStruct((b, 1, nb * ns), F32),
                   jax.ShapeDtypeStruct((b, 1, nb * ns), F32)),
        scratch_shapes=[pltpu.VMEM((t, LANES), F32),
                        pltpu.VMEM((t, 2 * ns), F32),
                        pltpu.VMEM((t // S5_SEG, 2 * ns), F32),
                        pltpu.VMEM((t // S5_SEG, 2 * ns), F32)],
        compiler_params=_params("parallel", "parallel"),
        name="s5_sequence",
    )(p_main, b_w, c_w, ab_re, ab_im, as_re, as_im, d_skip)
    return y, xre, xim


def _s5_step_kernel(u_ref, bw_ref, cw_ref, abr_ref, abi_ref, d_ref, x0r_ref, x0i_ref,
                    y_ref, x1r_ref, x1i_ref):
    n_lt = STATES_PER_BLOCK // LANES
    u = u_ref[...]
    bu = _dot(u, bw_ref[...])
    tiles = []
    for lt in range(n_lt):
        nat = slice(lt * LANES, (lt + 1) * LANES)
        ar, ai = abr_ref[:, nat], abi_ref[:, nat]
        x0r, x0i = x0r_ref[:, nat], x0i_ref[:, nat]
        xr = ar * x0r - ai * x0i + bu[:, 2 * lt * LANES:(2 * lt + 1) * LANES]
        xi = ar * x0i + ai * x0r + bu[:, (2 * lt + 1) * LANES:(2 * lt + 2) * LANES]
        x1r_ref[:, nat] = xr
        x1i_ref[:, nat] = xi
        tiles += [xr, xi]
    y_ref[...] = _dot(jnp.concatenate(tiles, axis=1), cw_ref[...]) + d_ref[...] * u


def _s5_step(p_rows, b_w, c_w, ab_re, ab_im, d_skip, x0_re, x0_im):
    rows = p_rows.shape[0]
    nb = b_w.shape[0]
    ns = STATES_PER_BLOCK
    vec = lambda: pl.BlockSpec((1, ns), lambda j: (0, j))
    st = lambda: pl.BlockSpec((rows, ns), lambda j: (0, j))
    return pl.pallas_call(
        _s5_step_kernel,
        grid=(nb,),
        in_specs=[pl.BlockSpec((rows, LANES), lambda j: (0, j)),
                  pl.BlockSpec((None, LANES, 2 * ns), lambda j: (j, 0, 0)),
                  pl.BlockSpec((None, 2 * ns, LANES), lambda j: (j, 0, 0)),
                  vec(), vec(),
                  pl.BlockSpec((1, LANES), lambda j: (0, j)),
                  st(), st()],
        out_specs=[pl.BlockSpec((rows, LANES), lambda j: (0, j)), st(), st()],
        out_shape=(jax.ShapeDtypeStruct((rows, nb * LANES), F32),
                   jax.ShapeDtypeStruct((rows, nb * ns), F32),
                   jax.ShapeDtypeStruct((rows, nb * ns), F32)),
        compiler_params=_params("parallel"),
        name="s5_step",
    )(p_rows, b_w, c_w, ab_re, ab_im, d_skip, x0_re, x0_im)


def _glu_kernel(y_ref, z_ref, w_ref, b_ref, o_ref):
    y = jax.nn.gelu(y_ref[...], approximate=True)
    gate = _sigmoid(_dot(y, w_ref[...]) + b_ref[...])
    o_ref[...] = (y * gate * _silu(z_ref[...])).astype(o_ref.dtype)


def _glu(y, p_rows, z_block, w_bf, b, tm):
    rows, d = y.shape
    tm = min(tm, rows)
    return pl.pallas_call(
        _glu_kernel,
        grid=(rows // tm,),
        in_specs=[pl.BlockSpec((tm, d), lambda i: (i, 0)),
                  pl.BlockSpec((tm, d), lambda i: (i, z_block)),
                  pl.BlockSpec((d, d), lambda i: (0, 0)),
                  pl.BlockSpec((1, d), lambda i: (0, 0))],
        out_specs=pl.BlockSpec((tm, d), lambda i: (i, 0)),
        out_shape=jax.ShapeDtypeStruct((rows, d), BF16),
        compiler_params=_params("parallel"),
        name="s5_glu",
    )(y, p_rows, w_bf, b.reshape(1, d))


def _softplus(x):
    return jnp.maximum(x, 0.0) + jnp.log(1.0 + jnp.exp(-jnp.abs(x)))


def _rwkv_token_terms(r, k, lo, w0, w2p, a0, a2p, k_a):
    w = -_softplus(-(w0 + _dot(jnp.tanh(lo), w2p))) - 0.5
    logd = -jnp.exp(w)
    a = _sigmoid(a0 + _dot(lo, a2p))
    k2 = k * (1.0 + (a - 1.0) * k_a)
    return logd, a, k2


def _rwkv_chunk_kernel(r_ref, k_ref, v_ref, lo_ref, z_ref,
                       mur_ref, muk_ref, muv_ref, mulo_ref,
                       w0_ref, w2p_ref, a0_ref, a2p_ref, kk_ref, ka_ref, rk_ref, gw_ref, gb_ref,
                       o_ref, hs_ref,
                       h_scr, pr_scr, pk_scr, pv_scr, plo_scr, *, n_heads, n_seqs):
    L = RW_CHUNK
    W = 2 * RW_HEAD
    n_pairs = n_heads // 2
    c = pl.program_id(1)

    @pl.when(c == 0)
    def _():
        h_scr[...] = jnp.zeros_like(h_scr)
        pr_scr[...] = jnp.zeros_like(pr_scr)
        pk_scr[...] = jnp.zeros_like(pk_scr)
        pv_scr[...] = jnp.zeros_like(pv_scr)
        plo_scr[...] = jnp.zeros_like(plo_scr)

    row1 = lax.broadcasted_iota(jnp.int32, (L, 1), 0)

    def token_shift(cur, prev_scr, mu_ref):
        prev = jnp.where(row1 == 0, prev_scr[...], pltpu.roll(cur, 1, 0))
        prev_scr[...] = cur[L - 1:L, :]
        return cur + (prev - cur) * mu_ref[...]

    tri = (lax.broadcasted_iota(jnp.int32, (L, L), 0)
           >= lax.broadcasted_iota(jnp.int32, (L, L), 1)).astype(BF16)
    first_head = lax.broadcasted_iota(jnp.int32, (L, W), 1) < RW_HEAD

    def head_sum(x):
        s0 = jnp.sum(jnp.where(first_head, x, 0.0), axis=-1, keepdims=True)
        s1 = jnp.sum(jnp.where(first_head, 0.0, x), axis=-1, keepdims=True)
        return jnp.where(first_head, s0, s1)

    def stack_heads(x):
        return jnp.concatenate([jnp.where(first_head, x, 0.0), jnp.where(first_head, 0.0, x)], axis=0)

    ri = lax.broadcasted_iota(jnp.int32, (2 * L, 2 * L), 0)
    ci = lax.broadcasted_iota(jnp.int32, (2 * L, 2 * L), 1)
    t_row, t_col = ri & (L - 1), ci & (L - 1)
    strict = t_row > t_col
    incl = t_row >= t_col
    eye = (ri == ci).astype(F32)
    blk_masks = []
    s = INV_BASE
    while s <= L:
        blk_masks.append((ri // s) == (ci // s))
        s *= 2

    pairs = range(n_pairs)
    loc = [slice(p * W, (p + 1) * W) for p in pairs]

    def seq_phases(si):
        r = token_shift(r_ref[si], pr_scr.at[si], mur_ref)
        k = token_shift(k_ref[si], pk_scr.at[si], muk_ref)
        v = token_shift(v_ref[si], pv_scr.at[si], muv_ref)
        lo = token_shift(lo_ref[si], plo_scr.at[si], mulo_ref)
        logd, a, k2 = _rwkv_token_terms(r, k, lo, w0_ref[...], w2p_ref[...], a0_ref[...], a2p_ref[...],
                                        ka_ref[...])
        kk = k * kk_ref[...]
        hi = logd.astype(BF16)
        rem = logd - hi.astype(F32)
        mid = rem.astype(BF16)
        low = (rem - mid.astype(F32)).astype(BF16)
        lp = (jnp.dot(tri, hi, preferred_element_type=F32)
              + jnp.dot(tri, mid, preferred_element_type=F32)
              + jnp.dot(tri, low, preferred_element_type=F32))
        p_inc = jnp.exp(lp)
        p_exc = jnp.exp(lp - logd)
        p_inv = jnp.exp(-lp)
        yield

        lhs, nm, mm, ab, vs, kb, p_end = [], [], [], [], [], [], []
        for sl in loc:
            kk_p = kk[:, sl]
            kkn = kk_p * lax.rsqrt(jnp.maximum(head_sum(kk_p * kk_p), 1e-24))
            pinc, pinv = p_inc[:, sl], p_inv[:, sl]
            kd = k2[:, sl] * pinv
            bd = kkn * a[:, sl] * pinv
            pe = pinc[L - 1:L, :]
            lhs_p = jnp.concatenate([stack_heads(kkn * p_exc[:, sl]), stack_heads(r[:, sl] * pinc)],
                                    axis=0).astype(BF16)
            rhs_p = jnp.concatenate([stack_heads(bd), stack_heads(kd)], axis=0).astype(BF16)
            amat = lax.dot_general(lhs_p, rhs_p, _NT, preferred_element_type=F32)
            lhs.append(lhs_p)
            nm.append(jnp.where(strict, amat[:2 * L, :2 * L], 0.0))
            mm.append(jnp.where(strict, amat[:2 * L, 2 * L:], 0.0).astype(BF16))
            ab.append(jnp.concatenate([jnp.where(incl, amat[2 * L:, 2 * L:], 0.0),
                                       -jnp.where(incl, amat[2 * L:, :2 * L], 0.0)], axis=1).astype(BF16))
            vs.append(stack_heads(v[:, sl]).astype(BF16))
            kb.append(jnp.concatenate([stack_heads(kd * pe), stack_heads(bd * pe)], axis=0).astype(BF16))
            p_end.append(pe)
            yield

        d = [jnp.where(blk_masks[0], n_p, 0.0).astype(BF16) for n_p in nm]
        x = [eye - d_p.astype(F32) for d_p in d]
        pw = [jnp.dot(d_p, d_p, preferred_element_type=F32) for d_p in d]
        yield
        s = 2
        while s < INV_BASE:
            x = [x_p + _dot(x_p, pw_p) for x_p, pw_p in zip(x, pw)]
            s *= 2
            if s < INV_BASE:
                pw = [_dot(pw_p, pw_p) for pw_p in pw]
            yield
        for lvl in range(1, len(blk_masks)):
            off = blk_masks[lvl] & ~blk_masks[lvl - 1]
            xc = [_dot(x_p, jnp.where(off, n_p, 0.0)) for x_p, n_p in zip(x, nm)]
            yield
            x = [x_p - _dot(xc_p, x_p) for x_p, xc_p in zip(x, xc)]
            yield

        hs = [h_scr[si, p] for p in pairs]
        lh = [jnp.dot(lhs_p, hs_p.astype(BF16), preferred_element_type=F32) for lhs_p, hs_p in zip(lhs, hs)]
        mv = [jnp.dot(mm_p, vs_p, preferred_element_type=F32) for mm_p, vs_p in zip(mm, vs)]
        yield
        u = [_dot(x_p, lh_p[:2 * L] + mv_p).astype(BF16) for x_p, lh_p, mv_p in zip(x, lh, mv)]
        yield
        o_st = [lh_p[2 * L:] + jnp.dot(ab_p, jnp.concatenate([vs_p, u_p], axis=0), preferred_element_type=F32)
                for lh_p, ab_p, vs_p, u_p in zip(lh, ab, vs, u)]
        for p in pairs:
            p_end_col = jnp.sum(eye * p_end[p], axis=1, keepdims=True)
            h_scr[si, p] = p_end_col * hs[p] + lax.dot_general(
                kb[p], jnp.concatenate([vs[p], -u[p]], axis=0), _TN, preferred_element_type=F32)
        yield

        for p, sl in enumerate(loc):
            o = o_st[p][:L] + o_st[p][L:]
            mu = head_sum(o) * (1.0 / RW_HEAD)
            var = head_sum((o - mu) ** 2) * (1.0 / RW_HEAD)
            o = (o - mu) * lax.rsqrt(var + GN_EPS) * gw_ref[:, sl] + gb_ref[:, sl]
            o = o + head_sum(r[:, sl] * k2[:, sl] * rk_ref[:, sl]) * v[:, sl]
            z = z_ref[si, :, LANES + p * W:LANES + (p + 1) * W]
            o_ref[si, :, sl] = (o * _silu(z)).astype(o_ref.dtype)
        yield

    gens = [seq_phases(si) for si in range(n_seqs)]
    started = 0
    live = []
    step = 0
    while started < n_seqs or live:
        if started < n_seqs and step % RW_SEQ_LAG == 0:
            live.append(gens[started])
            started += 1
        for g in list(live):
            if next(g, "done") == "done":
                live.remove(g)
        step += 1

    @pl.when(c == pl.num_programs(1) - 1)
    def _():
        hs_ref[...] = h_scr[...]


def _rwkv_sequence(p_main, p_lz, col, mu, w0, w2p, a0, a2p, k_k, k_a, r_k, gn_w, gn_b):
    b, t, _ = p_main.shape
    wlz = p_lz.shape[-1]
    d = w0.shape[-1]
    n_heads = d // RW_HEAD
    n_pairs, pw = n_heads // 2, 2 * RW_HEAD
    L = RW_CHUNK
    ns = RW_SEQS if b % RW_SEQS == 0 else 1
    blk = lambda cb: pl.BlockSpec((ns, L, d), lambda i, j, cb=cb: (i, j, cb))
    vec = lambda n: pl.BlockSpec((1, n), lambda i, j: (0, 0))
    full = lambda shp: pl.BlockSpec(shp, lambda i, j: (0,) * len(shp))
    mu_r, mu_k, mu_v, mu_lo = mu
    o, hs = pl.pallas_call(
        functools.partial(_rwkv_chunk_kernel, n_heads=n_heads, n_seqs=ns),
        grid=(b // ns, t // L),
        in_specs=[blk(col["r"]), blk(col["k"]), blk(col["v"]),
                  pl.BlockSpec((ns, L, LANES), lambda i, j: (i, j, 0)),
                  pl.BlockSpec((ns, L, wlz), lambda i, j: (i, j, 0)),
                  vec(d), vec(d), vec(d), vec(LANES),
                  vec(d), full((LANES, d)), vec(d), full((LANES, d)),
                  vec(d), vec(d), vec(d), vec(d), vec(d)],
        out_specs=[pl.BlockSpec((ns, L, d), lambda i, j: (i, j, 0)),
                   pl.BlockSpec((ns, n_pairs, pw, pw), lambda i, j: (i, 0, 0, 0))],
        out_shape=(jax.ShapeDtypeStruct((b, t, d), BF16),
                   jax.ShapeDtypeStruct((b, n_pairs, pw, pw), F32)),
        scratch_shapes=[pltpu.VMEM((ns, n_pairs, pw, pw), F32),
                        pltpu.VMEM((ns, 1, d), F32), pltpu.VMEM((ns, 1, d), F32), pltpu.VMEM((ns, 1, d), F32),
                        pltpu.VMEM((ns, 1, LANES), F32)],
        compiler_params=_params("parallel", "arbitrary"),
        name="rwkv_sequence",
    )(p_main, p_main, p_main, p_lz, p_lz,
      mu_r, mu_k, mu_v, mu_lo, w0, w2p, a0, a2p, k_k, k_a, r_k, gn_w, gn_b)
    hs = hs.reshape(b, n_pairs, 2, RW_HEAD, 2, RW_HEAD)
    hs = jnp.stack([hs[:, :, 0, :, 0, :], hs[:, :, 1, :, 1, :]], axis=2)
    return o, jnp.swapaxes(hs.reshape(b, n_heads, RW_HEAD, RW_HEAD), -1, -2)


def _rwkv_step_prep_kernel(cr_ref, ck_ref, cv_ref, clo_ref, cz_ref, pr_ref, pk_ref, pv_ref, plo_ref,
                           mur_ref, muk_ref, muv_ref, mulo_ref,
                           w0_ref, w2p_ref, a0_ref, a2p_ref, kk_ref, ka_ref,
                           r_o, k2_o, v_o, kk_o, a_o, d_o, z_o):
    def lerp(c_ref, p_ref, mu_ref):
        cur = c_ref[...]
        return cur + (p_ref[...] - cur) * mu_ref[...]

    r = lerp(cr_ref, pr_ref, mur_ref)
    k = lerp(ck_ref, pk_ref, muk_ref)
    v = lerp(cv_ref, pv_ref, muv_ref)
    lo = lerp(clo_ref, plo_ref, mulo_ref)
    logd, a, k2 = _rwkv_token_terms(r, k, lo, w0_ref[...], w2p_ref[...], a0_ref[...], a2p_ref[...],
                                    ka_ref[...])
    r_o[...] = r.T
    k2_o[...] = k2.T
    v_o[...] = v.T
    kk_o[...] = (k * kk_ref[...]).T
    a_o[...] = a.T
    d_o[...] = jnp.exp(logd).T
    z_o[...] = cz_ref[:, LANES:].T


def _rwkv_step_kernel(s_ref, r_ref, k2_ref, v_ref, kk_ref, a_ref, d_ref, z_ref,
                      rk_ref, gw_ref, gb_ref, o_ref, s1_ref, o_scr):
    r, k2, v, dec = r_ref[...], k2_ref[...], v_ref[...], d_ref[...]
    kk = kk_ref[...]
    kkn = kk / jnp.maximum(jnp.sqrt(jnp.sum(kk * kk, axis=0, keepdims=True)), 1e-12)
    bvec = kkn * a_ref[...]
    for i in range(RW_HEAD):
        s = s_ref[i]
        sa = jnp.sum(s * kkn, axis=0, keepdims=True)
        s1 = s * dec - sa * bvec + v[i:i + 1, :] * k2
        s1_ref[i] = s1
        o_scr[pl.ds(i, 1), :] = jnp.sum(s1 * r, axis=0, keepdims=True)
    o = o_scr[...]
    mu = jnp.mean(o, axis=0, keepdims=True)
    var = jnp.mean((o - mu) ** 2, axis=0, keepdims=True)
    o = (o - mu) * lax.rsqrt(var + GN_EPS) * gw_ref[...] + gb_ref[...]
    o = o + jnp.sum(r * k2 * rk_ref[...], axis=0, keepdims=True) * v
    o_ref[...] = o * _silu(z_ref[...])


def _rwkv_step(p_rows, p_lz_rows, col, mu, w0, w2p, a0, a2p, k_k, k_a, r_k, gn_w, gn_b, s0):
    rows = s0.shape[0]
    wlz = p_lz_rows.shape[-1]
    d = w0.shape[-1]
    n_heads = d // RW_HEAD
    cur = lambda cb: pl.BlockSpec((rows, d), lambda i, cb=cb: (0, cb))
    prv = lambda cb: pl.BlockSpec((rows, d), lambda i, cb=cb: (1, cb))
    vec = lambda n: pl.BlockSpec((1, n), lambda i: (0, 0))
    mu_r, mu_k, mu_v, mu_lo = mu
    out = jax.ShapeDtypeStruct((d, rows), F32)
    terms = pl.pallas_call(
        _rwkv_step_prep_kernel,
        grid=(1,),
        in_specs=[cur(col["r"]), cur(col["k"]), cur(col["v"]),
                  pl.BlockSpec((rows, LANES), lambda i: (0, 0)),
                  pl.BlockSpec((rows, wlz), lambda i: (0, 0)),
                  prv(col["r"]), prv(col["k"]), prv(col["v"]),
                  pl.BlockSpec((rows, LANES), lambda i: (1, 0)),
                  vec(d), vec(d), vec(d), vec(LANES),
                  vec(d), pl.BlockSpec((LANES, d), lambda i: (0, 0)),
                  vec(d), pl.BlockSpec((LANES, d), lambda i: (0, 0)),
                  vec(d), vec(d)],
        out_specs=[pl.BlockSpec((d, rows), lambda i: (0, 0))] * 7,
        out_shape=(out,) * 7,
        compiler_params=_params("arbitrary"),
        name="rwkv_step_prep",
    )(p_rows, p_rows, p_rows, p_lz_rows, p_lz_rows, p_rows, p_rows, p_rows, p_lz_rows,
      mu_r, mu_k, mu_v, mu_lo, w0, w2p, a0, a2p, k_k, k_a)
    per_h = lambda: pl.BlockSpec((RW_HEAD, rows), lambda h: (h, 0))
    par = lambda: pl.BlockSpec((RW_HEAD, 1), lambda h: (h, 0))
    st = lambda: pl.BlockSpec((None, RW_HEAD, RW_HEAD, rows), lambda h: (h, 0, 0, 0))
    o_t, s1_t = pl.pallas_call(
        _rwkv_step_kernel,
        grid=(n_heads,),
        in_specs=[st()] + [per_h()] * 7 + [par()] * 3,
        out_specs=[per_h(), st()],
        out_shape=(jax.ShapeDtypeStruct((d, rows), F32),
                   jax.ShapeDtypeStruct((n_heads, RW_HEAD, RW_HEAD, rows), F32)),
        scratch_shapes=[pltpu.VMEM((RW_HEAD, rows), F32)],
        compiler_params=_params("parallel"),
        name="rwkv_step",
    )(jnp.transpose(s0, (1, 2, 3, 0)), *terms,
      r_k.reshape(d, 1), gn_w.reshape(d, 1), gn_b.reshape(d, 1))
    return o_t.T.astype(BF16), jnp.transpose(s1_t, (3, 0, 1, 2))


def _out_kernel(os_ref, or_ref, gs_ref, gr_ref, x_ref, gt_ref, w1_ref, w2_ref, fg_ref, y_ref):
    mixed = (gs_ref[...].astype(F32) * jnp.dot(os_ref[...], w1_ref[...], preferred_element_type=F32)
             + gr_ref[...].astype(F32) * jnp.dot(or_ref[...], w2_ref[...], preferred_element_type=F32))
    x = x_ref[...] + gt_ref[...] * mixed
    y_ref[...] = x * lax.rsqrt(jnp.mean(x * x, axis=-1, keepdims=True) + RMS_EPS) * fg_ref[...]


def _out_proj(o_s, o_r, p_gate, col, x, mod, w1, w2, final_g, tt):
    b, t, d = x.shape
    dh = o_s.shape[-1]
    tt = min(tt, t)
    tm = 1 if mod.shape[1] == 1 else tt
    gt_map = (lambda i, j: (i, 0, 2)) if tm == 1 else (lambda i, j: (i, j, 2))
    return pl.pallas_call(
        _out_kernel,
        grid=(b, t // tt),
        in_specs=[pl.BlockSpec((None, tt, dh), lambda i, j: (i, j, 0)),
                  pl.BlockSpec((None, tt, dh), lambda i, j: (i, j, 0)),
                  pl.BlockSpec((None, tt, d), lambda i, j: (i, j, col["g_s5"])),
                  pl.BlockSpec((None, tt, d), lambda i, j: (i, j, col["g_rw"])),
                  pl.BlockSpec((None, tt, d), lambda i, j: (i, j, 0)),
                  pl.BlockSpec((None, tm, d), gt_map),
                  pl.BlockSpec((dh, d), lambda i, j: (0, 0)),
                  pl.BlockSpec((dh, d), lambda i, j: (0, 0)),
                  pl.BlockSpec((1, d), lambda i, j: (0, 0))],
        out_specs=pl.BlockSpec((None, tt, d), lambda i, j: (i, j, 0)),
        out_shape=jax.ShapeDtypeStruct((b, t, d), F32),
        compiler_params=_params("parallel", "parallel"),
        name="out_proj",
    )(o_s, o_r, p_gate, p_gate, x, mod, w1, w2, final_g.reshape(1, d))


def kernel(x_prompt, x_sample, c_prompt, c_sample, state_s5_re, state_s5_im, state_wkv, state_shift, norm_g, w_ada, b_ada, w_in, mu_rw, A_re, A_im, log_step, B_re, B_im, C_re, C_im, D_skip, w_glu, b_glu, w0, w2, a0, a2, k_k, k_a, r_k, gn_w, gn_b, w_out, final_g):
    depth = norm_g.shape[0]
    assert depth == 1
    bp, tp, d = x_prompt.shape
    bs = x_sample.shape[0]
    assert x_sample.shape[1] == 1
    dh = d // 2
    l = 0

    w_main, w_lz, w_gate = 5 * dh, 2 * LORA + dh, 2 * d
    col = {"u": 0, "z_s5": 1, "r": 2, "k": 3, "v": 4, "g_s5": 0, "g_rw": 1}
    w = w_in[l]

    def project(rows, tm):
        return (_in_proj(rows, w, 0, w_main, w_main // 4, tm),
                _in_proj(rows, w, w_main, w_lz, w_lz, tm),
                _in_proj(rows, w, w_main + w_lz, w_gate, w_gate // 4, tm, gate=True))

    mu = mu_rw[l]
    mu_parts = (mu[None, :dh], mu[None, dh:2 * dh], mu[None, 2 * dh:3 * dh], mu[None, 3 * dh:])
    zpad = jnp.zeros((LORA, dh), F32)
    w2p = jnp.concatenate([w2[l], zpad], axis=0).astype(BF16)
    a2p = jnp.concatenate([zpad, a2[l]], axis=0).astype(BF16)
    row = lambda x: x.reshape(1, -1)
    rw_params = (row(w0[l]), w2p, row(a0[l]), a2p, row(k_k[l]), row(k_a[l]), row(r_k[l]),
                 row(gn_w[l]), row(gn_b[l]))
    w_out_bf = w_out[l].astype(BF16)
    wo1, wo2 = w_out_bf[:dh], w_out_bf[dh:]
    w_glu_bf = w_glu[l].astype(BF16)

    ab_re, ab_im, as_re, as_im, bbt_re, bbt_im = _s5_discretise(A_re[l], A_im[l], log_step[l], B_re[l], B_im[l])
    b_w, c_w = _s5_block_weights(bbt_re, bbt_im, C_re[l], C_im[l])
    flat = lambda x: x.reshape(1, -1)
    ab_re, ab_im, as_re, as_im = flat(ab_re), flat(ab_im), flat(as_re), flat(as_im)
    d_skip = flat(D_skip[l])

    mod = _mod(jnp.concatenate([c_prompt, c_sample], axis=0), w_ada[l], b_ada[l])
    mod_p = mod[:bp].reshape(bp, 1, 3 * d)
    mod_s = mod[bp:].reshape(1, bs, 3 * d)

    h_p = _modulated_norm(x_prompt, norm_g[l], mod_p, BF16, 512)
    shift_p = _modulated_norm(x_prompt[:, tp - 1:, :], norm_g[l], mod_p, F32, 1)[:, 0]
    a_p = h_p.reshape(bp * tp, d)
    pm, plz, pg = project(a_p, 1024)
    pm3 = pm.reshape(bp, tp, -1)
    y_s5, xre_p, xim_p = _s5_sequence(pm3, b_w, c_w, ab_re, ab_im, as_re, as_im, d_skip)
    o_s = _glu(y_s5.reshape(bp * tp, dh), pm, col["z_s5"], w_glu_bf, b_glu[l], 512)
    o_r, hs_p = _rwkv_sequence(pm3, plz.reshape(bp, tp, -1), col, mu_parts, *rw_params)
    y_prompt = _out_proj(o_s.reshape(bp, tp, dh), o_r, pg.reshape(bp, tp, -1), col, x_prompt, mod_p,
                         wo1, wo2, final_g, 512)
    g_s5 = A_re.shape[1]
    s5_shape = (1, bp, g_s5, P_S5)
    wkv_p = hs_p[None]

    xs = x_sample.reshape(1, bs, d)
    h_s = _modulated_norm(xs, norm_g[l], mod_s, F32, bs)[0]
    a_s = jnp.concatenate([h_s, state_shift[l]], axis=0).astype(BF16)
    ps, pslz, psg = project(a_s, 2 * bs)
    y_s5s, xre_s, xim_s = _s5_step(ps[:bs], b_w, c_w, ab_re, ab_im, d_skip,
                                   state_s5_re[l].reshape(bs, -1), state_s5_im[l].reshape(bs, -1))
    o_ss = _glu(y_s5s, ps[:bs], col["z_s5"], w_glu_bf, b_glu[l], bs)
    o_rs, wkv_s = _rwkv_step(ps, pslz, col, mu_parts, *rw_params, state_wkv[l])
    y_sample = _out_proj(o_ss[None], o_rs[None], psg[:bs][None], col, xs, mod_s, wo1, wo2, final_g, bs)
    y_sample = y_sample.reshape(bs, 1, d)

    return (y_prompt, y_sample,
            xre_p.reshape(s5_shape), xim_p.reshape(s5_shape), wkv_p, shift_p[None],
            xre_s.reshape(1, bs, g_s5, P_S5), xim_s.reshape(1, bs, g_s5, P_S5), wkv_s[None], h_s[None])
```

```python
import functools

import jax
import jax.numpy as jnp
from jax import lax
from jax.experimental import pallas as pl
from jax.experimental.pallas import tpu as pltpu

F32 = jnp.float32
BF16 = jnp.bfloat16

RMS_EPS = 1e-6
GN_EPS = 64e-5
S5_GROUP = 16
P_S5 = 64
RW_HEAD = 64
LORA = 64
LANES = 128
GROUPS_PER_BLOCK = LANES // S5_GROUP
STATES_PER_BLOCK = GROUPS_PER_BLOCK * P_S5
S5_TC = 256
RW_CHUNK = 64
INV_BASE = 8
RW_SEQS = 4
RW_SEQ_LAG = 1

_NT = (((1,), (1,)), ((), ()))
_TN = (((0,), (0,)), ((), ()))


def _dot(a, b):
    return jnp.dot(a.astype(BF16), b.astype(BF16), preferred_element_type=F32)


def _sigmoid(x):
    return 1.0 / (1.0 + jnp.exp(-x))


def _silu(x):
    return x * _sigmoid(x)


def _params(*sem):
    return pltpu.CompilerParams(dimension_semantics=sem, vmem_limit_bytes=56 * 1024 * 1024)


def _mod_kernel(c_ref, w_ref, b_ref, o_ref):
    o_ref[...] = _dot(_silu(c_ref[...]), w_ref[...]) + b_ref[...]


def _mod(c, w_bf, b):
    rows, d = c.shape
    n = w_bf.shape[1]
    tn = 1024
    return pl.pallas_call(
        _mod_kernel,
        grid=(n // tn,),
        in_specs=[pl.BlockSpec((rows, d), lambda j: (0, 0)),
                  pl.BlockSpec((d, tn), lambda j: (0, j)),
                  pl.BlockSpec((1, tn), lambda j: (0, j))],
        out_specs=pl.BlockSpec((rows, tn), lambda j: (0, j)),
        out_shape=jax.ShapeDtypeStruct((rows, n), F32),
        compiler_params=_params("parallel"),
        name="adaln_mod",
    )(c, w_bf, b.reshape(1, n))


def _h_kernel(x_ref, g_ref, sh_ref, sc_ref, h_ref):
    x = x_ref[...]
    y = x * lax.rsqrt(jnp.mean(x * x, axis=-1, keepdims=True) + RMS_EPS) * g_ref[...]
    h_ref[...] = (y * (1.0 + sc_ref[...]) + sh_ref[...]).astype(h_ref.dtype)


def _modulated_norm(x, g, mod, out_dtype, tt):
    b, t, d = x.shape
    tt = min(tt, t)
    tm = 1 if mod.shape[1] == 1 else tt
    mod_map = (lambda i, j: (i, 0, 0)) if tm == 1 else (lambda i, j: (i, j, 0))
    mod_map1 = (lambda i, j: (i, 0, 1)) if tm == 1 else (lambda i, j: (i, j, 1))
    return pl.pallas_call(
        _h_kernel,
        grid=(b, t // tt),
        in_specs=[pl.BlockSpec((None, tt, d), lambda i, j: (i, j, 0)),
                  pl.BlockSpec((1, d), lambda i, j: (0, 0)),
                  pl.BlockSpec((None, tm, d), mod_map),
                  pl.BlockSpec((None, tm, d), mod_map1)],
        out_specs=pl.BlockSpec((None, tt, d), lambda i, j: (i, j, 0)),
        out_shape=jax.ShapeDtypeStruct((b, t, d), out_dtype),
        compiler_params=_params("parallel", "parallel"),
        name="modulated_norm",
    )(x, g.reshape(1, d), mod, mod)


def _in_proj_kernel(a_ref, w_ref, o_ref, wbf_ref, *, gate):
    @pl.when(pl.program_id(1) == 0)
    def _():
        wbf_ref[...] = w_ref[...].astype(BF16)

    p = jnp.dot(a_ref[...], wbf_ref[...], preferred_element_type=F32)
    o_ref[...] = (_sigmoid(p) if gate else p).astype(o_ref.dtype)


def _in_proj(a, w, col0, width, tn, tm, gate=False):
    m, k = a.shape
    tm = min(tm, m)
    assert width % tn == 0 and m % tm == 0 and col0 % LANES == 0
    return pl.pallas_call(
        functools.partial(_in_proj_kernel, gate=gate),
        grid=(width // tn, m // tm),
        in_specs=[pl.BlockSpec((tm, k), lambda j, i: (i, 0)),
                  pl.BlockSpec((pl.Element(k), pl.Element(tn)), lambda j, i: (0, pl.multiple_of(col0 + j * tn, LANES)))],
        out_specs=pl.BlockSpec((tm, tn), lambda j, i: (i, j)),
        out_shape=jax.ShapeDtypeStruct((m, width), BF16 if gate else F32),
        scratch_shapes=[pltpu.VMEM((k, tn), BF16)],
        compiler_params=_params("parallel", "arbitrary"),
        name="in_proj",
    )(a, w)


def _s5_disc_kernel(are_ref, aim_ref, ls_ref, bre_ref, bim_ref,
                    abr_ref, abi_ref, bbr_ref, bbi_ref):
    step = jnp.exp(ls_ref[...])
    lam_re = jnp.minimum(are_ref[...], -1e-4)
    lam_im = aim_ref[...]
    mag = jnp.exp(lam_re * step)
    ab_re = mag * jnp.cos(lam_im * step)
    ab_im = mag * jnp.sin(lam_im * step)
    den = lam_re * lam_re + lam_im * lam_im
    f_re = ((ab_re - 1.0) * lam_re + ab_im * lam_im) / den
    f_im = (ab_im * lam_re - (ab_re - 1.0) * lam_im) / den
    br, bi = bre_ref[...], bim_ref[...]
    bbr_ref[...] = f_re * br - f_im * bi
    bbi_ref[...] = f_re * bi + f_im * br
    abr_ref[...] = ab_re
    abi_ref[...] = ab_im


def _s5_discretise(a_re, a_im, log_step, b_re, b_im):
    g, p = a_re.shape
    c = b_re.shape[-1]
    gp = jax.ShapeDtypeStruct((g, 1, p), F32)
    gcp = jax.ShapeDtypeStruct((g, c, p), F32)
    return pl.pallas_call(
        _s5_disc_kernel,
        out_shape=(gp, gp, gcp, gcp),
        name="s5_discretise",
    )(a_re.reshape(g, 1, p), a_im.reshape(g, 1, p), log_step.reshape(g, 1, 1),
      jnp.swapaxes(b_re, 1, 2), jnp.swapaxes(b_im, 1, 2))


def _s5_block_weights(bbt_re, bbt_im, c_re, c_im):
    g, c, p = bbt_re.shape
    nb = g // GROUPS_PER_BLOCK
    n_lt = STATES_PER_BLOCK // LANES
    eye = jnp.eye(GROUPS_PER_BLOCK, dtype=F32)

    def b_blk(x):
        x = x.reshape(nb, GROUPS_PER_BLOCK, c, p)
        return jnp.einsum("jacp,ab->jacbp", x, eye).reshape(nb, LANES, n_lt, LANES)

    def c_blk(x):
        x = x.reshape(nb, GROUPS_PER_BLOCK, c, p)
        return jnp.einsum("jbcp,ab->japbc", x, eye).reshape(nb, n_lt, LANES, LANES)

    b_w = jnp.stack([b_blk(bbt_re), b_blk(bbt_im)], axis=3).reshape(nb, LANES, 2 * STATES_PER_BLOCK)
    c_w = jnp.stack([c_blk(c_re), -c_blk(c_im)], axis=2).reshape(nb, 2 * STATES_PER_BLOCK, LANES)
    return b_w.astype(BF16), c_w.astype(BF16)


def _s5_seq_kernel(u0_ref, u1_ref, bw_ref, cw_ref, are_ref, aim_ref, d_ref,
                   y_ref, xre_ref, xim_ref, lhs_ref, bu_ref, y_scr, x_scr, *, tc, n_b):
    n_lt = STATES_PER_BLOCK // LANES
    slots = 2 * n_b
    re_l = lambda lt: slice(2 * lt * LANES, (2 * lt + 1) * LANES)
    im_l = lambda lt: slice((2 * lt + 1) * LANES, (2 * lt + 2) * LANES)
    nat = lambda lt: slice(lt * LANES, (lt + 1) * LANES)
    c = pl.program_id(1)

    @pl.when(c == 0)
    def _():
        x_scr[...] = jnp.zeros_like(x_scr)
        lhs_ref[...] = jnp.zeros_like(lhs_ref)

    for b in range(n_b):
        lhs_ref.at[0][pl.ds(2 * b, tc, stride=slots), :] = u0_ref[b]
        lhs_ref.at[1][pl.ds(2 * b + 1, tc, stride=slots), :] = u1_ref[b]
    u_rows = lhs_ref[0] + lhs_ref[1]
    lhs = jnp.concatenate([lhs_ref[0], lhs_ref[1]], axis=1)
    bu_ref[...] = _dot(lhs, bw_ref[...])

    a_re = [are_ref[:, nat(lt)] for lt in range(n_lt)]
    a_im = [aim_ref[:, nat(lt)] for lt in range(n_lt)]
    x = [(x_scr[:, re_l(lt)], x_scr[:, im_l(lt)]) for lt in range(n_lt)]
    for t in range(tc):
        rows = slice(t * slots, (t + 1) * slots)
        for lt in range(n_lt):
            xr, xi = x[lt]
            nr = a_re[lt] * xr - a_im[lt] * xi + bu_ref[rows, re_l(lt)]
            ni = a_re[lt] * xi + a_im[lt] * xr + bu_ref[rows, im_l(lt)]
            bu_ref[rows, re_l(lt)] = nr
            bu_ref[rows, im_l(lt)] = ni
            x[lt] = (nr, ni)
    for lt in range(n_lt):
        x_scr[:, re_l(lt)] = x[lt][0]
        x_scr[:, im_l(lt)] = x[lt][1]

    yf = _dot(bu_ref[...], cw_ref[...])
    first_half = (lax.broadcasted_iota(jnp.int32, (tc * slots, LANES), 0) & 1) == 0
    y = jnp.where(first_half, yf[:, :LANES], yf[:, LANES:])
    skip = (u_rows.reshape(tc, slots, LANES) * d_ref[...][None]).reshape(tc * slots, LANES)
    y_scr[...] = y + skip
    for b in range(n_b):
        y_ref[b, :, :LANES] = y_scr[pl.ds(2 * b, tc, stride=slots), :]
        y_ref[b, :, LANES:] = y_scr[pl.ds(2 * b + 1, tc, stride=slots), :]

    @pl.when(c == pl.num_programs(1) - 1)
    def _():
        for lt in range(n_lt):
            xre_ref[:, nat(lt)] = x[lt][0]
            xim_ref[:, nat(lt)] = x[lt][1]


def _s5_sequence(p_main, b_w, c_w, ab_re, ab_im, d_skip):
    b, t, _ = p_main.shape
    nb = b_w.shape[0]
    nq = nb // 2
    ns = STATES_PER_BLOCK
    slots = 2 * b
    assert slots == 8, "one 8-row tile must hold every (sequence, half) slot"
    tc = min(S5_TC, t)
    half_rows = lambda x: jnp.stack([x[:nq], x[nq:]], axis=1)
    per_slot = lambda x: jnp.tile(half_rows(x.reshape(nb, -1)), (1, b, 1))
    bw2 = jnp.concatenate([b_w[:nq], b_w[nq:]], axis=1)
    cw2 = jnp.concatenate([c_w[:nq], c_w[nq:]], axis=2)
    slot_vec = lambda n: pl.BlockSpec((None, slots, n), lambda q, c: (q, 0, 0))
    y, xre, xim = pl.pallas_call(
        functools.partial(_s5_seq_kernel, tc=tc, n_b=b),
        grid=(nq, t // tc),
        in_specs=[pl.BlockSpec((b, tc, LANES), lambda q, c: (0, c, q)),
                  pl.BlockSpec((b, tc, LANES), lambda q, c: (0, c, q + nq)),
                  pl.BlockSpec((None, 2 * LANES, 2 * ns), lambda q, c: (q, 0, 0)),
                  pl.BlockSpec((None, 2 * ns, 2 * LANES), lambda q, c: (q, 0, 0)),
                  slot_vec(ns), slot_vec(ns), slot_vec(LANES)],
        out_specs=[pl.BlockSpec((b, tc, 2 * LANES), lambda q, c: (0, c, q)),
                   slot_vec(ns), slot_vec(ns)],
        out_shape=(jax.ShapeDtypeStruct((b, t, nb * LANES), F32),
                   jax.ShapeDtypeStruct((nq, slots, ns), F32),
                   jax.ShapeDtypeStruct((nq, slots, ns), F32)),
        scratch_shapes=[pltpu.VMEM((2, tc * slots, LANES), F32),
                        pltpu.VMEM((tc * slots, 2 * ns), F32),
                        pltpu.VMEM((tc * slots, LANES), F32),
                        pltpu.VMEM((slots, 2 * ns), F32)],
        compiler_params=_params("parallel", "arbitrary"),
        name="s5_sequence",
    )(p_main, p_main, bw2, cw2, per_slot(ab_re), per_slot(ab_im), per_slot(d_skip))

    def state(x):
        x = x.reshape(nq, b, 2, ns)
        return jnp.transpose(x, (1, 2, 0, 3)).reshape(b, nb * ns)

    return y, state(xre), state(xim)


def _s5_step_kernel(u_ref, bw_ref, cw_ref, abr_ref, abi_ref, d_ref, x0r_ref, x0i_ref,
                    y_ref, x1r_ref, x1i_ref):
    n_lt = STATES_PER_BLOCK // LANES
    u = u_ref[...]
    bu = _dot(u, bw_ref[...])
    tiles = []
    for lt in range(n_lt):
        nat = slice(lt * LANES, (lt + 1) * LANES)
        ar, ai = abr_ref[:, nat], abi_ref[:, nat]
        x0r, x0i = x0r_ref[:, nat], x0i_ref[:, nat]
        xr = ar * x0r - ai * x0i + bu[:, 2 * lt * LANES:(2 * lt + 1) * LANES]
        xi = ar * x0i + ai * x0r + bu[:, (2 * lt + 1) * LANES:(2 * lt + 2) * LANES]
        x1r_ref[:, nat] = xr
        x1i_ref[:, nat] = xi
        tiles += [xr, xi]
    y_ref[...] = _dot(jnp.concatenate(tiles, axis=1), cw_ref[...]) + d_ref[...] * u


def _s5_step(p_rows, b_w, c_w, ab_re, ab_im, d_skip, x0_re, x0_im):
    rows = p_rows.shape[0]
    nb = b_w.shape[0]
    ns = STATES_PER_BLOCK
    vec = lambda: pl.BlockSpec((1, ns), lambda j: (0, j))
    st = lambda: pl.BlockSpec((rows, ns), lambda j: (0, j))
    return pl.pallas_call(
        _s5_step_kernel,
        grid=(nb,),
        in_specs=[pl.BlockSpec((rows, LANES), lambda j: (0, j)),
                  pl.BlockSpec((None, LANES, 2 * ns), lambda j: (j, 0, 0)),
                  pl.BlockSpec((None, 2 * ns, LANES), lambda j: (j, 0, 0)),
                  vec(), vec(),
                  pl.BlockSpec((1, LANES), lambda j: (0, j)),
                  st(), st()],
        out_specs=[pl.BlockSpec((rows, LANES), lambda j: (0, j)), st(), st()],
        out_shape=(jax.ShapeDtypeStruct((rows, nb * LANES), F32),
                   jax.ShapeDtypeStruct((rows, nb * ns), F32),
                   jax.ShapeDtypeStruct((rows, nb * ns), F32)),
        compiler_params=_params("parallel"),
        name="s5_step",
    )(p_rows, b_w, c_w, ab_re, ab_im, d_skip, x0_re, x0_im)


def _glu_kernel(y_ref, z_ref, w_ref, b_ref, o_ref, *, paired_blocks):
    y = y_ref[...]
    if paired_blocks:
        nb = y.shape[1] // LANES
        pos = [2 * j if j < nb // 2 else 2 * (j - nb // 2) + 1 for j in range(nb)]
        y = jnp.concatenate([y[:, p * LANES:(p + 1) * LANES] for p in pos], axis=1)
    y = jax.nn.gelu(y, approximate=True)
    gate = _sigmoid(_dot(y, w_ref[...]) + b_ref[...])
    o_ref[...] = (y * gate * _silu(z_ref[...])).astype(o_ref.dtype)


def _glu(y, p_rows, z_block, w_bf, b, tm, paired_blocks=False):
    rows, d = y.shape
    tm = min(tm, rows)
    return pl.pallas_call(
        functools.partial(_glu_kernel, paired_blocks=paired_blocks),
        grid=(rows // tm,),
        in_specs=[pl.BlockSpec((tm, d), lambda i: (i, 0)),
                  pl.BlockSpec((tm, d), lambda i: (i, z_block)),
                  pl.BlockSpec((d, d), lambda i: (0, 0)),
                  pl.BlockSpec((1, d), lambda i: (0, 0))],
        out_specs=pl.BlockSpec((tm, d), lambda i: (i, 0)),
        out_shape=jax.ShapeDtypeStruct((rows, d), BF16),
        compiler_params=_params("parallel"),
        name="s5_glu",
    )(y, p_rows, w_bf, b.reshape(1, d))


def _softplus(x):
    return jnp.maximum(x, 0.0) + jnp.log(1.0 + jnp.exp(-jnp.abs(x)))


def _rwkv_token_terms(r, k, lo, w0, w2p, a0, a2p, k_a):
    w = -_softplus(-(w0 + _dot(jnp.tanh(lo), w2p))) - 0.5
    logd = -jnp.exp(w)
    a = _sigmoid(a0 + _dot(lo, a2p))
    k2 = k * (1.0 + (a - 1.0) * k_a)
    return logd, a, k2


def _rwkv_chunk_kernel(r_ref, k_ref, v_ref, lo_ref, z_ref,
                       mur_ref, muk_ref, muv_ref, mulo_ref,
                       w0_ref, w2p_ref, a0_ref, a2p_ref, kk_ref, ka_ref, rk_ref, gw_ref, gb_ref,
                       o_ref, hs_ref,
                       h_scr, pr_scr, pk_scr, pv_scr, plo_scr, *, n_heads, n_seqs):
    L = RW_CHUNK
    W = 2 * RW_HEAD
    n_pairs = n_heads // 2
    c = pl.program_id(1)

    @pl.when(c == 0)
    def _():
        h_scr[...] = jnp.zeros_like(h_scr)
        pr_scr[...] = jnp.zeros_like(pr_scr)
        pk_scr[...] = jnp.zeros_like(pk_scr)
        pv_scr[...] = jnp.zeros_like(pv_scr)
        plo_scr[...] = jnp.zeros_like(plo_scr)

    row1 = lax.broadcasted_iota(jnp.int32, (L, 1), 0)

    def token_shift(cur, prev_scr, mu_ref):
        prev = jnp.where(row1 == 0, prev_scr[...], pltpu.roll(cur, 1, 0))
        prev_scr[...] = cur[L - 1:L, :]
        return cur + (prev - cur) * mu_ref[...]

    tri = (lax.broadcasted_iota(jnp.int32, (L, L), 0)
           >= lax.broadcasted_iota(jnp.int32, (L, L), 1)).astype(BF16)
    first_head = lax.broadcasted_iota(jnp.int32, (L, W), 1) < RW_HEAD

    def head_sum(x):
        s0 = jnp.sum(jnp.where(first_head, x, 0.0), axis=-1, keepdims=True)
        s1 = jnp.sum(jnp.where(first_head, 0.0, x), axis=-1, keepdims=True)
        return jnp.where(first_head, s0, s1)

    def stack_heads(x):
        return jnp.concatenate([jnp.where(first_head, x, 0.0), jnp.where(first_head, 0.0, x)], axis=0)

    ri = lax.broadcasted_iota(jnp.int32, (2 * L, 2 * L), 0)
    ci = lax.broadcasted_iota(jnp.int32, (2 * L, 2 * L), 1)
    t_row, t_col = ri & (L - 1), ci & (L - 1)
    strict = t_row > t_col
    incl = t_row >= t_col
    eye = (ri == ci).astype(F32)
    blk_masks = []
    s = INV_BASE
    while s <= L:
        blk_masks.append((ri // s) == (ci // s))
        s *= 2

    pairs = range(n_pairs)
    loc = [slice(p * W, (p + 1) * W) for p in pairs]

    def seq_phases(si):
        r = token_shift(r_ref[si], pr_scr.at[si], mur_ref)
        k = token_shift(k_ref[si], pk_scr.at[si], muk_ref)
        v = token_shift(v_ref[si], pv_scr.at[si], muv_ref)
        lo = token_shift(lo_ref[si], plo_scr.at[si], mulo_ref)
        logd, a, k2 = _rwkv_token_terms(r, k, lo, w0_ref[...], w2p_ref[...], a0_ref[...], a2p_ref[...],
                                        ka_ref[...])
        kk = k * kk_ref[...]
        hi = logd.astype(BF16)
        rem = logd - hi.astype(F32)
        mid = rem.astype(BF16)
        low = (rem - mid.astype(F32)).astype(BF16)
        lp = (jnp.dot(tri, hi, preferred_element_type=F32)
              + jnp.dot(tri, mid, preferred_element_type=F32)
              + jnp.dot(tri, low, preferred_element_type=F32))
        p_inc = jnp.exp(lp)
        p_exc = jnp.exp(lp - logd)
        p_inv = jnp.exp(-lp)
        yield

        lhs, nm, mm, ab, vs, kb, p_end = [], [], [], [], [], [], []
        for sl in loc:
            kk_p = kk[:, sl]
            kkn = kk_p * lax.rsqrt(jnp.maximum(head_sum(kk_p * kk_p), 1e-24))
            pinc, pinv = p_inc[:, sl], p_inv[:, sl]
            kd = k2[:, sl] * pinv
            bd = kkn * a[:, sl] * pinv
            pe = pinc[L - 1:L, :]
            lhs_p = jnp.concatenate([stack_heads(kkn * p_exc[:, sl]), stack_heads(r[:, sl] * pinc)],
                                    axis=0).astype(BF16)
            rhs_p = jnp.concatenate([stack_heads(bd), stack_heads(kd)], axis=0).astype(BF16)
            amat = lax.dot_general(lhs_p, rhs_p, _NT, preferred_element_type=F32)
            lhs.append(lhs_p)
            nm.append(jnp.where(strict, amat[:2 * L, :2 * L], 0.0))
            mm.append(jnp.where(strict, amat[:2 * L, 2 * L:], 0.0).astype(BF16))
            ab.append(jnp.concatenate([jnp.where(incl, amat[2 * L:, 2 * L:], 0.0),
                                       -jnp.where(incl, amat[2 * L:, :2 * L], 0.0)], axis=1).astype(BF16))
            vs.append(stack_heads(v[:, sl]).astype(BF16))
            kb.append(jnp.concatenate([stack_heads(kd * pe), stack_heads(bd * pe)], axis=0).astype(BF16))
            p_end.append(pe)
            yield

        d = [jnp.where(blk_masks[0], n_p, 0.0).astype(BF16) for n_p in nm]
        x = [eye - d_p.astype(F32) for d_p in d]
        pw = [jnp.dot(d_p, d_p, preferred_element_type=F32) for d_p in d]
        yield
        s = 2
        while s < INV_BASE:
            x = [x_p + _dot(x_p, pw_p) for x_p, pw_p in zip(x, pw)]
            s *= 2
            if s < INV_BASE:
                pw = [_dot(pw_p, pw_p) for pw_p in pw]
            yield
        for lvl in range(1, len(blk_masks)):
            off = blk_masks[lvl] & ~blk_masks[lvl - 1]
            xc = [_dot(x_p, jnp.where(off, n_p, 0.0)) for x_p, n_p in zip(x, nm)]
            yield
            x = [x_p - _dot(xc_p, x_p) for x_p, xc_p in zip(x, xc)]
            yield

        hs = [h_scr[si, p] for p in pairs]
        lh = [jnp.dot(lhs_p, hs_p.astype(BF16), preferred_element_type=F32) for lhs_p, hs_p in zip(lhs, hs)]
        mv = [jnp.dot(mm_p, vs_p, preferred_element_type=F32) for mm_p, vs_p in zip(mm, vs)]
        yield
        u = [_dot(x_p, lh_p[:2 * L] + mv_p).astype(BF16) for x_p, lh_p, mv_p in zip(x, lh, mv)]
        yield
        o_st = [lh_p[2 * L:] + jnp.dot(ab_p, jnp.concatenate([vs_p, u_p], axis=0), preferred_element_type=F32)
                for lh_p, ab_p, vs_p, u_p in zip(lh, ab, vs, u)]
        for p in pairs:
            p_end_col = jnp.sum(eye * p_end[p], axis=1, keepdims=True)
            h_scr[si, p] = p_end_col * hs[p] + lax.dot_general(
                kb[p], jnp.concatenate([vs[p], -u[p]], axis=0), _TN, preferred_element_type=F32)
        yield

        for p, sl in enumerate(loc):
            o = o_st[p][:L] + o_st[p][L:]
            mu = head_sum(o) * (1.0 / RW_HEAD)
            var = head_sum((o - mu) ** 2) * (1.0 / RW_HEAD)
            o = (o - mu) * lax.rsqrt(var + GN_EPS) * gw_ref[:, sl] + gb_ref[:, sl]
            o = o + head_sum(r[:, sl] * k2[:, sl] * rk_ref[:, sl]) * v[:, sl]
            z = z_ref[si, :, LANES + p * W:LANES + (p + 1) * W]
            o_ref[si, :, sl] = (o * _silu(z)).astype(o_ref.dtype)
        yield

    gens = [seq_phases(si) for si in range(n_seqs)]
    started = 0
    live = []
    step = 0
    while started < n_seqs or live:
        if started < n_seqs and step % RW_SEQ_LAG == 0:
            live.append(gens[started])
            started += 1
        for g in list(live):
            if next(g, "done") == "done":
                live.remove(g)
        step += 1

    @pl.when(c == pl.num_programs(1) - 1)
    def _():
        hs_ref[...] = h_scr[...]


def _rwkv_sequence(p_main, p_lz, col, mu, w0, w2p, a0, a2p, k_k, k_a, r_k, gn_w, gn_b):
    b, t, _ = p_main.shape
    wlz = p_lz.shape[-1]
    d = w0.shape[-1]
    n_heads = d // RW_HEAD
    n_pairs, pw = n_heads // 2, 2 * RW_HEAD
    L = RW_CHUNK
    ns = RW_SEQS if b % RW_SEQS == 0 else 1
    blk = lambda cb: pl.BlockSpec((ns, L, d), lambda i, j, cb=cb: (i, j, cb))
    vec = lambda n: pl.BlockSpec((1, n), lambda i, j: (0, 0))
    full = lambda shp: pl.BlockSpec(shp, lambda i, j: (0,) * len(shp))
    mu_r, mu_k, mu_v, mu_lo = mu
    o, hs = pl.pallas_call(
        functools.partial(_rwkv_chunk_kernel, n_heads=n_heads, n_seqs=ns),
        grid=(b // ns, t // L),
        in_specs=[blk(col["r"]), blk(col["k"]), blk(col["v"]),
                  pl.BlockSpec((ns, L, LANES), lambda i, j: (i, j, 0)),
                  pl.BlockSpec((ns, L, wlz), lambda i, j: (i, j, 0)),
                  vec(d), vec(d), vec(d), vec(LANES),
                  vec(d), full((LANES, d)), vec(d), full((LANES, d)),
                  vec(d), vec(d), vec(d), vec(d), vec(d)],
        out_specs=[pl.BlockSpec((ns, L, d), lambda i, j: (i, j, 0)),
                   pl.BlockSpec((ns, n_pairs, pw, pw), lambda i, j: (i, 0, 0, 0))],
        out_shape=(jax.ShapeDtypeStruct((b, t, d), BF16),
                   jax.ShapeDtypeStruct((b, n_pairs, pw, pw), F32)),
        scratch_shapes=[pltpu.VMEM((ns, n_pairs, pw, pw), F32),
                        pltpu.VMEM((ns, 1, d), F32), pltpu.VMEM((ns, 1, d), F32), pltpu.VMEM((ns, 1, d), F32),
                        pltpu.VMEM((ns, 1, LANES), F32)],
        compiler_params=_params("parallel", "arbitrary"),
        name="rwkv_sequence",
    )(p_main, p_main, p_main, p_lz, p_lz,
      mu_r, mu_k, mu_v, mu_lo, w0, w2p, a0, a2p, k_k, k_a, r_k, gn_w, gn_b)
    hs = hs.reshape(b, n_pairs, 2, RW_HEAD, 2, RW_HEAD)
    hs = jnp.stack([hs[:, :, 0, :, 0, :], hs[:, :, 1, :, 1, :]], axis=2)
    return o, jnp.swapaxes(hs.reshape(b, n_heads, RW_HEAD, RW_HEAD), -1, -2)


def _rwkv_step_prep_kernel(cr_ref, ck_ref, cv_ref, clo_ref, cz_ref, pr_ref, pk_ref, pv_ref, plo_ref,
                           mur_ref, muk_ref, muv_ref, mulo_ref,
                           w0_ref, w2p_ref, a0_ref, a2p_ref, kk_ref, ka_ref,
                           r_o, k2_o, v_o, kk_o, a_o, d_o, z_o):
    def lerp(c_ref, p_ref, mu_ref):
        cur = c_ref[...]
        return cur + (p_ref[...] - cur) * mu_ref[...]

    r = lerp(cr_ref, pr_ref, mur_ref)
    k = lerp(ck_ref, pk_ref, muk_ref)
    v = lerp(cv_ref, pv_ref, muv_ref)
    lo = lerp(clo_ref, plo_ref, mulo_ref)
    logd, a, k2 = _rwkv_token_terms(r, k, lo, w0_ref[...], w2p_ref[...], a0_ref[...], a2p_ref[...],
                                    ka_ref[...])
    r_o[...] = r.T
    k2_o[...] = k2.T
    v_o[...] = v.T
    kk_o[...] = (k * kk_ref[...]).T
    a_o[...] = a.T
    d_o[...] = jnp.exp(logd).T
    z_o[...] = cz_ref[:, LANES:].T


def _rwkv_step_kernel(s_ref, r_ref, k2_ref, v_ref, kk_ref, a_ref, d_ref, z_ref,
                      rk_ref, gw_ref, gb_ref, o_ref, s1_ref, o_scr):
    r, k2, v, dec = r_ref[...], k2_ref[...], v_ref[...], d_ref[...]
    kk = kk_ref[...]
    kkn = kk / jnp.maximum(jnp.sqrt(jnp.sum(kk * kk, axis=0, keepdims=True)), 1e-12)
    bvec = kkn * a_ref[...]
    for i in range(RW_HEAD):
        s = s_ref[i]
        sa = jnp.sum(s * kkn, axis=0, keepdims=True)
        s1 = s * dec - sa * bvec + v[i:i + 1, :] * k2
        s1_ref[i] = s1
        o_scr[pl.ds(i, 1), :] = jnp.sum(s1 * r, axis=0, keepdims=True)
    o = o_scr[...]
    mu = jnp.mean(o, axis=0, keepdims=True)
    var = jnp.mean((o - mu) ** 2, axis=0, keepdims=True)
    o = (o - mu) * lax.rsqrt(var + GN_EPS) * gw_ref[...] + gb_ref[...]
    o = o + jnp.sum(r * k2 * rk_ref[...], axis=0, keepdims=True) * v
    o_ref[...] = o * _silu(z_ref[...])


def _rwkv_step(p_rows, p_lz_rows, col, mu, w0, w2p, a0, a2p, k_k, k_a, r_k, gn_w, gn_b, s0):
    rows = s0.shape[0]
    wlz = p_lz_rows.shape[-1]
    d = w0.shape[-1]
    n_heads = d // RW_HEAD
    cur = lambda cb: pl.BlockSpec((rows, d), lambda i, cb=cb: (0, cb))
    prv = lambda cb: pl.BlockSpec((rows, d), lambda i, cb=cb: (1, cb))
    vec = lambda n: pl.BlockSpec((1, n), lambda i: (0, 0))
    mu_r, mu_k, mu_v, mu_lo = mu
    out = jax.ShapeDtypeStruct((d, rows), F32)
    terms = pl.pallas_call(
        _rwkv_step_prep_kernel,
        grid=(1,),
        in_specs=[cur(col["r"]), cur(col["k"]), cur(col["v"]),
                  pl.BlockSpec((rows, LANES), lambda i: (0, 0)),
                  pl.BlockSpec((rows, wlz), lambda i: (0, 0)),
                  prv(col["r"]), prv(col["k"]), prv(col["v"]),
                  pl.BlockSpec((rows, LANES), lambda i: (1, 0)),
                  vec(d), vec(d), vec(d), vec(LANES),
                  vec(d), pl.BlockSpec((LANES, d), lambda i: (0, 0)),
                  vec(d), pl.BlockSpec((LANES, d), lambda i: (0, 0)),
                  vec(d), vec(d)],
        out_specs=[pl.BlockSpec((d, rows), lambda i: (0, 0))] * 7,
        out_shape=(out,) * 7,
        compiler_params=_params("arbitrary"),
        name="rwkv_step_prep",
    )(p_rows, p_rows, p_rows, p_lz_rows, p_lz_rows, p_rows, p_rows, p_rows, p_lz_rows,
      mu_r, mu_k, mu_v, mu_lo, w0, w2p, a0, a2p, k_k, k_a)
    per_h = lambda: pl.BlockSpec((RW_HEAD, rows), lambda h: (h, 0))
    par = lambda: pl.BlockSpec((RW_HEAD, 1), lambda h: (h, 0))
    st = lambda: pl.BlockSpec((None, RW_HEAD, RW_HEAD, rows), lambda h: (h, 0, 0, 0))
    o_t, s1_t = pl.pallas_call(
        _rwkv_step_kernel,
        grid=(n_heads,),
        in_specs=[st()] + [per_h()] * 7 + [par()] * 3,
        out_specs=[per_h(), st()],
        out_shape=(jax.ShapeDtypeStruct((d, rows), F32),
                   jax.ShapeDtypeStruct((n_heads, RW_HEAD, RW_HEAD, rows), F32)),
        scratch_shapes=[pltpu.VMEM((RW_HEAD, rows), F32)],
        compiler_params=_params("parallel"),
        name="rwkv_step",
    )(jnp.transpose(s0, (1, 2, 3, 0)), *terms,
      r_k.reshape(d, 1), gn_w.reshape(d, 1), gn_b.reshape(d, 1))
    return o_t.T.astype(BF16), jnp.transpose(s1_t, (3, 0, 1, 2))


def _out_kernel(os_ref, or_ref, gs_ref, gr_ref, x_ref, gt_ref, w1_ref, w2_ref, fg_ref, y_ref):
    mixed = (gs_ref[...].astype(F32) * jnp.dot(os_ref[...], w1_ref[...], preferred_element_type=F32)
             + gr_ref[...].astype(F32) * jnp.dot(or_ref[...], w2_ref[...], preferred_element_type=F32))
    x = x_ref[...] + gt_ref[...] * mixed
    y_ref[...] = x * lax.rsqrt(jnp.mean(x * x, axis=-1, keepdims=True) + RMS_EPS) * fg_ref[...]


def _out_proj(o_s, o_r, p_gate, col, x, mod, w1, w2, final_g, tt):
    b, t, d = x.shape
    dh = o_s.shape[-1]
    tt = min(tt, t)
    tm = 1 if mod.shape[1] == 1 else tt
    gt_map = (lambda i, j: (i, 0, 2)) if tm == 1 else (lambda i, j: (i, j, 2))
    return pl.pallas_call(
        _out_kernel,
        grid=(b, t // tt),
        in_specs=[pl.BlockSpec((None, tt, dh), lambda i, j: (i, j, 0)),
                  pl.BlockSpec((None, tt, dh), lambda i, j: (i, j, 0)),
                  pl.BlockSpec((None, tt, d), lambda i, j: (i, j, col["g_s5"])),
                  pl.BlockSpec((None, tt, d), lambda i, j: (i, j, col["g_rw"])),
                  pl.BlockSpec((None, tt, d), lambda i, j: (i, j, 0)),
                  pl.BlockSpec((None, tm, d), gt_map),
                  pl.BlockSpec((dh, d), lambda i, j: (0, 0)),
                  pl.BlockSpec((dh, d), lambda i, j: (0, 0)),
                  pl.BlockSpec((1, d), lambda i, j: (0, 0))],
        out_specs=pl.BlockSpec((None, tt, d), lambda i, j: (i, j, 0)),
        out_shape=jax.ShapeDtypeStruct((b, t, d), F32),
        compiler_params=_params("parallel", "parallel"),
        name="out_proj",
    )(o_s, o_r, p_gate, p_gate, x, mod, w1, w2, final_g.reshape(1, d))


def kernel(x_prompt, x_sample, c_prompt, c_sample, state_s5_re, state_s5_im, state_wkv, state_shift, norm_g, w_ada, b_ada, w_in, mu_rw, A_re, A_im, log_step, B_re, B_im, C_re, C_im, D_skip, w_glu, b_glu, w0, w2, a0, a2, k_k, k_a, r_k, gn_w, gn_b, w_out, final_g):
    depth = norm_g.shape[0]
    assert depth == 1
    bp, tp, d = x_prompt.shape
    bs = x_sample.shape[0]
    assert x_sample.shape[1] == 1
    dh = d // 2
    l = 0

    w_main, w_lz, w_gate = 5 * dh, 2 * LORA + dh, 2 * d
    col = {"u": 0, "z_s5": 1, "r": 2, "k": 3, "v": 4, "g_s5": 0, "g_rw": 1}
    w = w_in[l]

    def project(rows, tm):
        return (_in_proj(rows, w, 0, w_main, w_main // 4, tm),
                _in_proj(rows, w, w_main, w_lz, w_lz, tm),
                _in_proj(rows, w, w_main + w_lz, w_gate, w_gate // 4, tm, gate=True))

    mu = mu_rw[l]
    mu_parts = (mu[None, :dh], mu[None, dh:2 * dh], mu[None, 2 * dh:3 * dh], mu[None, 3 * dh:])
    zpad = jnp.zeros((LORA, dh), F32)
    w2p = jnp.concatenate([w2[l], zpad], axis=0).astype(BF16)
    a2p = jnp.concatenate([zpad, a2[l]], axis=0).astype(BF16)
    row = lambda x: x.reshape(1, -1)
    rw_params = (row(w0[l]), w2p, row(a0[l]), a2p, row(k_k[l]), row(k_a[l]), row(r_k[l]),
                 row(gn_w[l]), row(gn_b[l]))
    w_out_bf = w_out[l].astype(BF16)
    wo1, wo2 = w_out_bf[:dh], w_out_bf[dh:]
    w_glu_bf = w_glu[l].astype(BF16)

    ab_re, ab_im, bbt_re, bbt_im = _s5_discretise(A_re[l], A_im[l], log_step[l], B_re[l], B_im[l])
    b_w, c_w = _s5_block_weights(bbt_re, bbt_im, C_re[l], C_im[l])
    flat = lambda x: x.reshape(1, -1)
    ab_re, ab_im = flat(ab_re), flat(ab_im)
    d_skip = flat(D_skip[l])

    mod = _mod(jnp.concatenate([c_prompt, c_sample], axis=0), w_ada[l], b_ada[l])
    mod_p = mod[:bp].reshape(bp, 1, 3 * d)
    mod_s = mod[bp:].reshape(1, bs, 3 * d)

    h_p = _modulated_norm(x_prompt, norm_g[l], mod_p, BF16, 512)
    shift_p = _modulated_norm(x_prompt[:, tp - 1:, :], norm_g[l], mod_p, F32, 1)[:, 0]
    a_p = h_p.reshape(bp * tp, d)
    pm, plz, pg = project(a_p, 1024)
    pm3 = pm.reshape(bp, tp, -1)
    y_s5, xre_p, xim_p = _s5_sequence(pm3, b_w, c_w, ab_re, ab_im, d_skip)
    o_s = _glu(y_s5.reshape(bp * tp, dh), pm, col["z_s5"], w_glu_bf, b_glu[l], 512, paired_blocks=True)
    o_r, hs_p = _rwkv_sequence(pm3, plz.reshape(bp, tp, -1), col, mu_parts, *rw_params)
    y_prompt = _out_proj(o_s.reshape(bp, tp, dh), o_r, pg.reshape(bp, tp, -1), col, x_prompt, mod_p,
                         wo1, wo2, final_g, 512)
    g_s5 = A_re.shape[1]
    s5_shape = (1, bp, g_s5, P_S5)
    wkv_p = hs_p[None]

    xs = x_sample.reshape(1, bs, d)
    h_s = _modulated_norm(xs, norm_g[l], mod_s, F32, bs)[0]
    a_s = jnp.concatenate([h_s, state_shift[l]], axis=0).astype(BF16)
    ps, pslz, psg = project(a_s, 2 * bs)
    y_s5s, xre_s, xim_s = _s5_step(ps[:bs], b_w, c_w, ab_re, ab_im, d_skip,
                                   state_s5_re[l].reshape(bs, -1), state_s5_im[l].reshape(bs, -1))
    o_ss = _glu(y_s5s, ps[:bs], col["z_s5"], w_glu_bf, b_glu[l], bs)
    o_rs, wkv_s = _rwkv_step(ps, pslz, col, mu_parts, *rw_params, state_wkv[l])
    y_sample = _out_proj(o_ss[None], o_rs[None], psg[:bs][None], col, xs, mod_s, wo1, wo2, final_g, bs)
    y_sample = y_sample.reshape(bs, 1, d)

    return (y_prompt, y_sample,
            xre_p.reshape(s5_shape), xim_p.reshape(s5_shape), wkv_p, shift_p[None],
            xre_s.reshape(1, bs, g_s5, P_S5), xim_s.reshape(1, bs, g_s5, P_S5), wkv_s[None], h_s[None])
```

```python
import functools

import jax
import jax.numpy as jnp
from jax import lax
from jax.experimental import pallas as pl
from jax.experimental.pallas import tpu as pltpu

F32 = jnp.float32
BF16 = jnp.bfloat16

RMS_EPS = 1e-6
GN_EPS = 64e-5
S5_GROUP = 16
P_S5 = 64
RW_HEAD = 64
LORA = 64
LANES = 128
GROUPS_PER_BLOCK = LANES // S5_GROUP
STATES_PER_BLOCK = GROUPS_PER_BLOCK * P_S5
S5_TC = 256
RW_CHUNK = 64
INV_BASE = 8
RW_SEQS = 4
RW_SEQ_LAG = 1

_NT = (((1,), (1,)), ((), ()))
_TN = (((0,), (0,)), ((), ()))


def _dot(a, b):
    return jnp.dot(a.astype(BF16), b.astype(BF16), preferred_element_type=F32)


def _sigmoid(x):
    return 1.0 / (1.0 + jnp.exp(-x))


def _silu(x):
    return x * _sigmoid(x)


def _params(*sem):
    return pltpu.CompilerParams(dimension_semantics=sem, vmem_limit_bytes=56 * 1024 * 1024)


def _mod_kernel(c_ref, w_ref, b_ref, o_ref):
    o_ref[...] = _dot(_silu(c_ref[...]), w_ref[...]) + b_ref[...]


def _mod(c, w_bf, b):
    rows, d = c.shape
    n = w_bf.shape[1]
    tn = 1024
    return pl.pallas_call(
        _mod_kernel,
        grid=(n // tn,),
        in_specs=[pl.BlockSpec((rows, d), lambda j: (0, 0)),
                  pl.BlockSpec((d, tn), lambda j: (0, j)),
                  pl.BlockSpec((1, tn), lambda j: (0, j))],
        out_specs=pl.BlockSpec((rows, tn), lambda j: (0, j)),
        out_shape=jax.ShapeDtypeStruct((rows, n), F32),
        compiler_params=_params("parallel"),
        name="adaln_mod",
    )(c, w_bf, b.reshape(1, n))


def _h_kernel(x_ref, g_ref, sh_ref, sc_ref, h_ref):
    x = x_ref[...]
    y = x * lax.rsqrt(jnp.mean(x * x, axis=-1, keepdims=True) + RMS_EPS) * g_ref[...]
    h_ref[...] = (y * (1.0 + sc_ref[...]) + sh_ref[...]).astype(h_ref.dtype)


def _modulated_norm(x, g, mod, out_dtype, tt):
    b, t, d = x.shape
    tt = min(tt, t)
    tm = 1 if mod.shape[1] == 1 else tt
    mod_map = (lambda i, j: (i, 0, 0)) if tm == 1 else (lambda i, j: (i, j, 0))
    mod_map1 = (lambda i, j: (i, 0, 1)) if tm == 1 else (lambda i, j: (i, j, 1))
    return pl.pallas_call(
        _h_kernel,
        grid=(b, t // tt),
        in_specs=[pl.BlockSpec((None, tt, d), lambda i, j: (i, j, 0)),
                  pl.BlockSpec((1, d), lambda i, j: (0, 0)),
                  pl.BlockSpec((None, tm, d), mod_map),
                  pl.BlockSpec((None, tm, d), mod_map1)],
        out_specs=pl.BlockSpec((None, tt, d), lambda i, j: (i, j, 0)),
        out_shape=jax.ShapeDtypeStruct((b, t, d), out_dtype),
        compiler_params=_params("parallel", "parallel"),
        name="modulated_norm",
    )(x, g.reshape(1, d), mod, mod)


def _in_proj_kernel(a_ref, w_ref, o_ref, wbf_ref, *, gate):
    @pl.when(pl.program_id(1) == 0)
    def _():
        wbf_ref[...] = w_ref[...].astype(BF16)

    p = jnp.dot(a_ref[...], wbf_ref[...], preferred_element_type=F32)
    o_ref[...] = (_sigmoid(p) if gate else p).astype(o_ref.dtype)


def _in_proj(a, w, col0, width, tn, tm, gate=False):
    m, k = a.shape
    tm = min(tm, m)
    assert width % tn == 0 and m % tm == 0 and col0 % LANES == 0
    return pl.pallas_call(
        functools.partial(_in_proj_kernel, gate=gate),
        grid=(width // tn, m // tm),
        in_specs=[pl.BlockSpec((tm, k), lambda j, i: (i, 0)),
                  pl.BlockSpec((pl.Element(k), pl.Element(tn)), lambda j, i: (0, pl.multiple_of(col0 + j * tn, LANES)))],
        out_specs=pl.BlockSpec((tm, tn), lambda j, i: (i, j)),
        out_shape=jax.ShapeDtypeStruct((m, width), BF16 if gate else F32),
        scratch_shapes=[pltpu.VMEM((k, tn), BF16)],
        compiler_params=_params("parallel", "arbitrary"),
        name="in_proj",
    )(a, w)


def _s5_disc_kernel(are_ref, aim_ref, ls_ref, bre_ref, bim_ref,
                    abr_ref, abi_ref, bbr_ref, bbi_ref):
    step = jnp.exp(ls_ref[...])
    lam_re = jnp.minimum(are_ref[...], -1e-4)
    lam_im = aim_ref[...]
    mag = jnp.exp(lam_re * step)
    ab_re = mag * jnp.cos(lam_im * step)
    ab_im = mag * jnp.sin(lam_im * step)
    den = lam_re * lam_re + lam_im * lam_im
    f_re = ((ab_re - 1.0) * lam_re + ab_im * lam_im) / den
    f_im = (ab_im * lam_re - (ab_re - 1.0) * lam_im) / den
    br, bi = bre_ref[...], bim_ref[...]
    bbr_ref[...] = f_re * br - f_im * bi
    bbi_ref[...] = f_re * bi + f_im * br
    abr_ref[...] = ab_re
    abi_ref[...] = ab_im


def _s5_discretise(a_re, a_im, log_step, b_re, b_im):
    g, p = a_re.shape
    c = b_re.shape[-1]
    gp = jax.ShapeDtypeStruct((g, 1, p), F32)
    gcp = jax.ShapeDtypeStruct((g, c, p), F32)
    return pl.pallas_call(
        _s5_disc_kernel,
        out_shape=(gp, gp, gcp, gcp),
        name="s5_discretise",
    )(a_re.reshape(g, 1, p), a_im.reshape(g, 1, p), log_step.reshape(g, 1, 1),
      jnp.swapaxes(b_re, 1, 2), jnp.swapaxes(b_im, 1, 2))


def _s5_block_weights(bbt_re, bbt_im, c_re, c_im):
    g, c, p = bbt_re.shape
    nb = g // GROUPS_PER_BLOCK
    n_lt = STATES_PER_BLOCK // LANES
    eye = jnp.eye(GROUPS_PER_BLOCK, dtype=F32)

    def b_blk(x):
        x = x.reshape(nb, GROUPS_PER_BLOCK, c, p)
        return jnp.einsum("jacp,ab->jacbp", x, eye).reshape(nb, LANES, n_lt, LANES)

    def c_blk(x):
        x = x.reshape(nb, GROUPS_PER_BLOCK, c, p)
        return jnp.einsum("jbcp,ab->japbc", x, eye).reshape(nb, n_lt, LANES, LANES)

    b_w = jnp.stack([b_blk(bbt_re), b_blk(bbt_im)], axis=3).reshape(nb, LANES, 2 * STATES_PER_BLOCK)
    c_w = jnp.stack([c_blk(c_re), -c_blk(c_im)], axis=2).reshape(nb, 2 * STATES_PER_BLOCK, LANES)
    return b_w.astype(BF16), c_w.astype(BF16)


def _s5_seq_kernel(u0_ref, u1_ref, bw_ref, cw_ref, are_ref, aim_ref, d_ref,
                   y_ref, xre_ref, xim_ref, lhs_ref, bu_ref, y_scr, x_scr, *, tc, n_b):
    n_lt = STATES_PER_BLOCK // LANES
    slots = 2 * n_b
    re_l = lambda lt: slice(2 * lt * LANES, (2 * lt + 1) * LANES)
    im_l = lambda lt: slice((2 * lt + 1) * LANES, (2 * lt + 2) * LANES)
    nat = lambda lt: slice(lt * LANES, (lt + 1) * LANES)
    c = pl.program_id(1)

    @pl.when(c == 0)
    def _():
        x_scr[...] = jnp.zeros_like(x_scr)
        lhs_ref[...] = jnp.zeros_like(lhs_ref)

    for b in range(n_b):
        lhs_ref.at[0][pl.ds(2 * b, tc, stride=slots), :] = u0_ref[b]
        lhs_ref.at[1][pl.ds(2 * b + 1, tc, stride=slots), :] = u1_ref[b]
    u_rows = lhs_ref[0] + lhs_ref[1]
    lhs = jnp.concatenate([lhs_ref[0], lhs_ref[1]], axis=1)
    bu_ref[...] = _dot(lhs, bw_ref[...])

    a_re = [are_ref[:, nat(lt)] for lt in range(n_lt)]
    a_im = [aim_ref[:, nat(lt)] for lt in range(n_lt)]
    x = [(x_scr[:, re_l(lt)], x_scr[:, im_l(lt)]) for lt in range(n_lt)]
    for t in range(tc):
        rows = slice(t * slots, (t + 1) * slots)
        for lt in range(n_lt):
            xr, xi = x[lt]
            nr = a_re[lt] * xr - a_im[lt] * xi + bu_ref[rows, re_l(lt)]
            ni = a_re[lt] * xi + a_im[lt] * xr + bu_ref[rows, im_l(lt)]
            bu_ref[rows, re_l(lt)] = nr
            bu_ref[rows, im_l(lt)] = ni
            x[lt] = (nr, ni)
    for lt in range(n_lt):
        x_scr[:, re_l(lt)] = x[lt][0]
        x_scr[:, im_l(lt)] = x[lt][1]

    yf = _dot(bu_ref[...], cw_ref[...])
    first_half = (lax.broadcasted_iota(jnp.int32, (tc * slots, LANES), 0) & 1) == 0
    y = jnp.where(first_half, yf[:, :LANES], yf[:, LANES:])
    skip = (u_rows.reshape(tc, slots, LANES) * d_ref[...][None]).reshape(tc * slots, LANES)
    y_scr[...] = y + skip
    for b in range(n_b):
        y_ref[b, :, :LANES] = y_scr[pl.ds(2 * b, tc, stride=slots), :]
        y_ref[b, :, LANES:] = y_scr[pl.ds(2 * b + 1, tc, stride=slots), :]

    @pl.when(c == pl.num_programs(1) - 1)
    def _():
        for lt in range(n_lt):
            xre_ref[:, nat(lt)] = x[lt][0]
            xim_ref[:, nat(lt)] = x[lt][1]


def _s5_sequence(p_main, b_w, c_w, ab_re, ab_im, d_skip):
    b, t, _ = p_main.shape
    nb = b_w.shape[0]
    nq = nb // 2
    ns = STATES_PER_BLOCK
    slots = 2 * b
    assert slots == 8, "one 8-row tile must hold every (sequence, half) slot"
    tc = min(S5_TC, t)
    half_rows = lambda x: jnp.stack([x[:nq], x[nq:]], axis=1)
    per_slot = lambda x: jnp.tile(half_rows(x.reshape(nb, -1)), (1, b, 1))
    bw2 = jnp.concatenate([b_w[:nq], b_w[nq:]], axis=1)
    cw2 = jnp.concatenate([c_w[:nq], c_w[nq:]], axis=2)
    slot_vec = lambda n: pl.BlockSpec((None, slots, n), lambda q, c: (q, 0, 0))
    y, xre, xim = pl.pallas_call(
        functools.partial(_s5_seq_kernel, tc=tc, n_b=b),
        grid=(nq, t // tc),
        in_specs=[pl.BlockSpec((b, tc, LANES), lambda q, c: (0, c, q)),
                  pl.BlockSpec((b, tc, LANES), lambda q, c: (0, c, q + nq)),
                  pl.BlockSpec((None, 2 * LANES, 2 * ns), lambda q, c: (q, 0, 0)),
                  pl.BlockSpec((None, 2 * ns, 2 * LANES), lambda q, c: (q, 0, 0)),
                  slot_vec(ns), slot_vec(ns), slot_vec(LANES)],
        out_specs=[pl.BlockSpec((b, tc, 2 * LANES), lambda q, c: (0, c, q)),
                   slot_vec(ns), slot_vec(ns)],
        out_shape=(jax.ShapeDtypeStruct((b, t, nb * LANES), F32),
                   jax.ShapeDtypeStruct((nq, slots, ns), F32),
                   jax.ShapeDtypeStruct((nq, slots, ns), F32)),
        scratch_shapes=[pltpu.VMEM((2, tc * slots, LANES), F32),
                        pltpu.VMEM((tc * slots, 2 * ns), F32),
                        pltpu.VMEM((tc * slots, LANES), F32),
                        pltpu.VMEM((slots, 2 * ns), F32)],
        compiler_params=_params("parallel", "arbitrary"),
        name="s5_sequence",
    )(p_main, p_main, bw2, cw2, per_slot(ab_re), per_slot(ab_im), per_slot(d_skip))

    def state(x):
        x = x.reshape(nq, b, 2, ns)
        return jnp.transpose(x, (1, 2, 0, 3)).reshape(b, nb * ns)

    return y, state(xre), state(xim)


def _s5_step_kernel(u_ref, bw_ref, cw_ref, abr_ref, abi_ref, d_ref, x0r_ref, x0i_ref,
                    y_ref, x1r_ref, x1i_ref):
    n_lt = STATES_PER_BLOCK // LANES
    u = u_ref[...]
    bu = _dot(u, bw_ref[...])
    tiles = []
    for lt in range(n_lt):
        nat = slice(lt * LANES, (lt + 1) * LANES)
        ar, ai = abr_ref[:, nat], abi_ref[:, nat]
        x0r, x0i = x0r_ref[:, nat], x0i_ref[:, nat]
        xr = ar * x0r - ai * x0i + bu[:, 2 * lt * LANES:(2 * lt + 1) * LANES]
        xi = ar * x0i + ai * x0r + bu[:, (2 * lt + 1) * LANES:(2 * lt + 2) * LANES]
        x1r_ref[:, nat] = xr
        x1i_ref[:, nat] = xi
        tiles += [xr, xi]
    y_ref[...] = _dot(jnp.concatenate(tiles, axis=1), cw_ref[...]) + d_ref[...] * u


def _s5_step(p_rows, b_w, c_w, ab_re, ab_im, d_skip, x0_re, x0_im):
    rows = x0_re.shape[0]
    nb = b_w.shape[0]
    ns = STATES_PER_BLOCK
    vec = lambda: pl.BlockSpec((1, ns), lambda j: (0, j))
    st = lambda: pl.BlockSpec((rows, ns), lambda j: (0, j))
    return pl.pallas_call(
        _s5_step_kernel,
        grid=(nb,),
        in_specs=[pl.BlockSpec((rows, LANES), lambda j: (0, j)),
                  pl.BlockSpec((None, LANES, 2 * ns), lambda j: (j, 0, 0)),
                  pl.BlockSpec((None, 2 * ns, LANES), lambda j: (j, 0, 0)),
                  vec(), vec(),
                  pl.BlockSpec((1, LANES), lambda j: (0, j)),
                  st(), st()],
        out_specs=[pl.BlockSpec((rows, LANES), lambda j: (0, j)), st(), st()],
        out_shape=(jax.ShapeDtypeStruct((rows, nb * LANES), F32),
                   jax.ShapeDtypeStruct((rows, nb * ns), F32),
                   jax.ShapeDtypeStruct((rows, nb * ns), F32)),
        compiler_params=_params("parallel"),
        name="s5_step",
    )(p_rows, b_w, c_w, ab_re, ab_im, d_skip, x0_re, x0_im)


def _softplus(x):
    return jnp.maximum(x, 0.0) + jnp.log(1.0 + jnp.exp(-jnp.abs(x)))


def _rwkv_token_terms(r, k, lo, w0, w2p, a0, a2p, k_a):
    w = -_softplus(-(w0 + _dot(jnp.tanh(lo), w2p))) - 0.5
    logd = -jnp.exp(w)
    a = _sigmoid(a0 + _dot(lo, a2p))
    k2 = k * (1.0 + (a - 1.0) * k_a)
    return logd, a, k2


def _rwkv_chunk_kernel(r_ref, k_ref, v_ref, lo_ref, z_ref,
                       mur_ref, muk_ref, muv_ref, mulo_ref,
                       w0_ref, w2p_ref, a0_ref, a2p_ref, kk_ref, ka_ref, rk_ref, gw_ref, gb_ref,
                       o_ref, hs_ref,
                       h_scr, pr_scr, pk_scr, pv_scr, plo_scr, *, n_heads, n_seqs):
    L = RW_CHUNK
    W = 2 * RW_HEAD
    n_pairs = n_heads // 2
    c = pl.program_id(1)

    @pl.when(c == 0)
    def _():
        h_scr[...] = jnp.zeros_like(h_scr)
        pr_scr[...] = jnp.zeros_like(pr_scr)
        pk_scr[...] = jnp.zeros_like(pk_scr)
        pv_scr[...] = jnp.zeros_like(pv_scr)
        plo_scr[...] = jnp.zeros_like(plo_scr)

    row1 = lax.broadcasted_iota(jnp.int32, (L, 1), 0)

    def token_shift(cur, prev_scr, mu_ref):
        prev = jnp.where(row1 == 0, prev_scr[...], pltpu.roll(cur, 1, 0))
        prev_scr[...] = cur[L - 1:L, :]
        return cur + (prev - cur) * mu_ref[...]

    tri = (lax.broadcasted_iota(jnp.int32, (L, L), 0)
           >= lax.broadcasted_iota(jnp.int32, (L, L), 1)).astype(BF16)
    first_head = lax.broadcasted_iota(jnp.int32, (L, W), 1) < RW_HEAD

    def head_sum(x):
        s0 = jnp.sum(jnp.where(first_head, x, 0.0), axis=-1, keepdims=True)
        s1 = jnp.sum(jnp.where(first_head, 0.0, x), axis=-1, keepdims=True)
        return jnp.where(first_head, s0, s1)

    def stack_heads(x):
        return jnp.concatenate([jnp.where(first_head, x, 0.0), jnp.where(first_head, 0.0, x)], axis=0)

    ri = lax.broadcasted_iota(jnp.int32, (2 * L, 2 * L), 0)
    ci = lax.broadcasted_iota(jnp.int32, (2 * L, 2 * L), 1)
    t_row, t_col = ri & (L - 1), ci & (L - 1)
    strict = t_row > t_col
    incl = t_row >= t_col
    eye = (ri == ci).astype(F32)
    blk_masks = []
    s = INV_BASE
    while s <= L:
        blk_masks.append((ri // s) == (ci // s))
        s *= 2

    pairs = range(n_pairs)
    loc = [slice(p * W, (p + 1) * W) for p in pairs]

    def seq_phases(si):
        r = token_shift(r_ref[si], pr_scr.at[si], mur_ref)
        k = token_shift(k_ref[si], pk_scr.at[si], muk_ref)
        v = token_shift(v_ref[si], pv_scr.at[si], muv_ref)
        lo = token_shift(lo_ref[si], plo_scr.at[si], mulo_ref)
        logd, a, k2 = _rwkv_token_terms(r, k, lo, w0_ref[...], w2p_ref[...], a0_ref[...], a2p_ref[...],
                                        ka_ref[...])
        kk = k * kk_ref[...]
        hi = logd.astype(BF16)
        rem = logd - hi.astype(F32)
        mid = rem.astype(BF16)
        low = (rem - mid.astype(F32)).astype(BF16)
        lp = (jnp.dot(tri, hi, preferred_element_type=F32)
              + jnp.dot(tri, mid, preferred_element_type=F32)
              + jnp.dot(tri, low, preferred_element_type=F32))
        p_inc = jnp.exp(lp)
        p_exc = jnp.exp(lp - logd)
        p_inv = jnp.exp(-lp)
        yield

        lhs, nm, mm, ab, vs, kb, p_end = [], [], [], [], [], [], []
        for sl in loc:
            kk_p = kk[:, sl]
            kkn = kk_p * lax.rsqrt(jnp.maximum(head_sum(kk_p * kk_p), 1e-24))
            pinc, pinv = p_inc[:, sl], p_inv[:, sl]
            kd = k2[:, sl] * pinv
            bd = kkn * a[:, sl] * pinv
            pe = pinc[L - 1:L, :]
            lhs_p = jnp.concatenate([stack_heads(kkn * p_exc[:, sl]), stack_heads(r[:, sl] * pinc)],
                                    axis=0).astype(BF16)
            rhs_p = jnp.concatenate([stack_heads(bd), stack_heads(kd)], axis=0).astype(BF16)
            amat = lax.dot_general(lhs_p, rhs_p, _NT, preferred_element_type=F32)
            lhs.append(lhs_p)
            nm.append(jnp.where(strict, amat[:2 * L, :2 * L], 0.0))
            mm.append(jnp.where(strict, amat[:2 * L, 2 * L:], 0.0).astype(BF16))
            ab.append(jnp.concatenate([jnp.where(incl, amat[2 * L:, 2 * L:], 0.0),
                                       -jnp.where(incl, amat[2 * L:, :2 * L], 0.0)], axis=1).astype(BF16))
            vs.append(stack_heads(v[:, sl]).astype(BF16))
            kb.append(jnp.concatenate([stack_heads(kd * pe), stack_heads(bd * pe)], axis=0).astype(BF16))
            p_end.append(pe)
            yield

        d = [jnp.where(blk_masks[0], n_p, 0.0).astype(BF16) for n_p in nm]
        x = [eye - d_p.astype(F32) for d_p in d]
        pw = [jnp.dot(d_p, d_p, preferred_element_type=F32) for d_p in d]
        yield
        s = 2
        while s < INV_BASE:
            x = [x_p + _dot(x_p, pw_p) for x_p, pw_p in zip(x, pw)]
            s *= 2
            if s < INV_BASE:
                pw = [_dot(pw_p, pw_p) for pw_p in pw]
            yield
        for lvl in range(1, len(blk_masks)):
            off = blk_masks[lvl] & ~blk_masks[lvl - 1]
            xc = [_dot(x_p, jnp.where(off, n_p, 0.0)) for x_p, n_p in zip(x, nm)]
            yield
            x = [x_p - _dot(xc_p, x_p) for x_p, xc_p in zip(x, xc)]
            yield

        hs = [h_scr[si, p] for p in pairs]
        lh = [jnp.dot(lhs_p, hs_p.astype(BF16), preferred_element_type=F32) for lhs_p, hs_p in zip(lhs, hs)]
        mv = [jnp.dot(mm_p, vs_p, preferred_element_type=F32) for mm_p, vs_p in zip(mm, vs)]
        yield
        u = [_dot(x_p, lh_p[:2 * L] + mv_p).astype(BF16) for x_p, lh_p, mv_p in zip(x, lh, mv)]
        yield
        o_st = [lh_p[2 * L:] + jnp.dot(ab_p, jnp.concatenate([vs_p, u_p], axis=0), preferred_element_type=F32)
                for lh_p, ab_p, vs_p, u_p in zip(lh, ab, vs, u)]
        for p in pairs:
            p_end_col = jnp.sum(eye * p_end[p], axis=1, keepdims=True)
            h_scr[si, p] = p_end_col * hs[p] + lax.dot_general(
                kb[p], jnp.concatenate([vs[p], -u[p]], axis=0), _TN, preferred_element_type=F32)
        yield

        for p, sl in enumerate(loc):
            o = o_st[p][:L] + o_st[p][L:]
            mu = head_sum(o) * (1.0 / RW_HEAD)
            var = head_sum((o - mu) ** 2) * (1.0 / RW_HEAD)
            o = (o - mu) * lax.rsqrt(var + GN_EPS) * gw_ref[:, sl] + gb_ref[:, sl]
            o = o + head_sum(r[:, sl] * k2[:, sl] * rk_ref[:, sl]) * v[:, sl]
            z = z_ref[si, :, LANES + p * W:LANES + (p + 1) * W]
            o_ref[si, :, sl] = (o * _silu(z)).astype(o_ref.dtype)
        yield

    gens = [seq_phases(si) for si in range(n_seqs)]
    started = 0
    live = []
    step = 0
    while started < n_seqs or live:
        if started < n_seqs and step % RW_SEQ_LAG == 0:
            live.append(gens[started])
            started += 1
        for g in list(live):
            if next(g, "done") == "done":
                live.remove(g)
        step += 1

    @pl.when(c == pl.num_programs(1) - 1)
    def _():
        for si in range(n_seqs):
            for p in pairs:
                st = h_scr[si, p].T
                hs_ref[si, 2 * p] = st[:RW_HEAD, :RW_HEAD]
                hs_ref[si, 2 * p + 1] = st[RW_HEAD:, RW_HEAD:]


def _rwkv_sequence(p_main, p_lz, col, mu, w0, w2p, a0, a2p, k_k, k_a, r_k, gn_w, gn_b):
    b, t, _ = p_main.shape
    wlz = p_lz.shape[-1]
    d = w0.shape[-1]
    n_heads = d // RW_HEAD
    n_pairs, pw = n_heads // 2, 2 * RW_HEAD
    L = RW_CHUNK
    ns = RW_SEQS if b % RW_SEQS == 0 else 1
    blk = lambda cb: pl.BlockSpec((ns, L, d), lambda i, j, cb=cb: (i, j, cb))
    vec = lambda n: pl.BlockSpec((1, n), lambda i, j: (0, 0))
    full = lambda shp: pl.BlockSpec(shp, lambda i, j: (0,) * len(shp))
    mu_r, mu_k, mu_v, mu_lo = mu
    o, hs = pl.pallas_call(
        functools.partial(_rwkv_chunk_kernel, n_heads=n_heads, n_seqs=ns),
        grid=(b // ns, t // L),
        in_specs=[blk(col["r"]), blk(col["k"]), blk(col["v"]),
                  pl.BlockSpec((ns, L, LANES), lambda i, j: (i, j, 0)),
                  pl.BlockSpec((ns, L, wlz), lambda i, j: (i, j, 0)),
                  vec(d), vec(d), vec(d), vec(LANES),
                  vec(d), full((LANES, d)), vec(d), full((LANES, d)),
                  vec(d), vec(d), vec(d), vec(d), vec(d)],
        out_specs=[pl.BlockSpec((ns, L, d), lambda i, j: (i, j, 0)),
                   pl.BlockSpec((ns, n_heads, RW_HEAD, RW_HEAD), lambda i, j: (i, 0, 0, 0))],
        out_shape=(jax.ShapeDtypeStruct((b, t, d), BF16),
                   jax.ShapeDtypeStruct((b, n_heads, RW_HEAD, RW_HEAD), F32)),
        scratch_shapes=[pltpu.VMEM((ns, n_pairs, pw, pw), F32),
                        pltpu.VMEM((ns, 1, d), F32), pltpu.VMEM((ns, 1, d), F32), pltpu.VMEM((ns, 1, d), F32),
                        pltpu.VMEM((ns, 1, LANES), F32)],
        compiler_params=_params("parallel", "arbitrary"),
        name="rwkv_sequence",
    )(p_main, p_main, p_main, p_lz, p_lz,
      mu_r, mu_k, mu_v, mu_lo, w0, w2p, a0, a2p, k_k, k_a, r_k, gn_w, gn_b)
    return o, hs


def _rwkv_step_prep_kernel(cr_ref, ck_ref, cv_ref, clo_ref, cz_ref, pr_ref, pk_ref, pv_ref, plo_ref,
                           mur_ref, muk_ref, muv_ref, mulo_ref,
                           w0_ref, w2p_ref, a0_ref, a2p_ref, kk_ref, ka_ref,
                           r_o, k2_o, v_o, kk_o, a_o, d_o, z_o):
    def lerp(c_ref, p_ref, mu_ref):
        cur = c_ref[...]
        return cur + (p_ref[...] - cur) * mu_ref[...]

    r = lerp(cr_ref, pr_ref, mur_ref)
    k = lerp(ck_ref, pk_ref, muk_ref)
    v = lerp(cv_ref, pv_ref, muv_ref)
    lo = lerp(clo_ref, plo_ref, mulo_ref)
    logd, a, k2 = _rwkv_token_terms(r, k, lo, w0_ref[...], w2p_ref[...], a0_ref[...], a2p_ref[...],
                                    ka_ref[...])
    r_o[...] = r.T
    k2_o[...] = k2.T
    v_o[...] = v.T
    kk_o[...] = (k * kk_ref[...]).T
    a_o[...] = a.T
    d_o[...] = jnp.exp(logd).T
    z_o[...] = cz_ref[:, LANES:].T


def _rwkv_step_kernel(s_ref, r_ref, k2_ref, v_ref, kk_ref, a_ref, d_ref, z_ref,
                      rk_ref, gw_ref, gb_ref, o_ref, s1_ref, o_scr):
    r, k2, v, dec = r_ref[...], k2_ref[...], v_ref[...], d_ref[...]
    kk = kk_ref[...]
    kkn = kk / jnp.maximum(jnp.sqrt(jnp.sum(kk * kk, axis=0, keepdims=True)), 1e-12)
    bvec = kkn * a_ref[...]
    for i in range(RW_HEAD):
        s = s_ref[i]
        sa = jnp.sum(s * kkn, axis=0, keepdims=True)
        s1 = s * dec - sa * bvec + v[i:i + 1, :] * k2
        s1_ref[i] = s1
        o_scr[pl.ds(i, 1), :] = jnp.sum(s1 * r, axis=0, keepdims=True)
    o = o_scr[...]
    mu = jnp.mean(o, axis=0, keepdims=True)
    var = jnp.mean((o - mu) ** 2, axis=0, keepdims=True)
    o = (o - mu) * lax.rsqrt(var + GN_EPS) * gw_ref[...] + gb_ref[...]
    o = o + jnp.sum(r * k2 * rk_ref[...], axis=0, keepdims=True) * v
    o_ref[...] = o * _silu(z_ref[...])


def _rwkv_step(p_rows, p_lz_rows, col, mu, w0, w2p, a0, a2p, k_k, k_a, r_k, gn_w, gn_b, s0):
    rows = s0.shape[0]
    wlz = p_lz_rows.shape[-1]
    d = w0.shape[-1]
    n_heads = d // RW_HEAD
    cur = lambda cb: pl.BlockSpec((rows, d), lambda i, cb=cb: (0, cb))
    prv = lambda cb: pl.BlockSpec((rows, d), lambda i, cb=cb: (1, cb))
    vec = lambda n: pl.BlockSpec((1, n), lambda i: (0, 0))
    mu_r, mu_k, mu_v, mu_lo = mu
    out = jax.ShapeDtypeStruct((d, rows), F32)
    terms = pl.pallas_call(
        _rwkv_step_prep_kernel,
        grid=(1,),
        in_specs=[cur(col["r"]), cur(col["k"]), cur(col["v"]),
                  pl.BlockSpec((rows, LANES), lambda i: (0, 0)),
                  pl.BlockSpec((rows, wlz), lambda i: (0, 0)),
                  prv(col["r"]), prv(col["k"]), prv(col["v"]),
                  pl.BlockSpec((rows, LANES), lambda i: (1, 0)),
                  vec(d), vec(d), vec(d), vec(LANES),
                  vec(d), pl.BlockSpec((LANES, d), lambda i: (0, 0)),
                  vec(d), pl.BlockSpec((LANES, d), lambda i: (0, 0)),
                  vec(d), vec(d)],
        out_specs=[pl.BlockSpec((d, rows), lambda i: (0, 0))] * 7,
        out_shape=(out,) * 7,
        compiler_params=_params("arbitrary"),
        name="rwkv_step_prep",
    )(p_rows, p_rows, p_rows, p_lz_rows, p_lz_rows, p_rows, p_rows, p_rows, p_lz_rows,
      mu_r, mu_k, mu_v, mu_lo, w0, w2p, a0, a2p, k_k, k_a)
    per_h = lambda: pl.BlockSpec((RW_HEAD, rows), lambda h: (h, 0))
    par = lambda: pl.BlockSpec((RW_HEAD, 1), lambda h: (h, 0))
    st = lambda: pl.BlockSpec((None, RW_HEAD, RW_HEAD, rows), lambda h: (h, 0, 0, 0))
    o_t, s1_t = pl.pallas_call(
        _rwkv_step_kernel,
        grid=(n_heads,),
        in_specs=[st()] + [per_h()] * 7 + [par()] * 3,
        out_specs=[per_h(), st()],
        out_shape=(jax.ShapeDtypeStruct((d, rows), F32),
                   jax.ShapeDtypeStruct((n_heads, RW_HEAD, RW_HEAD, rows), F32)),
        scratch_shapes=[pltpu.VMEM((RW_HEAD, rows), F32)],
        compiler_params=_params("parallel"),
        name="rwkv_step",
    )(jnp.transpose(s0, (1, 2, 3, 0)), *terms,
      r_k.reshape(d, 1), gn_w.reshape(d, 1), gn_b.reshape(d, 1))
    return o_t.T.astype(BF16), jnp.transpose(s1_t, (3, 0, 1, 2))


def _out_kernel(ys_ref, z_ref, wg_ref, bg_ref, or_ref, gs_ref, gr_ref, x_ref, gt_ref, w1_ref, w2_ref,
                fg_ref, y_ref, *, paired_blocks):
    ys = ys_ref[...]
    if paired_blocks:
        nb = ys.shape[1] // LANES
        pos = [2 * j if j < nb // 2 else 2 * (j - nb // 2) + 1 for j in range(nb)]
        ys = jnp.concatenate([ys[:, p * LANES:(p + 1) * LANES] for p in pos], axis=1)
    ys = jax.nn.gelu(ys, approximate=True)
    o_s = (ys * _sigmoid(_dot(ys, wg_ref[...]) + bg_ref[...]) * _silu(z_ref[...])).astype(BF16)
    mixed = (gs_ref[...].astype(F32) * jnp.dot(o_s, w1_ref[...], preferred_element_type=F32)
             + gr_ref[...].astype(F32) * jnp.dot(or_ref[...], w2_ref[...], preferred_element_type=F32))
    x = x_ref[...] + gt_ref[...] * mixed
    y_ref[...] = x * lax.rsqrt(jnp.mean(x * x, axis=-1, keepdims=True) + RMS_EPS) * fg_ref[...]


def _out_proj(y_s5, p_main, o_r, p_gate, col, x, mod, w_glu, b_glu, w_out, final_g, tt, paired_blocks):
    b, t, d = x.shape
    dh = o_r.shape[-1]
    tt = min(tt, t)
    tm = 1 if mod.shape[1] == 1 else tt
    gt_map = (lambda i, j: (i, 0, 2)) if tm == 1 else (lambda i, j: (i, j, 2))
    const = lambda shp, r=0: pl.BlockSpec(shp, lambda i, j: (r, 0), pipeline_mode=pl.Buffered(1))
    return pl.pallas_call(
        functools.partial(_out_kernel, paired_blocks=paired_blocks),
        grid=(b, t // tt),
        in_specs=[pl.BlockSpec((None, tt, dh), lambda i, j: (i, j, 0)),
                  pl.BlockSpec((None, tt, dh), lambda i, j: (i, j, col["z_s5"])),
                  const((dh, dh)), const((1, dh)),
                  pl.BlockSpec((None, tt, dh), lambda i, j: (i, j, 0)),
                  pl.BlockSpec((None, tt, d), lambda i, j: (i, j, col["g_s5"])),
                  pl.BlockSpec((None, tt, d), lambda i, j: (i, j, col["g_rw"])),
                  pl.BlockSpec((None, tt, d), lambda i, j: (i, j, 0)),
                  pl.BlockSpec((None, tm, d), gt_map),
                  const((dh, d), 0), const((dh, d), 1), const((1, d))],
        out_specs=pl.BlockSpec((None, tt, d), lambda i, j: (i, j, 0)),
        out_shape=jax.ShapeDtypeStruct((b, t, d), F32),
        compiler_params=_params("parallel", "parallel"),
        name="out_proj",
    )(y_s5, p_main, w_glu, b_glu.reshape(1, dh), o_r, p_gate, p_gate, x, mod, w_out, w_out,
      final_g.reshape(1, d))


def kernel(x_prompt, x_sample, c_prompt, c_sample, state_s5_re, state_s5_im, state_wkv, state_shift, norm_g, w_ada, b_ada, w_in, mu_rw, A_re, A_im, log_step, B_re, B_im, C_re, C_im, D_skip, w_glu, b_glu, w0, w2, a0, a2, k_k, k_a, r_k, gn_w, gn_b, w_out, final_g):
    depth = norm_g.shape[0]
    assert depth == 1
    bp, tp, d = x_prompt.shape
    bs = x_sample.shape[0]
    assert x_sample.shape[1] == 1
    dh = d // 2
    l = 0

    w_main, w_lz, w_gate = 5 * dh, 2 * LORA + dh, 2 * d
    col = {"u": 0, "z_s5": 1, "r": 2, "k": 3, "v": 4, "g_s5": 0, "g_rw": 1}
    w = w_in[l]

    def project(rows, tm):
        return (_in_proj(rows, w, 0, w_main, w_main // 4, tm),
                _in_proj(rows, w, w_main, w_lz, w_lz, tm),
                _in_proj(rows, w, w_main + w_lz, w_gate, w_gate // 4, tm, gate=True))

    mu = mu_rw[l]
    mu_parts = (mu[None, :dh], mu[None, dh:2 * dh], mu[None, 2 * dh:3 * dh], mu[None, 3 * dh:])
    zpad = jnp.zeros((LORA, dh), F32)
    w2p = jnp.concatenate([w2[l], zpad], axis=0).astype(BF16)
    a2p = jnp.concatenate([zpad, a2[l]], axis=0).astype(BF16)
    row = lambda x: x.reshape(1, -1)
    rw_params = (row(w0[l]), w2p, row(a0[l]), a2p, row(k_k[l]), row(k_a[l]), row(r_k[l]),
                 row(gn_w[l]), row(gn_b[l]))
    w_out_bf = w_out[l].astype(BF16)
    w_glu_bf = w_glu[l].astype(BF16)

    ab_re, ab_im, bbt_re, bbt_im = _s5_discretise(A_re[l], A_im[l], log_step[l], B_re[l], B_im[l])
    b_w, c_w = _s5_block_weights(bbt_re, bbt_im, C_re[l], C_im[l])
    flat = lambda x: x.reshape(1, -1)
    ab_re, ab_im = flat(ab_re), flat(ab_im)
    d_skip = flat(D_skip[l])

    mod = _mod(jnp.concatenate([c_prompt, c_sample], axis=0), w_ada[l], b_ada[l])
    mod_p = mod[:bp].reshape(bp, 1, 3 * d)
    mod_s = mod[bp:].reshape(1, bs, 3 * d)

    h_p = _modulated_norm(x_prompt, norm_g[l], mod_p, BF16, 512)
    shift_p = _modulated_norm(x_prompt[:, tp - 1:, :], norm_g[l], mod_p, F32, 1)[:, 0]
    a_p = h_p.reshape(bp * tp, d)
    pm, plz, pg = project(a_p, 1024)
    pm3 = pm.reshape(bp, tp, -1)
    y_s5, xre_p, xim_p = _s5_sequence(pm3, b_w, c_w, ab_re, ab_im, d_skip)
    o_r, hs_p = _rwkv_sequence(pm3, plz.reshape(bp, tp, -1), col, mu_parts, *rw_params)
    y_prompt = _out_proj(y_s5, pm3, o_r, pg.reshape(bp, tp, -1), col, x_prompt, mod_p,
                         w_glu_bf, b_glu[l], w_out_bf, final_g, 512, True)
    g_s5 = A_re.shape[1]
    s5_shape = (1, bp, g_s5, P_S5)
    wkv_p = hs_p[None]

    xs = x_sample.reshape(1, bs, d)
    h_s = _modulated_norm(xs, norm_g[l], mod_s, F32, bs)[0]
    a_s = jnp.concatenate([h_s, state_shift[l]], axis=0).astype(BF16)
    ps, pslz, psg = project(a_s, 2 * bs)
    y_s5s, xre_s, xim_s = _s5_step(ps, b_w, c_w, ab_re, ab_im, d_skip,
                                   state_s5_re[l].reshape(bs, -1), state_s5_im[l].reshape(bs, -1))
    o_rs, wkv_s = _rwkv_step(ps, pslz, col, mu_parts, *rw_params, state_wkv[l])
    y_sample = _out_proj(y_s5s[None], ps.reshape(2, bs, -1), o_rs[None], psg.reshape(2, bs, -1), col, xs, mod_s,
                         w_glu_bf, b_glu[l], w_out_bf, final_g, bs, False)
    y_sample = y_sample.reshape(bs, 1, d)

    return (y_prompt, y_sample,
            xre_p.reshape(s5_shape), xim_p.reshape(s5_shape), wkv_p, shift_p[None],
            xre_s.reshape(1, bs, g_s5, P_S5), xim_s.reshape(1, bs, g_s5, P_S5), wkv_s[None], h_s[None])
```

```python
import functools

import jax
import jax.numpy as jnp
from jax import lax
from jax.experimental import pallas as pl
from jax.experimental.pallas import tpu as pltpu

F32 = jnp.float32
BF16 = jnp.bfloat16

RMS_EPS = 1e-6
GN_EPS = 64e-5
S5_GROUP = 16
P_S5 = 64
RW_HEAD = 64
LORA = 64
LANES = 128
GROUPS_PER_BLOCK = LANES // S5_GROUP
STATES_PER_BLOCK = GROUPS_PER_BLOCK * P_S5
S5_TC = 256
RW_CHUNK = 64
INV_BASE = 8
RW_SEQS = 4
RW_SEQ_LAG = 1

_NT = (((1,), (1,)), ((), ()))
_TN = (((0,), (0,)), ((), ()))


def _dot(a, b):
    return jnp.dot(a.astype(BF16), b.astype(BF16), preferred_element_type=F32)


def _sigmoid(x):
    return 0.5 * jnp.tanh(0.5 * x) + 0.5


def _silu(x):
    return x * _sigmoid(x)


def _params(*sem):
    return pltpu.CompilerParams(dimension_semantics=sem, vmem_limit_bytes=56 * 1024 * 1024)


def _mod_kernel(c_ref, w_ref, b_ref, o_ref):
    o_ref[...] = _dot(_silu(c_ref[...]), w_ref[...]) + b_ref[...]


def _mod(c, w_bf, b):
    rows, d = c.shape
    n = w_bf.shape[1]
    tn = 1024
    return pl.pallas_call(
        _mod_kernel,
        grid=(n // tn,),
        in_specs=[pl.BlockSpec((rows, d), lambda j: (0, 0)),
                  pl.BlockSpec((d, tn), lambda j: (0, j)),
                  pl.BlockSpec((1, tn), lambda j: (0, j))],
        out_specs=pl.BlockSpec((rows, tn), lambda j: (0, j)),
        out_shape=jax.ShapeDtypeStruct((rows, n), F32),
        compiler_params=_params("parallel"),
        name="adaln_mod",
    )(c, w_bf, b.reshape(1, n))


def _h_kernel(x_ref, g_ref, sh_ref, sc_ref, h_ref):
    x = x_ref[...]
    y = x * lax.rsqrt(jnp.mean(x * x, axis=-1, keepdims=True) + RMS_EPS) * g_ref[...]
    h_ref[...] = (y * (1.0 + sc_ref[...]) + sh_ref[...]).astype(h_ref.dtype)


def _modulated_norm(x, g, mod, out_dtype, tt):
    b, t, d = x.shape
    tt = min(tt, t)
    tm = 1 if mod.shape[1] == 1 else tt
    mod_map = (lambda i, j: (i, 0, 0)) if tm == 1 else (lambda i, j: (i, j, 0))
    mod_map1 = (lambda i, j: (i, 0, 1)) if tm == 1 else (lambda i, j: (i, j, 1))
    return pl.pallas_call(
        _h_kernel,
        grid=(b, t // tt),
        in_specs=[pl.BlockSpec((None, tt, d), lambda i, j: (i, j, 0)),
                  pl.BlockSpec((1, d), lambda i, j: (0, 0)),
                  pl.BlockSpec((None, tm, d), mod_map),
                  pl.BlockSpec((None, tm, d), mod_map1)],
        out_specs=pl.BlockSpec((None, tt, d), lambda i, j: (i, j, 0)),
        out_shape=jax.ShapeDtypeStruct((b, t, d), out_dtype),
        compiler_params=_params("parallel", "parallel"),
        name="modulated_norm",
    )(x, g.reshape(1, d), mod, mod)


def _in_proj_kernel(a_ref, a2_ref, w_ref, o_ref, o2_ref, wbf_ref, *, gate, n_main):
    i = pl.program_id(1)

    @pl.when(i == 0)
    def _():
        wbf_ref[...] = w_ref[...].astype(BF16)

    def project(x_ref, out_ref):
        p = jnp.dot(x_ref[...], wbf_ref[...], preferred_element_type=F32)
        out_ref[...] = (_sigmoid(p) if gate else p).astype(out_ref.dtype)

    @pl.when(i < n_main)
    def _():
        project(a_ref, o_ref)

    @pl.when(i == n_main)
    def _():
        project(a2_ref, o2_ref)


def _in_proj(a, a2, w, col0, width, tn, tm, gate=False):
    m, k = a.shape
    m2 = a2.shape[0]
    tm = min(tm, m)
    assert width % tn == 0 and m % tm == 0 and col0 % LANES == 0
    n_main = m // tm
    last = n_main - 1
    dt = BF16 if gate else F32
    return pl.pallas_call(
        functools.partial(_in_proj_kernel, gate=gate, n_main=n_main),
        grid=(width // tn, n_main + 1),
        in_specs=[pl.BlockSpec((tm, k), lambda j, i: (jnp.minimum(i, last), 0)),
                  pl.BlockSpec((m2, k), lambda j, i: (0, 0)),
                  pl.BlockSpec((pl.Element(k), pl.Element(tn)), lambda j, i: (0, pl.multiple_of(col0 + j * tn, LANES)))],
        out_specs=[pl.BlockSpec((tm, tn), lambda j, i: (jnp.minimum(i, last), j)),
                   pl.BlockSpec((m2, tn), lambda j, i: (0, j))],
        out_shape=(jax.ShapeDtypeStruct((m, width), dt), jax.ShapeDtypeStruct((m2, width), dt)),
        scratch_shapes=[pltpu.VMEM((k, tn), BF16)],
        compiler_params=_params("parallel", "arbitrary"),
        name="in_proj",
    )(a, a2, w)


def _s5_disc_kernel(are_ref, aim_ref, ls_ref, bre_ref, bim_ref,
                    abr_ref, abi_ref, bbr_ref, bbi_ref):
    step = jnp.exp(ls_ref[...])
    lam_re = jnp.minimum(are_ref[...], -1e-4)
    lam_im = aim_ref[...]
    mag = jnp.exp(lam_re * step)
    ab_re = mag * jnp.cos(lam_im * step)
    ab_im = mag * jnp.sin(lam_im * step)
    den = lam_re * lam_re + lam_im * lam_im
    f_re = ((ab_re - 1.0) * lam_re + ab_im * lam_im) / den
    f_im = (ab_im * lam_re - (ab_re - 1.0) * lam_im) / den
    br, bi = bre_ref[...], bim_ref[...]
    bbr_ref[...] = f_re * br - f_im * bi
    bbi_ref[...] = f_re * bi + f_im * br
    abr_ref[...] = ab_re
    abi_ref[...] = ab_im


def _s5_discretise(a_re, a_im, log_step, b_re, b_im):
    g, p = a_re.shape
    c = b_re.shape[-1]
    gp = jax.ShapeDtypeStruct((g, 1, p), F32)
    gcp = jax.ShapeDtypeStruct((g, c, p), F32)
    return pl.pallas_call(
        _s5_disc_kernel,
        out_shape=(gp, gp, gcp, gcp),
        name="s5_discretise",
    )(a_re.reshape(g, 1, p), a_im.reshape(g, 1, p), log_step.reshape(g, 1, 1),
      jnp.swapaxes(b_re, 1, 2), jnp.swapaxes(b_im, 1, 2))


def _s5_block_weights(bbt_re, bbt_im, c_re, c_im):
    g, c, p = bbt_re.shape
    nb = g // GROUPS_PER_BLOCK
    n_lt = STATES_PER_BLOCK // LANES
    eye = jnp.eye(GROUPS_PER_BLOCK, dtype=F32)

    def b_blk(x):
        x = x.reshape(nb, GROUPS_PER_BLOCK, c, p)
        return jnp.einsum("jacp,ab->jacbp", x, eye).reshape(nb, LANES, n_lt, LANES)

    def c_blk(x):
        x = x.reshape(nb, GROUPS_PER_BLOCK, c, p)
        return jnp.einsum("jbcp,ab->japbc", x, eye).reshape(nb, n_lt, LANES, LANES)

    b_w = jnp.stack([b_blk(bbt_re), b_blk(bbt_im)], axis=3).reshape(nb, LANES, 2 * STATES_PER_BLOCK)
    c_w = jnp.stack([c_blk(c_re), -c_blk(c_im)], axis=2).reshape(nb, 2 * STATES_PER_BLOCK, LANES)
    return b_w.astype(BF16), c_w.astype(BF16)


def _s5_seq_kernel(u0_ref, u1_ref, bw_ref, cw_ref, are_ref, aim_ref, d_ref,
                   y_ref, xre_ref, xim_ref, lhs_ref, bu_ref, y_scr, x_scr, *, tc, n_b):
    n_lt = STATES_PER_BLOCK // LANES
    slots = 2 * n_b
    re_l = lambda lt: slice(2 * lt * LANES, (2 * lt + 1) * LANES)
    im_l = lambda lt: slice((2 * lt + 1) * LANES, (2 * lt + 2) * LANES)
    nat = lambda lt: slice(lt * LANES, (lt + 1) * LANES)
    c = pl.program_id(1)

    @pl.when(c == 0)
    def _():
        x_scr[...] = jnp.zeros_like(x_scr)
        lhs_ref[...] = jnp.zeros_like(lhs_ref)

    for b in range(n_b):
        lhs_ref.at[0][pl.ds(2 * b, tc, stride=slots), :] = u0_ref[b]
        lhs_ref.at[1][pl.ds(2 * b + 1, tc, stride=slots), :] = u1_ref[b]
    u_rows = lhs_ref[0] + lhs_ref[1]
    lhs = jnp.concatenate([lhs_ref[0], lhs_ref[1]], axis=1)
    bu_ref[...] = _dot(lhs, bw_ref[...])

    a_re = [are_ref[:, nat(lt)] for lt in range(n_lt)]
    a_im = [aim_ref[:, nat(lt)] for lt in range(n_lt)]
    x = [(x_scr[:, re_l(lt)], x_scr[:, im_l(lt)]) for lt in range(n_lt)]
    for t in range(tc):
        rows = slice(t * slots, (t + 1) * slots)
        for lt in range(n_lt):
            xr, xi = x[lt]
            nr = a_re[lt] * xr - a_im[lt] * xi + bu_ref[rows, re_l(lt)]
            ni = a_re[lt] * xi + a_im[lt] * xr + bu_ref[rows, im_l(lt)]
            bu_ref[rows, re_l(lt)] = nr
            bu_ref[rows, im_l(lt)] = ni
            x[lt] = (nr, ni)
    for lt in range(n_lt):
        x_scr[:, re_l(lt)] = x[lt][0]
        x_scr[:, im_l(lt)] = x[lt][1]

    yf = _dot(bu_ref[...], cw_ref[...])
    first_half = (lax.broadcasted_iota(jnp.int32, (tc * slots, LANES), 0) & 1) == 0
    y = jnp.where(first_half, yf[:, :LANES], yf[:, LANES:])
    skip = (u_rows.reshape(tc, slots, LANES) * d_ref[...][None]).reshape(tc * slots, LANES)
    y_scr[...] = y + skip
    for b in range(n_b):
        y_ref[b, :, :LANES] = y_scr[pl.ds(2 * b, tc, stride=slots), :]
        y_ref[b, :, LANES:] = y_scr[pl.ds(2 * b + 1, tc, stride=slots), :]

    @pl.when(c == pl.num_programs(1) - 1)
    def _():
        for lt in range(n_lt):
            xre_ref[:, nat(lt)] = x[lt][0]
            xim_ref[:, nat(lt)] = x[lt][1]


def _s5_sequence(p_main, b_w, c_w, ab_re, ab_im, d_skip):
    b, t, _ = p_main.shape
    nb = b_w.shape[0]
    nq = nb // 2
    ns = STATES_PER_BLOCK
    slots = 2 * b
    assert slots == 8, "one 8-row tile must hold every (sequence, half) slot"
    tc = min(S5_TC, t)
    half_rows = lambda x: jnp.stack([x[:nq], x[nq:]], axis=1)
    per_slot = lambda x: jnp.tile(half_rows(x.reshape(nb, -1)), (1, b, 1))
    bw2 = jnp.concatenate([b_w[:nq], b_w[nq:]], axis=1)
    cw2 = jnp.concatenate([c_w[:nq], c_w[nq:]], axis=2)
    slot_vec = lambda n: pl.BlockSpec((None, slots, n), lambda q, c: (q, 0, 0))
    y, xre, xim = pl.pallas_call(
        functools.partial(_s5_seq_kernel, tc=tc, n_b=b),
        grid=(nq, t // tc),
        in_specs=[pl.BlockSpec((b, tc, LANES), lambda q, c: (0, c, q)),
                  pl.BlockSpec((b, tc, LANES), lambda q, c: (0, c, q + nq)),
                  pl.BlockSpec((None, 2 * LANES, 2 * ns), lambda q, c: (q, 0, 0)),
                  pl.BlockSpec((None, 2 * ns, 2 * LANES), lambda q, c: (q, 0, 0)),
                  slot_vec(ns), slot_vec(ns), slot_vec(LANES)],
        out_specs=[pl.BlockSpec((b, tc, 2 * LANES), lambda q, c: (0, c, q)),
                   slot_vec(ns), slot_vec(ns)],
        out_shape=(jax.ShapeDtypeStruct((b, t, nb * LANES), F32),
                   jax.ShapeDtypeStruct((nq, slots, ns), F32),
                   jax.ShapeDtypeStruct((nq, slots, ns), F32)),
        scratch_shapes=[pltpu.VMEM((2, tc * slots, LANES), F32),
                        pltpu.VMEM((tc * slots, 2 * ns), F32),
                        pltpu.VMEM((tc * slots, LANES), F32),
                        pltpu.VMEM((slots, 2 * ns), F32)],
        compiler_params=_params("parallel", "arbitrary"),
        name="s5_sequence",
    )(p_main, p_main, bw2, cw2, per_slot(ab_re), per_slot(ab_im), per_slot(d_skip))

    def state(x):
        x = x.reshape(nq, b, 2, ns)
        return jnp.transpose(x, (1, 2, 0, 3)).reshape(b, nb * ns)

    return y, state(xre), state(xim)


def _s5_step_kernel(u_ref, bw_ref, cw_ref, abr_ref, abi_ref, d_ref, x0r_ref, x0i_ref,
                    y_ref, x1r_ref, x1i_ref):
    n_lt = STATES_PER_BLOCK // LANES
    u = u_ref[...]
    bu = _dot(u, bw_ref[...])
    tiles = []
    for lt in range(n_lt):
        nat = slice(lt * LANES, (lt + 1) * LANES)
        ar, ai = abr_ref[:, nat], abi_ref[:, nat]
        x0r, x0i = x0r_ref[:, nat], x0i_ref[:, nat]
        xr = ar * x0r - ai * x0i + bu[:, 2 * lt * LANES:(2 * lt + 1) * LANES]
        xi = ar * x0i + ai * x0r + bu[:, (2 * lt + 1) * LANES:(2 * lt + 2) * LANES]
        x1r_ref[:, nat] = xr
        x1i_ref[:, nat] = xi
        tiles += [xr, xi]
    y_ref[...] = _dot(jnp.concatenate(tiles, axis=1), cw_ref[...]) + d_ref[...] * u


def _s5_step(p_rows, b_w, c_w, ab_re, ab_im, d_skip, x0_re, x0_im):
    rows = x0_re.shape[0]
    nb = b_w.shape[0]
    ns = STATES_PER_BLOCK
    vec = lambda: pl.BlockSpec((1, ns), lambda j: (0, j))
    st = lambda: pl.BlockSpec((rows, ns), lambda j: (0, j))
    return pl.pallas_call(
        _s5_step_kernel,
        grid=(nb,),
        in_specs=[pl.BlockSpec((rows, LANES), lambda j: (0, j)),
                  pl.BlockSpec((None, LANES, 2 * ns), lambda j: (j, 0, 0)),
                  pl.BlockSpec((None, 2 * ns, LANES), lambda j: (j, 0, 0)),
                  vec(), vec(),
                  pl.BlockSpec((1, LANES), lambda j: (0, j)),
                  st(), st()],
        out_specs=[pl.BlockSpec((rows, LANES), lambda j: (0, j)), st(), st()],
        out_shape=(jax.ShapeDtypeStruct((rows, nb * LANES), F32),
                   jax.ShapeDtypeStruct((rows, nb * ns), F32),
                   jax.ShapeDtypeStruct((rows, nb * ns), F32)),
        compiler_params=_params("parallel"),
        name="s5_step",
    )(p_rows, b_w, c_w, ab_re, ab_im, d_skip, x0_re, x0_im)


def _softplus(x):
    return jnp.maximum(x, 0.0) + jnp.log(1.0 + jnp.exp(-jnp.abs(x)))


def _rwkv_token_terms(r, k, lo, w0, w2p, a0, a2p, k_a):
    w = -_softplus(-(w0 + _dot(jnp.tanh(lo), w2p))) - 0.5
    logd = -jnp.exp(w)
    a = _sigmoid(a0 + _dot(lo, a2p))
    k2 = k * (1.0 + (a - 1.0) * k_a)
    return logd, a, k2


def _rwkv_chunk_kernel(r_ref, k_ref, v_ref, lo_ref, z_ref,
                       mur_ref, muk_ref, muv_ref, mulo_ref,
                       w0_ref, w2p_ref, a0_ref, a2p_ref, kk_ref, ka_ref, rk_ref, gw_ref, gb_ref,
                       o_ref, hs_ref,
                       h_scr, pr_scr, pk_scr, pv_scr, plo_scr, *, n_heads, n_seqs):
    L = RW_CHUNK
    W = 2 * RW_HEAD
    n_pairs = n_heads // 2
    c = pl.program_id(1)

    @pl.when(c == 0)
    def _():
        h_scr[...] = jnp.zeros_like(h_scr)
        pr_scr[...] = jnp.zeros_like(pr_scr)
        pk_scr[...] = jnp.zeros_like(pk_scr)
        pv_scr[...] = jnp.zeros_like(pv_scr)
        plo_scr[...] = jnp.zeros_like(plo_scr)

    row1 = lax.broadcasted_iota(jnp.int32, (L, 1), 0)

    def token_shift(cur, prev_scr, mu_ref):
        prev = jnp.where(row1 == 0, prev_scr[...], pltpu.roll(cur, 1, 0))
        prev_scr[...] = cur[L - 1:L, :]
        return cur + (prev - cur) * mu_ref[...]

    tri = (lax.broadcasted_iota(jnp.int32, (L, L), 0)
           >= lax.broadcasted_iota(jnp.int32, (L, L), 1)).astype(BF16)
    first_head = lax.broadcasted_iota(jnp.int32, (L, W), 1) < RW_HEAD

    def head_sum(x):
        s0 = jnp.sum(jnp.where(first_head, x, 0.0), axis=-1, keepdims=True)
        s1 = jnp.sum(jnp.where(first_head, 0.0, x), axis=-1, keepdims=True)
        return jnp.where(first_head, s0, s1)

    def stack_heads(x):
        return jnp.concatenate([jnp.where(first_head, x, 0.0), jnp.where(first_head, 0.0, x)], axis=0)

    ri = lax.broadcasted_iota(jnp.int32, (2 * L, 2 * L), 0)
    ci = lax.broadcasted_iota(jnp.int32, (2 * L, 2 * L), 1)
    t_row, t_col = ri & (L - 1), ci & (L - 1)
    strict = t_row > t_col
    incl = t_row >= t_col
    eye = (ri == ci).astype(F32)
    blk_masks = []
    s = INV_BASE
    while s <= L:
        blk_masks.append((ri // s) == (ci // s))
        s *= 2

    pairs = range(n_pairs)
    loc = [slice(p * W, (p + 1) * W) for p in pairs]

    def seq_phases(si):
        r = token_shift(r_ref[si], pr_scr.at[si], mur_ref)
        k = token_shift(k_ref[si], pk_scr.at[si], muk_ref)
        v = token_shift(v_ref[si], pv_scr.at[si], muv_ref)
        lo = token_shift(lo_ref[si], plo_scr.at[si], mulo_ref)
        logd, a, k2 = _rwkv_token_terms(r, k, lo, w0_ref[...], w2p_ref[...], a0_ref[...], a2p_ref[...],
                                        ka_ref[...])
        kk = k * kk_ref[...]
        hi = logd.astype(BF16)
        rem = logd - hi.astype(F32)
        mid = rem.astype(BF16)
        low = (rem - mid.astype(F32)).astype(BF16)
        lp = (jnp.dot(tri, hi, preferred_element_type=F32)
              + jnp.dot(tri, mid, preferred_element_type=F32)
              + jnp.dot(tri, low, preferred_element_type=F32))
        p_inc = jnp.exp(lp)
        p_exc = jnp.exp(lp - logd)
        p_inv = jnp.exp(-lp)
        yield

        lhs, nm, mm, ab, vs, kb, p_end = [], [], [], [], [], [], []
        for sl in loc:
            kk_p = kk[:, sl]
            kkn = kk_p * lax.rsqrt(jnp.maximum(head_sum(kk_p * kk_p), 1e-24))
            pinc, pinv = p_inc[:, sl], p_inv[:, sl]
            kd = k2[:, sl] * pinv
            bd = kkn * a[:, sl] * pinv
            pe = pinc[L - 1:L, :]
            lhs_p = jnp.concatenate([stack_heads(kkn * p_exc[:, sl]), stack_heads(r[:, sl] * pinc)],
                                    axis=0).astype(BF16)
            rhs_p = jnp.concatenate([stack_heads(bd), stack_heads(kd)], axis=0).astype(BF16)
            amat = lax.dot_general(lhs_p, rhs_p, _NT, preferred_element_type=F32)
            lhs.append(lhs_p)
            nm.append(jnp.where(strict, amat[:2 * L, :2 * L], 0.0))
            mm.append(jnp.where(strict, amat[:2 * L, 2 * L:], 0.0).astype(BF16))
            ab.append(jnp.concatenate([jnp.where(incl, amat[2 * L:, 2 * L:], 0.0),
                                       -jnp.where(incl, amat[2 * L:, :2 * L], 0.0)], axis=1).astype(BF16))
            vs.append(stack_heads(v[:, sl]).astype(BF16))
            kb.append(jnp.concatenate([stack_heads(kd * pe), stack_heads(bd * pe)], axis=0).astype(BF16))
            p_end.append(pe)
            yield

        d = [jnp.where(blk_masks[0], n_p, 0.0).astype(BF16) for n_p in nm]
        x = [eye - d_p.astype(F32) for d_p in d]
        pw = [jnp.dot(d_p, d_p, preferred_element_type=F32) for d_p in d]
        yield
        s = 2
        while s < INV_BASE:
            x = [x_p + _dot(x_p, pw_p) for x_p, pw_p in zip(x, pw)]
            s *= 2
            if s < INV_BASE:
                pw = [_dot(pw_p, pw_p) for pw_p in pw]
            yield
        for lvl in range(1, len(blk_masks)):
            off = blk_masks[lvl] & ~blk_masks[lvl - 1]
            xc = [_dot(x_p, jnp.where(off, n_p, 0.0)) for x_p, n_p in zip(x, nm)]
            yield
            x = [x_p - _dot(xc_p, x_p) for x_p, xc_p in zip(x, xc)]
            yield

        hs = [h_scr[si, p] for p in pairs]
        lh = [jnp.dot(lhs_p, hs_p.astype(BF16), preferred_element_type=F32) for lhs_p, hs_p in zip(lhs, hs)]
        mv = [jnp.dot(mm_p, vs_p, preferred_element_type=F32) for mm_p, vs_p in zip(mm, vs)]
        yield
        u = [_dot(x_p, lh_p[:2 * L] + mv_p).astype(BF16) for x_p, lh_p, mv_p in zip(x, lh, mv)]
        yield
        o_st = [lh_p[2 * L:] + jnp.dot(ab_p, jnp.concatenate([vs_p, u_p], axis=0), preferred_element_type=F32)
                for lh_p, ab_p, vs_p, u_p in zip(lh, ab, vs, u)]
        for p in pairs:
            p_end_col = jnp.sum(eye * p_end[p], axis=1, keepdims=True)
            h_scr[si, p] = p_end_col * hs[p] + lax.dot_general(
                kb[p], jnp.concatenate([vs[p], -u[p]], axis=0), _TN, preferred_element_type=F32)
        yield

        for p, sl in enumerate(loc):
            o = o_st[p][:L] + o_st[p][L:]
            mu = head_sum(o) * (1.0 / RW_HEAD)
            var = head_sum((o - mu) ** 2) * (1.0 / RW_HEAD)
            o = (o - mu) * lax.rsqrt(var + GN_EPS) * gw_ref[:, sl] + gb_ref[:, sl]
            o = o + head_sum(r[:, sl] * k2[:, sl] * rk_ref[:, sl]) * v[:, sl]
            z = z_ref[si, :, LANES + p * W:LANES + (p + 1) * W]
            o_ref[si, :, sl] = (o * _silu(z)).astype(o_ref.dtype)
        yield

    gens = [seq_phases(si) for si in range(n_seqs)]
    started = 0
    live = []
    step = 0
    while started < n_seqs or live:
        if started < n_seqs and step % RW_SEQ_LAG == 0:
            live.append(gens[started])
            started += 1
        for g in list(live):
            if next(g, "done") == "done":
                live.remove(g)
        step += 1

    @pl.when(c == pl.num_programs(1) - 1)
    def _():
        for si in range(n_seqs):
            for p in pairs:
                st = h_scr[si, p].T
                hs_ref[si, 2 * p] = st[:RW_HEAD, :RW_HEAD]
                hs_ref[si, 2 * p + 1] = st[RW_HEAD:, RW_HEAD:]


def _rwkv_sequence(p_main, p_lz, col, mu, w0, w2p, a0, a2p, k_k, k_a, r_k, gn_w, gn_b):
    b, t, _ = p_main.shape
    wlz = p_lz.shape[-1]
    d = w0.shape[-1]
    n_heads = d // RW_HEAD
    n_pairs, pw = n_heads // 2, 2 * RW_HEAD
    L = RW_CHUNK
    ns = RW_SEQS if b % RW_SEQS == 0 else 1
    blk = lambda cb: pl.BlockSpec((ns, L, d), lambda i, j, cb=cb: (i, j, cb))
    vec = lambda n: pl.BlockSpec((1, n), lambda i, j: (0, 0))
    full = lambda shp: pl.BlockSpec(shp, lambda i, j: (0,) * len(shp))
    mu_r, mu_k, mu_v, mu_lo = mu
    o, hs = pl.pallas_call(
        functools.partial(_rwkv_chunk_kernel, n_heads=n_heads, n_seqs=ns),
        grid=(b // ns, t // L),
        in_specs=[blk(col["r"]), blk(col["k"]), blk(col["v"]),
                  pl.BlockSpec((ns, L, LANES), lambda i, j: (i, j, 0)),
                  pl.BlockSpec((ns, L, wlz), lambda i, j: (i, j, 0)),
                  vec(d), vec(d), vec(d), vec(LANES),
                  vec(d), full((LANES, d)), vec(d), full((LANES, d)),
                  vec(d), vec(d), vec(d), vec(d), vec(d)],
        out_specs=[pl.BlockSpec((ns, L, d), lambda i, j: (i, j, 0)),
                   pl.BlockSpec((ns, n_heads, RW_HEAD, RW_HEAD), lambda i, j: (i, 0, 0, 0))],
        out_shape=(jax.ShapeDtypeStruct((b, t, d), BF16),
                   jax.ShapeDtypeStruct((b, n_heads, RW_HEAD, RW_HEAD), F32)),
        scratch_shapes=[pltpu.VMEM((ns, n_pairs, pw, pw), F32),
                        pltpu.VMEM((ns, 1, d), F32), pltpu.VMEM((ns, 1, d), F32), pltpu.VMEM((ns, 1, d), F32),
                        pltpu.VMEM((ns, 1, LANES), F32)],
        compiler_params=_params("parallel", "arbitrary"),
        name="rwkv_sequence",
    )(p_main, p_main, p_main, p_lz, p_lz,
      mu_r, mu_k, mu_v, mu_lo, w0, w2p, a0, a2p, k_k, k_a, r_k, gn_w, gn_b)
    return o, hs


def _rwkv_step_prep_kernel(cr_ref, ck_ref, cv_ref, clo_ref, cz_ref, pr_ref, pk_ref, pv_ref, plo_ref,
                           mur_ref, muk_ref, muv_ref, mulo_ref,
                           w0_ref, w2p_ref, a0_ref, a2p_ref, kk_ref, ka_ref,
                           r_o, k2_o, v_o, kk_o, a_o, d_o, z_o):
    def lerp(c_ref, p_ref, mu_ref):
        cur = c_ref[...]
        return cur + (p_ref[...] - cur) * mu_ref[...]

    r = lerp(cr_ref, pr_ref, mur_ref)
    k = lerp(ck_ref, pk_ref, muk_ref)
    v = lerp(cv_ref, pv_ref, muv_ref)
    lo = lerp(clo_ref, plo_ref, mulo_ref)
    logd, a, k2 = _rwkv_token_terms(r, k, lo, w0_ref[...], w2p_ref[...], a0_ref[...], a2p_ref[...],
                                    ka_ref[...])
    r_o[...] = r.T
    k2_o[...] = k2.T
    v_o[...] = v.T
    kk_o[...] = (k * kk_ref[...]).T
    a_o[...] = a.T
    d_o[...] = jnp.exp(logd).T
    z_o[...] = cz_ref[:, LANES:].T


def _rwkv_step_kernel(s_ref, r_ref, k2_ref, v_ref, kk_ref, a_ref, d_ref, z_ref,
                      rk_ref, gw_ref, gb_ref, o_ref, s1_ref, o_scr):
    r, k2, v, dec = r_ref[...], k2_ref[...], v_ref[...], d_ref[...]
    kk = kk_ref[...]
    kkn = kk / jnp.maximum(jnp.sqrt(jnp.sum(kk * kk, axis=0, keepdims=True)), 1e-12)
    bvec = kkn * a_ref[...]
    for i in range(RW_HEAD):
        s = s_ref[i]
        sa = jnp.sum(s * kkn, axis=0, keepdims=True)
        s1 = s * dec - sa * bvec + v[i:i + 1, :] * k2
        s1_ref[i] = s1
        o_scr[pl.ds(i, 1), :] = jnp.sum(s1 * r, axis=0, keepdims=True)
    o = o_scr[...]
    mu = jnp.mean(o, axis=0, keepdims=True)
    var = jnp.mean((o - mu) ** 2, axis=0, keepdims=True)
    o = (o - mu) * lax.rsqrt(var + GN_EPS) * gw_ref[...] + gb_ref[...]
    o = o + jnp.sum(r * k2 * rk_ref[...], axis=0, keepdims=True) * v
    o_ref[...] = o * _silu(z_ref[...])


def _rwkv_step(p_rows, p_lz_rows, col, mu, w0, w2p, a0, a2p, k_k, k_a, r_k, gn_w, gn_b, s0):
    rows = s0.shape[0]
    wlz = p_lz_rows.shape[-1]
    d = w0.shape[-1]
    n_heads = d // RW_HEAD
    cur = lambda cb: pl.BlockSpec((rows, d), lambda i, cb=cb: (0, cb))
    prv = lambda cb: pl.BlockSpec((rows, d), lambda i, cb=cb: (1, cb))
    vec = lambda n: pl.BlockSpec((1, n), lambda i: (0, 0))
    mu_r, mu_k, mu_v, mu_lo = mu
    out = jax.ShapeDtypeStruct((d, rows), F32)
    terms = pl.pallas_call(
        _rwkv_step_prep_kernel,
        grid=(1,),
        in_specs=[cur(col["r"]), cur(col["k"]), cur(col["v"]),
                  pl.BlockSpec((rows, LANES), lambda i: (0, 0)),
                  pl.BlockSpec((rows, wlz), lambda i: (0, 0)),
                  prv(col["r"]), prv(col["k"]), prv(col["v"]),
                  pl.BlockSpec((rows, LANES), lambda i: (1, 0)),
                  vec(d), vec(d), vec(d), vec(LANES),
                  vec(d), pl.BlockSpec((LANES, d), lambda i: (0, 0)),
                  vec(d), pl.BlockSpec((LANES, d), lambda i: (0, 0)),
                  vec(d), vec(d)],
        out_specs=[pl.BlockSpec((d, rows), lambda i: (0, 0))] * 7,
        out_shape=(out,) * 7,
        compiler_params=_params("arbitrary"),
        name="rwkv_step_prep",
    )(p_rows, p_rows, p_rows, p_lz_rows, p_lz_rows, p_rows, p_rows, p_rows, p_lz_rows,
      mu_r, mu_k, mu_v, mu_lo, w0, w2p, a0, a2p, k_k, k_a)
    per_h = lambda: pl.BlockSpec((RW_HEAD, rows), lambda h: (h, 0))
    par = lambda: pl.BlockSpec((RW_HEAD, 1), lambda h: (h, 0))
    st = lambda: pl.BlockSpec((None, RW_HEAD, RW_HEAD, rows), lambda h: (h, 0, 0, 0))
    o_t, s1_t = pl.pallas_call(
        _rwkv_step_kernel,
        grid=(n_heads,),
        in_specs=[st()] + [per_h()] * 7 + [par()] * 3,
        out_specs=[per_h(), st()],
        out_shape=(jax.ShapeDtypeStruct((d, rows), F32),
                   jax.ShapeDtypeStruct((n_heads, RW_HEAD, RW_HEAD, rows), F32)),
        scratch_shapes=[pltpu.VMEM((RW_HEAD, rows), F32)],
        compiler_params=_params("parallel"),
        name="rwkv_step",
    )(jnp.transpose(s0, (1, 2, 3, 0)), *terms,
      r_k.reshape(d, 1), gn_w.reshape(d, 1), gn_b.reshape(d, 1))
    return o_t.T.astype(BF16), jnp.transpose(s1_t, (3, 0, 1, 2))


def _out_kernel(ys_ref, z_ref, wg_ref, bg_ref, or_ref, gs_ref, gr_ref, x_ref, gt_ref, w1_ref, w2_ref,
                fg_ref, y_ref, *, paired_blocks):
    ys = ys_ref[...]
    if paired_blocks:
        nb = ys.shape[1] // LANES
        pos = [2 * j if j < nb // 2 else 2 * (j - nb // 2) + 1 for j in range(nb)]
        ys = jnp.concatenate([ys[:, p * LANES:(p + 1) * LANES] for p in pos], axis=1)
    ys = jax.nn.gelu(ys, approximate=True)
    o_s = (ys * _sigmoid(_dot(ys, wg_ref[...]) + bg_ref[...]) * _silu(z_ref[...])).astype(BF16)
    mixed = (gs_ref[...].astype(F32) * jnp.dot(o_s, w1_ref[...], preferred_element_type=F32)
             + gr_ref[...].astype(F32) * jnp.dot(or_ref[...], w2_ref[...], preferred_element_type=F32))
    x = x_ref[...] + gt_ref[...] * mixed
    y_ref[...] = x * lax.rsqrt(jnp.mean(x * x, axis=-1, keepdims=True) + RMS_EPS) * fg_ref[...]


def _out_proj(y_s5, p_main, o_r, p_gate, col, x, mod, w_glu, b_glu, w_out, final_g, tt, paired_blocks):
    b, t, d = x.shape
    dh = o_r.shape[-1]
    tt = min(tt, t)
    tm = 1 if mod.shape[1] == 1 else tt
    gt_map = (lambda i, j: (i, 0, 2)) if tm == 1 else (lambda i, j: (i, j, 2))
    const = lambda shp, r=0: pl.BlockSpec(shp, lambda i, j: (r, 0), pipeline_mode=pl.Buffered(1))
    return pl.pallas_call(
        functools.partial(_out_kernel, paired_blocks=paired_blocks),
        grid=(b, t // tt),
        in_specs=[pl.BlockSpec((None, tt, dh), lambda i, j: (i, j, 0)),
                  pl.BlockSpec((None, tt, dh), lambda i, j: (i, j, col["z_s5"])),
                  const((dh, dh)), const((1, dh)),
                  pl.BlockSpec((None, tt, dh), lambda i, j: (i, j, 0)),
                  pl.BlockSpec((None, tt, d), lambda i, j: (i, j, col["g_s5"])),
                  pl.BlockSpec((None, tt, d), lambda i, j: (i, j, col["g_rw"])),
                  pl.BlockSpec((None, tt, d), lambda i, j: (i, j, 0)),
                  pl.BlockSpec((None, tm, d), gt_map),
                  const((dh, d), 0), const((dh, d), 1), const((1, d))],
        out_specs=pl.BlockSpec((None, tt, d), lambda i, j: (i, j, 0)),
        out_shape=jax.ShapeDtypeStruct((b, t, d), F32),
        compiler_params=_params("parallel", "parallel"),
        name="out_proj",
    )(y_s5, p_main, w_glu, b_glu.reshape(1, dh), o_r, p_gate, p_gate, x, mod, w_out, w_out,
      final_g.reshape(1, d))


def kernel(x_prompt, x_sample, c_prompt, c_sample, state_s5_re, state_s5_im, state_wkv, state_shift, norm_g, w_ada, b_ada, w_in, mu_rw, A_re, A_im, log_step, B_re, B_im, C_re, C_im, D_skip, w_glu, b_glu, w0, w2, a0, a2, k_k, k_a, r_k, gn_w, gn_b, w_out, final_g):
    depth = norm_g.shape[0]
    assert depth == 1
    bp, tp, d = x_prompt.shape
    bs = x_sample.shape[0]
    assert x_sample.shape[1] == 1
    dh = d // 2
    l = 0

    w_main, w_lz, w_gate = 5 * dh, 2 * LORA + dh, 2 * d
    col = {"u": 0, "z_s5": 1, "r": 2, "k": 3, "v": 4, "g_s5": 0, "g_rw": 1}
    w = w_in[l]

    def project(rows, rows2):
        return (_in_proj(rows, rows2, w, 0, w_main, w_main // 4, 1024),
                _in_proj(rows, rows2, w, w_main, w_lz, w_lz, 1024),
                _in_proj(rows, rows2, w, w_main + w_lz, w_gate, w_gate // 4, 1024, gate=True))

    mu = mu_rw[l]
    mu_parts = (mu[None, :dh], mu[None, dh:2 * dh], mu[None, 2 * dh:3 * dh], mu[None, 3 * dh:])
    zpad = jnp.zeros((LORA, dh), F32)
    w2p = jnp.concatenate([w2[l], zpad], axis=0).astype(BF16)
    a2p = jnp.concatenate([zpad, a2[l]], axis=0).astype(BF16)
    row = lambda x: x.reshape(1, -1)
    rw_params = (row(w0[l]), w2p, row(a0[l]), a2p, row(k_k[l]), row(k_a[l]), row(r_k[l]),
                 row(gn_w[l]), row(gn_b[l]))
    w_out_bf = w_out[l].astype(BF16)
    w_glu_bf = w_glu[l].astype(BF16)

    ab_re, ab_im, bbt_re, bbt_im = _s5_discretise(A_re[l], A_im[l], log_step[l], B_re[l], B_im[l])
    b_w, c_w = _s5_block_weights(bbt_re, bbt_im, C_re[l], C_im[l])
    flat = lambda x: x.reshape(1, -1)
    ab_re, ab_im = flat(ab_re), flat(ab_im)
    d_skip = flat(D_skip[l])

    mod = _mod(jnp.concatenate([c_prompt, c_sample], axis=0), w_ada[l], b_ada[l])
    mod_p = mod[:bp].reshape(bp, 1, 3 * d)
    mod_s = mod[bp:].reshape(1, bs, 3 * d)

    h_p = _modulated_norm(x_prompt, norm_g[l], mod_p, BF16, 512)
    shift_p = _modulated_norm(x_prompt[:, tp - 1:, :], norm_g[l], mod_p, F32, 1)[:, 0]
    xs = x_sample.reshape(1, bs, d)
    h_s = _modulated_norm(xs, norm_g[l], mod_s, F32, bs)[0]
    a_s = jnp.concatenate([h_s, state_shift[l]], axis=0).astype(BF16)
    (pm, ps), (plz, pslz), (pg, psg) = project(h_p.reshape(bp * tp, d), a_s)

    pm3 = pm.reshape(bp, tp, -1)
    y_s5, xre_p, xim_p = _s5_sequence(pm3, b_w, c_w, ab_re, ab_im, d_skip)
    o_r, hs_p = _rwkv_sequence(pm3, plz.reshape(bp, tp, -1), col, mu_parts, *rw_params)
    y_prompt = _out_proj(y_s5, pm3, o_r, pg.reshape(bp, tp, -1), col, x_prompt, mod_p,
                         w_glu_bf, b_glu[l], w_out_bf, final_g, 512, True)
    g_s5 = A_re.shape[1]
    s5_shape = (1, bp, g_s5, P_S5)
    wkv_p = hs_p[None]

    y_s5s, xre_s, xim_s = _s5_step(ps, b_w, c_w, ab_re, ab_im, d_skip,
                                   state_s5_re[l].reshape(bs, -1), state_s5_im[l].reshape(bs, -1))
    o_rs, wkv_s = _rwkv_step(ps, pslz, col, mu_parts, *rw_params, state_wkv[l])
    y_sample = _out_proj(y_s5s[None], ps.reshape(2, bs, -1), o_rs[None], psg.reshape(2, bs, -1), col, xs, mod_s,
                         w_glu_bf, b_glu[l], w_out_bf, final_g, bs, False)
    y_sample = y_sample.reshape(bs, 1, d)

    return (y_prompt, y_sample,
            xre_p.reshape(s5_shape), xim_p.reshape(s5_shape), wkv_p, shift_p[None],
            xre_s.reshape(1, bs, g_s5, P_S5), xim_s.reshape(1, bs, g_s5, P_S5), wkv_s[None], h_s[None])
```

```python
import functools

import jax
import jax.numpy as jnp
from jax import lax
from jax.experimental import pallas as pl
from jax.experimental.pallas import tpu as pltpu

F32 = jnp.float32
BF16 = jnp.bfloat16

RMS_EPS = 1e-6
GN_EPS = 64e-5
S5_GROUP = 16
P_S5 = 64
RW_HEAD = 64
LORA = 64
LANES = 128
GROUPS_PER_BLOCK = LANES // S5_GROUP
STATES_PER_BLOCK = GROUPS_PER_BLOCK * P_S5
S5_TC = 512
RW_CHUNK = 64
INV_BASE = 8
RW_SEQS = 4
RW_SEQ_LAG = 1

_NT = (((1,), (1,)), ((), ()))
_TN = (((0,), (0,)), ((), ()))


def _dot(a, b):
    return jnp.dot(a.astype(BF16), b.astype(BF16), preferred_element_type=F32)


def _sigmoid(x):
    return 0.5 * jnp.tanh(0.5 * x) + 0.5


def _silu(x):
    return x * _sigmoid(x)


def _params(*sem):
    return pltpu.CompilerParams(dimension_semantics=sem, vmem_limit_bytes=56 * 1024 * 1024)


def _mod_kernel(c_ref, w_ref, b_ref, o_ref):
    o_ref[...] = _dot(_silu(c_ref[...]), w_ref[...]) + b_ref[...]


def _mod(c, w_bf, b):
    rows, d = c.shape
    n = w_bf.shape[1]
    tn = 1024
    return pl.pallas_call(
        _mod_kernel,
        grid=(n // tn,),
        in_specs=[pl.BlockSpec((rows, d), lambda j: (0, 0)),
                  pl.BlockSpec((d, tn), lambda j: (0, j)),
                  pl.BlockSpec((1, tn), lambda j: (0, j))],
        out_specs=pl.BlockSpec((rows, tn), lambda j: (0, j)),
        out_shape=jax.ShapeDtypeStruct((rows, n), F32),
        compiler_params=_params("parallel"),
        name="adaln_mod",
    )(c, w_bf, b.reshape(1, n))


def _h_kernel(x_ref, g_ref, sh_ref, sc_ref, h_ref):
    x = x_ref[...]
    y = x * lax.rsqrt(jnp.mean(x * x, axis=-1, keepdims=True) + RMS_EPS) * g_ref[...]
    h_ref[...] = (y * (1.0 + sc_ref[...]) + sh_ref[...]).astype(h_ref.dtype)


def _modulated_norm(x, g, mod, out_dtype, tt):
    b, t, d = x.shape
    tt = min(tt, t)
    tm = 1 if mod.shape[1] == 1 else tt
    mod_map = (lambda i, j: (i, 0, 0)) if tm == 1 else (lambda i, j: (i, j, 0))
    mod_map1 = (lambda i, j: (i, 0, 1)) if tm == 1 else (lambda i, j: (i, j, 1))
    return pl.pallas_call(
        _h_kernel,
        grid=(b, t // tt),
        in_specs=[pl.BlockSpec((None, tt, d), lambda i, j: (i, j, 0)),
                  pl.BlockSpec((1, d), lambda i, j: (0, 0)),
                  pl.BlockSpec((None, tm, d), mod_map),
                  pl.BlockSpec((None, tm, d), mod_map1)],
        out_specs=pl.BlockSpec((None, tt, d), lambda i, j: (i, j, 0)),
        out_shape=jax.ShapeDtypeStruct((b, t, d), out_dtype),
        compiler_params=_params("parallel", "parallel"),
        name="modulated_norm",
    )(x, g.reshape(1, d), mod, mod)


def _in_proj_kernel(a_ref, a2_ref, w_ref, o_ref, o2_ref, wbf_ref, *, gate, n_main):
    i = pl.program_id(1)

    @pl.when(i == 0)
    def _():
        wbf_ref[...] = w_ref[...].astype(BF16)

    def project(x_ref, out_ref):
        p = jnp.dot(x_ref[...], wbf_ref[...], preferred_element_type=F32)
        out_ref[...] = (_sigmoid(p) if gate else p).astype(out_ref.dtype)

    @pl.when(i < n_main)
    def _():
        project(a_ref, o_ref)

    @pl.when(i == n_main)
    def _():
        project(a2_ref, o2_ref)


def _in_proj(a, a2, w, col0, width, tn, tm, gate=False):
    m, k = a.shape
    m2 = a2.shape[0]
    tm = min(tm, m)
    assert width % tn == 0 and m % tm == 0 and col0 % LANES == 0
    n_main = m // tm
    last = n_main - 1
    dt = BF16 if gate else F32
    return pl.pallas_call(
        functools.partial(_in_proj_kernel, gate=gate, n_main=n_main),
        grid=(width // tn, n_main + 1),
        in_specs=[pl.BlockSpec((tm, k), lambda j, i: (jnp.minimum(i, last), 0)),
                  pl.BlockSpec((m2, k), lambda j, i: (0, 0)),
                  pl.BlockSpec((pl.Element(k), pl.Element(tn)), lambda j, i: (0, pl.multiple_of(col0 + j * tn, LANES)))],
        out_specs=[pl.BlockSpec((tm, tn), lambda j, i: (jnp.minimum(i, last), j)),
                   pl.BlockSpec((m2, tn), lambda j, i: (0, j))],
        out_shape=(jax.ShapeDtypeStruct((m, width), dt), jax.ShapeDtypeStruct((m2, width), dt)),
        scratch_shapes=[pltpu.VMEM((k, tn), BF16)],
        compiler_params=_params("parallel", "arbitrary"),
        name="in_proj",
    )(a, a2, w)


def _s5_weights_kernel(*refs, n_b):
    (are0, are1, aim0, aim1, ls0, ls1, bre0, bre1, bim0, bim1, cre0, cre1, cim0, cim1,
     bw_ref, cw_ref, abr_ref, abi_ref, apr_ref, api_ref) = refs
    n_gl = GROUPS_PER_BLOCK
    n_lt = STATES_PER_BLOCK // LANES

    def discretise(are_ref, aim_ref, ls_ref, bre_ref, bim_ref):
        step = jnp.exp(ls_ref[...])
        lam_re = jnp.minimum(are_ref[...], -1e-4)
        lam_im = aim_ref[...]
        mag = jnp.exp(lam_re * step)
        ab_re = mag * jnp.cos(lam_im * step)
        ab_im = mag * jnp.sin(lam_im * step)
        den = lam_re * lam_re + lam_im * lam_im
        f_re = ((ab_re - 1.0) * lam_re + ab_im * lam_im) / den
        f_im = (ab_im * lam_re - (ab_re - 1.0) * lam_im) / den
        br, bi = bre_ref[...], bim_ref[...]
        return ab_re, ab_im, f_re * br - f_im * bi, f_re * bi + f_im * br

    halves = [discretise(are0, aim0, ls0, bre0, bim0), discretise(are1, aim1, ls1, bre1, bim1)]
    zero = jnp.zeros((S5_GROUP, P_S5), F32)

    def band(t_re, t_im, gl):
        pieces = []
        for lt in range(n_lt):
            for tile in (t_re, t_im):
                for half in range(2):
                    pieces.append(tile if 2 * lt + half == gl else zero)
        return jnp.concatenate(pieces, axis=1)

    def block_rows(tiles_re, tiles_im):
        return jnp.concatenate([band(tiles_re[h][gl], tiles_im[h][gl], gl)
                                for h in range(2) for gl in range(n_gl)], axis=0)

    bw_ref[...] = block_rows([h[2] for h in halves], [h[3] for h in halves]).astype(BF16)
    cw_ref[...] = block_rows([cre0[...], cre1[...]], [-cim0[...], -cim1[...]]).T.astype(BF16)

    for h in range(2):
        for a_ref, a in ((abr_ref, halves[h][0]), (abi_ref, halves[h][1])):
            a_ref[h] = jnp.concatenate([a[gl] for gl in range(n_gl)], axis=1)
    for p_ref, a_ref in ((apr_ref, abr_ref), (api_ref, abi_ref)):
        p_ref[...] = jnp.concatenate([a_ref[0], a_ref[1]] * n_b, axis=0)


def _s5_weights(a_re, a_im, log_step, b_re, b_im, c_re, c_im, n_b):
    g, p = a_re.shape
    c = b_re.shape[-1]
    nq = g // GROUPS_PER_BLOCK // 2
    ns = STATES_PER_BLOCK
    args, specs = [], []
    for x, shp in ((a_re.reshape(g, 1, p), (1, p)), (a_im.reshape(g, 1, p), (1, p)),
                   (log_step.reshape(g, 1, 1), (1, 1)),
                   (jnp.swapaxes(b_re, 1, 2), (c, p)), (jnp.swapaxes(b_im, 1, 2), (c, p)),
                   (c_re, (c, p)), (c_im, (c, p))):
        for half in range(2):
            args.append(x)
            specs.append(pl.BlockSpec((GROUPS_PER_BLOCK,) + shp, lambda q, half=half: (q + half * nq, 0, 0)))
    return pl.pallas_call(
        functools.partial(_s5_weights_kernel, n_b=n_b),
        grid=(nq,),
        in_specs=specs,
        out_specs=[pl.BlockSpec((None, 2 * LANES, 2 * ns), lambda q: (q, 0, 0)),
                   pl.BlockSpec((None, 2 * ns, 2 * LANES), lambda q: (q, 0, 0)),
                   pl.BlockSpec((None, 2, 1, ns), lambda q: (q, 0, 0, 0)),
                   pl.BlockSpec((None, 2, 1, ns), lambda q: (q, 0, 0, 0)),
                   pl.BlockSpec((None, 2 * n_b, ns), lambda q: (q, 0, 0)),
                   pl.BlockSpec((None, 2 * n_b, ns), lambda q: (q, 0, 0))],
        out_shape=(jax.ShapeDtypeStruct((nq, 2 * LANES, 2 * ns), BF16),
                   jax.ShapeDtypeStruct((nq, 2 * ns, 2 * LANES), BF16),
                   jax.ShapeDtypeStruct((nq, 2, 1, ns), F32), jax.ShapeDtypeStruct((nq, 2, 1, ns), F32),
                   jax.ShapeDtypeStruct((nq, 2 * n_b, ns), F32), jax.ShapeDtypeStruct((nq, 2 * n_b, ns), F32)),
        compiler_params=_params("parallel"),
        name="s5_weights",
    )(*args)


def _s5_seq_kernel(u0_ref, u1_ref, bw_ref, cw_ref, are_ref, aim_ref, d_ref,
                   y_ref, xre_ref, xim_ref, lhs_ref, bu_ref, y_scr, x_scr, *, tc, n_b):
    n_lt = STATES_PER_BLOCK // LANES
    slots = 2 * n_b
    re_l = lambda lt: slice(2 * lt * LANES, (2 * lt + 1) * LANES)
    im_l = lambda lt: slice((2 * lt + 1) * LANES, (2 * lt + 2) * LANES)
    nat = lambda lt: slice(lt * LANES, (lt + 1) * LANES)
    c = pl.program_id(1)

    @pl.when(c == 0)
    def _():
        x_scr[...] = jnp.zeros_like(x_scr)
        lhs_ref[...] = jnp.zeros_like(lhs_ref)

    for b in range(n_b):
        lhs_ref.at[0][pl.ds(2 * b, tc, stride=slots), :] = u0_ref[b]
        lhs_ref.at[1][pl.ds(2 * b + 1, tc, stride=slots), :] = u1_ref[b]
    u_rows = lhs_ref[0] + lhs_ref[1]
    lhs = jnp.concatenate([lhs_ref[0], lhs_ref[1]], axis=1)
    bu_ref[...] = _dot(lhs, bw_ref[...])

    a_re = [are_ref[:, nat(lt)] for lt in range(n_lt)]
    a_im = [aim_ref[:, nat(lt)] for lt in range(n_lt)]
    x = [(x_scr[:, re_l(lt)], x_scr[:, im_l(lt)]) for lt in range(n_lt)]
    for t in range(tc):
        rows = slice(t * slots, (t + 1) * slots)
        for lt in range(n_lt):
            xr, xi = x[lt]
            nr = a_re[lt] * xr - a_im[lt] * xi + bu_ref[rows, re_l(lt)]
            ni = a_re[lt] * xi + a_im[lt] * xr + bu_ref[rows, im_l(lt)]
            bu_ref[rows, re_l(lt)] = nr
            bu_ref[rows, im_l(lt)] = ni
            x[lt] = (nr, ni)
    for lt in range(n_lt):
        x_scr[:, re_l(lt)] = x[lt][0]
        x_scr[:, im_l(lt)] = x[lt][1]

    yf = _dot(bu_ref[...], cw_ref[...])
    first_half = (lax.broadcasted_iota(jnp.int32, (tc * slots, LANES), 0) & 1) == 0
    y = jnp.where(first_half, yf[:, :LANES], yf[:, LANES:])
    skip = (u_rows.reshape(tc, slots, LANES) * d_ref[...][None]).reshape(tc * slots, LANES)
    y_scr[...] = y + skip
    for b in range(n_b):
        y_ref[b, :, :LANES] = y_scr[pl.ds(2 * b, tc, stride=slots), :]
        y_ref[b, :, LANES:] = y_scr[pl.ds(2 * b + 1, tc, stride=slots), :]

    @pl.when(c == pl.num_programs(1) - 1)
    def _():
        for lt in range(n_lt):
            xre_ref[:, nat(lt)] = x[lt][0]
            xim_ref[:, nat(lt)] = x[lt][1]


def _s5_sequence(p_main, bw2, cw2, a_slot_re, a_slot_im, d_skip):
    b, t, _ = p_main.shape
    nq = bw2.shape[0]
    nb = 2 * nq
    ns = STATES_PER_BLOCK
    slots = 2 * b
    assert slots == 8, "one 8-row tile must hold every (sequence, half) slot"
    tc = min(S5_TC, t)
    d_blk = d_skip.reshape(nb, LANES)
    d_slot = jnp.tile(jnp.stack([d_blk[:nq], d_blk[nq:]], axis=1), (1, b, 1))
    slot_vec = lambda n: pl.BlockSpec((None, slots, n), lambda q, c: (q, 0, 0))
    y, xre, xim = pl.pallas_call(
        functools.partial(_s5_seq_kernel, tc=tc, n_b=b),
        grid=(nq, t // tc),
        in_specs=[pl.BlockSpec((b, tc, LANES), lambda q, c: (0, c, q)),
                  pl.BlockSpec((b, tc, LANES), lambda q, c: (0, c, q + nq)),
                  pl.BlockSpec((None, 2 * LANES, 2 * ns), lambda q, c: (q, 0, 0)),
                  pl.BlockSpec((None, 2 * ns, 2 * LANES), lambda q, c: (q, 0, 0)),
                  slot_vec(ns), slot_vec(ns), slot_vec(LANES)],
        out_specs=[pl.BlockSpec((b, tc, 2 * LANES), lambda q, c: (0, c, q)),
                   slot_vec(ns), slot_vec(ns)],
        out_shape=(jax.ShapeDtypeStruct((b, t, nb * LANES), F32),
                   jax.ShapeDtypeStruct((nq, slots, ns), F32),
                   jax.ShapeDtypeStruct((nq, slots, ns), F32)),
        scratch_shapes=[pltpu.VMEM((2, tc * slots, LANES), F32),
                        pltpu.VMEM((tc * slots, 2 * ns), F32),
                        pltpu.VMEM((tc * slots, LANES), F32),
                        pltpu.VMEM((slots, 2 * ns), F32)],
        compiler_params=_params("parallel", "arbitrary"),
        name="s5_sequence",
    )(p_main, p_main, bw2, cw2, a_slot_re, a_slot_im, d_slot)

    def state(x):
        x = x.reshape(nq, b, 2, ns)
        return jnp.transpose(x, (1, 2, 0, 3)).reshape(b, nb * ns)

    return y, state(xre), state(xim)


def _s5_step_kernel(u_ref, bw_ref, cw_ref, abr_ref, abi_ref, d_ref, x0r_ref, x0i_ref,
                    y_ref, x1r_ref, x1i_ref):
    n_lt = STATES_PER_BLOCK // LANES
    u = u_ref[...]
    bu = _dot(u, bw_ref[...])
    tiles = []
    for lt in range(n_lt):
        nat = slice(lt * LANES, (lt + 1) * LANES)
        ar, ai = abr_ref[:, nat], abi_ref[:, nat]
        x0r, x0i = x0r_ref[:, nat], x0i_ref[:, nat]
        xr = ar * x0r - ai * x0i + bu[:, 2 * lt * LANES:(2 * lt + 1) * LANES]
        xi = ar * x0i + ai * x0r + bu[:, (2 * lt + 1) * LANES:(2 * lt + 2) * LANES]
        x1r_ref[:, nat] = xr
        x1i_ref[:, nat] = xi
        tiles += [xr, xi]
    y_ref[...] = _dot(jnp.concatenate(tiles, axis=1), cw_ref[...]) + d_ref[...] * u


def _s5_step(p_rows, bw2, cw2, a_blk_re, a_blk_im, d_skip, x0_re, x0_im):
    rows = x0_re.shape[0]
    nq = bw2.shape[0]
    nb = 2 * nq
    ns = STATES_PER_BLOCK
    vec = lambda: pl.BlockSpec((None, None, 1, ns), lambda j: (j % nq, j // nq, 0, 0))
    st = lambda: pl.BlockSpec((rows, ns), lambda j: (0, j))
    return pl.pallas_call(
        _s5_step_kernel,
        grid=(nb,),
        in_specs=[pl.BlockSpec((rows, LANES), lambda j: (0, j)),
                  pl.BlockSpec((None, LANES, 2 * ns), lambda j: (j % nq, j // nq, 0)),
                  pl.BlockSpec((None, 2 * ns, LANES), lambda j: (j % nq, 0, j // nq)),
                  vec(), vec(),
                  pl.BlockSpec((1, LANES), lambda j: (0, j)),
                  st(), st()],
        out_specs=[pl.BlockSpec((rows, LANES), lambda j: (0, j)), st(), st()],
        out_shape=(jax.ShapeDtypeStruct((rows, nb * LANES), F32),
                   jax.ShapeDtypeStruct((rows, nb * ns), F32),
                   jax.ShapeDtypeStruct((rows, nb * ns), F32)),
        compiler_params=_params("parallel"),
        name="s5_step",
    )(p_rows, bw2, cw2, a_blk_re, a_blk_im, d_skip, x0_re, x0_im)


def _softplus(x):
    return jnp.maximum(x, 0.0) + jnp.log(1.0 + jnp.exp(-jnp.abs(x)))


def _rwkv_token_terms(r, k, lo, w0, w2p, a0, a2p, k_a):
    w = -_softplus(-(w0 + _dot(jnp.tanh(lo), w2p))) - 0.5
    logd = -jnp.exp(w)
    a = _sigmoid(a0 + _dot(lo, a2p))
    k2 = k * (1.0 + (a - 1.0) * k_a)
    return logd, a, k2


def _rwkv_chunk_kernel(r_ref, k_ref, v_ref, lo_ref, z_ref,
                       mur_ref, muk_ref, muv_ref, mulo_ref,
                       w0_ref, w2p_ref, a0_ref, a2p_ref, kk_ref, ka_ref, rk_ref, gw_ref, gb_ref,
                       o_ref, hs_ref,
                       h_scr, pr_scr, pk_scr, pv_scr, plo_scr, *, n_heads, n_seqs):
    L = RW_CHUNK
    W = 2 * RW_HEAD
    n_pairs = n_heads // 2
    c = pl.program_id(1)

    @pl.when(c == 0)
    def _():
        h_scr[...] = jnp.zeros_like(h_scr)
        pr_scr[...] = jnp.zeros_like(pr_scr)
        pk_scr[...] = jnp.zeros_like(pk_scr)
        pv_scr[...] = jnp.zeros_like(pv_scr)
        plo_scr[...] = jnp.zeros_like(plo_scr)

    row1 = lax.broadcasted_iota(jnp.int32, (L, 1), 0)

    def token_shift(cur, prev_scr, mu_ref):
        prev = jnp.where(row1 == 0, prev_scr[...], pltpu.roll(cur, 1, 0))
        prev_scr[...] = cur[L - 1:L, :]
        return cur + (prev - cur) * mu_ref[...]

    tri = (lax.broadcasted_iota(jnp.int32, (L, L), 0)
           >= lax.broadcasted_iota(jnp.int32, (L, L), 1)).astype(BF16)
    first_head = lax.broadcasted_iota(jnp.int32, (L, W), 1) < RW_HEAD

    def head_sum(x):
        s0 = jnp.sum(jnp.where(first_head, x, 0.0), axis=-1, keepdims=True)
        s1 = jnp.sum(jnp.where(first_head, 0.0, x), axis=-1, keepdims=True)
        return jnp.where(first_head, s0, s1)

    def stack_heads(x):
        return jnp.concatenate([jnp.where(first_head, x, 0.0), jnp.where(first_head, 0.0, x)], axis=0)

    ri = lax.broadcasted_iota(jnp.int32, (2 * L, 2 * L), 0)
    ci = lax.broadcasted_iota(jnp.int32, (2 * L, 2 * L), 1)
    t_row, t_col = ri & (L - 1), ci & (L - 1)
    strict = t_row > t_col
    incl = t_row >= t_col
    eye = (ri == ci).astype(F32)
    blk_masks = []
    s = INV_BASE
    while s <= L:
        blk_masks.append((ri // s) == (ci // s))
        s *= 2

    pairs = range(n_pairs)
    loc = [slice(p * W, (p + 1) * W) for p in pairs]

    def seq_phases(si):
        r = token_shift(r_ref[si], pr_scr.at[si], mur_ref)
        k = token_shift(k_ref[si], pk_scr.at[si], muk_ref)
        v = token_shift(v_ref[si], pv_scr.at[si], muv_ref)
        lo = token_shift(lo_ref[si], plo_scr.at[si], mulo_ref)
        logd, a, k2 = _rwkv_token_terms(r, k, lo, w0_ref[...], w2p_ref[...], a0_ref[...], a2p_ref[...],
                                        ka_ref[...])
        kk = k * kk_ref[...]
        hi = logd.astype(BF16)
        rem = logd - hi.astype(F32)
        mid = rem.astype(BF16)
        low = (rem - mid.astype(F32)).astype(BF16)
        lp = (jnp.dot(tri, hi, preferred_element_type=F32)
              + jnp.dot(tri, mid, preferred_element_type=F32)
              + jnp.dot(tri, low, preferred_element_type=F32))
        p_inc = jnp.exp(lp)
        p_exc = jnp.exp(lp - logd)
        p_inv = jnp.exp(-lp)
        yield

        lhs, nm, mm, ab, vs, kb, p_end = [], [], [], [], [], [], []
        for sl in loc:
            kk_p = kk[:, sl]
            kkn = kk_p * lax.rsqrt(jnp.maximum(head_sum(kk_p * kk_p), 1e-24))
            pinc, pinv = p_inc[:, sl], p_inv[:, sl]
            kd = k2[:, sl] * pinv
            bd = kkn * a[:, sl] * pinv
            pe = pinc[L - 1:L, :]
            lhs_p = jnp.concatenate([stack_heads(kkn * p_exc[:, sl]), stack_heads(r[:, sl] * pinc)],
                                    axis=0).astype(BF16)
            rhs_p = jnp.concatenate([stack_heads(bd), stack_heads(kd)], axis=0).astype(BF16)
            amat = lax.dot_general(lhs_p, rhs_p, _NT, preferred_element_type=F32)
            lhs.append(lhs_p)
            nm.append(jnp.where(strict, amat[:2 * L, :2 * L], 0.0))
            mm.append(jnp.where(strict, amat[:2 * L, 2 * L:], 0.0).astype(BF16))
            ab.append(jnp.concatenate([jnp.where(incl, amat[2 * L:, 2 * L:], 0.0),
                                       -jnp.where(incl, amat[2 * L:, :2 * L], 0.0)], axis=1).astype(BF16))
            vs.append(stack_heads(v[:, sl]).astype(BF16))
            kb.append(jnp.concatenate([stack_heads(kd * pe), stack_heads(bd * pe)], axis=0).astype(BF16))
            p_end.append(pe)
            yield

        d = [jnp.where(blk_masks[0], n_p, 0.0).astype(BF16) for n_p in nm]
        x = [eye - d_p.astype(F32) for d_p in d]
        pw = [jnp.dot(d_p, d_p, preferred_element_type=F32) for d_p in d]
        yield
        s = 2
        while s < INV_BASE:
            x = [x_p + _dot(x_p, pw_p) for x_p, pw_p in zip(x, pw)]
            s *= 2
            if s < INV_BASE:
                pw = [_dot(pw_p, pw_p) for pw_p in pw]
            yield
        for lvl in range(1, len(blk_masks)):
            off = blk_masks[lvl] & ~blk_masks[lvl - 1]
            xc = [_dot(x_p, jnp.where(off, n_p, 0.0)) for x_p, n_p in zip(x, nm)]
            yield
            x = [x_p - _dot(xc_p, x_p) for x_p, xc_p in zip(x, xc)]
            yield

        hs = [h_scr[si, p] for p in pairs]
        lh = [jnp.dot(lhs_p, hs_p.astype(BF16), preferred_element_type=F32) for lhs_p, hs_p in zip(lhs, hs)]
        mv = [jnp.dot(mm_p, vs_p, preferred_element_type=F32) for mm_p, vs_p in zip(mm, vs)]
        yield
        u = [_dot(x_p, lh_p[:2 * L] + mv_p).astype(BF16) for x_p, lh_p, mv_p in zip(x, lh, mv)]
        yield
        o_st = [lh_p[2 * L:] + jnp.dot(ab_p, jnp.concatenate([vs_p, u_p], axis=0), preferred_element_type=F32)
                for lh_p, ab_p, vs_p, u_p in zip(lh, ab, vs, u)]
        for p in pairs:
            p_end_col = jnp.sum(eye * p_end[p], axis=1, keepdims=True)
            h_scr[si, p] = p_end_col * hs[p] + lax.dot_general(
                kb[p], jnp.concatenate([vs[p], -u[p]], axis=0), _TN, preferred_element_type=F32)
        yield

        for p, sl in enumerate(loc):
            o = o_st[p][:L] + o_st[p][L:]
            mu = head_sum(o) * (1.0 / RW_HEAD)
            var = head_sum((o - mu) ** 2) * (1.0 / RW_HEAD)
            o = (o - mu) * lax.rsqrt(var + GN_EPS) * gw_ref[:, sl] + gb_ref[:, sl]
            o = o + head_sum(r[:, sl] * k2[:, sl] * rk_ref[:, sl]) * v[:, sl]
            z = z_ref[si, :, LANES + p * W:LANES + (p + 1) * W]
            o_ref[si, :, sl] = (o * _silu(z)).astype(o_ref.dtype)
        yield

    gens = [seq_phases(si) for si in range(n_seqs)]
    started = 0
    live = []
    step = 0
    while started < n_seqs or live:
        if started < n_seqs and step % RW_SEQ_LAG == 0:
            live.append(gens[started])
            started += 1
        for g in list(live):
            if next(g, "done") == "done":
                live.remove(g)
        step += 1

    @pl.when(c == pl.num_programs(1) - 1)
    def _():
        for si in range(n_seqs):
            for p in pairs:
                st = h_scr[si, p].T
                hs_ref[si, 2 * p] = st[:RW_HEAD, :RW_HEAD]
                hs_ref[si, 2 * p + 1] = st[RW_HEAD:, RW_HEAD:]


def _rwkv_sequence(p_main, p_lz, col, mu, w0, w2p, a0, a2p, k_k, k_a, r_k, gn_w, gn_b):
    b, t, _ = p_main.shape
    wlz = p_lz.shape[-1]
    d = w0.shape[-1]
    n_heads = d // RW_HEAD
    n_pairs, pw = n_heads // 2, 2 * RW_HEAD
    L = RW_CHUNK
    ns = RW_SEQS if b % RW_SEQS == 0 else 1
    blk = lambda cb: pl.BlockSpec((ns, L, d), lambda i, j, cb=cb: (i, j, cb))
    vec = lambda n: pl.BlockSpec((1, n), lambda i, j: (0, 0))
    full = lambda shp: pl.BlockSpec(shp, lambda i, j: (0,) * len(shp))
    mu_r, mu_k, mu_v, mu_lo = mu
    o, hs = pl.pallas_call(
        functools.partial(_rwkv_chunk_kernel, n_heads=n_heads, n_seqs=ns),
        grid=(b // ns, t // L),
        in_specs=[blk(col["r"]), blk(col["k"]), blk(col["v"]),
                  pl.BlockSpec((ns, L, LANES), lambda i, j: (i, j, 0)),
                  pl.BlockSpec((ns, L, wlz), lambda i, j: (i, j, 0)),
                  vec(d), vec(d), vec(d), vec(LANES),
                  vec(d), full((LANES, d)), vec(d), full((LANES, d)),
                  vec(d), vec(d), vec(d), vec(d), vec(d)],
        out_specs=[pl.BlockSpec((ns, L, d), lambda i, j: (i, j, 0)),
                   pl.BlockSpec((ns, n_heads, RW_HEAD, RW_HEAD), lambda i, j: (i, 0, 0, 0))],
        out_shape=(jax.ShapeDtypeStruct((b, t, d), BF16),
                   jax.ShapeDtypeStruct((b, n_heads, RW_HEAD, RW_HEAD), F32)),
        scratch_shapes=[pltpu.VMEM((ns, n_pairs, pw, pw), F32),
                        pltpu.VMEM((ns, 1, d), F32), pltpu.VMEM((ns, 1, d), F32), pltpu.VMEM((ns, 1, d), F32),
                        pltpu.VMEM((ns, 1, LANES), F32)],
        compiler_params=_params("parallel", "arbitrary"),
        name="rwkv_sequence",
    )(p_main, p_main, p_main, p_lz, p_lz,
      mu_r, mu_k, mu_v, mu_lo, w0, w2p, a0, a2p, k_k, k_a, r_k, gn_w, gn_b)
    return o, hs


def _rwkv_step_prep_kernel(cr_ref, ck_ref, cv_ref, clo_ref, cz_ref, pr_ref, pk_ref, pv_ref, plo_ref,
                           mur_ref, muk_ref, muv_ref, mulo_ref,
                           w0_ref, w2p_ref, a0_ref, a2p_ref, kk_ref, ka_ref,
                           r_o, k2_o, v_o, kk_o, a_o, d_o, z_o):
    def lerp(c_ref, p_ref, mu_ref):
        cur = c_ref[...]
        return cur + (p_ref[...] - cur) * mu_ref[...]

    r = lerp(cr_ref, pr_ref, mur_ref)
    k = lerp(ck_ref, pk_ref, muk_ref)
    v = lerp(cv_ref, pv_ref, muv_ref)
    lo = lerp(clo_ref, plo_ref, mulo_ref)
    logd, a, k2 = _rwkv_token_terms(r, k, lo, w0_ref[...], w2p_ref[...], a0_ref[...], a2p_ref[...],
                                    ka_ref[...])
    r_o[...] = r.T
    k2_o[...] = k2.T
    v_o[...] = v.T
    kk_o[...] = (k * kk_ref[...]).T
    a_o[...] = a.T
    d_o[...] = jnp.exp(logd).T
    z_o[...] = cz_ref[:, LANES:].T


def _rwkv_step_kernel(s_ref, r_ref, k2_ref, v_ref, kk_ref, a_ref, d_ref, z_ref,
                      rk_ref, gw_ref, gb_ref, o_ref, s1_ref, o_scr):
    r, k2, v, dec = r_ref[...], k2_ref[...], v_ref[...], d_ref[...]
    kk = kk_ref[...]
    kkn = kk / jnp.maximum(jnp.sqrt(jnp.sum(kk * kk, axis=0, keepdims=True)), 1e-12)
    bvec = kkn * a_ref[...]
    for i in range(RW_HEAD):
        s = s_ref[i]
        sa = jnp.sum(s * kkn, axis=0, keepdims=True)
        s1 = s * dec - sa * bvec + v[i:i + 1, :] * k2
        s1_ref[i] = s1
        o_scr[pl.ds(i, 1), :] = jnp.sum(s1 * r, axis=0, keepdims=True)
    o = o_scr[...]
    mu = jnp.mean(o, axis=0, keepdims=True)
    var = jnp.mean((o - mu) ** 2, axis=0, keepdims=True)
    o = (o - mu) * lax.rsqrt(var + GN_EPS) * gw_ref[...] + gb_ref[...]
    o = o + jnp.sum(r * k2 * rk_ref[...], axis=0, keepdims=True) * v
    o_ref[...] = o * _silu(z_ref[...])


def _rwkv_step(p_rows, p_lz_rows, col, mu, w0, w2p, a0, a2p, k_k, k_a, r_k, gn_w, gn_b, s0):
    rows = s0.shape[0]
    wlz = p_lz_rows.shape[-1]
    d = w0.shape[-1]
    n_heads = d // RW_HEAD
    cur = lambda cb: pl.BlockSpec((rows, d), lambda i, cb=cb: (0, cb))
    prv = lambda cb: pl.BlockSpec((rows, d), lambda i, cb=cb: (1, cb))
    vec = lambda n: pl.BlockSpec((1, n), lambda i: (0, 0))
    mu_r, mu_k, mu_v, mu_lo = mu
    out = jax.ShapeDtypeStruct((d, rows), F32)
    terms = pl.pallas_call(
        _rwkv_step_prep_kernel,
        grid=(1,),
        in_specs=[cur(col["r"]), cur(col["k"]), cur(col["v"]),
                  pl.BlockSpec((rows, LANES), lambda i: (0, 0)),
                  pl.BlockSpec((rows, wlz), lambda i: (0, 0)),
                  prv(col["r"]), prv(col["k"]), prv(col["v"]),
                  pl.BlockSpec((rows, LANES), lambda i: (1, 0)),
                  vec(d), vec(d), vec(d), vec(LANES),
                  vec(d), pl.BlockSpec((LANES, d), lambda i: (0, 0)),
                  vec(d), pl.BlockSpec((LANES, d), lambda i: (0, 0)),
                  vec(d), vec(d)],
        out_specs=[pl.BlockSpec((d, rows), lambda i: (0, 0))] * 7,
        out_shape=(out,) * 7,
        compiler_params=_params("arbitrary"),
        name="rwkv_step_prep",
    )(p_rows, p_rows, p_rows, p_lz_rows, p_lz_rows, p_rows, p_rows, p_rows, p_lz_rows,
      mu_r, mu_k, mu_v, mu_lo, w0, w2p, a0, a2p, k_k, k_a)
    per_h = lambda: pl.BlockSpec((RW_HEAD, rows), lambda h: (h, 0))
    par = lambda: pl.BlockSpec((RW_HEAD, 1), lambda h: (h, 0))
    st = lambda: pl.BlockSpec((None, RW_HEAD, RW_HEAD, rows), lambda h: (h, 0, 0, 0))
    o_t, s1_t = pl.pallas_call(
        _rwkv_step_kernel,
        grid=(n_heads,),
        in_specs=[st()] + [per_h()] * 7 + [par()] * 3,
        out_specs=[per_h(), st()],
        out_shape=(jax.ShapeDtypeStruct((d, rows), F32),
                   jax.ShapeDtypeStruct((n_heads, RW_HEAD, RW_HEAD, rows), F32)),
        scratch_shapes=[pltpu.VMEM((RW_HEAD, rows), F32)],
        compiler_params=_params("parallel"),
        name="rwkv_step",
    )(jnp.transpose(s0, (1, 2, 3, 0)), *terms,
      r_k.reshape(d, 1), gn_w.reshape(d, 1), gn_b.reshape(d, 1))
    return o_t.T.astype(BF16), jnp.transpose(s1_t, (3, 0, 1, 2))


def _out_kernel(ys_ref, z_ref, wg_ref, bg_ref, or_ref, gs_ref, gr_ref, x_ref, gt_ref, w1_ref, w2_ref,
                fg_ref, y_ref, *, paired_blocks):
    ys = ys_ref[...]
    if paired_blocks:
        nb = ys.shape[1] // LANES
        pos = [2 * j if j < nb // 2 else 2 * (j - nb // 2) + 1 for j in range(nb)]
        ys = jnp.concatenate([ys[:, p * LANES:(p + 1) * LANES] for p in pos], axis=1)
    ys = jax.nn.gelu(ys, approximate=True)
    o_s = (ys * _sigmoid(_dot(ys, wg_ref[...]) + bg_ref[...]) * _silu(z_ref[...])).astype(BF16)
    mixed = (gs_ref[...].astype(F32) * jnp.dot(o_s, w1_ref[...], preferred_element_type=F32)
             + gr_ref[...].astype(F32) * jnp.dot(or_ref[...], w2_ref[...], preferred_element_type=F32))
    x = x_ref[...] + gt_ref[...] * mixed
    y_ref[...] = x * lax.rsqrt(jnp.mean(x * x, axis=-1, keepdims=True) + RMS_EPS) * fg_ref[...]


def _out_proj(y_s5, p_main, o_r, p_gate, col, x, mod, w_glu, b_glu, w_out, final_g, tt, paired_blocks):
    b, t, d = x.shape
    dh = o_r.shape[-1]
    tt = min(tt, t)
    tm = 1 if mod.shape[1] == 1 else tt
    gt_map = (lambda i, j: (i, 0, 2)) if tm == 1 else (lambda i, j: (i, j, 2))
    const = lambda shp, r=0: pl.BlockSpec(shp, lambda i, j: (r, 0), pipeline_mode=pl.Buffered(1))
    return pl.pallas_call(
        functools.partial(_out_kernel, paired_blocks=paired_blocks),
        grid=(b, t // tt),
        in_specs=[pl.BlockSpec((None, tt, dh), lambda i, j: (i, j, 0)),
                  pl.BlockSpec((None, tt, dh), lambda i, j: (i, j, col["z_s5"])),
                  const((dh, dh)), const((1, dh)),
                  pl.BlockSpec((None, tt, dh), lambda i, j: (i, j, 0)),
                  pl.BlockSpec((None, tt, d), lambda i, j: (i, j, col["g_s5"])),
                  pl.BlockSpec((None, tt, d), lambda i, j: (i, j, col["g_rw"])),
                  pl.BlockSpec((None, tt, d), lambda i, j: (i, j, 0)),
                  pl.BlockSpec((None, tm, d), gt_map),
                  const((dh, d), 0), const((dh, d), 1), const((1, d))],
        out_specs=pl.BlockSpec((None, tt, d), lambda i, j: (i, j, 0)),
        out_shape=jax.ShapeDtypeStruct((b, t, d), F32),
        compiler_params=_params("parallel", "parallel"),
        name="out_proj",
    )(y_s5, p_main, w_glu, b_glu.reshape(1, dh), o_r, p_gate, p_gate, x, mod, w_out, w_out,
      final_g.reshape(1, d))


def kernel(x_prompt, x_sample, c_prompt, c_sample, state_s5_re, state_s5_im, state_wkv, state_shift, norm_g, w_ada, b_ada, w_in, mu_rw, A_re, A_im, log_step, B_re, B_im, C_re, C_im, D_skip, w_glu, b_glu, w0, w2, a0, a2, k_k, k_a, r_k, gn_w, gn_b, w_out, final_g):
    depth = norm_g.shape[0]
    assert depth == 1
    bp, tp, d = x_prompt.shape
    bs = x_sample.shape[0]
    assert x_sample.shape[1] == 1
    dh = d // 2
    l = 0

    w_main, w_lz, w_gate = 5 * dh, 2 * LORA + dh, 2 * d
    col = {"u": 0, "z_s5": 1, "r": 2, "k": 3, "v": 4, "g_s5": 0, "g_rw": 1}
    w = w_in[l]

    def project(rows, rows2):
        return (_in_proj(rows, rows2, w, 0, w_main, w_main // 4, 1024),
                _in_proj(rows, rows2, w, w_main, w_lz, w_lz, 1024),
                _in_proj(rows, rows2, w, w_main + w_lz, w_gate, w_gate // 4, 1024, gate=True))

    mu = mu_rw[l]
    mu_parts = (mu[None, :dh], mu[None, dh:2 * dh], mu[None, 2 * dh:3 * dh], mu[None, 3 * dh:])
    zpad = jnp.zeros((LORA, dh), F32)
    w2p = jnp.concatenate([w2[l], zpad], axis=0).astype(BF16)
    a2p = jnp.concatenate([zpad, a2[l]], axis=0).astype(BF16)
    row = lambda x: x.reshape(1, -1)
    rw_params = (row(w0[l]), w2p, row(a0[l]), a2p, row(k_k[l]), row(k_a[l]), row(r_k[l]),
                 row(gn_w[l]), row(gn_b[l]))
    w_out_bf = w_out[l].astype(BF16)
    w_glu_bf = w_glu[l].astype(BF16)

    bw2, cw2, a_blk_re, a_blk_im, a_slot_re, a_slot_im = _s5_weights(
        A_re[l], A_im[l], log_step[l], B_re[l], B_im[l], C_re[l], C_im[l], bp)
    d_skip = D_skip[l].reshape(1, -1)

    mod = _mod(jnp.concatenate([c_prompt, c_sample], axis=0), w_ada[l], b_ada[l])
    mod_p = mod[:bp].reshape(bp, 1, 3 * d)
    mod_s = mod[bp:].reshape(1, bs, 3 * d)

    h_p = _modulated_norm(x_prompt, norm_g[l], mod_p, BF16, 1024)
    shift_p = _modulated_norm(x_prompt[:, tp - 1:, :], norm_g[l], mod_p, F32, 1)[:, 0]
    xs = x_sample.reshape(1, bs, d)
    h_s = _modulated_norm(xs, norm_g[l], mod_s, F32, bs)[0]
    a_s = jnp.concatenate([h_s, state_shift[l]], axis=0).astype(BF16)
    (pm, ps), (plz, pslz), (pg, psg) = project(h_p.reshape(bp * tp, d), a_s)

    pm3 = pm.reshape(bp, tp, -1)
    y_s5, xre_p, xim_p = _s5_sequence(pm3, bw2, cw2, a_slot_re, a_slot_im, d_skip)
    o_r, hs_p = _rwkv_sequence(pm3, plz.reshape(bp, tp, -1), col, mu_parts, *rw_params)
    y_prompt = _out_proj(y_s5, pm3, o_r, pg.reshape(bp, tp, -1), col, x_prompt, mod_p,
                         w_glu_bf, b_glu[l], w_out_bf, final_g, 512, True)
    g_s5 = A_re.shape[1]
    s5_shape = (1, bp, g_s5, P_S5)
    wkv_p = hs_p[None]

    y_s5s, xre_s, xim_s = _s5_step(ps, bw2, cw2, a_blk_re, a_blk_im, d_skip,
                                   state_s5_re[l].reshape(bs, -1), state_s5_im[l].reshape(bs, -1))
    o_rs, wkv_s = _rwkv_step(ps, pslz, col, mu_parts, *rw_params, state_wkv[l])
    y_sample = _out_proj(y_s5s[None], ps.reshape(2, bs, -1), o_rs[None], psg.reshape(2, bs, -1), col, xs, mod_s,
                         w_glu_bf, b_glu[l], w_out_bf, final_g, bs, False)
    y_sample = y_sample.reshape(bs, 1, d)

    return (y_prompt, y_sample,
            xre_p.reshape(s5_shape), xim_p.reshape(s5_shape), wkv_p, shift_p[None],
            xre_s.reshape(1, bs, g_s5, P_S5), xim_s.reshape(1, bs, g_s5, P_S5), wkv_s[None], h_s[None])
```

```python
import functools

import jax
import jax.numpy as jnp
from jax import lax
from jax.experimental import pallas as pl
from jax.experimental.pallas import tpu as pltpu

F32 = jnp.float32
BF16 = jnp.bfloat16

RMS_EPS = 1e-6
GN_EPS = 64e-5
S5_GROUP = 16
P_S5 = 64
RW_HEAD = 64
LORA = 64
LANES = 128
GROUPS_PER_BLOCK = LANES // S5_GROUP
STATES_PER_BLOCK = GROUPS_PER_BLOCK * P_S5
S5_TC = 512
RW_CHUNK = 64
INV_BASE = 8
RW_SEQS = 4
RW_SEQ_LAG = 1

V7X_VMEM_BYTES = 64 * 1024 * 1024
VMEM_LIMIT_BYTES = V7X_VMEM_BYTES // 8 * 7
MOD_TN = 1024
NORM_ROWS = 1024
IN_PROJ_ROWS = 1024
IN_PROJ_COL_TILES = 4
OUT_ROWS = 512

_NT = (((1,), (1,)), ((), ()))
_TN = (((0,), (0,)), ((), ()))


def _dot(a, b):
    return jnp.dot(a.astype(BF16), b.astype(BF16), preferred_element_type=F32)


def _sigmoid(x):
    return 0.5 * jnp.tanh(0.5 * x) + 0.5


def _silu(x):
    return x * _sigmoid(x)


def _params(*sem):
    return pltpu.CompilerParams(dimension_semantics=sem, vmem_limit_bytes=VMEM_LIMIT_BYTES)


def _mod_kernel(c1_ref, c2_ref, w_ref, b_ref, o1_ref, o2_ref):
    w = w_ref[...].astype(BF16)
    for c_ref, o_ref in ((c1_ref, o1_ref), (c2_ref, o2_ref)):
        o_ref[...] = _dot(_silu(c_ref[...]), w) + b_ref[...]


def _mod(c1, c2, w, b):
    d, n = w.shape
    rows = lambda c: pl.BlockSpec((c.shape[0], d), lambda j: (0, 0))
    out = lambda c: pl.BlockSpec((c.shape[0], MOD_TN), lambda j: (0, j))
    return pl.pallas_call(
        _mod_kernel,
        grid=(n // MOD_TN,),
        in_specs=[rows(c1), rows(c2),
                  pl.BlockSpec((d, MOD_TN), lambda j: (0, j)),
                  pl.BlockSpec((1, MOD_TN), lambda j: (0, j))],
        out_specs=[out(c1), out(c2)],
        out_shape=(jax.ShapeDtypeStruct((c1.shape[0], n), F32), jax.ShapeDtypeStruct((c2.shape[0], n), F32)),
        compiler_params=_params("parallel"),
        name="adaln_mod",
    )(c1, c2, w, b.reshape(1, n))


def _h_kernel(x_ref, g_ref, sh_ref, sc_ref, h_ref):
    x = x_ref[...]
    y = x * lax.rsqrt(jnp.mean(x * x, axis=-1, keepdims=True) + RMS_EPS) * g_ref[...]
    h_ref[...] = (y * (1.0 + sc_ref[...]) + sh_ref[...]).astype(h_ref.dtype)


def _modulated_norm(x, g, mod, out_dtype, tt):
    b, t, d = x.shape
    tt = min(tt, t)
    tm = 1 if mod.shape[1] == 1 else tt
    mod_map = (lambda i, j: (i, 0, 0)) if tm == 1 else (lambda i, j: (i, j, 0))
    mod_map1 = (lambda i, j: (i, 0, 1)) if tm == 1 else (lambda i, j: (i, j, 1))
    return pl.pallas_call(
        _h_kernel,
        grid=(b, t // tt),
        in_specs=[pl.BlockSpec((None, tt, d), lambda i, j: (i, j, 0)),
                  pl.BlockSpec((1, d), lambda i, j: (0, 0)),
                  pl.BlockSpec((None, tm, d), mod_map),
                  pl.BlockSpec((None, tm, d), mod_map1)],
        out_specs=pl.BlockSpec((None, tt, d), lambda i, j: (i, j, 0)),
        out_shape=jax.ShapeDtypeStruct((b, t, d), out_dtype),
        compiler_params=_params("parallel", "parallel"),
        name="modulated_norm",
    )(x, g.reshape(1, d), mod, mod)


def _in_proj_kernel(a_ref, a2_ref, w_ref, o_ref, o2_ref, wbf_ref, *, gate, n_main):
    i = pl.program_id(1)

    @pl.when(i == 0)
    def _():
        wbf_ref[...] = w_ref[...].astype(BF16)

    def project(x_ref, out_ref):
        p = jnp.dot(x_ref[...], wbf_ref[...], preferred_element_type=F32)
        out_ref[...] = (_sigmoid(p) if gate else p).astype(out_ref.dtype)

    @pl.when(i < n_main)
    def _():
        project(a_ref, o_ref)

    @pl.when(i == n_main)
    def _():
        project(a2_ref, o2_ref)


def _in_proj(a, a2, w, col0, width, tn, tm, gate=False):
    m, k = a.shape
    m2 = a2.shape[0]
    tm = min(tm, m)
    assert width % tn == 0 and m % tm == 0 and col0 % LANES == 0
    n_main = m // tm
    last = n_main - 1
    dt = BF16 if gate else F32
    return pl.pallas_call(
        functools.partial(_in_proj_kernel, gate=gate, n_main=n_main),
        grid=(width // tn, n_main + 1),
        in_specs=[pl.BlockSpec((tm, k), lambda j, i: (jnp.minimum(i, last), 0)),
                  pl.BlockSpec((m2, k), lambda j, i: (0, 0)),
                  pl.BlockSpec((pl.Element(k), pl.Element(tn)), lambda j, i: (0, pl.multiple_of(col0 + j * tn, LANES)))],
        out_specs=[pl.BlockSpec((tm, tn), lambda j, i: (jnp.minimum(i, last), j)),
                   pl.BlockSpec((m2, tn), lambda j, i: (0, j))],
        out_shape=(jax.ShapeDtypeStruct((m, width), dt), jax.ShapeDtypeStruct((m2, width), dt)),
        scratch_shapes=[pltpu.VMEM((k, tn), BF16)],
        compiler_params=_params("parallel", "arbitrary"),
        name="in_proj",
    )(a, a2, w)


def _s5_weights_kernel(*refs, n_b):
    (are0, are1, aim0, aim1, ls0, ls1, bre0, bre1, bim0, bim1, cre0, cre1, cim0, cim1,
     bw_ref, cw_ref, abr_ref, abi_ref, apr_ref, api_ref) = refs
    n_gl = GROUPS_PER_BLOCK
    n_lt = STATES_PER_BLOCK // LANES

    def discretise(are_ref, aim_ref, ls_ref, bre_ref, bim_ref):
        step = jnp.exp(ls_ref[...])
        lam_re = jnp.minimum(are_ref[...], -1e-4)
        lam_im = aim_ref[...]
        mag = jnp.exp(lam_re * step)
        ab_re = mag * jnp.cos(lam_im * step)
        ab_im = mag * jnp.sin(lam_im * step)
        den = lam_re * lam_re + lam_im * lam_im
        f_re = ((ab_re - 1.0) * lam_re + ab_im * lam_im) / den
        f_im = (ab_im * lam_re - (ab_re - 1.0) * lam_im) / den
        br, bi = bre_ref[...], bim_ref[...]
        return ab_re, ab_im, f_re * br - f_im * bi, f_re * bi + f_im * br

    halves = [discretise(are0, aim0, ls0, bre0, bim0), discretise(are1, aim1, ls1, bre1, bim1)]
    zero = jnp.zeros((S5_GROUP, P_S5), F32)

    def band(t_re, t_im, gl):
        pieces = []
        for lt in range(n_lt):
            for tile in (t_re, t_im):
                for half in range(2):
                    pieces.append(tile if 2 * lt + half == gl else zero)
        return jnp.concatenate(pieces, axis=1)

    def block_rows(tiles_re, tiles_im):
        return jnp.concatenate([band(tiles_re[h][gl], tiles_im[h][gl], gl)
                                for h in range(2) for gl in range(n_gl)], axis=0)

    bw_ref[...] = block_rows([h[2] for h in halves], [h[3] for h in halves]).astype(BF16)
    cw_ref[...] = block_rows([cre0[...], cre1[...]], [-cim0[...], -cim1[...]]).T.astype(BF16)

    for h in range(2):
        for a_ref, a in ((abr_ref, halves[h][0]), (abi_ref, halves[h][1])):
            a_ref[h] = jnp.concatenate([a[gl] for gl in range(n_gl)], axis=1)
    for p_ref, a_ref in ((apr_ref, abr_ref), (api_ref, abi_ref)):
        p_ref[...] = jnp.concatenate([a_ref[0], a_ref[1]] * n_b, axis=0)


def _s5_weights(a_re, a_im, log_step, b_re, b_im, c_re, c_im, n_b):
    g, p = a_re.shape
    c = b_re.shape[-1]
    nq = g // GROUPS_PER_BLOCK // 2
    ns = STATES_PER_BLOCK
    args, specs = [], []
    for x, shp in ((a_re.reshape(g, 1, p), (1, p)), (a_im.reshape(g, 1, p), (1, p)),
                   (log_step.reshape(g, 1, 1), (1, 1)),
                   (jnp.swapaxes(b_re, 1, 2), (c, p)), (jnp.swapaxes(b_im, 1, 2), (c, p)),
                   (c_re, (c, p)), (c_im, (c, p))):
        for half in range(2):
            args.append(x)
            specs.append(pl.BlockSpec((GROUPS_PER_BLOCK,) + shp, lambda q, half=half: (q + half * nq, 0, 0)))
    return pl.pallas_call(
        functools.partial(_s5_weights_kernel, n_b=n_b),
        grid=(nq,),
        in_specs=specs,
        out_specs=[pl.BlockSpec((None, 2 * LANES, 2 * ns), lambda q: (q, 0, 0)),
                   pl.BlockSpec((None, 2 * ns, 2 * LANES), lambda q: (q, 0, 0)),
                   pl.BlockSpec((None, 2, 1, ns), lambda q: (q, 0, 0, 0)),
                   pl.BlockSpec((None, 2, 1, ns), lambda q: (q, 0, 0, 0)),
                   pl.BlockSpec((None, 2 * n_b, ns), lambda q: (q, 0, 0)),
                   pl.BlockSpec((None, 2 * n_b, ns), lambda q: (q, 0, 0))],
        out_shape=(jax.ShapeDtypeStruct((nq, 2 * LANES, 2 * ns), BF16),
                   jax.ShapeDtypeStruct((nq, 2 * ns, 2 * LANES), BF16),
                   jax.ShapeDtypeStruct((nq, 2, 1, ns), F32), jax.ShapeDtypeStruct((nq, 2, 1, ns), F32),
                   jax.ShapeDtypeStruct((nq, 2 * n_b, ns), F32), jax.ShapeDtypeStruct((nq, 2 * n_b, ns), F32)),
        compiler_params=_params("parallel"),
        name="s5_weights",
    )(*args)


def _s5_seq_kernel(u0_ref, u1_ref, bw_ref, cw_ref, are_ref, aim_ref, d_ref,
                   y_ref, xre_ref, xim_ref, lhs_ref, bu_ref, y_scr, x_scr, *, tc, n_b):
    n_lt = STATES_PER_BLOCK // LANES
    slots = 2 * n_b
    re_l = lambda lt: slice(2 * lt * LANES, (2 * lt + 1) * LANES)
    im_l = lambda lt: slice((2 * lt + 1) * LANES, (2 * lt + 2) * LANES)
    nat = lambda lt: slice(lt * LANES, (lt + 1) * LANES)
    c = pl.program_id(1)

    @pl.when(c == 0)
    def _():
        x_scr[...] = jnp.zeros_like(x_scr)
        lhs_ref[...] = jnp.zeros_like(lhs_ref)

    for b in range(n_b):
        lhs_ref.at[0][pl.ds(2 * b, tc, stride=slots), :] = u0_ref[b]
        lhs_ref.at[1][pl.ds(2 * b + 1, tc, stride=slots), :] = u1_ref[b]
    u_rows = lhs_ref[0] + lhs_ref[1]
    lhs = jnp.concatenate([lhs_ref[0], lhs_ref[1]], axis=1)
    bu_ref[...] = _dot(lhs, bw_ref[...])

    a_re = [are_ref[:, nat(lt)] for lt in range(n_lt)]
    a_im = [aim_ref[:, nat(lt)] for lt in range(n_lt)]
    x = [(x_scr[:, re_l(lt)], x_scr[:, im_l(lt)]) for lt in range(n_lt)]
    for t in range(tc):
        rows = slice(t * slots, (t + 1) * slots)
        for lt in range(n_lt):
            xr, xi = x[lt]
            nr = a_re[lt] * xr - a_im[lt] * xi + bu_ref[rows, re_l(lt)]
            ni = a_re[lt] * xi + a_im[lt] * xr + bu_ref[rows, im_l(lt)]
            bu_ref[rows, re_l(lt)] = nr
            bu_ref[rows, im_l(lt)] = ni
            x[lt] = (nr, ni)
    for lt in range(n_lt):
        x_scr[:, re_l(lt)] = x[lt][0]
        x_scr[:, im_l(lt)] = x[lt][1]

    yf = _dot(bu_ref[...], cw_ref[...])
    first_half = (lax.broadcasted_iota(jnp.int32, (tc * slots, LANES), 0) & 1) == 0
    y = jnp.where(first_half, yf[:, :LANES], yf[:, LANES:])
    skip = (u_rows.reshape(tc, slots, LANES) * d_ref[...][None]).reshape(tc * slots, LANES)
    y_scr[...] = y + skip
    for b in range(n_b):
        y_ref[b, :, :LANES] = y_scr[pl.ds(2 * b, tc, stride=slots), :].astype(y_ref.dtype)
        y_ref[b, :, LANES:] = y_scr[pl.ds(2 * b + 1, tc, stride=slots), :].astype(y_ref.dtype)

    @pl.when(c == pl.num_programs(1) - 1)
    def _():
        for lt in range(n_lt):
            xre_ref[:, nat(lt)] = x[lt][0]
            xim_ref[:, nat(lt)] = x[lt][1]


def _s5_sequence(p_main, bw2, cw2, a_slot_re, a_slot_im, d_skip):
    b, t, _ = p_main.shape
    nq = bw2.shape[0]
    nb = 2 * nq
    ns = STATES_PER_BLOCK
    slots = 2 * b
    assert slots == 8, "one 8-row tile must hold every (sequence, half) slot"
    tc = min(S5_TC, t)
    d_blk = d_skip.reshape(nb, LANES)
    d_slot = jnp.tile(jnp.stack([d_blk[:nq], d_blk[nq:]], axis=1), (1, b, 1))
    slot_vec = lambda n: pl.BlockSpec((None, slots, n), lambda q, c: (q, 0, 0))
    y, xre, xim = pl.pallas_call(
        functools.partial(_s5_seq_kernel, tc=tc, n_b=b),
        grid=(nq, t // tc),
        in_specs=[pl.BlockSpec((b, tc, LANES), lambda q, c: (0, c, q)),
                  pl.BlockSpec((b, tc, LANES), lambda q, c: (0, c, q + nq)),
                  pl.BlockSpec((None, 2 * LANES, 2 * ns), lambda q, c: (q, 0, 0)),
                  pl.BlockSpec((None, 2 * ns, 2 * LANES), lambda q, c: (q, 0, 0)),
                  slot_vec(ns), slot_vec(ns), slot_vec(LANES)],
        out_specs=[pl.BlockSpec((b, tc, 2 * LANES), lambda q, c: (0, c, q)),
                   slot_vec(ns), slot_vec(ns)],
        out_shape=(jax.ShapeDtypeStruct((b, t, nb * LANES), BF16),
                   jax.ShapeDtypeStruct((nq, slots, ns), F32),
                   jax.ShapeDtypeStruct((nq, slots, ns), F32)),
        scratch_shapes=[pltpu.VMEM((2, tc * slots, LANES), F32),
                        pltpu.VMEM((tc * slots, 2 * ns), F32),
                        pltpu.VMEM((tc * slots, LANES), F32),
                        pltpu.VMEM((slots, 2 * ns), F32)],
        compiler_params=_params("parallel", "arbitrary"),
        name="s5_sequence",
    )(p_main, p_main, bw2, cw2, a_slot_re, a_slot_im, d_slot)

    def state(x):
        x = x.reshape(nq, b, 2, ns)
        return jnp.transpose(x, (1, 2, 0, 3)).reshape(b, nb * ns)

    return y, state(xre), state(xim)


def _s5_step_kernel(u_ref, bw_ref, cw_ref, abr_ref, abi_ref, d_ref, x0r_ref, x0i_ref,
                    y_ref, x1r_ref, x1i_ref):
    n_lt = STATES_PER_BLOCK // LANES
    u = u_ref[...]
    bu = _dot(u, bw_ref[...])
    tiles = []
    for lt in range(n_lt):
        nat = slice(lt * LANES, (lt + 1) * LANES)
        ar, ai = abr_ref[:, nat], abi_ref[:, nat]
        x0r, x0i = x0r_ref[:, nat], x0i_ref[:, nat]
        xr = ar * x0r - ai * x0i + bu[:, 2 * lt * LANES:(2 * lt + 1) * LANES]
        xi = ar * x0i + ai * x0r + bu[:, (2 * lt + 1) * LANES:(2 * lt + 2) * LANES]
        x1r_ref[:, nat] = xr
        x1i_ref[:, nat] = xi
        tiles += [xr, xi]
    y_ref[...] = _dot(jnp.concatenate(tiles, axis=1), cw_ref[...]) + d_ref[...] * u


def _s5_step(p_rows, bw2, cw2, a_blk_re, a_blk_im, d_skip, x0_re, x0_im):
    rows = x0_re.shape[0]
    nq = bw2.shape[0]
    nb = 2 * nq
    ns = STATES_PER_BLOCK
    vec = lambda: pl.BlockSpec((None, None, 1, ns), lambda j: (j % nq, j // nq, 0, 0))
    st = lambda: pl.BlockSpec((rows, ns), lambda j: (0, j))
    return pl.pallas_call(
        _s5_step_kernel,
        grid=(nb,),
        in_specs=[pl.BlockSpec((rows, LANES), lambda j: (0, j)),
                  pl.BlockSpec((None, LANES, 2 * ns), lambda j: (j % nq, j // nq, 0)),
                  pl.BlockSpec((None, 2 * ns, LANES), lambda j: (j % nq, 0, j // nq)),
                  vec(), vec(),
                  pl.BlockSpec((1, LANES), lambda j: (0, j)),
                  st(), st()],
        out_specs=[pl.BlockSpec((rows, LANES), lambda j: (0, j)), st(), st()],
        out_shape=(jax.ShapeDtypeStruct((rows, nb * LANES), F32),
                   jax.ShapeDtypeStruct((rows, nb * ns), F32),
                   jax.ShapeDtypeStruct((rows, nb * ns), F32)),
        compiler_params=_params("parallel"),
        name="s5_step",
    )(p_rows, bw2, cw2, a_blk_re, a_blk_im, d_skip, x0_re, x0_im)


def _softplus(x):
    return jnp.maximum(x, 0.0) + jnp.log(1.0 + jnp.exp(-jnp.abs(x)))


def _rwkv_token_terms(r, k, lo, w0, w2p, a0, a2p, k_a):
    w = -_softplus(-(w0 + _dot(jnp.tanh(lo), w2p))) - 0.5
    logd = -jnp.exp(w)
    a = _sigmoid(a0 + _dot(lo, a2p))
    k2 = k * (1.0 + (a - 1.0) * k_a)
    return logd, a, k2


def _rwkv_chunk_kernel(r_ref, k_ref, v_ref, lo_ref, z_ref,
                       mur_ref, muk_ref, muv_ref, mulo_ref,
                       w0_ref, w2p_ref, a0_ref, a2p_ref, kk_ref, ka_ref, rk_ref, gw_ref, gb_ref,
                       o_ref, hs_ref,
                       h_scr, pr_scr, pk_scr, pv_scr, plo_scr, *, n_heads, n_seqs):
    L = RW_CHUNK
    W = 2 * RW_HEAD
    n_pairs = n_heads // 2
    c = pl.program_id(1)

    @pl.when(c == 0)
    def _():
        h_scr[...] = jnp.zeros_like(h_scr)
        pr_scr[...] = jnp.zeros_like(pr_scr)
        pk_scr[...] = jnp.zeros_like(pk_scr)
        pv_scr[...] = jnp.zeros_like(pv_scr)
        plo_scr[...] = jnp.zeros_like(plo_scr)

    row1 = lax.broadcasted_iota(jnp.int32, (L, 1), 0)

    def token_shift(cur, prev_scr, mu_ref):
        prev = jnp.where(row1 == 0, prev_scr[...], pltpu.roll(cur, 1, 0))
        prev_scr[...] = cur[L - 1:L, :]
        return cur + (prev - cur) * mu_ref[...]

    tri = (lax.broadcasted_iota(jnp.int32, (L, L), 0)
           >= lax.broadcasted_iota(jnp.int32, (L, L), 1)).astype(BF16)
    first_head = lax.broadcasted_iota(jnp.int32, (L, W), 1) < RW_HEAD

    def head_sum(x):
        s0 = jnp.sum(jnp.where(first_head, x, 0.0), axis=-1, keepdims=True)
        s1 = jnp.sum(jnp.where(first_head, 0.0, x), axis=-1, keepdims=True)
        return jnp.where(first_head, s0, s1)

    def stack_heads(x):
        return jnp.concatenate([jnp.where(first_head, x, 0.0), jnp.where(first_head, 0.0, x)], axis=0)

    ri = lax.broadcasted_iota(jnp.int32, (2 * L, 2 * L), 0)
    ci = lax.broadcasted_iota(jnp.int32, (2 * L, 2 * L), 1)
    t_row, t_col = ri & (L - 1), ci & (L - 1)
    strict = t_row > t_col
    incl = t_row >= t_col
    eye = (ri == ci).astype(F32)
    blk_masks = []
    s = INV_BASE
    while s <= L:
        blk_masks.append((ri // s) == (ci // s))
        s *= 2

    pairs = range(n_pairs)
    loc = [slice(p * W, (p + 1) * W) for p in pairs]

    def seq_phases(si):
        r = token_shift(r_ref[si], pr_scr.at[si], mur_ref)
        k = token_shift(k_ref[si], pk_scr.at[si], muk_ref)
        v = token_shift(v_ref[si], pv_scr.at[si], muv_ref)
        lo = token_shift(lo_ref[si], plo_scr.at[si], mulo_ref)
        logd, a, k2 = _rwkv_token_terms(r, k, lo, w0_ref[...], w2p_ref[...], a0_ref[...], a2p_ref[...],
                                        ka_ref[...])
        kk = k * kk_ref[...]
        hi = logd.astype(BF16)
        rem = logd - hi.astype(F32)
        mid = rem.astype(BF16)
        low = (rem - mid.astype(F32)).astype(BF16)
        lp = (jnp.dot(tri, hi, preferred_element_type=F32)
              + jnp.dot(tri, mid, preferred_element_type=F32)
              + jnp.dot(tri, low, preferred_element_type=F32))
        p_inc = jnp.exp(lp)
        p_exc = jnp.exp(lp - logd)
        p_inv = jnp.exp(-lp)
        yield

        lhs, nm, mm, ab, vs, kb, p_end = [], [], [], [], [], [], []
        for sl in loc:
            kk_p = kk[:, sl]
            kkn = kk_p * lax.rsqrt(jnp.maximum(head_sum(kk_p * kk_p), 1e-24))
            pinc, pinv = p_inc[:, sl], p_inv[:, sl]
            kd = k2[:, sl] * pinv
            bd = kkn * a[:, sl] * pinv
            pe = pinc[L - 1:L, :]
            lhs_p = jnp.concatenate([stack_heads(kkn * p_exc[:, sl]), stack_heads(r[:, sl] * pinc)],
                                    axis=0).astype(BF16)
            rhs_p = jnp.concatenate([stack_heads(bd), stack_heads(kd)], axis=0).astype(BF16)
            amat = lax.dot_general(lhs_p, rhs_p, _NT, preferred_element_type=F32)
            lhs.append(lhs_p)
            nm.append(jnp.where(strict, amat[:2 * L, :2 * L], 0.0))
            mm.append(jnp.where(strict, amat[:2 * L, 2 * L:], 0.0).astype(BF16))
            ab.append(jnp.concatenate([jnp.where(incl, amat[2 * L:, 2 * L:], 0.0),
                                       -jnp.where(incl, amat[2 * L:, :2 * L], 0.0)], axis=1).astype(BF16))
            vs.append(stack_heads(v[:, sl]).astype(BF16))
            kb.append(jnp.concatenate([stack_heads(kd * pe), stack_heads(bd * pe)], axis=0).astype(BF16))
            p_end.append(pe)
            yield

        d = [jnp.where(blk_masks[0], n_p, 0.0).astype(BF16) for n_p in nm]
        x = [eye - d_p.astype(F32) for d_p in d]
        pw = [jnp.dot(d_p, d_p, preferred_element_type=F32) for d_p in d]
        yield
        s = 2
        while s < INV_BASE:
            x = [x_p + _dot(x_p, pw_p) for x_p, pw_p in zip(x, pw)]
            s *= 2
            if s < INV_BASE:
                pw = [_dot(pw_p, pw_p) for pw_p in pw]
            yield
        for lvl in range(1, len(blk_masks)):
            off = blk_masks[lvl] & ~blk_masks[lvl - 1]
            xc = [_dot(x_p, jnp.where(off, n_p, 0.0)) for x_p, n_p in zip(x, nm)]
            yield
            x = [x_p - _dot(xc_p, x_p) for x_p, xc_p in zip(x, xc)]
            yield

        hs = [h_scr[si, p] for p in pairs]
        lh = [jnp.dot(lhs_p, hs_p.astype(BF16), preferred_element_type=F32) for lhs_p, hs_p in zip(lhs, hs)]
        mv = [jnp.dot(mm_p, vs_p, preferred_element_type=F32) for mm_p, vs_p in zip(mm, vs)]
        yield
        u = [_dot(x_p, lh_p[:2 * L] + mv_p).astype(BF16) for x_p, lh_p, mv_p in zip(x, lh, mv)]
        yield
        o_st = [lh_p[2 * L:] + jnp.dot(ab_p, jnp.concatenate([vs_p, u_p], axis=0), preferred_element_type=F32)
                for lh_p, ab_p, vs_p, u_p in zip(lh, ab, vs, u)]
        for p in pairs:
            p_end_col = jnp.sum(eye * p_end[p], axis=1, keepdims=True)
            h_scr[si, p] = p_end_col * hs[p] + lax.dot_general(
                kb[p], jnp.concatenate([vs[p], -u[p]], axis=0), _TN, preferred_element_type=F32)
        yield

        for p, sl in enumerate(loc):
            o = o_st[p][:L] + o_st[p][L:]
            mu = head_sum(o) * (1.0 / RW_HEAD)
            var = head_sum((o - mu) ** 2) * (1.0 / RW_HEAD)
            o = (o - mu) * lax.rsqrt(var + GN_EPS) * gw_ref[:, sl] + gb_ref[:, sl]
            o = o + head_sum(r[:, sl] * k2[:, sl] * rk_ref[:, sl]) * v[:, sl]
            z = z_ref[si, :, LANES + p * W:LANES + (p + 1) * W]
            o_ref[si, :, sl] = (o * _silu(z)).astype(o_ref.dtype)
        yield

    gens = [seq_phases(si) for si in range(n_seqs)]
    started = 0
    live = []
    step = 0
    while started < n_seqs or live:
        if started < n_seqs and step % RW_SEQ_LAG == 0:
            live.append(gens[started])
            started += 1
        for g in list(live):
            if next(g, "done") == "done":
                live.remove(g)
        step += 1

    @pl.when(c == pl.num_programs(1) - 1)
    def _():
        for si in range(n_seqs):
            for p in pairs:
                st = h_scr[si, p].T
                hs_ref[si, 2 * p] = st[:RW_HEAD, :RW_HEAD]
                hs_ref[si, 2 * p + 1] = st[RW_HEAD:, RW_HEAD:]


def _rwkv_sequence(p_main, p_lz, col, mu, w0, w2p, a0, a2p, k_k, k_a, r_k, gn_w, gn_b):
    b, t, _ = p_main.shape
    wlz = p_lz.shape[-1]
    d = w0.shape[-1]
    n_heads = d // RW_HEAD
    n_pairs, pw = n_heads // 2, 2 * RW_HEAD
    L = RW_CHUNK
    ns = RW_SEQS if b % RW_SEQS == 0 else 1
    blk = lambda cb: pl.BlockSpec((ns, L, d), lambda i, j, cb=cb: (i, j, cb))
    vec = lambda n: pl.BlockSpec((1, n), lambda i, j: (0, 0))
    full = lambda shp: pl.BlockSpec(shp, lambda i, j: (0,) * len(shp))
    mu_r, mu_k, mu_v, mu_lo = mu
    o, hs = pl.pallas_call(
        functools.partial(_rwkv_chunk_kernel, n_heads=n_heads, n_seqs=ns),
        grid=(b // ns, t // L),
        in_specs=[blk(col["r"]), blk(col["k"]), blk(col["v"]),
                  pl.BlockSpec((ns, L, LANES), lambda i, j: (i, j, 0)),
                  pl.BlockSpec((ns, L, wlz), lambda i, j: (i, j, 0)),
                  vec(d), vec(d), vec(d), vec(LANES),
                  vec(d), full((LANES, d)), vec(d), full((LANES, d)),
                  vec(d), vec(d), vec(d), vec(d), vec(d)],
        out_specs=[pl.BlockSpec((ns, L, d), lambda i, j: (i, j, 0)),
                   pl.BlockSpec((ns, n_heads, RW_HEAD, RW_HEAD), lambda i, j: (i, 0, 0, 0))],
        out_shape=(jax.ShapeDtypeStruct((b, t, d), BF16),
                   jax.ShapeDtypeStruct((b, n_heads, RW_HEAD, RW_HEAD), F32)),
        scratch_shapes=[pltpu.VMEM((ns, n_pairs, pw, pw), F32),
                        pltpu.VMEM((ns, 1, d), F32), pltpu.VMEM((ns, 1, d), F32), pltpu.VMEM((ns, 1, d), F32),
                        pltpu.VMEM((ns, 1, LANES), F32)],
        compiler_params=_params("parallel", "arbitrary"),
        name="rwkv_sequence",
    )(p_main, p_main, p_main, p_lz, p_lz,
      mu_r, mu_k, mu_v, mu_lo, w0, w2p, a0, a2p, k_k, k_a, r_k, gn_w, gn_b)
    return o, hs


def _rwkv_step_prep_kernel(cr_ref, ck_ref, cv_ref, clo_ref, cz_ref, pr_ref, pk_ref, pv_ref, plo_ref,
                           mur_ref, muk_ref, muv_ref, mulo_ref,
                           w0_ref, w2p_ref, a0_ref, a2p_ref, kk_ref, ka_ref,
                           r_o, k2_o, v_o, kk_o, a_o, d_o, z_o):
    def lerp(c_ref, p_ref, mu_ref):
        cur = c_ref[...]
        return cur + (p_ref[...] - cur) * mu_ref[...]

    r = lerp(cr_ref, pr_ref, mur_ref)
    k = lerp(ck_ref, pk_ref, muk_ref)
    v = lerp(cv_ref, pv_ref, muv_ref)
    lo = lerp(clo_ref, plo_ref, mulo_ref)
    logd, a, k2 = _rwkv_token_terms(r, k, lo, w0_ref[...], w2p_ref[...], a0_ref[...], a2p_ref[...],
                                    ka_ref[...])
    r_o[...] = r.T
    k2_o[...] = k2.T
    v_o[...] = v.T
    kk_o[...] = (k * kk_ref[...]).T
    a_o[...] = a.T
    d_o[...] = jnp.exp(logd).T
    z_o[...] = cz_ref[:, LANES:].T


def _rwkv_step_kernel(s_ref, r_ref, k2_ref, v_ref, kk_ref, a_ref, d_ref, z_ref,
                      rk_ref, gw_ref, gb_ref, o_ref, s1_ref, o_scr):
    r, k2, v, dec = r_ref[...], k2_ref[...], v_ref[...], d_ref[...]
    kk = kk_ref[...]
    kkn = kk / jnp.maximum(jnp.sqrt(jnp.sum(kk * kk, axis=0, keepdims=True)), 1e-12)
    bvec = kkn * a_ref[...]
    for i in range(RW_HEAD):
        s = s_ref[i]
        sa = jnp.sum(s * kkn, axis=0, keepdims=True)
        s1 = s * dec - sa * bvec + v[i:i + 1, :] * k2
        s1_ref[i] = s1
        o_scr[pl.ds(i, 1), :] = jnp.sum(s1 * r, axis=0, keepdims=True)
    o = o_scr[...]
    mu = jnp.mean(o, axis=0, keepdims=True)
    var = jnp.mean((o - mu) ** 2, axis=0, keepdims=True)
    o = (o - mu) * lax.rsqrt(var + GN_EPS) * gw_ref[...] + gb_ref[...]
    o = o + jnp.sum(r * k2 * rk_ref[...], axis=0, keepdims=True) * v
    o_ref[...] = o * _silu(z_ref[...])


def _rwkv_step(p_rows, p_lz_rows, col, mu, w0, w2p, a0, a2p, k_k, k_a, r_k, gn_w, gn_b, s0):
    rows = s0.shape[0]
    wlz = p_lz_rows.shape[-1]
    d = w0.shape[-1]
    n_heads = d // RW_HEAD
    cur = lambda cb: pl.BlockSpec((rows, d), lambda i, cb=cb: (0, cb))
    prv = lambda cb: pl.BlockSpec((rows, d), lambda i, cb=cb: (1, cb))
    vec = lambda n: pl.BlockSpec((1, n), lambda i: (0, 0))
    mu_r, mu_k, mu_v, mu_lo = mu
    out = jax.ShapeDtypeStruct((d, rows), F32)
    terms = pl.pallas_call(
        _rwkv_step_prep_kernel,
        grid=(1,),
        in_specs=[cur(col["r"]), cur(col["k"]), cur(col["v"]),
                  pl.BlockSpec((rows, LANES), lambda i: (0, 0)),
                  pl.BlockSpec((rows, wlz), lambda i: (0, 0)),
                  prv(col["r"]), prv(col["k"]), prv(col["v"]),
                  pl.BlockSpec((rows, LANES), lambda i: (1, 0)),
                  vec(d), vec(d), vec(d), vec(LANES),
                  vec(d), pl.BlockSpec((LANES, d), lambda i: (0, 0)),
                  vec(d), pl.BlockSpec((LANES, d), lambda i: (0, 0)),
                  vec(d), vec(d)],
        out_specs=[pl.BlockSpec((d, rows), lambda i: (0, 0))] * 7,
        out_shape=(out,) * 7,
        compiler_params=_params("arbitrary"),
        name="rwkv_step_prep",
    )(p_rows, p_rows, p_rows, p_lz_rows, p_lz_rows, p_rows, p_rows, p_rows, p_lz_rows,
      mu_r, mu_k, mu_v, mu_lo, w0, w2p, a0, a2p, k_k, k_a)
    per_h = lambda: pl.BlockSpec((RW_HEAD, rows), lambda h: (h, 0))
    par = lambda: pl.BlockSpec((RW_HEAD, 1), lambda h: (h, 0))
    st = lambda: pl.BlockSpec((None, RW_HEAD, RW_HEAD, rows), lambda h: (h, 0, 0, 0))
    o_t, s1_t = pl.pallas_call(
        _rwkv_step_kernel,
        grid=(n_heads,),
        in_specs=[st()] + [per_h()] * 7 + [par()] * 3,
        out_specs=[per_h(), st()],
        out_shape=(jax.ShapeDtypeStruct((d, rows), F32),
                   jax.ShapeDtypeStruct((n_heads, RW_HEAD, RW_HEAD, rows), F32)),
        scratch_shapes=[pltpu.VMEM((RW_HEAD, rows), F32)],
        compiler_params=_params("parallel"),
        name="rwkv_step",
    )(jnp.transpose(s0, (1, 2, 3, 0)), *terms,
      r_k.reshape(d, 1), gn_w.reshape(d, 1), gn_b.reshape(d, 1))
    return o_t.T.astype(BF16), jnp.transpose(s1_t, (3, 0, 1, 2))


def _out_kernel(ys_ref, z_ref, wg_ref, bg_ref, or_ref, gs_ref, gr_ref, x_ref, gt_ref, w1_ref, w2_ref,
                fg_ref, y_ref, *, paired_blocks):
    ys = ys_ref[...].astype(F32)
    if paired_blocks:
        nb = ys.shape[1] // LANES
        pos = [2 * j if j < nb // 2 else 2 * (j - nb // 2) + 1 for j in range(nb)]
        ys = jnp.concatenate([ys[:, p * LANES:(p + 1) * LANES] for p in pos], axis=1)
    ys = jax.nn.gelu(ys, approximate=True)
    o_s = (ys * _sigmoid(_dot(ys, wg_ref[...]) + bg_ref[...]) * _silu(z_ref[...])).astype(BF16)
    mixed = (gs_ref[...].astype(F32) * jnp.dot(o_s, w1_ref[...], preferred_element_type=F32)
             + gr_ref[...].astype(F32) * jnp.dot(or_ref[...], w2_ref[...], preferred_element_type=F32))
    x = x_ref[...] + gt_ref[...] * mixed
    y_ref[...] = x * lax.rsqrt(jnp.mean(x * x, axis=-1, keepdims=True) + RMS_EPS) * fg_ref[...]


def _out_proj(y_s5, p_main, o_r, p_gate, col, x, mod, w_glu, b_glu, w_out, final_g, tt, paired_blocks):
    b, t, d = x.shape
    dh = o_r.shape[-1]
    tt = min(tt, t)
    tm = 1 if mod.shape[1] == 1 else tt
    gt_map = (lambda i, j: (i, 0, 2)) if tm == 1 else (lambda i, j: (i, j, 2))
    const = lambda shp, r=0: pl.BlockSpec(shp, lambda i, j: (r, 0), pipeline_mode=pl.Buffered(1))
    return pl.pallas_call(
        functools.partial(_out_kernel, paired_blocks=paired_blocks),
        grid=(b, t // tt),
        in_specs=[pl.BlockSpec((None, tt, dh), lambda i, j: (i, j, 0)),
                  pl.BlockSpec((None, tt, dh), lambda i, j: (i, j, col["z_s5"])),
                  const((dh, dh)), const((1, dh)),
                  pl.BlockSpec((None, tt, dh), lambda i, j: (i, j, 0)),
                  pl.BlockSpec((None, tt, d), lambda i, j: (i, j, col["g_s5"])),
                  pl.BlockSpec((None, tt, d), lambda i, j: (i, j, col["g_rw"])),
                  pl.BlockSpec((None, tt, d), lambda i, j: (i, j, 0)),
                  pl.BlockSpec((None, tm, d), gt_map),
                  const((dh, d), 0), const((dh, d), 1), const((1, d))],
        out_specs=pl.BlockSpec((None, tt, d), lambda i, j: (i, j, 0)),
        out_shape=jax.ShapeDtypeStruct((b, t, d), F32),
        compiler_params=_params("parallel", "parallel"),
        name="out_proj",
    )(y_s5, p_main, w_glu, b_glu.reshape(1, dh), o_r, p_gate, p_gate, x, mod, w_out, w_out,
      final_g.reshape(1, d))


def kernel(x_prompt, x_sample, c_prompt, c_sample, state_s5_re, state_s5_im, state_wkv, state_shift, norm_g, w_ada, b_ada, w_in, mu_rw, A_re, A_im, log_step, B_re, B_im, C_re, C_im, D_skip, w_glu, b_glu, w0, w2, a0, a2, k_k, k_a, r_k, gn_w, gn_b, w_out, final_g):
    depth = norm_g.shape[0]
    assert depth == 1
    bp, tp, d = x_prompt.shape
    bs = x_sample.shape[0]
    assert x_sample.shape[1] == 1
    dh = d // 2
    l = 0

    w_main, w_lz, w_gate = 5 * dh, 2 * LORA + dh, 2 * d
    col = {"u": 0, "z_s5": 1, "r": 2, "k": 3, "v": 4, "g_s5": 0, "g_rw": 1}
    w = w_in[l]

    def project(rows, rows2):
        return (_in_proj(rows, rows2, w, 0, w_main, w_main // IN_PROJ_COL_TILES, IN_PROJ_ROWS),
                _in_proj(rows, rows2, w, w_main, w_lz, w_lz, IN_PROJ_ROWS),
                _in_proj(rows, rows2, w, w_main + w_lz, w_gate, w_gate // IN_PROJ_COL_TILES, IN_PROJ_ROWS,
                         gate=True))

    mu = mu_rw[l]
    mu_parts = (mu[None, :dh], mu[None, dh:2 * dh], mu[None, 2 * dh:3 * dh], mu[None, 3 * dh:])
    zpad = jnp.zeros((LORA, dh), F32)
    w2p = jnp.concatenate([w2[l], zpad], axis=0).astype(BF16)
    a2p = jnp.concatenate([zpad, a2[l]], axis=0).astype(BF16)
    row = lambda x: x.reshape(1, -1)
    rw_params = (row(w0[l]), w2p, row(a0[l]), a2p, row(k_k[l]), row(k_a[l]), row(r_k[l]),
                 row(gn_w[l]), row(gn_b[l]))
    w_out_bf = w_out[l].astype(BF16)
    w_glu_bf = w_glu[l].astype(BF16)

    bw2, cw2, a_blk_re, a_blk_im, a_slot_re, a_slot_im = _s5_weights(
        A_re[l], A_im[l], log_step[l], B_re[l], B_im[l], C_re[l], C_im[l], bp)
    d_skip = D_skip[l].reshape(1, -1)

    mod_p, mod_s = _mod(c_prompt, c_sample, w_ada[l], b_ada[l])
    mod_p = mod_p.reshape(bp, 1, 3 * d)
    mod_s = mod_s.reshape(1, bs, 3 * d)

    h_p = _modulated_norm(x_prompt, norm_g[l], mod_p, BF16, NORM_ROWS)
    shift_p = _modulated_norm(x_prompt[:, tp - 1:, :], norm_g[l], mod_p, F32, 1)[:, 0]
    xs = x_sample.reshape(1, bs, d)
    h_s = _modulated_norm(xs, norm_g[l], mod_s, F32, bs)[0]
    a_s = jnp.concatenate([h_s, state_shift[l]], axis=0).astype(BF16)
    (pm, ps), (plz, pslz), (pg, psg) = project(h_p.reshape(bp * tp, d), a_s)

    pm3 = pm.reshape(bp, tp, -1)
    y_s5, xre_p, xim_p = _s5_sequence(pm3, bw2, cw2, a_slot_re, a_slot_im, d_skip)
    o_r, hs_p = _rwkv_sequence(pm3, plz.reshape(bp, tp, -1), col, mu_parts, *rw_params)
    y_prompt = _out_proj(y_s5, pm3, o_r, pg.reshape(bp, tp, -1), col, x_prompt, mod_p,
                         w_glu_bf, b_glu[l], w_out_bf, final_g, OUT_ROWS, True)
    g_s5 = A_re.shape[1]
    s5_shape = (1, bp, g_s5, P_S5)
    wkv_p = hs_p[None]

    y_s5s, xre_s, xim_s = _s5_step(ps, bw2, cw2, a_blk_re, a_blk_im, d_skip,
                                   state_s5_re[l].reshape(bs, -1), state_s5_im[l].reshape(bs, -1))
    o_rs, wkv_s = _rwkv_step(ps, pslz, col, mu_parts, *rw_params, state_wkv[l])
    y_sample = _out_proj(y_s5s[None], ps.reshape(2, bs, -1), o_rs[None], psg.reshape(2, bs, -1), col, xs, mod_s,
                         w_glu_bf, b_glu[l], w_out_bf, final_g, bs, False)
    y_sample = y_sample.reshape(bs, 1, d)

    return (y_prompt, y_sample,
            xre_p.reshape(s5_shape), xim_p.reshape(s5_shape), wkv_p, shift_p[None],
            xre_s.reshape(1, bs, g_s5, P_S5), xim_s.reshape(1, bs, g_s5, P_S5), wkv_s[None], h_s[None])
```

```python
import functools

import jax
import jax.numpy as jnp
from jax import lax
from jax.experimental import pallas as pl
from jax.experimental.pallas import tpu as pltpu

F32 = jnp.float32
BF16 = jnp.bfloat16

RMS_EPS = 1e-6
GN_EPS = 64e-5
S5_GROUP = 16
P_S5 = 64
RW_HEAD = 64
LORA = 64
LANES = 128
GROUPS_PER_BLOCK = LANES // S5_GROUP
STATES_PER_BLOCK = GROUPS_PER_BLOCK * P_S5
S5_TC = 512
RW_CHUNK = 64
INV_BASE = 8
RW_SEQS = 4
RW_SEQ_LAG = 1

V7X_VMEM_BYTES = 64 * 1024 * 1024
VMEM_LIMIT_BYTES = V7X_VMEM_BYTES // 8 * 7
MOD_TN = 1024
NORM_ROWS = 1024
IN_PROJ_ROWS = 1024
IN_PROJ_COL_TILES = 4
OUT_ROWS = 512

_NT = (((1,), (1,)), ((), ()))
_TN = (((0,), (0,)), ((), ()))


def _dot(a, b):
    return jnp.dot(a.astype(BF16), b.astype(BF16), preferred_element_type=F32)


def _sigmoid(x):
    return 0.5 * jnp.tanh(0.5 * x) + 0.5


def _silu(x):
    return x * _sigmoid(x)


def _params(*sem):
    return pltpu.CompilerParams(dimension_semantics=sem, vmem_limit_bytes=VMEM_LIMIT_BYTES)


def _mod_kernel(c1_ref, c2_ref, w_ref, b_ref, o1_ref, o2_ref):
    w = w_ref[...].astype(BF16)
    for c_ref, o_ref in ((c1_ref, o1_ref), (c2_ref, o2_ref)):
        o_ref[...] = _dot(_silu(c_ref[...]), w) + b_ref[...]


def _mod(c1, c2, w, b):
    d, n = w.shape
    rows = lambda c: pl.BlockSpec((c.shape[0], d), lambda j: (0, 0))
    out = lambda c: pl.BlockSpec((c.shape[0], MOD_TN), lambda j: (0, j))
    return pl.pallas_call(
        _mod_kernel,
        grid=(n // MOD_TN,),
        in_specs=[rows(c1), rows(c2),
                  pl.BlockSpec((d, MOD_TN), lambda j: (0, j)),
                  pl.BlockSpec((1, MOD_TN), lambda j: (0, j))],
        out_specs=[out(c1), out(c2)],
        out_shape=(jax.ShapeDtypeStruct((c1.shape[0], n), F32), jax.ShapeDtypeStruct((c2.shape[0], n), F32)),
        compiler_params=_params("parallel"),
        name="adaln_mod",
    )(c1, c2, w, b.reshape(1, n))


def _h_kernel(x_ref, g_ref, sh_ref, sc_ref, h_ref):
    x = x_ref[...]
    y = x * lax.rsqrt(jnp.mean(x * x, axis=-1, keepdims=True) + RMS_EPS) * g_ref[...]
    h_ref[...] = (y * (1.0 + sc_ref[...]) + sh_ref[...]).astype(h_ref.dtype)


def _modulated_norm(x, g, mod, out_dtype, tt):
    b, t, d = x.shape
    tt = min(tt, t)
    tm = 1 if mod.shape[1] == 1 else tt
    mod_map = (lambda i, j: (i, 0, 0)) if tm == 1 else (lambda i, j: (i, j, 0))
    mod_map1 = (lambda i, j: (i, 0, 1)) if tm == 1 else (lambda i, j: (i, j, 1))
    return pl.pallas_call(
        _h_kernel,
        grid=(b, t // tt),
        in_specs=[pl.BlockSpec((None, tt, d), lambda i, j: (i, j, 0)),
                  pl.BlockSpec((1, d), lambda i, j: (0, 0)),
                  pl.BlockSpec((None, tm, d), mod_map),
                  pl.BlockSpec((None, tm, d), mod_map1)],
        out_specs=pl.BlockSpec((None, tt, d), lambda i, j: (i, j, 0)),
        out_shape=jax.ShapeDtypeStruct((b, t, d), out_dtype),
        compiler_params=_params("parallel", "parallel"),
        name="modulated_norm",
    )(x, g.reshape(1, d), mod, mod)


def _in_proj_kernel(a_ref, a2_ref, w_ref, o_ref, o2_ref, wbf_ref, *, gate):
    i = pl.program_id(1)

    def project(x_ref, out_ref):
        p = jnp.dot(x_ref[...], wbf_ref[...], preferred_element_type=F32)
        out_ref[...] = (_sigmoid(p) if gate else p).astype(out_ref.dtype)

    @pl.when(i == 0)
    def _():
        wbf_ref[...] = w_ref[...].astype(BF16)
        project(a2_ref, o2_ref)

    @pl.when(i > 0)
    def _():
        project(a_ref, o_ref)


def _in_proj(a, a2, w, col0, width, tn, tm, gate=False):
    m, k = a.shape
    m2 = a2.shape[0]
    tm = min(tm, m)
    assert width % tn == 0 and m % tm == 0 and col0 % LANES == 0
    main = lambda i: jnp.maximum(i - 1, 0)
    dt = BF16 if gate else F32
    return pl.pallas_call(
        functools.partial(_in_proj_kernel, gate=gate),
        grid=(width // tn, m // tm + 1),
        in_specs=[pl.BlockSpec((tm, k), lambda j, i: (main(i), 0)),
                  pl.BlockSpec((m2, k), lambda j, i: (0, 0)),
                  pl.BlockSpec((pl.Element(k), pl.Element(tn)), lambda j, i: (0, pl.multiple_of(col0 + j * tn, LANES)))],
        out_specs=[pl.BlockSpec((tm, tn), lambda j, i: (main(i), j)),
                   pl.BlockSpec((m2, tn), lambda j, i: (0, j))],
        out_shape=(jax.ShapeDtypeStruct((m, width), dt), jax.ShapeDtypeStruct((m2, width), dt)),
        scratch_shapes=[pltpu.VMEM((k, tn), BF16)],
        compiler_params=_params("parallel", "arbitrary"),
        name="in_proj",
    )(a, a2, w)


def _s5_weights_kernel(*refs, n_b):
    (are0, are1, aim0, aim1, ls0, ls1, bre0, bre1, bim0, bim1, cre0, cre1, cim0, cim1,
     bw_ref, cw_ref, abr_ref, abi_ref, apr_ref, api_ref) = refs
    n_gl = GROUPS_PER_BLOCK
    n_lt = STATES_PER_BLOCK // LANES

    def discretise(are_ref, aim_ref, ls_ref, bre_ref, bim_ref):
        step = jnp.exp(ls_ref[...])
        lam_re = jnp.minimum(are_ref[...], -1e-4)
        lam_im = aim_ref[...]
        mag = jnp.exp(lam_re * step)
        ab_re = mag * jnp.cos(lam_im * step)
        ab_im = mag * jnp.sin(lam_im * step)
        den = lam_re * lam_re + lam_im * lam_im
        f_re = ((ab_re - 1.0) * lam_re + ab_im * lam_im) / den
        f_im = (ab_im * lam_re - (ab_re - 1.0) * lam_im) / den
        br, bi = bre_ref[...], bim_ref[...]
        return ab_re, ab_im, f_re * br - f_im * bi, f_re * bi + f_im * br

    halves = [discretise(are0, aim0, ls0, bre0, bim0), discretise(are1, aim1, ls1, bre1, bim1)]
    zero = jnp.zeros((S5_GROUP, P_S5), F32)

    def band(t_re, t_im, gl):
        pieces = []
        for lt in range(n_lt):
            for tile in (t_re, t_im):
                for half in range(2):
                    pieces.append(tile if 2 * lt + half == gl else zero)
        return jnp.concatenate(pieces, axis=1)

    def block_rows(tiles_re, tiles_im):
        return jnp.concatenate([band(tiles_re[h][gl], tiles_im[h][gl], gl)
                                for h in range(2) for gl in range(n_gl)], axis=0)

    bw_ref[...] = block_rows([h[2] for h in halves], [h[3] for h in halves]).astype(BF16)
    cw_ref[...] = block_rows([cre0[...], cre1[...]], [-cim0[...], -cim1[...]]).T.astype(BF16)

    for h in range(2):
        for a_ref, a in ((abr_ref, halves[h][0]), (abi_ref, halves[h][1])):
            a_ref[h] = jnp.concatenate([a[gl] for gl in range(n_gl)], axis=1)
    for p_ref, a_ref in ((apr_ref, abr_ref), (api_ref, abi_ref)):
        p_ref[...] = jnp.concatenate([a_ref[0], a_ref[1]] * n_b, axis=0)


def _s5_weights(a_re, a_im, log_step, b_re, b_im, c_re, c_im, n_b):
    g, p = a_re.shape
    c = b_re.shape[-1]
    nq = g // GROUPS_PER_BLOCK // 2
    ns = STATES_PER_BLOCK
    args, specs = [], []
    for x, shp in ((a_re.reshape(g, 1, p), (1, p)), (a_im.reshape(g, 1, p), (1, p)),
                   (log_step.reshape(g, 1, 1), (1, 1)),
                   (jnp.swapaxes(b_re, 1, 2), (c, p)), (jnp.swapaxes(b_im, 1, 2), (c, p)),
                   (c_re, (c, p)), (c_im, (c, p))):
        for half in range(2):
            args.append(x)
            specs.append(pl.BlockSpec((GROUPS_PER_BLOCK,) + shp, lambda q, half=half: (q + half * nq, 0, 0)))
    return pl.pallas_call(
        functools.partial(_s5_weights_kernel, n_b=n_b),
        grid=(nq,),
        in_specs=specs,
        out_specs=[pl.BlockSpec((None, 2 * LANES, 2 * ns), lambda q: (q, 0, 0)),
                   pl.BlockSpec((None, 2 * ns, 2 * LANES), lambda q: (q, 0, 0)),
                   pl.BlockSpec((None, 2, 1, ns), lambda q: (q, 0, 0, 0)),
                   pl.BlockSpec((None, 2, 1, ns), lambda q: (q, 0, 0, 0)),
                   pl.BlockSpec((None, 2 * n_b, ns), lambda q: (q, 0, 0)),
                   pl.BlockSpec((None, 2 * n_b, ns), lambda q: (q, 0, 0))],
        out_shape=(jax.ShapeDtypeStruct((nq, 2 * LANES, 2 * ns), BF16),
                   jax.ShapeDtypeStruct((nq, 2 * ns, 2 * LANES), BF16),
                   jax.ShapeDtypeStruct((nq, 2, 1, ns), F32), jax.ShapeDtypeStruct((nq, 2, 1, ns), F32),
                   jax.ShapeDtypeStruct((nq, 2 * n_b, ns), F32), jax.ShapeDtypeStruct((nq, 2 * n_b, ns), F32)),
        compiler_params=_params("parallel"),
        name="s5_weights",
    )(*args)


def _s5_seq_kernel(u0_ref, u1_ref, bw_ref, cw_ref, are_ref, aim_ref, d_ref,
                   y_ref, xre_ref, xim_ref, lhs_ref, bu_ref, y_scr, x_scr, *, tc, n_b):
    n_lt = STATES_PER_BLOCK // LANES
    slots = 2 * n_b
    re_l = lambda lt: slice(2 * lt * LANES, (2 * lt + 1) * LANES)
    im_l = lambda lt: slice((2 * lt + 1) * LANES, (2 * lt + 2) * LANES)
    nat = lambda lt: slice(lt * LANES, (lt + 1) * LANES)
    c = pl.program_id(1)

    @pl.when(c == 0)
    def _():
        x_scr[...] = jnp.zeros_like(x_scr)
        lhs_ref[...] = jnp.zeros_like(lhs_ref)

    for b in range(n_b):
        lhs_ref.at[0][pl.ds(2 * b, tc, stride=slots), :] = u0_ref[b]
        lhs_ref.at[1][pl.ds(2 * b + 1, tc, stride=slots), :] = u1_ref[b]
    u_rows = lhs_ref[0] + lhs_ref[1]
    lhs = jnp.concatenate([lhs_ref[0], lhs_ref[1]], axis=1)
    bu_ref[...] = _dot(lhs, bw_ref[...])

    a_re = [are_ref[:, nat(lt)] for lt in range(n_lt)]
    a_im = [aim_ref[:, nat(lt)] for lt in range(n_lt)]
    x = [(x_scr[:, re_l(lt)], x_scr[:, im_l(lt)]) for lt in range(n_lt)]
    for t in range(tc):
        rows = slice(t * slots, (t + 1) * slots)
        for lt in range(n_lt):
            xr, xi = x[lt]
            nr = a_re[lt] * xr - a_im[lt] * xi + bu_ref[rows, re_l(lt)]
            ni = a_re[lt] * xi + a_im[lt] * xr + bu_ref[rows, im_l(lt)]
            bu_ref[rows, re_l(lt)] = nr
            bu_ref[rows, im_l(lt)] = ni
            x[lt] = (nr, ni)
    for lt in range(n_lt):
        x_scr[:, re_l(lt)] = x[lt][0]
        x_scr[:, im_l(lt)] = x[lt][1]

    yf = _dot(bu_ref[...], cw_ref[...])
    first_half = (lax.broadcasted_iota(jnp.int32, (tc * slots, LANES), 0) & 1) == 0
    y = jnp.where(first_half, yf[:, :LANES], yf[:, LANES:])
    skip = (u_rows.reshape(tc, slots, LANES) * d_ref[...][None]).reshape(tc * slots, LANES)
    y_scr[...] = y + skip
    for b in range(n_b):
        y_ref[b, :, :LANES] = y_scr[pl.ds(2 * b, tc, stride=slots), :].astype(y_ref.dtype)
        y_ref[b, :, LANES:] = y_scr[pl.ds(2 * b + 1, tc, stride=slots), :].astype(y_ref.dtype)

    @pl.when(c == pl.num_programs(1) - 1)
    def _():
        for lt in range(n_lt):
            xre_ref[:, nat(lt)] = x[lt][0]
            xim_ref[:, nat(lt)] = x[lt][1]


def _s5_sequence(p_main, bw2, cw2, a_slot_re, a_slot_im, d_skip):
    b, t, _ = p_main.shape
    nq = bw2.shape[0]
    nb = 2 * nq
    ns = STATES_PER_BLOCK
    slots = 2 * b
    assert slots == 8, "one 8-row tile must hold every (sequence, half) slot"
    tc = min(S5_TC, t)
    d_blk = d_skip.reshape(nb, LANES)
    d_slot = jnp.tile(jnp.stack([d_blk[:nq], d_blk[nq:]], axis=1), (1, b, 1))
    slot_vec = lambda n: pl.BlockSpec((None, slots, n), lambda q, c: (q, 0, 0))
    y, xre, xim = pl.pallas_call(
        functools.partial(_s5_seq_kernel, tc=tc, n_b=b),
        grid=(nq, t // tc),
        in_specs=[pl.BlockSpec((b, tc, LANES), lambda q, c: (0, c, q)),
                  pl.BlockSpec((b, tc, LANES), lambda q, c: (0, c, q + nq)),
                  pl.BlockSpec((None, 2 * LANES, 2 * ns), lambda q, c: (q, 0, 0)),
                  pl.BlockSpec((None, 2 * ns, 2 * LANES), lambda q, c: (q, 0, 0)),
                  slot_vec(ns), slot_vec(ns), slot_vec(LANES)],
        out_specs=[pl.BlockSpec((b, tc, 2 * LANES), lambda q, c: (0, c, q)),
                   slot_vec(ns), slot_vec(ns)],
        out_shape=(jax.ShapeDtypeStruct((b, t, nb * LANES), BF16),
                   jax.ShapeDtypeStruct((nq, slots, ns), F32),
                   jax.ShapeDtypeStruct((nq, slots, ns), F32)),
        scratch_shapes=[pltpu.VMEM((2, tc * slots, LANES), F32),
                        pltpu.VMEM((tc * slots, 2 * ns), F32),
                        pltpu.VMEM((tc * slots, LANES), F32),
                        pltpu.VMEM((slots, 2 * ns), F32)],
        compiler_params=_params("parallel", "arbitrary"),
        name="s5_sequence",
    )(p_main, p_main, bw2, cw2, a_slot_re, a_slot_im, d_slot)

    def state(x):
        x = x.reshape(nq, b, 2, ns)
        return jnp.transpose(x, (1, 2, 0, 3)).reshape(b, nb * ns)

    return y, state(xre), state(xim)


def _s5_step_kernel(u_ref, bw_ref, cw_ref, abr_ref, abi_ref, d_ref, x0r_ref, x0i_ref,
                    y_ref, x1r_ref, x1i_ref):
    n_lt = STATES_PER_BLOCK // LANES
    u = u_ref[...]
    bu = _dot(u, bw_ref[...])
    tiles = []
    for lt in range(n_lt):
        nat = slice(lt * LANES, (lt + 1) * LANES)
        ar, ai = abr_ref[:, nat], abi_ref[:, nat]
        x0r, x0i = x0r_ref[:, nat], x0i_ref[:, nat]
        xr = ar * x0r - ai * x0i + bu[:, 2 * lt * LANES:(2 * lt + 1) * LANES]
        xi = ar * x0i + ai * x0r + bu[:, (2 * lt + 1) * LANES:(2 * lt + 2) * LANES]
        x1r_ref[:, nat] = xr
        x1i_ref[:, nat] = xi
        tiles += [xr, xi]
    y_ref[...] = _dot(jnp.concatenate(tiles, axis=1), cw_ref[...]) + d_ref[...] * u


def _s5_step(p_rows, bw2, cw2, a_blk_re, a_blk_im, d_skip, x0_re, x0_im):
    rows = x0_re.shape[0]
    nq = bw2.shape[0]
    nb = 2 * nq
    ns = STATES_PER_BLOCK
    vec = lambda: pl.BlockSpec((None, None, 1, ns), lambda j: (j % nq, j // nq, 0, 0))
    st = lambda: pl.BlockSpec((rows, ns), lambda j: (0, j))
    return pl.pallas_call(
        _s5_step_kernel,
        grid=(nb,),
        in_specs=[pl.BlockSpec((rows, LANES), lambda j: (0, j)),
                  pl.BlockSpec((None, LANES, 2 * ns), lambda j: (j % nq, j // nq, 0)),
                  pl.BlockSpec((None, 2 * ns, LANES), lambda j: (j % nq, 0, j // nq)),
                  vec(), vec(),
                  pl.BlockSpec((1, LANES), lambda j: (0, j)),
                  st(), st()],
        out_specs=[pl.BlockSpec((rows, LANES), lambda j: (0, j)), st(), st()],
        out_shape=(jax.ShapeDtypeStruct((rows, nb * LANES), F32),
                   jax.ShapeDtypeStruct((rows, nb * ns), F32),
                   jax.ShapeDtypeStruct((rows, nb * ns), F32)),
        compiler_params=_params("parallel"),
        name="s5_step",
    )(p_rows, bw2, cw2, a_blk_re, a_blk_im, d_skip, x0_re, x0_im)


def _softplus(x):
    return jnp.maximum(x, 0.0) + jnp.log(1.0 + jnp.exp(-jnp.abs(x)))


def _rwkv_token_terms(r, k, lo, w0, w2p, a0, a2p, k_a):
    w = -_softplus(-(w0 + _dot(jnp.tanh(lo), w2p))) - 0.5
    logd = -jnp.exp(w)
    a = _sigmoid(a0 + _dot(lo, a2p))
    k2 = k * (1.0 + (a - 1.0) * k_a)
    return logd, a, k2


def _rwkv_chunk_kernel(r_ref, k_ref, v_ref, lo_ref, z_ref,
                       mur_ref, muk_ref, muv_ref, mulo_ref,
                       w0_ref, w2p_ref, a0_ref, a2p_ref, kk_ref, ka_ref, rk_ref, gw_ref, gb_ref,
                       o_ref, hs_ref,
                       h_scr, pr_scr, pk_scr, pv_scr, plo_scr, *, n_heads, n_seqs):
    L = RW_CHUNK
    W = 2 * RW_HEAD
    n_pairs = n_heads // 2
    c = pl.program_id(1)

    @pl.when(c == 0)
    def _():
        h_scr[...] = jnp.zeros_like(h_scr)
        pr_scr[...] = jnp.zeros_like(pr_scr)
        pk_scr[...] = jnp.zeros_like(pk_scr)
        pv_scr[...] = jnp.zeros_like(pv_scr)
        plo_scr[...] = jnp.zeros_like(plo_scr)

    row1 = lax.broadcasted_iota(jnp.int32, (L, 1), 0)

    def token_shift(cur, prev_scr, mu_ref):
        prev = jnp.where(row1 == 0, prev_scr[...], pltpu.roll(cur, 1, 0))
        prev_scr[...] = cur[L - 1:L, :]
        return cur + (prev - cur) * mu_ref[...]

    tri = (lax.broadcasted_iota(jnp.int32, (L, L), 0)
           >= lax.broadcasted_iota(jnp.int32, (L, L), 1)).astype(BF16)
    first_head = lax.broadcasted_iota(jnp.int32, (L, W), 1) < RW_HEAD

    def head_sum(x):
        s0 = jnp.sum(jnp.where(first_head, x, 0.0), axis=-1, keepdims=True)
        s1 = jnp.sum(jnp.where(first_head, 0.0, x), axis=-1, keepdims=True)
        return jnp.where(first_head, s0, s1)

    def stack_heads(x):
        return jnp.concatenate([jnp.where(first_head, x, 0.0), jnp.where(first_head, 0.0, x)], axis=0)

    ri = lax.broadcasted_iota(jnp.int32, (2 * L, 2 * L), 0)
    ci = lax.broadcasted_iota(jnp.int32, (2 * L, 2 * L), 1)
    t_row, t_col = ri & (L - 1), ci & (L - 1)
    strict = t_row > t_col
    incl = t_row >= t_col
    eye = (ri == ci).astype(F32)
    blk_masks = []
    s = INV_BASE
    while s <= L:
        blk_masks.append((ri // s) == (ci // s))
        s *= 2

    pairs = range(n_pairs)
    loc = [slice(p * W, (p + 1) * W) for p in pairs]

    def seq_phases(si):
        r = token_shift(r_ref[si], pr_scr.at[si], mur_ref)
        k = token_shift(k_ref[si], pk_scr.at[si], muk_ref)
        v = token_shift(v_ref[si], pv_scr.at[si], muv_ref)
        lo = token_shift(lo_ref[si], plo_scr.at[si], mulo_ref)
        logd, a, k2 = _rwkv_token_terms(r, k, lo, w0_ref[...], w2p_ref[...], a0_ref[...], a2p_ref[...],
                                        ka_ref[...])
        kk = k * kk_ref[...]
        hi = logd.astype(BF16)
        rem = logd - hi.astype(F32)
        mid = rem.astype(BF16)
        low = (rem - mid.astype(F32)).astype(BF16)
        lp = (jnp.dot(tri, hi, preferred_element_type=F32)
              + jnp.dot(tri, mid, preferred_element_type=F32)
              + jnp.dot(tri, low, preferred_element_type=F32))
        p_inc = jnp.exp(lp)
        p_exc = jnp.exp(lp - logd)
        p_inv = jnp.exp(-lp)
        yield

        lhs, nm, mm, ab, vs, kb, p_end = [], [], [], [], [], [], []
        for sl in loc:
            kk_p = kk[:, sl]
            kkn = kk_p * lax.rsqrt(jnp.maximum(head_sum(kk_p * kk_p), 1e-24))
            pinc, pinv = p_inc[:, sl], p_inv[:, sl]
            kd = k2[:, sl] * pinv
            bd = kkn * a[:, sl] * pinv
            pe = pinc[L - 1:L, :]
            lhs_p = jnp.concatenate([stack_heads(kkn * p_exc[:, sl]), stack_heads(r[:, sl] * pinc)],
                                    axis=0).astype(BF16)
            rhs_p = jnp.concatenate([stack_heads(bd), stack_heads(kd)], axis=0).astype(BF16)
            amat = lax.dot_general(lhs_p, rhs_p, _NT, preferred_element_type=F32)
            lhs.append(lhs_p)
            nm.append(jnp.where(strict, amat[:2 * L, :2 * L], 0.0))
            mm.append(jnp.where(strict, amat[:2 * L, 2 * L:], 0.0).astype(BF16))
            ab.append(jnp.concatenate([jnp.where(incl, amat[2 * L:, 2 * L:], 0.0),
                                       -jnp.where(incl, amat[2 * L:, :2 * L], 0.0)], axis=1).astype(BF16))
            vs.append(stack_heads(v[:, sl]).astype(BF16))
            kb.append(jnp.concatenate([stack_heads(kd * pe), stack_heads(bd * pe)], axis=0).astype(BF16))
            p_end.append(pe)
            yield

        d = [jnp.where(blk_masks[0], n_p, 0.0).astype(BF16) for n_p in nm]
        x = [eye - d_p.astype(F32) for d_p in d]
        pw = [jnp.dot(d_p, d_p, preferred_element_type=F32) for d_p in d]
        yield
        s = 2
        while s < INV_BASE:
            x = [x_p + _dot(x_p, pw_p) for x_p, pw_p in zip(x, pw)]
            s *= 2
            if s < INV_BASE:
                pw = [_dot(pw_p, pw_p) for pw_p in pw]
            yield
        for lvl in range(1, len(blk_masks)):
            off = blk_masks[lvl] & ~blk_masks[lvl - 1]
            xc = [_dot(x_p, jnp.where(off, n_p, 0.0)) for x_p, n_p in zip(x, nm)]
            yield
            x = [x_p - _dot(xc_p, x_p) for x_p, xc_p in zip(x, xc)]
            yield

        hs = [h_scr[si, p] for p in pairs]
        lh = [jnp.dot(lhs_p, hs_p.astype(BF16), preferred_element_type=F32) for lhs_p, hs_p in zip(lhs, hs)]
        mv = [jnp.dot(mm_p, vs_p, preferred_element_type=F32) for mm_p, vs_p in zip(mm, vs)]
        yield
        u = [_dot(x_p, lh_p[:2 * L] + mv_p).astype(BF16) for x_p, lh_p, mv_p in zip(x, lh, mv)]
        yield
        o_st = [lh_p[2 * L:] + jnp.dot(ab_p, jnp.concatenate([vs_p, u_p], axis=0), preferred_element_type=F32)
                for lh_p, ab_p, vs_p, u_p in zip(lh, ab, vs, u)]
        for p in pairs:
            p_end_col = jnp.sum(eye * p_end[p], axis=1, keepdims=True)
            h_scr[si, p] = p_end_col * hs[p] + lax.dot_general(
                kb[p], jnp.concatenate([vs[p], -u[p]], axis=0), _TN, preferred_element_type=F32)
        yield

        for p, sl in enumerate(loc):
            o = o_st[p][:L] + o_st[p][L:]
            mu = head_sum(o) * (1.0 / RW_HEAD)
            var = head_sum((o - mu) ** 2) * (1.0 / RW_HEAD)
            o = (o - mu) * lax.rsqrt(var + GN_EPS) * gw_ref[:, sl] + gb_ref[:, sl]
            o = o + head_sum(r[:, sl] * k2[:, sl] * rk_ref[:, sl]) * v[:, sl]
            z = z_ref[si, :, LANES + p * W:LANES + (p + 1) * W]
            o_ref[si, :, sl] = (o * _silu(z)).astype(o_ref.dtype)
        yield

    gens = [seq_phases(si) for si in range(n_seqs)]
    started = 0
    live = []
    step = 0
    while started < n_seqs or live:
        if started < n_seqs and step % RW_SEQ_LAG == 0:
            live.append(gens[started])
            started += 1
        for g in list(live):
            if next(g, "done") == "done":
                live.remove(g)
        step += 1

    @pl.when(c == pl.num_programs(1) - 1)
    def _():
        for si in range(n_seqs):
            for p in pairs:
                st = h_scr[si, p].T
                hs_ref[si, 2 * p] = st[:RW_HEAD, :RW_HEAD]
                hs_ref[si, 2 * p + 1] = st[RW_HEAD:, RW_HEAD:]


def _rwkv_sequence(p_main, p_lz, col, mu, w0, w2p, a0, a2p, k_k, k_a, r_k, gn_w, gn_b):
    b, t, _ = p_main.shape
    wlz = p_lz.shape[-1]
    d = w0.shape[-1]
    n_heads = d // RW_HEAD
    n_pairs, pw = n_heads // 2, 2 * RW_HEAD
    L = RW_CHUNK
    ns = RW_SEQS if b % RW_SEQS == 0 else 1
    blk = lambda cb: pl.BlockSpec((ns, L, d), lambda i, j, cb=cb: (i, j, cb))
    vec = lambda n: pl.BlockSpec((1, n), lambda i, j: (0, 0))
    full = lambda shp: pl.BlockSpec(shp, lambda i, j: (0,) * len(shp))
    mu_r, mu_k, mu_v, mu_lo = mu
    o, hs = pl.pallas_call(
        functools.partial(_rwkv_chunk_kernel, n_heads=n_heads, n_seqs=ns),
        grid=(b // ns, t // L),
        in_specs=[blk(col["r"]), blk(col["k"]), blk(col["v"]),
                  pl.BlockSpec((ns, L, LANES), lambda i, j: (i, j, 0)),
                  pl.BlockSpec((ns, L, wlz), lambda i, j: (i, j, 0)),
                  vec(d), vec(d), vec(d), vec(LANES),
                  vec(d), full((LANES, d)), vec(d), full((LANES, d)),
                  vec(d), vec(d), vec(d), vec(d), vec(d)],
        out_specs=[pl.BlockSpec((ns, L, d), lambda i, j: (i, j, 0)),
                   pl.BlockSpec((ns, n_heads, RW_HEAD, RW_HEAD), lambda i, j: (i, 0, 0, 0))],
        out_shape=(jax.ShapeDtypeStruct((b, t, d), BF16),
                   jax.ShapeDtypeStruct((b, n_heads, RW_HEAD, RW_HEAD), F32)),
        scratch_shapes=[pltpu.VMEM((ns, n_pairs, pw, pw), F32),
                        pltpu.VMEM((ns, 1, d), F32), pltpu.VMEM((ns, 1, d), F32), pltpu.VMEM((ns, 1, d), F32),
                        pltpu.VMEM((ns, 1, LANES), F32)],
        compiler_params=_params("parallel", "arbitrary"),
        name="rwkv_sequence",
    )(p_main, p_main, p_main, p_lz, p_lz,
      mu_r, mu_k, mu_v, mu_lo, w0, w2p, a0, a2p, k_k, k_a, r_k, gn_w, gn_b)
    return o, hs


def _rwkv_step_prep_kernel(cr_ref, ck_ref, cv_ref, clo_ref, cz_ref, pr_ref, pk_ref, pv_ref, plo_ref,
                           mur_ref, muk_ref, muv_ref, mulo_ref,
                           w0_ref, w2p_ref, a0_ref, a2p_ref, kk_ref, ka_ref,
                           r_o, k2_o, v_o, kk_o, a_o, d_o, z_o):
    def lerp(c_ref, p_ref, mu_ref):
        cur = c_ref[...]
        return cur + (p_ref[...] - cur) * mu_ref[...]

    r = lerp(cr_ref, pr_ref, mur_ref)
    k = lerp(ck_ref, pk_ref, muk_ref)
    v = lerp(cv_ref, pv_ref, muv_ref)
    lo = lerp(clo_ref, plo_ref, mulo_ref)
    logd, a, k2 = _rwkv_token_terms(r, k, lo, w0_ref[...], w2p_ref[...], a0_ref[...], a2p_ref[...],
                                    ka_ref[...])
    r_o[...] = r.T
    k2_o[...] = k2.T
    v_o[...] = v.T
    kk_o[...] = (k * kk_ref[...]).T
    a_o[...] = a.T
    d_o[...] = jnp.exp(logd).T
    z_o[...] = cz_ref[:, LANES:].T


def _rwkv_step_kernel(s_ref, r_ref, k2_ref, v_ref, kk_ref, a_ref, d_ref, z_ref,
                      rk_ref, gw_ref, gb_ref, o_ref, s1_ref, o_scr):
    r, k2, v, dec = r_ref[...], k2_ref[...], v_ref[...], d_ref[...]
    kk = kk_ref[...]
    kkn = kk / jnp.maximum(jnp.sqrt(jnp.sum(kk * kk, axis=0, keepdims=True)), 1e-12)
    bvec = kkn * a_ref[...]
    for i in range(RW_HEAD):
        s = s_ref[i]
        sa = jnp.sum(s * kkn, axis=0, keepdims=True)
        s1 = s * dec - sa * bvec + v[i:i + 1, :] * k2
        s1_ref[i] = s1
        o_scr[pl.ds(i, 1), :] = jnp.sum(s1 * r, axis=0, keepdims=True)
    o = o_scr[...]
    mu = jnp.mean(o, axis=0, keepdims=True)
    var = jnp.mean((o - mu) ** 2, axis=0, keepdims=True)
    o = (o - mu) * lax.rsqrt(var + GN_EPS) * gw_ref[...] + gb_ref[...]
    o = o + jnp.sum(r * k2 * rk_ref[...], axis=0, keepdims=True) * v
    o_ref[...] = o * _silu(z_ref[...])


def _rwkv_step(p_rows, p_lz_rows, col, mu, w0, w2p, a0, a2p, k_k, k_a, r_k, gn_w, gn_b, s0):
    rows = s0.shape[0]
    wlz = p_lz_rows.shape[-1]
    d = w0.shape[-1]
    n_heads = d // RW_HEAD
    cur = lambda cb: pl.BlockSpec((rows, d), lambda i, cb=cb: (0, cb))
    prv = lambda cb: pl.BlockSpec((rows, d), lambda i, cb=cb: (1, cb))
    vec = lambda n: pl.BlockSpec((1, n), lambda i: (0, 0))
    mu_r, mu_k, mu_v, mu_lo = mu
    out = jax.ShapeDtypeStruct((d, rows), F32)
    terms = pl.pallas_call(
        _rwkv_step_prep_kernel,
        grid=(1,),
        in_specs=[cur(col["r"]), cur(col["k"]), cur(col["v"]),
                  pl.BlockSpec((rows, LANES), lambda i: (0, 0)),
                  pl.BlockSpec((rows, wlz), lambda i: (0, 0)),
                  prv(col["r"]), prv(col["k"]), prv(col["v"]),
                  pl.BlockSpec((rows, LANES), lambda i: (1, 0)),
                  vec(d), vec(d), vec(d), vec(LANES),
                  vec(d), pl.BlockSpec((LANES, d), lambda i: (0, 0)),
                  vec(d), pl.BlockSpec((LANES, d), lambda i: (0, 0)),
                  vec(d), vec(d)],
        out_specs=[pl.BlockSpec((d, rows), lambda i: (0, 0))] * 7,
        out_shape=(out,) * 7,
        compiler_params=_params("arbitrary"),
        name="rwkv_step_prep",
    )(p_rows, p_rows, p_rows, p_lz_rows, p_lz_rows, p_rows, p_rows, p_rows, p_lz_rows,
      mu_r, mu_k, mu_v, mu_lo, w0, w2p, a0, a2p, k_k, k_a)
    per_h = lambda: pl.BlockSpec((RW_HEAD, rows), lambda h: (h, 0))
    par = lambda: pl.BlockSpec((RW_HEAD, 1), lambda h: (h, 0))
    st = lambda: pl.BlockSpec((None, RW_HEAD, RW_HEAD, rows), lambda h: (h, 0, 0, 0))
    o_t, s1_t = pl.pallas_call(
        _rwkv_step_kernel,
        grid=(n_heads,),
        in_specs=[st()] + [per_h()] * 7 + [par()] * 3,
        out_specs=[per_h(), st()],
        out_shape=(jax.ShapeDtypeStruct((d, rows), F32),
                   jax.ShapeDtypeStruct((n_heads, RW_HEAD, RW_HEAD, rows), F32)),
        scratch_shapes=[pltpu.VMEM((RW_HEAD, rows), F32)],
        compiler_params=_params("parallel"),
        name="rwkv_step",
    )(jnp.transpose(s0, (1, 2, 3, 0)), *terms,
      r_k.reshape(d, 1), gn_w.reshape(d, 1), gn_b.reshape(d, 1))
    return o_t.T.astype(BF16), jnp.transpose(s1_t, (3, 0, 1, 2))


def _out_kernel(ys_ref, z_ref, wg_ref, bg_ref, or_ref, gs_ref, gr_ref, x_ref, gt_ref, w1_ref, w2_ref,
                fg_ref, y_ref, *, paired_blocks):
    ys = ys_ref[...].astype(F32)
    if paired_blocks:
        nb = ys.shape[1] // LANES
        pos = [2 * j if j < nb // 2 else 2 * (j - nb // 2) + 1 for j in range(nb)]
        ys = jnp.concatenate([ys[:, p * LANES:(p + 1) * LANES] for p in pos], axis=1)
    ys = jax.nn.gelu(ys, approximate=True)
    o_s = (ys * _sigmoid(_dot(ys, wg_ref[...]) + bg_ref[...]) * _silu(z_ref[...])).astype(BF16)
    mixed = (gs_ref[...].astype(F32) * jnp.dot(o_s, w1_ref[...], preferred_element_type=F32)
             + gr_ref[...].astype(F32) * jnp.dot(or_ref[...], w2_ref[...], preferred_element_type=F32))
    x = x_ref[...] + gt_ref[...] * mixed
    y_ref[...] = x * lax.rsqrt(jnp.mean(x * x, axis=-1, keepdims=True) + RMS_EPS) * fg_ref[...]


def _out_proj(y_s5, p_main, o_r, p_gate, col, x, mod, w_glu, b_glu, w_out, final_g, tt, paired_blocks):
    b, t, d = x.shape
    dh = o_r.shape[-1]
    tt = min(tt, t)
    tm = 1 if mod.shape[1] == 1 else tt
    gt_map = (lambda i, j: (i, 0, 2)) if tm == 1 else (lambda i, j: (i, j, 2))
    const = lambda shp, r=0: pl.BlockSpec(shp, lambda i, j: (r, 0), pipeline_mode=pl.Buffered(1))
    return pl.pallas_call(
        functools.partial(_out_kernel, paired_blocks=paired_blocks),
        grid=(b, t // tt),
        in_specs=[pl.BlockSpec((None, tt, dh), lambda i, j: (i, j, 0)),
                  pl.BlockSpec((None, tt, dh), lambda i, j: (i, j, col["z_s5"])),
                  const((dh, dh)), const((1, dh)),
                  pl.BlockSpec((None, tt, dh), lambda i, j: (i, j, 0)),
                  pl.BlockSpec((None, tt, d), lambda i, j: (i, j, col["g_s5"])),
                  pl.BlockSpec((None, tt, d), lambda i, j: (i, j, col["g_rw"])),
                  pl.BlockSpec((None, tt, d), lambda i, j: (i, j, 0)),
                  pl.BlockSpec((None, tm, d), gt_map),
                  const((dh, d), 0), const((dh, d), 1), const((1, d))],
        out_specs=pl.BlockSpec((None, tt, d), lambda i, j: (i, j, 0)),
        out_shape=jax.ShapeDtypeStruct((b, t, d), F32),
        compiler_params=_params("parallel", "parallel"),
        name="out_proj",
    )(y_s5, p_main, w_glu, b_glu.reshape(1, dh), o_r, p_gate, p_gate, x, mod, w_out, w_out,
      final_g.reshape(1, d))


def kernel(x_prompt, x_sample, c_prompt, c_sample, state_s5_re, state_s5_im, state_wkv, state_shift, norm_g, w_ada, b_ada, w_in, mu_rw, A_re, A_im, log_step, B_re, B_im, C_re, C_im, D_skip, w_glu, b_glu, w0, w2, a0, a2, k_k, k_a, r_k, gn_w, gn_b, w_out, final_g):
    depth = norm_g.shape[0]
    assert depth == 1
    bp, tp, d = x_prompt.shape
    bs = x_sample.shape[0]
    assert x_sample.shape[1] == 1
    dh = d // 2
    l = 0

    w_main, w_lz, w_gate = 5 * dh, 2 * LORA + dh, 2 * d
    col = {"u": 0, "z_s5": 1, "r": 2, "k": 3, "v": 4, "g_s5": 0, "g_rw": 1}
    w = w_in[l]

    def project(rows, rows2):
        return (_in_proj(rows, rows2, w, 0, w_main, w_main // IN_PROJ_COL_TILES, IN_PROJ_ROWS),
                _in_proj(rows, rows2, w, w_main, w_lz, w_lz, IN_PROJ_ROWS),
                _in_proj(rows, rows2, w, w_main + w_lz, w_gate, w_gate // IN_PROJ_COL_TILES, IN_PROJ_ROWS,
                         gate=True))

    mu = mu_rw[l]
    mu_parts = (mu[None, :dh], mu[None, dh:2 * dh], mu[None, 2 * dh:3 * dh], mu[None, 3 * dh:])
    zpad = jnp.zeros((LORA, dh), F32)
    w2p = jnp.concatenate([w2[l], zpad], axis=0).astype(BF16)
    a2p = jnp.concatenate([zpad, a2[l]], axis=0).astype(BF16)
    row = lambda x: x.reshape(1, -1)
    rw_params = (row(w0[l]), w2p, row(a0[l]), a2p, row(k_k[l]), row(k_a[l]), row(r_k[l]),
                 row(gn_w[l]), row(gn_b[l]))
    w_out_bf = w_out[l].astype(BF16)
    w_glu_bf = w_glu[l].astype(BF16)

    bw2, cw2, a_blk_re, a_blk_im, a_slot_re, a_slot_im = _s5_weights(
        A_re[l], A_im[l], log_step[l], B_re[l], B_im[l], C_re[l], C_im[l], bp)
    d_skip = D_skip[l].reshape(1, -1)

    mod_p, mod_s = _mod(c_prompt, c_sample, w_ada[l], b_ada[l])
    mod_p = mod_p.reshape(bp, 1, 3 * d)
    mod_s = mod_s.reshape(1, bs, 3 * d)

    h_p = _modulated_norm(x_prompt, norm_g[l], mod_p, BF16, NORM_ROWS)
    shift_p = _modulated_norm(x_prompt[:, tp - 1:, :], norm_g[l], mod_p, F32, 1)[:, 0]
    xs = x_sample.reshape(1, bs, d)
    h_s = _modulated_norm(xs, norm_g[l], mod_s, F32, bs)[0]
    a_s = jnp.concatenate([h_s, state_shift[l]], axis=0).astype(BF16)
    (pm, ps), (plz, pslz), (pg, psg) = project(h_p.reshape(bp * tp, d), a_s)

    pm3 = pm.reshape(bp, tp, -1)
    y_s5, xre_p, xim_p = _s5_sequence(pm3, bw2, cw2, a_slot_re, a_slot_im, d_skip)
    o_r, hs_p = _rwkv_sequence(pm3, plz.reshape(bp, tp, -1), col, mu_parts, *rw_params)
    y_prompt = _out_proj(y_s5, pm3, o_r, pg.reshape(bp, tp, -1), col, x_prompt, mod_p,
                         w_glu_bf, b_glu[l], w_out_bf, final_g, OUT_ROWS, True)
    g_s5 = A_re.shape[1]
    s5_shape = (1, bp, g_s5, P_S5)
    wkv_p = hs_p[None]

    y_s5s, xre_s, xim_s = _s5_step(ps, bw2, cw2, a_blk_re, a_blk_im, d_skip,
                                   state_s5_re[l].reshape(bs, -1), state_s5_im[l].reshape(bs, -1))
    o_rs, wkv_s = _rwkv_step(ps, pslz, col, mu_parts, *rw_params, state_wkv[l])
    y_sample = _out_proj(y_s5s[None], ps.reshape(2, bs, -1), o_rs[None], psg.reshape(2, bs, -1), col, xs, mod_s,
                         w_glu_bf, b_glu[l], w_out_bf, final_g, bs, False)
    y_sample = y_sample.reshape(bs, 1, d)

    return (y_prompt, y_sample,
            xre_p.reshape(s5_shape), xim_p.reshape(s5_shape), wkv_p, shift_p[None],
            xre_s.reshape(1, bs, g_s5, P_S5), xim_s.reshape(1, bs, g_s5, P_S5), wkv_s[None], h_s[None])
```

```python
import functools

import jax
import jax.numpy as jnp
from jax import lax
from jax.experimental import pallas as pl
from jax.experimental.pallas import tpu as pltpu

F32 = jnp.float32
BF16 = jnp.bfloat16

RMS_EPS = 1e-6
GN_EPS = 64e-5
S5_GROUP = 16
P_S5 = 64
RW_HEAD = 64
LORA = 64
LANES = 128
GROUPS_PER_BLOCK = LANES // S5_GROUP
STATES_PER_BLOCK = GROUPS_PER_BLOCK * P_S5
S5_TC = 512
RW_CHUNK = 64
INV_BASE = 8
RW_SEQS = 4
RW_SEQ_LAG = 1

V7X_VMEM_BYTES = 64 * 1024 * 1024
VMEM_LIMIT_BYTES = V7X_VMEM_BYTES // 8 * 7
MOD_TN = 1024
NORM_ROWS = 1024
IN_PROJ_ROWS = 1024
IN_PROJ_COL_TILES = 4
OUT_ROWS = 512

_NT = (((1,), (1,)), ((), ()))
_TN = (((0,), (0,)), ((), ()))


def _dot(a, b):
    return jnp.dot(a.astype(BF16), b.astype(BF16), preferred_element_type=F32)


def _sigmoid(x):
    return 0.5 * jnp.tanh(0.5 * x) + 0.5


def _silu(x):
    return x * _sigmoid(x)


def _params(*sem):
    return pltpu.CompilerParams(dimension_semantics=sem, vmem_limit_bytes=VMEM_LIMIT_BYTES)


def _mod_kernel(c1_ref, c2_ref, w_ref, b_ref, o1_ref, o2_ref):
    w = w_ref[...].astype(BF16)
    for c_ref, o_ref in ((c1_ref, o1_ref), (c2_ref, o2_ref)):
        o_ref[...] = _dot(_silu(c_ref[...]), w) + b_ref[...]


def _mod(c1, c2, w, b):
    d, n = w.shape
    rows = lambda c: pl.BlockSpec((c.shape[0], d), lambda j: (0, 0))
    out = lambda c: pl.BlockSpec((c.shape[0], MOD_TN), lambda j: (0, j))
    return pl.pallas_call(
        _mod_kernel,
        grid=(n // MOD_TN,),
        in_specs=[rows(c1), rows(c2),
                  pl.BlockSpec((d, MOD_TN), lambda j: (0, j)),
                  pl.BlockSpec((1, MOD_TN), lambda j: (0, j))],
        out_specs=[out(c1), out(c2)],
        out_shape=(jax.ShapeDtypeStruct((c1.shape[0], n), F32), jax.ShapeDtypeStruct((c2.shape[0], n), F32)),
        compiler_params=_params("parallel"),
        name="adaln_mod",
    )(c1, c2, w, b.reshape(1, n))


def _h_kernel(x_ref, g_ref, sh_ref, sc_ref, h_ref):
    x = x_ref[...]
    y = x * lax.rsqrt(jnp.mean(x * x, axis=-1, keepdims=True) + RMS_EPS) * g_ref[...]
    h_ref[...] = (y * (1.0 + sc_ref[...]) + sh_ref[...]).astype(h_ref.dtype)


def _modulated_norm(x, g, mod, out_dtype, tt):
    b, t, d = x.shape
    tt = min(tt, t)
    tm = 1 if mod.shape[1] == 1 else tt
    mod_map = (lambda i, j: (i, 0, 0)) if tm == 1 else (lambda i, j: (i, j, 0))
    mod_map1 = (lambda i, j: (i, 0, 1)) if tm == 1 else (lambda i, j: (i, j, 1))
    return pl.pallas_call(
        _h_kernel,
        grid=(b, t // tt),
        in_specs=[pl.BlockSpec((None, tt, d), lambda i, j: (i, j, 0)),
                  pl.BlockSpec((1, d), lambda i, j: (0, 0)),
                  pl.BlockSpec((None, tm, d), mod_map),
                  pl.BlockSpec((None, tm, d), mod_map1)],
        out_specs=pl.BlockSpec((None, tt, d), lambda i, j: (i, j, 0)),
        out_shape=jax.ShapeDtypeStruct((b, t, d), out_dtype),
        compiler_params=_params("parallel", "parallel"),
        name="modulated_norm",
    )(x, g.reshape(1, d), mod, mod)


def _in_proj_kernel(a_ref, a2_ref, w_ref, o_ref, o2_ref, wbf_ref, *, gate):
    i = pl.program_id(1)

    def project(x_ref, out_ref):
        p = jnp.dot(x_ref[...], wbf_ref[...], preferred_element_type=F32)
        out_ref[...] = (_sigmoid(p) if gate else p).astype(out_ref.dtype)

    @pl.when(i == 0)
    def _():
        wbf_ref[...] = w_ref[...].astype(BF16)
        project(a2_ref, o2_ref)

    @pl.when(i > 0)
    def _():
        project(a_ref, o_ref)


def _in_proj(a, a2, w, col0, width, tn, tm, gate=False, out_dtype=F32):
    m, k = a.shape
    m2 = a2.shape[0]
    tm = min(tm, m)
    assert width % tn == 0 and m % tm == 0 and col0 % LANES == 0
    main = lambda i: jnp.maximum(i - 1, 0)
    dt = BF16 if gate else out_dtype
    return pl.pallas_call(
        functools.partial(_in_proj_kernel, gate=gate),
        grid=(width // tn, m // tm + 1),
        in_specs=[pl.BlockSpec((tm, k), lambda j, i: (main(i), 0)),
                  pl.BlockSpec((m2, k), lambda j, i: (0, 0)),
                  pl.BlockSpec((pl.Element(k), pl.Element(tn)), lambda j, i: (0, pl.multiple_of(col0 + j * tn, LANES)))],
        out_specs=[pl.BlockSpec((tm, tn), lambda j, i: (main(i), j)),
                   pl.BlockSpec((m2, tn), lambda j, i: (0, j))],
        out_shape=(jax.ShapeDtypeStruct((m, width), dt), jax.ShapeDtypeStruct((m2, width), dt)),
        scratch_shapes=[pltpu.VMEM((k, tn), BF16)],
        compiler_params=_params("parallel", "arbitrary"),
        name="in_proj",
    )(a, a2, w)


def _s5_weights_kernel(*refs, n_b):
    (are0, are1, aim0, aim1, ls0, ls1, bre0, bre1, bim0, bim1, cre0, cre1, cim0, cim1,
     bw_ref, cw_ref, abr_ref, abi_ref, apr_ref, api_ref) = refs
    n_gl = GROUPS_PER_BLOCK
    n_lt = STATES_PER_BLOCK // LANES

    def discretise(are_ref, aim_ref, ls_ref, bre_ref, bim_ref):
        step = jnp.exp(ls_ref[...])
        lam_re = jnp.minimum(are_ref[...], -1e-4)
        lam_im = aim_ref[...]
        mag = jnp.exp(lam_re * step)
        ab_re = mag * jnp.cos(lam_im * step)
        ab_im = mag * jnp.sin(lam_im * step)
        den = lam_re * lam_re + lam_im * lam_im
        f_re = ((ab_re - 1.0) * lam_re + ab_im * lam_im) / den
        f_im = (ab_im * lam_re - (ab_re - 1.0) * lam_im) / den
        br, bi = bre_ref[...], bim_ref[...]
        return ab_re, ab_im, f_re * br - f_im * bi, f_re * bi + f_im * br

    halves = [discretise(are0, aim0, ls0, bre0, bim0), discretise(are1, aim1, ls1, bre1, bim1)]
    zero = jnp.zeros((S5_GROUP, P_S5), F32)

    def band(t_re, t_im, gl):
        pieces = []
        for lt in range(n_lt):
            for tile in (t_re, t_im):
                for half in range(2):
                    pieces.append(tile if 2 * lt + half == gl else zero)
        return jnp.concatenate(pieces, axis=1)

    def block_rows(tiles_re, tiles_im):
        return jnp.concatenate([band(tiles_re[h][gl], tiles_im[h][gl], gl)
                                for h in range(2) for gl in range(n_gl)], axis=0)

    bw_ref[...] = block_rows([h[2] for h in halves], [h[3] for h in halves]).astype(BF16)
    cw_ref[...] = block_rows([cre0[...], cre1[...]], [-cim0[...], -cim1[...]]).T.astype(BF16)

    for h in range(2):
        for a_ref, a in ((abr_ref, halves[h][0]), (abi_ref, halves[h][1])):
            a_ref[h] = jnp.concatenate([a[gl] for gl in range(n_gl)], axis=1)
    for p_ref, a_ref in ((apr_ref, abr_ref), (api_ref, abi_ref)):
        p_ref[...] = jnp.concatenate([a_ref[0], a_ref[1]] * n_b, axis=0)


def _s5_weights(a_re, a_im, log_step, b_re, b_im, c_re, c_im, n_b):
    g, p = a_re.shape
    c = b_re.shape[-1]
    nq = g // GROUPS_PER_BLOCK // 2
    ns = STATES_PER_BLOCK
    args, specs = [], []
    for x, shp in ((a_re.reshape(g, 1, p), (1, p)), (a_im.reshape(g, 1, p), (1, p)),
                   (log_step.reshape(g, 1, 1), (1, 1)),
                   (jnp.swapaxes(b_re, 1, 2), (c, p)), (jnp.swapaxes(b_im, 1, 2), (c, p)),
                   (c_re, (c, p)), (c_im, (c, p))):
        for half in range(2):
            args.append(x)
            specs.append(pl.BlockSpec((GROUPS_PER_BLOCK,) + shp, lambda q, half=half: (q + half * nq, 0, 0)))
    return pl.pallas_call(
        functools.partial(_s5_weights_kernel, n_b=n_b),
        grid=(nq,),
        in_specs=specs,
        out_specs=[pl.BlockSpec((None, 2 * LANES, 2 * ns), lambda q: (q, 0, 0)),
                   pl.BlockSpec((None, 2 * ns, 2 * LANES), lambda q: (q, 0, 0)),
                   pl.BlockSpec((None, 2, 1, ns), lambda q: (q, 0, 0, 0)),
                   pl.BlockSpec((None, 2, 1, ns), lambda q: (q, 0, 0, 0)),
                   pl.BlockSpec((None, 2 * n_b, ns), lambda q: (q, 0, 0)),
                   pl.BlockSpec((None, 2 * n_b, ns), lambda q: (q, 0, 0))],
        out_shape=(jax.ShapeDtypeStruct((nq, 2 * LANES, 2 * ns), BF16),
                   jax.ShapeDtypeStruct((nq, 2 * ns, 2 * LANES), BF16),
                   jax.ShapeDtypeStruct((nq, 2, 1, ns), F32), jax.ShapeDtypeStruct((nq, 2, 1, ns), F32),
                   jax.ShapeDtypeStruct((nq, 2 * n_b, ns), F32), jax.ShapeDtypeStruct((nq, 2 * n_b, ns), F32)),
        compiler_params=_params("parallel"),
        name="s5_weights",
    )(*args)


def _s5_seq_kernel(u0_ref, u1_ref, bw_ref, cw_ref, are_ref, aim_ref, d_ref,
                   y_ref, xre_ref, xim_ref, lhs_ref, bu_ref, y_scr, x_scr, *, tc, n_b):
    n_lt = STATES_PER_BLOCK // LANES
    slots = 2 * n_b
    re_l = lambda lt: slice(2 * lt * LANES, (2 * lt + 1) * LANES)
    im_l = lambda lt: slice((2 * lt + 1) * LANES, (2 * lt + 2) * LANES)
    nat = lambda lt: slice(lt * LANES, (lt + 1) * LANES)
    c = pl.program_id(1)

    @pl.when(c == 0)
    def _():
        x_scr[...] = jnp.zeros_like(x_scr)
        lhs_ref[...] = jnp.zeros_like(lhs_ref)

    for b in range(n_b):
        lhs_ref.at[0][pl.ds(2 * b, tc, stride=slots), :] = u0_ref[b].astype(F32)
        lhs_ref.at[1][pl.ds(2 * b + 1, tc, stride=slots), :] = u1_ref[b].astype(F32)
    u_rows = lhs_ref[0] + lhs_ref[1]
    lhs = jnp.concatenate([lhs_ref[0], lhs_ref[1]], axis=1)
    bu_ref[...] = _dot(lhs, bw_ref[...])

    a_re = [are_ref[:, nat(lt)] for lt in range(n_lt)]
    a_im = [aim_ref[:, nat(lt)] for lt in range(n_lt)]
    x = [(x_scr[:, re_l(lt)], x_scr[:, im_l(lt)]) for lt in range(n_lt)]
    for t in range(tc):
        rows = slice(t * slots, (t + 1) * slots)
        for lt in range(n_lt):
            xr, xi = x[lt]
            nr = a_re[lt] * xr - a_im[lt] * xi + bu_ref[rows, re_l(lt)]
            ni = a_re[lt] * xi + a_im[lt] * xr + bu_ref[rows, im_l(lt)]
            bu_ref[rows, re_l(lt)] = nr
            bu_ref[rows, im_l(lt)] = ni
            x[lt] = (nr, ni)
    for lt in range(n_lt):
        x_scr[:, re_l(lt)] = x[lt][0]
        x_scr[:, im_l(lt)] = x[lt][1]

    yf = _dot(bu_ref[...], cw_ref[...])
    first_half = (lax.broadcasted_iota(jnp.int32, (tc * slots, LANES), 0) & 1) == 0
    y = jnp.where(first_half, yf[:, :LANES], yf[:, LANES:])
    skip = (u_rows.reshape(tc, slots, LANES) * d_ref[...][None]).reshape(tc * slots, LANES)
    y_scr[...] = y + skip
    for b in range(n_b):
        y_ref[b, :, :LANES] = y_scr[pl.ds(2 * b, tc, stride=slots), :].astype(y_ref.dtype)
        y_ref[b, :, LANES:] = y_scr[pl.ds(2 * b + 1, tc, stride=slots), :].astype(y_ref.dtype)

    @pl.when(c == pl.num_programs(1) - 1)
    def _():
        for lt in range(n_lt):
            xre_ref[:, nat(lt)] = x[lt][0]
            xim_ref[:, nat(lt)] = x[lt][1]


def _s5_sequence(p_main, bw2, cw2, a_slot_re, a_slot_im, d_skip):
    b, t, _ = p_main.shape
    nq = bw2.shape[0]
    nb = 2 * nq
    ns = STATES_PER_BLOCK
    slots = 2 * b
    assert slots == 8, "one 8-row tile must hold every (sequence, half) slot"
    tc = min(S5_TC, t)
    d_blk = d_skip.reshape(nb, LANES)
    d_slot = jnp.tile(jnp.stack([d_blk[:nq], d_blk[nq:]], axis=1), (1, b, 1))
    slot_vec = lambda n: pl.BlockSpec((None, slots, n), lambda q, c: (q, 0, 0))
    y, xre, xim = pl.pallas_call(
        functools.partial(_s5_seq_kernel, tc=tc, n_b=b),
        grid=(nq, t // tc),
        in_specs=[pl.BlockSpec((b, tc, LANES), lambda q, c: (0, c, q)),
                  pl.BlockSpec((b, tc, LANES), lambda q, c: (0, c, q + nq)),
                  pl.BlockSpec((None, 2 * LANES, 2 * ns), lambda q, c: (q, 0, 0)),
                  pl.BlockSpec((None, 2 * ns, 2 * LANES), lambda q, c: (q, 0, 0)),
                  slot_vec(ns), slot_vec(ns), slot_vec(LANES)],
        out_specs=[pl.BlockSpec((b, tc, 2 * LANES), lambda q, c: (0, c, q)),
                   slot_vec(ns), slot_vec(ns)],
        out_shape=(jax.ShapeDtypeStruct((b, t, nb * LANES), BF16),
                   jax.ShapeDtypeStruct((nq, slots, ns), F32),
                   jax.ShapeDtypeStruct((nq, slots, ns), F32)),
        scratch_shapes=[pltpu.VMEM((2, tc * slots, LANES), F32),
                        pltpu.VMEM((tc * slots, 2 * ns), F32),
                        pltpu.VMEM((tc * slots, LANES), F32),
                        pltpu.VMEM((slots, 2 * ns), F32)],
        compiler_params=_params("parallel", "arbitrary"),
        name="s5_sequence",
    )(p_main, p_main, bw2, cw2, a_slot_re, a_slot_im, d_slot)

    def state(x):
        x = x.reshape(nq, b, 2, ns)
        return jnp.transpose(x, (1, 2, 0, 3)).reshape(b, nb * ns)

    return y, state(xre), state(xim)


def _s5_step_kernel(u_ref, bw_ref, cw_ref, abr_ref, abi_ref, d_ref, x0r_ref, x0i_ref,
                    y_ref, x1r_ref, x1i_ref):
    n_lt = STATES_PER_BLOCK // LANES
    u = u_ref[...].astype(F32)
    bu = _dot(u, bw_ref[...])
    tiles = []
    for lt in range(n_lt):
        nat = slice(lt * LANES, (lt + 1) * LANES)
        ar, ai = abr_ref[:, nat], abi_ref[:, nat]
        x0r, x0i = x0r_ref[:, nat], x0i_ref[:, nat]
        xr = ar * x0r - ai * x0i + bu[:, 2 * lt * LANES:(2 * lt + 1) * LANES]
        xi = ar * x0i + ai * x0r + bu[:, (2 * lt + 1) * LANES:(2 * lt + 2) * LANES]
        x1r_ref[:, nat] = xr
        x1i_ref[:, nat] = xi
        tiles += [xr, xi]
    y_ref[...] = _dot(jnp.concatenate(tiles, axis=1), cw_ref[...]) + d_ref[...] * u


def _s5_step(p_rows, bw2, cw2, a_blk_re, a_blk_im, d_skip, x0_re, x0_im):
    rows = x0_re.shape[0]
    nq = bw2.shape[0]
    nb = 2 * nq
    ns = STATES_PER_BLOCK
    vec = lambda: pl.BlockSpec((None, None, 1, ns), lambda j: (j % nq, j // nq, 0, 0))
    st = lambda: pl.BlockSpec((rows, ns), lambda j: (0, j))
    return pl.pallas_call(
        _s5_step_kernel,
        grid=(nb,),
        in_specs=[pl.BlockSpec((rows, LANES), lambda j: (0, j)),
                  pl.BlockSpec((None, LANES, 2 * ns), lambda j: (j % nq, j // nq, 0)),
                  pl.BlockSpec((None, 2 * ns, LANES), lambda j: (j % nq, 0, j // nq)),
                  vec(), vec(),
                  pl.BlockSpec((1, LANES), lambda j: (0, j)),
                  st(), st()],
        out_specs=[pl.BlockSpec((rows, LANES), lambda j: (0, j)), st(), st()],
        out_shape=(jax.ShapeDtypeStruct((rows, nb * LANES), F32),
                   jax.ShapeDtypeStruct((rows, nb * ns), F32),
                   jax.ShapeDtypeStruct((rows, nb * ns), F32)),
        compiler_params=_params("parallel"),
        name="s5_step",
    )(p_rows, bw2, cw2, a_blk_re, a_blk_im, d_skip, x0_re, x0_im)


def _softplus(x):
    return jnp.maximum(x, 0.0) + jnp.log(1.0 + jnp.exp(-jnp.abs(x)))


def _rwkv_token_terms(r, k, lo, w0, w2p, a0, a2p, k_a):
    w = -_softplus(-(w0 + _dot(jnp.tanh(lo), w2p))) - 0.5
    logd = -jnp.exp(w)
    a = _sigmoid(a0 + _dot(lo, a2p))
    k2 = k * (1.0 + (a - 1.0) * k_a)
    return logd, a, k2


def _rwkv_chunk_kernel(r_ref, k_ref, v_ref, lo_ref, z_ref,
                       mur_ref, muk_ref, muv_ref, mulo_ref,
                       w0_ref, w2p_ref, a0_ref, a2p_ref, kk_ref, ka_ref, rk_ref, gw_ref, gb_ref,
                       o_ref, hs_ref,
                       h_scr, pr_scr, pk_scr, pv_scr, plo_scr, *, n_heads, n_seqs):
    L = RW_CHUNK
    W = 2 * RW_HEAD
    n_pairs = n_heads // 2
    c = pl.program_id(1)

    @pl.when(c == 0)
    def _():
        h_scr[...] = jnp.zeros_like(h_scr)
        pr_scr[...] = jnp.zeros_like(pr_scr)
        pk_scr[...] = jnp.zeros_like(pk_scr)
        pv_scr[...] = jnp.zeros_like(pv_scr)
        plo_scr[...] = jnp.zeros_like(plo_scr)

    row1 = lax.broadcasted_iota(jnp.int32, (L, 1), 0)

    def token_shift(cur, prev_scr, mu_ref):
        prev = jnp.where(row1 == 0, prev_scr[...], pltpu.roll(cur, 1, 0))
        prev_scr[...] = cur[L - 1:L, :]
        return cur + (prev - cur) * mu_ref[...]

    tri = (lax.broadcasted_iota(jnp.int32, (L, L), 0)
           >= lax.broadcasted_iota(jnp.int32, (L, L), 1)).astype(BF16)
    first_head = lax.broadcasted_iota(jnp.int32, (L, W), 1) < RW_HEAD

    def head_sum(x):
        s0 = jnp.sum(jnp.where(first_head, x, 0.0), axis=-1, keepdims=True)
        s1 = jnp.sum(jnp.where(first_head, 0.0, x), axis=-1, keepdims=True)
        return jnp.where(first_head, s0, s1)

    def stack_heads(x):
        return jnp.concatenate([jnp.where(first_head, x, 0.0), jnp.where(first_head, 0.0, x)], axis=0)

    ri = lax.broadcasted_iota(jnp.int32, (2 * L, 2 * L), 0)
    ci = lax.broadcasted_iota(jnp.int32, (2 * L, 2 * L), 1)
    t_row, t_col = ri & (L - 1), ci & (L - 1)
    strict = t_row > t_col
    incl = t_row >= t_col
    eye = (ri == ci).astype(F32)
    blk_masks = []
    s = INV_BASE
    while s <= L:
        blk_masks.append((ri // s) == (ci // s))
        s *= 2

    pairs = range(n_pairs)
    loc = [slice(p * W, (p + 1) * W) for p in pairs]

    def seq_phases(si):
        r = token_shift(r_ref[si].astype(F32), pr_scr.at[si], mur_ref)
        k = token_shift(k_ref[si].astype(F32), pk_scr.at[si], muk_ref)
        v = token_shift(v_ref[si].astype(F32), pv_scr.at[si], muv_ref)
        lo = token_shift(lo_ref[si].astype(F32), plo_scr.at[si], mulo_ref)
        logd, a, k2 = _rwkv_token_terms(r, k, lo, w0_ref[...], w2p_ref[...], a0_ref[...], a2p_ref[...],
                                        ka_ref[...])
        kk = k * kk_ref[...]
        hi = logd.astype(BF16)
        rem = logd - hi.astype(F32)
        mid = rem.astype(BF16)
        low = (rem - mid.astype(F32)).astype(BF16)
        lp = (jnp.dot(tri, hi, preferred_element_type=F32)
              + jnp.dot(tri, mid, preferred_element_type=F32)
              + jnp.dot(tri, low, preferred_element_type=F32))
        p_inc = jnp.exp(lp)
        p_exc = jnp.exp(lp - logd)
        p_inv = jnp.exp(-lp)
        yield

        lhs, nm, mm, ab, vs, kb, p_end = [], [], [], [], [], [], []
        for sl in loc:
            kk_p = kk[:, sl]
            kkn = kk_p * lax.rsqrt(jnp.maximum(head_sum(kk_p * kk_p), 1e-24))
            pinc, pinv = p_inc[:, sl], p_inv[:, sl]
            kd = k2[:, sl] * pinv
            bd = kkn * a[:, sl] * pinv
            pe = pinc[L - 1:L, :]
            lhs_p = jnp.concatenate([stack_heads(kkn * p_exc[:, sl]), stack_heads(r[:, sl] * pinc)],
                                    axis=0).astype(BF16)
            rhs_p = jnp.concatenate([stack_heads(bd), stack_heads(kd)], axis=0).astype(BF16)
            amat = lax.dot_general(lhs_p, rhs_p, _NT, preferred_element_type=F32)
            lhs.append(lhs_p)
            nm.append(jnp.where(strict, amat[:2 * L, :2 * L], 0.0))
            mm.append(jnp.where(strict, amat[:2 * L, 2 * L:], 0.0).astype(BF16))
            ab.append(jnp.concatenate([jnp.where(incl, amat[2 * L:, 2 * L:], 0.0),
                                       -jnp.where(incl, amat[2 * L:, :2 * L], 0.0)], axis=1).astype(BF16))
            vs.append(stack_heads(v[:, sl]).astype(BF16))
            kb.append(jnp.concatenate([stack_heads(kd * pe), stack_heads(bd * pe)], axis=0).astype(BF16))
            p_end.append(pe)
            yield

        d = [jnp.where(blk_masks[0], n_p, 0.0).astype(BF16) for n_p in nm]
        x = [eye - d_p.astype(F32) for d_p in d]
        pw = [jnp.dot(d_p, d_p, preferred_element_type=F32) for d_p in d]
        yield
        s = 2
        while s < INV_BASE:
            x = [x_p + _dot(x_p, pw_p) for x_p, pw_p in zip(x, pw)]
            s *= 2
            if s < INV_BASE:
                pw = [_dot(pw_p, pw_p) for pw_p in pw]
            yield
        for lvl in range(1, len(blk_masks)):
            off = blk_masks[lvl] & ~blk_masks[lvl - 1]
            xc = [_dot(x_p, jnp.where(off, n_p, 0.0)) for x_p, n_p in zip(x, nm)]
            yield
            x = [x_p - _dot(xc_p, x_p) for x_p, xc_p in zip(x, xc)]
            yield

        hs = [h_scr[si, p] for p in pairs]
        lh = [jnp.dot(lhs_p, hs_p.astype(BF16), preferred_element_type=F32) for lhs_p, hs_p in zip(lhs, hs)]
        mv = [jnp.dot(mm_p, vs_p, preferred_element_type=F32) for mm_p, vs_p in zip(mm, vs)]
        yield
        u = [_dot(x_p, lh_p[:2 * L] + mv_p).astype(BF16) for x_p, lh_p, mv_p in zip(x, lh, mv)]
        yield
        o_st = [lh_p[2 * L:] + jnp.dot(ab_p, jnp.concatenate([vs_p, u_p], axis=0), preferred_element_type=F32)
                for lh_p, ab_p, vs_p, u_p in zip(lh, ab, vs, u)]
        for p in pairs:
            p_end_col = jnp.sum(eye * p_end[p], axis=1, keepdims=True)
            h_scr[si, p] = p_end_col * hs[p] + lax.dot_general(
                kb[p], jnp.concatenate([vs[p], -u[p]], axis=0), _TN, preferred_element_type=F32)
        yield

        for p, sl in enumerate(loc):
            o = o_st[p][:L] + o_st[p][L:]
            mu = head_sum(o) * (1.0 / RW_HEAD)
            var = head_sum((o - mu) ** 2) * (1.0 / RW_HEAD)
            o = (o - mu) * lax.rsqrt(var + GN_EPS) * gw_ref[:, sl] + gb_ref[:, sl]
            o = o + head_sum(r[:, sl] * k2[:, sl] * rk_ref[:, sl]) * v[:, sl]
            z = z_ref[si, :, LANES + p * W:LANES + (p + 1) * W].astype(F32)
            o_ref[si, :, sl] = (o * _silu(z)).astype(o_ref.dtype)
        yield

    gens = [seq_phases(si) for si in range(n_seqs)]
    started = 0
    live = []
    step = 0
    while started < n_seqs or live:
        if started < n_seqs and step % RW_SEQ_LAG == 0:
            live.append(gens[started])
            started += 1
        for g in list(live):
            if next(g, "done") == "done":
                live.remove(g)
        step += 1

    @pl.when(c == pl.num_programs(1) - 1)
    def _():
        for si in range(n_seqs):
            for p in pairs:
                st = h_scr[si, p].T
                hs_ref[si, 2 * p] = st[:RW_HEAD, :RW_HEAD]
                hs_ref[si, 2 * p + 1] = st[RW_HEAD:, RW_HEAD:]


def _rwkv_sequence(p_main, p_lz, col, mu, w0, w2p, a0, a2p, k_k, k_a, r_k, gn_w, gn_b):
    b, t, _ = p_main.shape
    wlz = p_lz.shape[-1]
    d = w0.shape[-1]
    n_heads = d // RW_HEAD
    n_pairs, pw = n_heads // 2, 2 * RW_HEAD
    L = RW_CHUNK
    ns = RW_SEQS if b % RW_SEQS == 0 else 1
    blk = lambda cb: pl.BlockSpec((ns, L, d), lambda i, j, cb=cb: (i, j, cb))
    vec = lambda n: pl.BlockSpec((1, n), lambda i, j: (0, 0))
    full = lambda shp: pl.BlockSpec(shp, lambda i, j: (0,) * len(shp))
    mu_r, mu_k, mu_v, mu_lo = mu
    o, hs = pl.pallas_call(
        functools.partial(_rwkv_chunk_kernel, n_heads=n_heads, n_seqs=ns),
        grid=(b // ns, t // L),
        in_specs=[blk(col["r"]), blk(col["k"]), blk(col["v"]),
                  pl.BlockSpec((ns, L, LANES), lambda i, j: (i, j, 0)),
                  pl.BlockSpec((ns, L, wlz), lambda i, j: (i, j, 0)),
                  vec(d), vec(d), vec(d), vec(LANES),
                  vec(d), full((LANES, d)), vec(d), full((LANES, d)),
                  vec(d), vec(d), vec(d), vec(d), vec(d)],
        out_specs=[pl.BlockSpec((ns, L, d), lambda i, j: (i, j, 0)),
                   pl.BlockSpec((ns, n_heads, RW_HEAD, RW_HEAD), lambda i, j: (i, 0, 0, 0))],
        out_shape=(jax.ShapeDtypeStruct((b, t, d), BF16),
                   jax.ShapeDtypeStruct((b, n_heads, RW_HEAD, RW_HEAD), F32)),
        scratch_shapes=[pltpu.VMEM((ns, n_pairs, pw, pw), F32),
                        pltpu.VMEM((ns, 1, d), F32), pltpu.VMEM((ns, 1, d), F32), pltpu.VMEM((ns, 1, d), F32),
                        pltpu.VMEM((ns, 1, LANES), F32)],
        compiler_params=_params("parallel", "arbitrary"),
        name="rwkv_sequence",
    )(p_main, p_main, p_main, p_lz, p_lz,
      mu_r, mu_k, mu_v, mu_lo, w0, w2p, a0, a2p, k_k, k_a, r_k, gn_w, gn_b)
    return o, hs


def _rwkv_step_prep_kernel(cr_ref, ck_ref, cv_ref, clo_ref, cz_ref, pr_ref, pk_ref, pv_ref, plo_ref,
                           mur_ref, muk_ref, muv_ref, mulo_ref,
                           w0_ref, w2p_ref, a0_ref, a2p_ref, kk_ref, ka_ref,
                           r_o, k2_o, v_o, kk_o, a_o, d_o, z_o):
    def lerp(c_ref, p_ref, mu_ref):
        cur = c_ref[...].astype(F32)
        return cur + (p_ref[...].astype(F32) - cur) * mu_ref[...]

    r = lerp(cr_ref, pr_ref, mur_ref)
    k = lerp(ck_ref, pk_ref, muk_ref)
    v = lerp(cv_ref, pv_ref, muv_ref)
    lo = lerp(clo_ref, plo_ref, mulo_ref)
    logd, a, k2 = _rwkv_token_terms(r, k, lo, w0_ref[...], w2p_ref[...], a0_ref[...], a2p_ref[...],
                                    ka_ref[...])
    r_o[...] = r.T
    k2_o[...] = k2.T
    v_o[...] = v.T
    kk_o[...] = (k * kk_ref[...]).T
    a_o[...] = a.T
    d_o[...] = jnp.exp(logd).T
    z_o[...] = cz_ref[:, LANES:].astype(F32).T


def _rwkv_step_kernel(s_ref, r_ref, k2_ref, v_ref, kk_ref, a_ref, d_ref, z_ref,
                      rk_ref, gw_ref, gb_ref, o_ref, s1_ref, o_scr):
    r, k2, v, dec = r_ref[...], k2_ref[...], v_ref[...], d_ref[...]
    kk = kk_ref[...]
    kkn = kk / jnp.maximum(jnp.sqrt(jnp.sum(kk * kk, axis=0, keepdims=True)), 1e-12)
    bvec = kkn * a_ref[...]
    for i in range(RW_HEAD):
        s = s_ref[i]
        sa = jnp.sum(s * kkn, axis=0, keepdims=True)
        s1 = s * dec - sa * bvec + v[i:i + 1, :] * k2
        s1_ref[i] = s1
        o_scr[pl.ds(i, 1), :] = jnp.sum(s1 * r, axis=0, keepdims=True)
    o = o_scr[...]
    mu = jnp.mean(o, axis=0, keepdims=True)
    var = jnp.mean((o - mu) ** 2, axis=0, keepdims=True)
    o = (o - mu) * lax.rsqrt(var + GN_EPS) * gw_ref[...] + gb_ref[...]
    o = o + jnp.sum(r * k2 * rk_ref[...], axis=0, keepdims=True) * v
    o_ref[...] = o * _silu(z_ref[...])


def _rwkv_step(p_rows, p_lz_rows, col, mu, w0, w2p, a0, a2p, k_k, k_a, r_k, gn_w, gn_b, s0):
    rows = s0.shape[0]
    wlz = p_lz_rows.shape[-1]
    d = w0.shape[-1]
    n_heads = d // RW_HEAD
    cur = lambda cb: pl.BlockSpec((rows, d), lambda i, cb=cb: (0, cb))
    prv = lambda cb: pl.BlockSpec((rows, d), lambda i, cb=cb: (1, cb))
    vec = lambda n: pl.BlockSpec((1, n), lambda i: (0, 0))
    mu_r, mu_k, mu_v, mu_lo = mu
    out = jax.ShapeDtypeStruct((d, rows), F32)
    terms = pl.pallas_call(
        _rwkv_step_prep_kernel,
        grid=(1,),
        in_specs=[cur(col["r"]), cur(col["k"]), cur(col["v"]),
                  pl.BlockSpec((rows, LANES), lambda i: (0, 0)),
                  pl.BlockSpec((rows, wlz), lambda i: (0, 0)),
                  prv(col["r"]), prv(col["k"]), prv(col["v"]),
                  pl.BlockSpec((rows, LANES), lambda i: (1, 0)),
                  vec(d), vec(d), vec(d), vec(LANES),
                  vec(d), pl.BlockSpec((LANES, d), lambda i: (0, 0)),
                  vec(d), pl.BlockSpec((LANES, d), lambda i: (0, 0)),
                  vec(d), vec(d)],
        out_specs=[pl.BlockSpec((d, rows), lambda i: (0, 0))] * 7,
        out_shape=(out,) * 7,
        compiler_params=_params("arbitrary"),
        name="rwkv_step_prep",
    )(p_rows, p_rows, p_rows, p_lz_rows, p_lz_rows, p_rows, p_rows, p_rows, p_lz_rows,
      mu_r, mu_k, mu_v, mu_lo, w0, w2p, a0, a2p, k_k, k_a)
    per_h = lambda: pl.BlockSpec((RW_HEAD, rows), lambda h: (h, 0))
    par = lambda: pl.BlockSpec((RW_HEAD, 1), lambda h: (h, 0))
    st = lambda: pl.BlockSpec((None, RW_HEAD, RW_HEAD, rows), lambda h: (h, 0, 0, 0))
    o_t, s1_t = pl.pallas_call(
        _rwkv_step_kernel,
        grid=(n_heads,),
        in_specs=[st()] + [per_h()] * 7 + [par()] * 3,
        out_specs=[per_h(), st()],
        out_shape=(jax.ShapeDtypeStruct((d, rows), F32),
                   jax.ShapeDtypeStruct((n_heads, RW_HEAD, RW_HEAD, rows), F32)),
        scratch_shapes=[pltpu.VMEM((RW_HEAD, rows), F32)],
        compiler_params=_params("parallel"),
        name="rwkv_step",
    )(jnp.transpose(s0, (1, 2, 3, 0)), *terms,
      r_k.reshape(d, 1), gn_w.reshape(d, 1), gn_b.reshape(d, 1))
    return o_t.T.astype(BF16), jnp.transpose(s1_t, (3, 0, 1, 2))


def _out_kernel(ys_ref, z_ref, wg_ref, bg_ref, or_ref, gs_ref, gr_ref, x_ref, gt_ref, w1_ref, w2_ref,
                fg_ref, y_ref, *, paired_blocks):
    ys = ys_ref[...].astype(F32)
    if paired_blocks:
        nb = ys.shape[1] // LANES
        pos = [2 * j if j < nb // 2 else 2 * (j - nb // 2) + 1 for j in range(nb)]
        ys = jnp.concatenate([ys[:, p * LANES:(p + 1) * LANES] for p in pos], axis=1)
    ys = jax.nn.gelu(ys, approximate=True)
    o_s = (ys * _sigmoid(_dot(ys, wg_ref[...]) + bg_ref[...]) * _silu(z_ref[...].astype(F32))).astype(BF16)
    mixed = (gs_ref[...].astype(F32) * jnp.dot(o_s, w1_ref[...], preferred_element_type=F32)
             + gr_ref[...].astype(F32) * jnp.dot(or_ref[...], w2_ref[...], preferred_element_type=F32))
    x = x_ref[...] + gt_ref[...] * mixed
    y_ref[...] = x * lax.rsqrt(jnp.mean(x * x, axis=-1, keepdims=True) + RMS_EPS) * fg_ref[...]


def _out_proj(y_s5, p_main, o_r, p_gate, col, x, mod, w_glu, b_glu, w_out, final_g, tt, paired_blocks):
    b, t, d = x.shape
    dh = o_r.shape[-1]
    tt = min(tt, t)
    tm = 1 if mod.shape[1] == 1 else tt
    gt_map = (lambda i, j: (i, 0, 2)) if tm == 1 else (lambda i, j: (i, j, 2))
    const = lambda shp, r=0: pl.BlockSpec(shp, lambda i, j: (r, 0), pipeline_mode=pl.Buffered(1))
    return pl.pallas_call(
        functools.partial(_out_kernel, paired_blocks=paired_blocks),
        grid=(b, t // tt),
        in_specs=[pl.BlockSpec((None, tt, dh), lambda i, j: (i, j, 0)),
                  pl.BlockSpec((None, tt, dh), lambda i, j: (i, j, col["z_s5"])),
                  const((dh, dh)), const((1, dh)),
                  pl.BlockSpec((None, tt, dh), lambda i, j: (i, j, 0)),
                  pl.BlockSpec((None, tt, d), lambda i, j: (i, j, col["g_s5"])),
                  pl.BlockSpec((None, tt, d), lambda i, j: (i, j, col["g_rw"])),
                  pl.BlockSpec((None, tt, d), lambda i, j: (i, j, 0)),
                  pl.BlockSpec((None, tm, d), gt_map),
                  const((dh, d), 0), const((dh, d), 1), const((1, d))],
        out_specs=pl.BlockSpec((None, tt, d), lambda i, j: (i, j, 0)),
        out_shape=jax.ShapeDtypeStruct((b, t, d), F32),
        compiler_params=_params("parallel", "parallel"),
        name="out_proj",
    )(y_s5, p_main, w_glu, b_glu.reshape(1, dh), o_r, p_gate, p_gate, x, mod, w_out, w_out,
      final_g.reshape(1, d))


def kernel(x_prompt, x_sample, c_prompt, c_sample, state_s5_re, state_s5_im, state_wkv, state_shift, norm_g, w_ada, b_ada, w_in, mu_rw, A_re, A_im, log_step, B_re, B_im, C_re, C_im, D_skip, w_glu, b_glu, w0, w2, a0, a2, k_k, k_a, r_k, gn_w, gn_b, w_out, final_g):
    depth = norm_g.shape[0]
    assert depth == 1
    bp, tp, d = x_prompt.shape
    bs = x_sample.shape[0]
    assert x_sample.shape[1] == 1
    dh = d // 2
    l = 0

    w_main, w_lz, w_gate = 5 * dh, 2 * LORA + dh, 2 * d
    col = {"u": 0, "z_s5": 1, "r": 2, "k": 3, "v": 4, "g_s5": 0, "g_rw": 1}
    w = w_in[l]

    def project(rows, rows2):
        return (_in_proj(rows, rows2, w, 0, w_main, w_main // IN_PROJ_COL_TILES, IN_PROJ_ROWS, out_dtype=BF16),
                _in_proj(rows, rows2, w, w_main, w_lz, w_lz, IN_PROJ_ROWS, out_dtype=BF16),
                _in_proj(rows, rows2, w, w_main + w_lz, w_gate, w_gate // IN_PROJ_COL_TILES, IN_PROJ_ROWS,
                         gate=True))

    mu = mu_rw[l]
    mu_parts = (mu[None, :dh], mu[None, dh:2 * dh], mu[None, 2 * dh:3 * dh], mu[None, 3 * dh:])
    zpad = jnp.zeros((LORA, dh), F32)
    w2p = jnp.concatenate([w2[l], zpad], axis=0).astype(BF16)
    a2p = jnp.concatenate([zpad, a2[l]], axis=0).astype(BF16)
    row = lambda x: x.reshape(1, -1)
    rw_params = (row(w0[l]), w2p, row(a0[l]), a2p, row(k_k[l]), row(k_a[l]), row(r_k[l]),
                 row(gn_w[l]), row(gn_b[l]))
    w_out_bf = w_out[l].astype(BF16)
    w_glu_bf = w_glu[l].astype(BF16)

    bw2, cw2, a_blk_re, a_blk_im, a_slot_re, a_slot_im = _s5_weights(
        A_re[l], A_im[l], log_step[l], B_re[l], B_im[l], C_re[l], C_im[l], bp)
    d_skip = D_skip[l].reshape(1, -1)

    mod_p, mod_s = _mod(c_prompt, c_sample, w_ada[l], b_ada[l])
    mod_p = mod_p.reshape(bp, 1, 3 * d)
    mod_s = mod_s.reshape(1, bs, 3 * d)

    h_p = _modulated_norm(x_prompt, norm_g[l], mod_p, BF16, NORM_ROWS)
    shift_p = _modulated_norm(x_prompt[:, tp - 1:, :], norm_g[l], mod_p, F32, 1)[:, 0]
    xs = x_sample.reshape(1, bs, d)
    h_s = _modulated_norm(xs, norm_g[l], mod_s, F32, bs)[0]
    a_s = jnp.concatenate([h_s, state_shift[l]], axis=0).astype(BF16)
    (pm, ps), (plz, pslz), (pg, psg) = project(h_p.reshape(bp * tp, d), a_s)

    pm3 = pm.reshape(bp, tp, -1)
    y_s5, xre_p, xim_p = _s5_sequence(pm3, bw2, cw2, a_slot_re, a_slot_im, d_skip)
    o_r, hs_p = _rwkv_sequence(pm3, plz.reshape(bp, tp, -1), col, mu_parts, *rw_params)
    y_prompt = _out_proj(y_s5, pm3, o_r, pg.reshape(bp, tp, -1), col, x_prompt, mod_p,
                         w_glu_bf, b_glu[l], w_out_bf, final_g, OUT_ROWS, True)
    g_s5 = A_re.shape[1]
    s5_shape = (1, bp, g_s5, P_S5)
    wkv_p = hs_p[None]

    y_s5s, xre_s, xim_s = _s5_step(ps, bw2, cw2, a_blk_re, a_blk_im, d_skip,
                                   state_s5_re[l].reshape(bs, -1), state_s5_im[l].reshape(bs, -1))
    o_rs, wkv_s = _rwkv_step(ps, pslz, col, mu_parts, *rw_params, state_wkv[l])
    y_sample = _out_proj(y_s5s[None], ps.reshape(2, bs, -1), o_rs[None], psg.reshape(2, bs, -1), col, xs, mod_s,
                         w_glu_bf, b_glu[l], w_out_bf, final_g, bs, False)
    y_sample = y_sample.reshape(bs, 1, d)

    return (y_prompt, y_sample,
            xre_p.reshape(s5_shape), xim_p.reshape(s5_shape), wkv_p, shift_p[None],
            xre_s.reshape(1, bs, g_s5, P_S5), xim_s.reshape(1, bs, g_s5, P_S5), wkv_s[None], h_s[None])
```

```python
import functools

import jax
import jax.numpy as jnp
from jax import lax
from jax.experimental import pallas as pl
from jax.experimental.pallas import tpu as pltpu

F32 = jnp.float32
BF16 = jnp.bfloat16

RMS_EPS = 1e-6
GN_EPS = 64e-5
S5_GROUP = 16
P_S5 = 64
RW_HEAD = 64
LORA = 64
LANES = 128
GROUPS_PER_BLOCK = LANES // S5_GROUP
STATES_PER_BLOCK = GROUPS_PER_BLOCK * P_S5
S5_TC = 512
S5_SUB = 8
RW_CHUNK = 64
INV_BASE = 16
RW_SEQS = 4
RW_SEQ_LAG = 1

V7X_VMEM_BYTES = 64 * 1024 * 1024
VMEM_LIMIT_BYTES = V7X_VMEM_BYTES // 8 * 7
MOD_TN = 1024
NORM_ROWS = 1024
IN_PROJ_ROWS = 1024
IN_PROJ_COL_TILES = 4
OUT_ROWS = 512

_NT = (((1,), (1,)), ((), ()))
_TN = (((0,), (0,)), ((), ()))


def _dot(a, b):
    return jnp.dot(a.astype(BF16), b.astype(BF16), preferred_element_type=F32)


def _sigmoid(x):
    return 0.5 * jnp.tanh(0.5 * x) + 0.5


def _silu(x):
    return x * _sigmoid(x)


def _params(*sem):
    return pltpu.CompilerParams(dimension_semantics=sem, vmem_limit_bytes=VMEM_LIMIT_BYTES)


def _mod_kernel(c1_ref, c2_ref, w_ref, b_ref, o1_ref, o2_ref):
    w = w_ref[...].astype(BF16)
    for c_ref, o_ref in ((c1_ref, o1_ref), (c2_ref, o2_ref)):
        o_ref[...] = _dot(_silu(c_ref[...]), w) + b_ref[...]


def _mod(c1, c2, w, b):
    d, n = w.shape
    rows = lambda c: pl.BlockSpec((c.shape[0], d), lambda j: (0, 0))
    out = lambda c: pl.BlockSpec((c.shape[0], MOD_TN), lambda j: (0, j))
    return pl.pallas_call(
        _mod_kernel,
        grid=(n // MOD_TN,),
        in_specs=[rows(c1), rows(c2),
                  pl.BlockSpec((d, MOD_TN), lambda j: (0, j)),
                  pl.BlockSpec((1, MOD_TN), lambda j: (0, j))],
        out_specs=[out(c1), out(c2)],
        out_shape=(jax.ShapeDtypeStruct((c1.shape[0], n), F32), jax.ShapeDtypeStruct((c2.shape[0], n), F32)),
        compiler_params=_params("parallel"),
        name="adaln_mod",
    )(c1, c2, w, b.reshape(1, n))


def _h_kernel(x_ref, g_ref, sh_ref, sc_ref, h_ref):
    x = x_ref[...]
    y = x * lax.rsqrt(jnp.mean(x * x, axis=-1, keepdims=True) + RMS_EPS) * g_ref[...]
    h_ref[...] = (y * (1.0 + sc_ref[...]) + sh_ref[...]).astype(h_ref.dtype)


def _modulated_norm(x, g, mod, out_dtype, tt):
    b, t, d = x.shape
    tt = min(tt, t)
    tm = 1 if mod.shape[1] == 1 else tt
    mod_map = (lambda i, j: (i, 0, 0)) if tm == 1 else (lambda i, j: (i, j, 0))
    mod_map1 = (lambda i, j: (i, 0, 1)) if tm == 1 else (lambda i, j: (i, j, 1))
    return pl.pallas_call(
        _h_kernel,
        grid=(b, t // tt),
        in_specs=[pl.BlockSpec((None, tt, d), lambda i, j: (i, j, 0)),
                  pl.BlockSpec((1, d), lambda i, j: (0, 0)),
                  pl.BlockSpec((None, tm, d), mod_map),
                  pl.BlockSpec((None, tm, d), mod_map1)],
        out_specs=pl.BlockSpec((None, tt, d), lambda i, j: (i, j, 0)),
        out_shape=jax.ShapeDtypeStruct((b, t, d), out_dtype),
        compiler_params=_params("parallel", "parallel"),
        name="modulated_norm",
    )(x, g.reshape(1, d), mod, mod)


def _in_proj_kernel(a_ref, a2_ref, w_ref, o_ref, o2_ref, wbf_ref, *, gate):
    i = pl.program_id(1)

    def project(x_ref, out_ref):
        p = jnp.dot(x_ref[...], wbf_ref[...], preferred_element_type=F32)
        out_ref[...] = (_sigmoid(p) if gate else p).astype(out_ref.dtype)

    @pl.when(i == 0)
    def _():
        wbf_ref[...] = w_ref[...].astype(BF16)
        project(a2_ref, o2_ref)

    @pl.when(i > 0)
    def _():
        project(a_ref, o_ref)


def _in_proj(a, a2, w, col0, width, tn, tm, gate=False):
    m, k = a.shape
    m2 = a2.shape[0]
    tm = min(tm, m)
    assert width % tn == 0 and m % tm == 0 and col0 % LANES == 0
    main = lambda i: jnp.maximum(i - 1, 0)
    dt = BF16 if gate else F32
    return pl.pallas_call(
        functools.partial(_in_proj_kernel, gate=gate),
        grid=(width // tn, m // tm + 1),
        in_specs=[pl.BlockSpec((tm, k), lambda j, i: (main(i), 0)),
                  pl.BlockSpec((m2, k), lambda j, i: (0, 0)),
                  pl.BlockSpec((pl.Element(k), pl.Element(tn)), lambda j, i: (0, pl.multiple_of(col0 + j * tn, LANES)))],
        out_specs=[pl.BlockSpec((tm, tn), lambda j, i: (main(i), j)),
                   pl.BlockSpec((m2, tn), lambda j, i: (0, j))],
        out_shape=(jax.ShapeDtypeStruct((m, width), dt), jax.ShapeDtypeStruct((m2, width), dt)),
        scratch_shapes=[pltpu.VMEM((k, tn), BF16)],
        compiler_params=_params("parallel", "arbitrary"),
        name="in_proj",
    )(a, a2, w)


def _s5_weights_kernel(*refs, n_b):
    (are0, are1, aim0, aim1, ls0, ls1, bre0, bre1, bim0, bim1, cre0, cre1, cim0, cim1,
     bw_ref, cw_ref, abr_ref, abi_ref, apr_ref, api_ref) = refs
    n_gl = GROUPS_PER_BLOCK
    n_lt = STATES_PER_BLOCK // LANES

    def discretise(are_ref, aim_ref, ls_ref, bre_ref, bim_ref):
        step = jnp.exp(ls_ref[...])
        lam_re = jnp.minimum(are_ref[...], -1e-4)
        lam_im = aim_ref[...]
        mag = jnp.exp(lam_re * step)
        ab_re = mag * jnp.cos(lam_im * step)
        ab_im = mag * jnp.sin(lam_im * step)
        den = lam_re * lam_re + lam_im * lam_im
        f_re = ((ab_re - 1.0) * lam_re + ab_im * lam_im) / den
        f_im = (ab_im * lam_re - (ab_re - 1.0) * lam_im) / den
        br, bi = bre_ref[...], bim_ref[...]
        return ab_re, ab_im, f_re * br - f_im * bi, f_re * bi + f_im * br

    halves = [discretise(are0, aim0, ls0, bre0, bim0), discretise(are1, aim1, ls1, bre1, bim1)]
    zero = jnp.zeros((S5_GROUP, P_S5), F32)

    def band(t_re, t_im, gl):
        pieces = []
        for lt in range(n_lt):
            for tile in (t_re, t_im):
                for half in range(2):
                    pieces.append(tile if 2 * lt + half == gl else zero)
        return jnp.concatenate(pieces, axis=1)

    def block_rows(tiles_re, tiles_im):
        return jnp.concatenate([band(tiles_re[h][gl], tiles_im[h][gl], gl)
                                for h in range(2) for gl in range(n_gl)], axis=0)

    bw_ref[...] = block_rows([h[2] for h in halves], [h[3] for h in halves]).astype(BF16)
    cw_ref[...] = block_rows([cre0[...], cre1[...]], [-cim0[...], -cim1[...]]).T.astype(BF16)

    for h in range(2):
        for a_ref, a in ((abr_ref, halves[h][0]), (abi_ref, halves[h][1])):
            a_ref[h] = jnp.concatenate([a[gl] for gl in range(n_gl)], axis=1)
    for p_ref, a_ref in ((apr_ref, abr_ref), (api_ref, abi_ref)):
        p_ref[...] = jnp.concatenate([a_ref[0], a_ref[1]] * n_b, axis=0)


def _s5_weights(a_re, a_im, log_step, b_re, b_im, c_re, c_im, n_b):
    g, p = a_re.shape
    c = b_re.shape[-1]
    nq = g // GROUPS_PER_BLOCK // 2
    ns = STATES_PER_BLOCK
    args, specs = [], []
    for x, shp in ((a_re.reshape(g, 1, p), (1, p)), (a_im.reshape(g, 1, p), (1, p)),
                   (log_step.reshape(g, 1, 1), (1, 1)),
                   (jnp.swapaxes(b_re, 1, 2), (c, p)), (jnp.swapaxes(b_im, 1, 2), (c, p)),
                   (c_re, (c, p)), (c_im, (c, p))):
        for half in range(2):
            args.append(x)
            specs.append(pl.BlockSpec((GROUPS_PER_BLOCK,) + shp, lambda q, half=half: (q + half * nq, 0, 0)))
    return pl.pallas_call(
        functools.partial(_s5_weights_kernel, n_b=n_b),
        grid=(nq,),
        in_specs=specs,
        out_specs=[pl.BlockSpec((None, 2 * LANES, 2 * ns), lambda q: (q, 0, 0)),
                   pl.BlockSpec((None, 2 * ns, 2 * LANES), lambda q: (q, 0, 0)),
                   pl.BlockSpec((None, 2, 1, ns), lambda q: (q, 0, 0, 0)),
                   pl.BlockSpec((None, 2, 1, ns), lambda q: (q, 0, 0, 0)),
                   pl.BlockSpec((None, 2 * n_b, ns), lambda q: (q, 0, 0)),
                   pl.BlockSpec((None, 2 * n_b, ns), lambda q: (q, 0, 0))],
        out_shape=(jax.ShapeDtypeStruct((nq, 2 * LANES, 2 * ns), BF16),
                   jax.ShapeDtypeStruct((nq, 2 * ns, 2 * LANES), BF16),
                   jax.ShapeDtypeStruct((nq, 2, 1, ns), F32), jax.ShapeDtypeStruct((nq, 2, 1, ns), F32),
                   jax.ShapeDtypeStruct((nq, 2 * n_b, ns), F32), jax.ShapeDtypeStruct((nq, 2 * n_b, ns), F32)),
        compiler_params=_params("parallel"),
        name="s5_weights",
    )(*args)


def _s5_seq_kernel(u0_ref, u1_ref, bw_ref, cw_ref, are_ref, aim_ref, d_ref,
                   y_ref, xre_ref, xim_ref, lhs_ref, bu_ref, y_scr, x_scr, *, tc, n_b):
    n_lt = STATES_PER_BLOCK // LANES
    slots = 2 * n_b
    re_l = lambda lt: slice(2 * lt * LANES, (2 * lt + 1) * LANES)
    im_l = lambda lt: slice((2 * lt + 1) * LANES, (2 * lt + 2) * LANES)
    nat = lambda lt: slice(lt * LANES, (lt + 1) * LANES)
    c = pl.program_id(1)

    @pl.when(c == 0)
    def _():
        x_scr[...] = jnp.zeros_like(x_scr)
        lhs_ref[...] = jnp.zeros_like(lhs_ref)

    for b in range(n_b):
        lhs_ref.at[0][pl.ds(2 * b, tc, stride=slots), :] = u0_ref[b]
        lhs_ref.at[1][pl.ds(2 * b + 1, tc, stride=slots), :] = u1_ref[b]
    a_re = [are_ref[:, nat(lt)] for lt in range(n_lt)]
    a_im = [aim_ref[:, nat(lt)] for lt in range(n_lt)]
    a2_re = [a_re[lt] * a_re[lt] - a_im[lt] * a_im[lt] for lt in range(n_lt)]
    a2_im = [2.0 * a_re[lt] * a_im[lt] for lt in range(n_lt)]
    x = [(x_scr[:, re_l(lt)], x_scr[:, im_l(lt)]) for lt in range(n_lt)]
    sub = tc // S5_SUB
    sub_rows = lambda k: slice(k * sub * slots, (k + 1) * sub * slots)
    first_half = (lax.broadcasted_iota(jnp.int32, (sub * slots, LANES), 0) & 1) == 0

    def project_in(k):
        rows = sub_rows(k)
        lhs = jnp.concatenate([lhs_ref[0, rows, :], lhs_ref[1, rows, :]], axis=1)
        bu_ref[rows, :] = _dot(lhs, bw_ref[...])

    def scan(k):
        for t in range(k * sub, (k + 1) * sub, 2):
            r0 = slice(t * slots, (t + 1) * slots)
            r1 = slice((t + 1) * slots, (t + 2) * slots)
            for lt in range(n_lt):
                xr, xi = x[lt]
                ar, ai = a_re[lt], a_im[lt]
                b0r, b0i = bu_ref[r0, re_l(lt)], bu_ref[r0, im_l(lt)]
                b1r, b1i = bu_ref[r1, re_l(lt)], bu_ref[r1, im_l(lt)]
                cr = ar * b0r - ai * b0i + b1r
                ci = ar * b0i + ai * b0r + b1i
                bu_ref[r0, re_l(lt)] = ar * xr - ai * xi + b0r
                bu_ref[r0, im_l(lt)] = ar * xi + ai * xr + b0i
                nr = a2_re[lt] * xr - a2_im[lt] * xi + cr
                ni = a2_re[lt] * xi + a2_im[lt] * xr + ci
                bu_ref[r1, re_l(lt)] = nr
                bu_ref[r1, im_l(lt)] = ni
                x[lt] = (nr, ni)

    def project_out(k):
        rows = sub_rows(k)
        yf = _dot(bu_ref[rows, :], cw_ref[...])
        y = jnp.where(first_half, yf[:, :LANES], yf[:, LANES:])
        u_rows = lhs_ref[0, rows, :] + lhs_ref[1, rows, :]
        skip = (u_rows.reshape(sub, slots, LANES) * d_ref[...][None]).reshape(sub * slots, LANES)
        y_scr[rows, :] = y + skip

    project_in(0)
    for k in range(S5_SUB):
        if k + 1 < S5_SUB:
            project_in(k + 1)
        scan(k)
        project_out(k)
    for lt in range(n_lt):
        x_scr[:, re_l(lt)] = x[lt][0]
        x_scr[:, im_l(lt)] = x[lt][1]

    for b in range(n_b):
        y_ref[b, :, :LANES] = y_scr[pl.ds(2 * b, tc, stride=slots), :].astype(y_ref.dtype)
        y_ref[b, :, LANES:] = y_scr[pl.ds(2 * b + 1, tc, stride=slots), :].astype(y_ref.dtype)

    @pl.when(c == pl.num_programs(1) - 1)
    def _():
        for lt in range(n_lt):
            xre_ref[:, nat(lt)] = x[lt][0]
            xim_ref[:, nat(lt)] = x[lt][1]


def _s5_sequence(p_main, bw2, cw2, a_slot_re, a_slot_im, d_skip):
    b, t, _ = p_main.shape
    nq = bw2.shape[0]
    nb = 2 * nq
    ns = STATES_PER_BLOCK
    slots = 2 * b
    assert slots == 8, "one 8-row tile must hold every (sequence, half) slot"
    tc = min(S5_TC, t)
    d_blk = d_skip.reshape(nb, LANES)
    d_slot = jnp.tile(jnp.stack([d_blk[:nq], d_blk[nq:]], axis=1), (1, b, 1))
    slot_vec = lambda n: pl.BlockSpec((None, slots, n), lambda q, c: (q, 0, 0))
    y, xre, xim = pl.pallas_call(
        functools.partial(_s5_seq_kernel, tc=tc, n_b=b),
        grid=(nq, t // tc),
        in_specs=[pl.BlockSpec((b, tc, LANES), lambda q, c: (0, c, q)),
                  pl.BlockSpec((b, tc, LANES), lambda q, c: (0, c, q + nq)),
                  pl.BlockSpec((None, 2 * LANES, 2 * ns), lambda q, c: (q, 0, 0)),
                  pl.BlockSpec((None, 2 * ns, 2 * LANES), lambda q, c: (q, 0, 0)),
                  slot_vec(ns), slot_vec(ns), slot_vec(LANES)],
        out_specs=[pl.BlockSpec((b, tc, 2 * LANES), lambda q, c: (0, c, q)),
                   slot_vec(ns), slot_vec(ns)],
        out_shape=(jax.ShapeDtypeStruct((b, t, nb * LANES), BF16),
                   jax.ShapeDtypeStruct((nq, slots, ns), F32),
                   jax.ShapeDtypeStruct((nq, slots, ns), F32)),
        scratch_shapes=[pltpu.VMEM((2, tc * slots, LANES), F32),
                        pltpu.VMEM((tc * slots, 2 * ns), F32),
                        pltpu.VMEM((tc * slots, LANES), F32),
                        pltpu.VMEM((slots, 2 * ns), F32)],
        compiler_params=_params("parallel", "arbitrary"),
        name="s5_sequence",
    )(p_main, p_main, bw2, cw2, a_slot_re, a_slot_im, d_slot)

    def state(x):
        x = x.reshape(nq, b, 2, ns)
        return jnp.transpose(x, (1, 2, 0, 3)).reshape(b, nb * ns)

    return y, state(xre), state(xim)


def _s5_step_kernel(u_ref, bw_ref, cw_ref, abr_ref, abi_ref, d_ref, x0r_ref, x0i_ref,
                    y_ref, x1r_ref, x1i_ref):
    n_lt = STATES_PER_BLOCK // LANES
    u = u_ref[...]
    bu = _dot(u, bw_ref[...])
    tiles = []
    for lt in range(n_lt):
        nat = slice(lt * LANES, (lt + 1) * LANES)
        ar, ai = abr_ref[:, nat], abi_ref[:, nat]
        x0r, x0i = x0r_ref[:, nat], x0i_ref[:, nat]
        xr = ar * x0r - ai * x0i + bu[:, 2 * lt * LANES:(2 * lt + 1) * LANES]
        xi = ar * x0i + ai * x0r + bu[:, (2 * lt + 1) * LANES:(2 * lt + 2) * LANES]
        x1r_ref[:, nat] = xr
        x1i_ref[:, nat] = xi
        tiles += [xr, xi]
    y_ref[...] = _dot(jnp.concatenate(tiles, axis=1), cw_ref[...]) + d_ref[...] * u


def _s5_step(p_rows, bw2, cw2, a_blk_re, a_blk_im, d_skip, x0_re, x0_im):
    rows = x0_re.shape[0]
    nq = bw2.shape[0]
    nb = 2 * nq
    ns = STATES_PER_BLOCK
    vec = lambda: pl.BlockSpec((None, None, 1, ns), lambda j: (j % nq, j // nq, 0, 0))
    st = lambda: pl.BlockSpec((rows, ns), lambda j: (0, j))
    return pl.pallas_call(
        _s5_step_kernel,
        grid=(nb,),
        in_specs=[pl.BlockSpec((rows, LANES), lambda j: (0, j)),
                  pl.BlockSpec((None, LANES, 2 * ns), lambda j: (j % nq, j // nq, 0)),
                  pl.BlockSpec((None, 2 * ns, LANES), lambda j: (j % nq, 0, j // nq)),
                  vec(), vec(),
                  pl.BlockSpec((1, LANES), lambda j: (0, j)),
                  st(), st()],
        out_specs=[pl.BlockSpec((rows, LANES), lambda j: (0, j)), st(), st()],
        out_shape=(jax.ShapeDtypeStruct((rows, nb * LANES), F32),
                   jax.ShapeDtypeStruct((rows, nb * ns), F32),
                   jax.ShapeDtypeStruct((rows, nb * ns), F32)),
        compiler_params=_params("parallel"),
        name="s5_step",
    )(p_rows, bw2, cw2, a_blk_re, a_blk_im, d_skip, x0_re, x0_im)


def _softplus(x):
    return jnp.maximum(x, 0.0) + jnp.log(1.0 + jnp.exp(-jnp.abs(x)))


def _rwkv_token_terms(r, k, lo, w0, w2p, a0, a2p, k_a):
    w = -_softplus(-(w0 + _dot(jnp.tanh(lo), w2p))) - 0.5
    logd = -jnp.exp(w)
    a = _sigmoid(a0 + _dot(lo, a2p))
    k2 = k * (1.0 + (a - 1.0) * k_a)
    return logd, a, k2


def _rwkv_chunk_kernel(r_ref, k_ref, v_ref, lo_ref, z_ref,
                       mur_ref, muk_ref, muv_ref, mulo_ref,
                       w0_ref, w2p_ref, a0_ref, a2p_ref, kk_ref, ka_ref, rk_ref, gw_ref, gb_ref,
                       o_ref, hs_ref,
                       h_scr, pr_scr, pk_scr, pv_scr, plo_scr, *, n_heads, n_seqs):
    L = RW_CHUNK
    W = 2 * RW_HEAD
    n_pairs = n_heads // 2
    c = pl.program_id(1)

    @pl.when(c == 0)
    def _():
        h_scr[...] = jnp.zeros_like(h_scr)
        pr_scr[...] = jnp.zeros_like(pr_scr)
        pk_scr[...] = jnp.zeros_like(pk_scr)
        pv_scr[...] = jnp.zeros_like(pv_scr)
        plo_scr[...] = jnp.zeros_like(plo_scr)

    row1 = lax.broadcasted_iota(jnp.int32, (L, 1), 0)

    def token_shift(cur, prev_scr, mu_ref):
        prev = jnp.where(row1 == 0, prev_scr[...], pltpu.roll(cur, 1, 0))
        prev_scr[...] = cur[L - 1:L, :]
        return cur + (prev - cur) * mu_ref[...]

    tri = (lax.broadcasted_iota(jnp.int32, (L, L), 0)
           >= lax.broadcasted_iota(jnp.int32, (L, L), 1)).astype(BF16)
    first_head = lax.broadcasted_iota(jnp.int32, (L, W), 1) < RW_HEAD

    def head_sum(x):
        s0 = jnp.sum(jnp.where(first_head, x, 0.0), axis=-1, keepdims=True)
        s1 = jnp.sum(jnp.where(first_head, 0.0, x), axis=-1, keepdims=True)
        return jnp.where(first_head, s0, s1)

    def stack_heads(x):
        return jnp.concatenate([jnp.where(first_head, x, 0.0), jnp.where(first_head, 0.0, x)], axis=0)

    ri = lax.broadcasted_iota(jnp.int32, (2 * L, 2 * L), 0)
    ci = lax.broadcasted_iota(jnp.int32, (2 * L, 2 * L), 1)
    t_row, t_col = ri & (L - 1), ci & (L - 1)
    same_head = (ri // L) == (ci // L)
    strict = same_head & (t_row > t_col)
    incl = same_head & (t_row >= t_col)
    eye = (ri == ci).astype(F32)
    blk_masks = []
    s = INV_BASE
    while s <= L:
        blk_masks.append((ri // s) == (ci // s))
        s *= 2

    pairs = range(n_pairs)
    loc = [slice(p * W, (p + 1) * W) for p in pairs]

    def seq_phases(si):
        r = token_shift(r_ref[si], pr_scr.at[si], mur_ref)
        k = token_shift(k_ref[si], pk_scr.at[si], muk_ref)
        v = token_shift(v_ref[si], pv_scr.at[si], muv_ref)
        lo = token_shift(lo_ref[si], plo_scr.at[si], mulo_ref)
        logd, a, k2 = _rwkv_token_terms(r, k, lo, w0_ref[...], w2p_ref[...], a0_ref[...], a2p_ref[...],
                                        ka_ref[...])
        kk = k * kk_ref[...]
        hi = logd.astype(BF16)
        rem = logd - hi.astype(F32)
        mid = rem.astype(BF16)
        low = (rem - mid.astype(F32)).astype(BF16)
        lp = (jnp.dot(tri, hi, preferred_element_type=F32)
              + jnp.dot(tri, mid, preferred_element_type=F32)
              + jnp.dot(tri, low, preferred_element_type=F32))
        p_inc = jnp.exp(lp)
        p_exc = jnp.exp(lp - logd)
        p_inv = jnp.exp(-lp)
        yield

        lhs, nm, mm, ab, vs, kb, p_end = [], [], [], [], [], [], []
        for sl in loc:
            kk_p = kk[:, sl]
            kkn = kk_p * lax.rsqrt(jnp.maximum(head_sum(kk_p * kk_p), 1e-24))
            pinc, pinv = p_inc[:, sl], p_inv[:, sl]
            kd = k2[:, sl] * pinv
            bd = kkn * a[:, sl] * pinv
            pe = pinc[L - 1:L, :]
            lhs_p = jnp.concatenate([stack_heads(kkn * p_exc[:, sl]), stack_heads(r[:, sl] * pinc)],
                                    axis=0).astype(BF16)
            rhs_p = jnp.concatenate([bd, bd, kd, kd], axis=0).astype(BF16)
            amat = lax.dot_general(lhs_p, rhs_p, _NT, preferred_element_type=F32)
            lhs.append(lhs_p)
            nm.append(jnp.where(strict, amat[:2 * L, :2 * L], 0.0))
            mm.append(jnp.where(strict, amat[:2 * L, 2 * L:], 0.0).astype(BF16))
            ab.append(jnp.concatenate([jnp.where(incl, amat[2 * L:, 2 * L:], 0.0),
                                       -jnp.where(incl, amat[2 * L:, :2 * L], 0.0)], axis=1).astype(BF16))
            vs.append(stack_heads(v[:, sl]).astype(BF16))
            kb.append(jnp.concatenate([stack_heads(kd * pe), stack_heads(bd * pe)], axis=0).astype(BF16))
            p_end.append(pe)
            yield

        d = [jnp.where(blk_masks[0], n_p, 0.0).astype(BF16) for n_p in nm]
        x = [eye - d_p.astype(F32) for d_p in d]
        pw = [jnp.dot(d_p, d_p, preferred_element_type=F32) for d_p in d]
        yield
        s = 2
        while s < INV_BASE:
            x = [x_p + _dot(x_p, pw_p) for x_p, pw_p in zip(x, pw)]
            s *= 2
            if s < INV_BASE:
                pw = [_dot(pw_p, pw_p) for pw_p in pw]
            yield
        for lvl in range(1, len(blk_masks)):
            off = blk_masks[lvl] & ~blk_masks[lvl - 1]
            xc = [_dot(x_p, jnp.where(off, n_p, 0.0)) for x_p, n_p in zip(x, nm)]
            yield
            x = [x_p - _dot(xc_p, x_p) for x_p, xc_p in zip(x, xc)]
            yield

        hs = [h_scr[si, p] for p in pairs]
        lh = [jnp.dot(lhs_p, hs_p.astype(BF16), preferred_element_type=F32) for lhs_p, hs_p in zip(lhs, hs)]
        mv = [jnp.dot(mm_p, vs_p, preferred_element_type=F32) for mm_p, vs_p in zip(mm, vs)]
        yield
        u = [_dot(x_p, lh_p[:2 * L] + mv_p).astype(BF16) for x_p, lh_p, mv_p in zip(x, lh, mv)]
        yield
        o_st = [lh_p[2 * L:] + jnp.dot(ab_p, jnp.concatenate([vs_p, u_p], axis=0), preferred_element_type=F32)
                for lh_p, ab_p, vs_p, u_p in zip(lh, ab, vs, u)]
        for p in pairs:
            p_end_col = jnp.sum(eye * p_end[p], axis=1, keepdims=True)
            h_scr[si, p] = p_end_col * hs[p] + lax.dot_general(
                kb[p], jnp.concatenate([vs[p], -u[p]], axis=0), _TN, preferred_element_type=F32)
        yield

        for p, sl in enumerate(loc):
            o = o_st[p][:L] + o_st[p][L:]
            mu = head_sum(o) * (1.0 / RW_HEAD)
            var = head_sum((o - mu) ** 2) * (1.0 / RW_HEAD)
            o = (o - mu) * lax.rsqrt(var + GN_EPS) * gw_ref[:, sl] + gb_ref[:, sl]
            o = o + head_sum(r[:, sl] * k2[:, sl] * rk_ref[:, sl]) * v[:, sl]
            z = z_ref[si, :, LANES + p * W:LANES + (p + 1) * W]
            o_ref[si, :, sl] = (o * _silu(z)).astype(o_ref.dtype)
        yield

    gens = [seq_phases(si) for si in range(n_seqs)]
    started = 0
    live = []
    step = 0
    while started < n_seqs or live:
        if started < n_seqs and step % RW_SEQ_LAG == 0:
            live.append(gens[started])
            started += 1
        for g in list(live):
            if next(g, "done") == "done":
                live.remove(g)
        step += 1

    @pl.when(c == pl.num_programs(1) - 1)
    def _():
        for si in range(n_seqs):
            for p in pairs:
                st = h_scr[si, p].T
                hs_ref[si, 2 * p] = st[:RW_HEAD, :RW_HEAD]
                hs_ref[si, 2 * p + 1] = st[RW_HEAD:, RW_HEAD:]


def _rwkv_sequence(p_main, p_lz, col, mu, w0, w2p, a0, a2p, k_k, k_a, r_k, gn_w, gn_b):
    b, t, _ = p_main.shape
    wlz = p_lz.shape[-1]
    d = w0.shape[-1]
    n_heads = d // RW_HEAD
    n_pairs, pw = n_heads // 2, 2 * RW_HEAD
    L = RW_CHUNK
    ns = RW_SEQS if b % RW_SEQS == 0 else 1
    blk = lambda cb: pl.BlockSpec((ns, L, d), lambda i, j, cb=cb: (i, j, cb))
    vec = lambda n: pl.BlockSpec((1, n), lambda i, j: (0, 0))
    full = lambda shp: pl.BlockSpec(shp, lambda i, j: (0,) * len(shp))
    mu_r, mu_k, mu_v, mu_lo = mu
    o, hs = pl.pallas_call(
        functools.partial(_rwkv_chunk_kernel, n_heads=n_heads, n_seqs=ns),
        grid=(b // ns, t // L),
        in_specs=[blk(col["r"]), blk(col["k"]), blk(col["v"]),
                  pl.BlockSpec((ns, L, LANES), lambda i, j: (i, j, 0)),
                  pl.BlockSpec((ns, L, wlz), lambda i, j: (i, j, 0)),
                  vec(d), vec(d), vec(d), vec(LANES),
                  vec(d), full((LANES, d)), vec(d), full((LANES, d)),
                  vec(d), vec(d), vec(d), vec(d), vec(d)],
        out_specs=[pl.BlockSpec((ns, L, d), lambda i, j: (i, j, 0)),
                   pl.BlockSpec((ns, n_heads, RW_HEAD, RW_HEAD), lambda i, j: (i, 0, 0, 0))],
        out_shape=(jax.ShapeDtypeStruct((b, t, d), BF16),
                   jax.ShapeDtypeStruct((b, n_heads, RW_HEAD, RW_HEAD), F32)),
        scratch_shapes=[pltpu.VMEM((ns, n_pairs, pw, pw), F32),
                        pltpu.VMEM((ns, 1, d), F32), pltpu.VMEM((ns, 1, d), F32), pltpu.VMEM((ns, 1, d), F32),
                        pltpu.VMEM((ns, 1, LANES), F32)],
        compiler_params=_params("parallel", "arbitrary"),
        name="rwkv_sequence",
    )(p_main, p_main, p_main, p_lz, p_lz,
      mu_r, mu_k, mu_v, mu_lo, w0, w2p, a0, a2p, k_k, k_a, r_k, gn_w, gn_b)
    return o, hs


def _rwkv_step_prep_kernel(cr_ref, ck_ref, cv_ref, clo_ref, cz_ref, pr_ref, pk_ref, pv_ref, plo_ref,
                           mur_ref, muk_ref, muv_ref, mulo_ref,
                           w0_ref, w2p_ref, a0_ref, a2p_ref, kk_ref, ka_ref,
                           r_o, k2_o, v_o, kk_o, a_o, d_o, z_o):
    def lerp(c_ref, p_ref, mu_ref):
        cur = c_ref[...]
        return cur + (p_ref[...] - cur) * mu_ref[...]

    r = lerp(cr_ref, pr_ref, mur_ref)
    k = lerp(ck_ref, pk_ref, muk_ref)
    v = lerp(cv_ref, pv_ref, muv_ref)
    lo = lerp(clo_ref, plo_ref, mulo_ref)
    logd, a, k2 = _rwkv_token_terms(r, k, lo, w0_ref[...], w2p_ref[...], a0_ref[...], a2p_ref[...],
                                    ka_ref[...])
    r_o[...] = r.T
    k2_o[...] = k2.T
    v_o[...] = v.T
    kk_o[...] = (k * kk_ref[...]).T
    a_o[...] = a.T
    d_o[...] = jnp.exp(logd).T
    z_o[...] = cz_ref[:, LANES:].T


def _rwkv_step_kernel(s_ref, r_ref, k2_ref, v_ref, kk_ref, a_ref, d_ref, z_ref,
                      rk_ref, gw_ref, gb_ref, o_ref, s1_ref, o_scr):
    r, k2, v, dec = r_ref[...], k2_ref[...], v_ref[...], d_ref[...]
    kk = kk_ref[...]
    kkn = kk / jnp.maximum(jnp.sqrt(jnp.sum(kk * kk, axis=0, keepdims=True)), 1e-12)
    bvec = kkn * a_ref[...]
    for i in range(RW_HEAD):
        s = s_ref[i]
        sa = jnp.sum(s * kkn, axis=0, keepdims=True)
        s1 = s * dec - sa * bvec + v[i:i + 1, :] * k2
        s1_ref[i] = s1
        o_scr[pl.ds(i, 1), :] = jnp.sum(s1 * r, axis=0, keepdims=True)
    o = o_scr[...]
    mu = jnp.mean(o, axis=0, keepdims=True)
    var = jnp.mean((o - mu) ** 2, axis=0, keepdims=True)
    o = (o - mu) * lax.rsqrt(var + GN_EPS) * gw_ref[...] + gb_ref[...]
    o = o + jnp.sum(r * k2 * rk_ref[...], axis=0, keepdims=True) * v
    o_ref[...] = o * _silu(z_ref[...])


def _rwkv_step(p_rows, p_lz_rows, col, mu, w0, w2p, a0, a2p, k_k, k_a, r_k, gn_w, gn_b, s0):
    rows = s0.shape[0]
    wlz = p_lz_rows.shape[-1]
    d = w0.shape[-1]
    n_heads = d // RW_HEAD
    cur = lambda cb: pl.BlockSpec((rows, d), lambda i, cb=cb: (0, cb))
    prv = lambda cb: pl.BlockSpec((rows, d), lambda i, cb=cb: (1, cb))
    vec = lambda n: pl.BlockSpec((1, n), lambda i: (0, 0))
    mu_r, mu_k, mu_v, mu_lo = mu
    out = jax.ShapeDtypeStruct((d, rows), F32)
    terms = pl.pallas_call(
        _rwkv_step_prep_kernel,
        grid=(1,),
        in_specs=[cur(col["r"]), cur(col["k"]), cur(col["v"]),
                  pl.BlockSpec((rows, LANES), lambda i: (0, 0)),
                  pl.BlockSpec((rows, wlz), lambda i: (0, 0)),
                  prv(col["r"]), prv(col["k"]), prv(col["v"]),
                  pl.BlockSpec((rows, LANES), lambda i: (1, 0)),
                  vec(d), vec(d), vec(d), vec(LANES),
                  vec(d), pl.BlockSpec((LANES, d), lambda i: (0, 0)),
                  vec(d), pl.BlockSpec((LANES, d), lambda i: (0, 0)),
                  vec(d), vec(d)],
        out_specs=[pl.BlockSpec((d, rows), lambda i: (0, 0))] * 7,
        out_shape=(out,) * 7,
        compiler_params=_params("arbitrary"),
        name="rwkv_step_prep",
    )(p_rows, p_rows, p_rows, p_lz_rows, p_lz_rows, p_rows, p_rows, p_rows, p_lz_rows,
      mu_r, mu_k, mu_v, mu_lo, w0, w2p, a0, a2p, k_k, k_a)
    per_h = lambda: pl.BlockSpec((RW_HEAD, rows), lambda h: (h, 0))
    par = lambda: pl.BlockSpec((RW_HEAD, 1), lambda h: (h, 0))
    st = lambda: pl.BlockSpec((None, RW_HEAD, RW_HEAD, rows), lambda h: (h, 0, 0, 0))
    o_t, s1_t = pl.pallas_call(
        _rwkv_step_kernel,
        grid=(n_heads,),
        in_specs=[st()] + [per_h()] * 7 + [par()] * 3,
        out_specs=[per_h(), st()],
        out_shape=(jax.ShapeDtypeStruct((d, rows), F32),
                   jax.ShapeDtypeStruct((n_heads, RW_HEAD, RW_HEAD, rows), F32)),
        scratch_shapes=[pltpu.VMEM((RW_HEAD, rows), F32)],
        compiler_params=_params("parallel"),
        name="rwkv_step",
    )(jnp.transpose(s0, (1, 2, 3, 0)), *terms,
      r_k.reshape(d, 1), gn_w.reshape(d, 1), gn_b.reshape(d, 1))
    return o_t.T.astype(BF16), jnp.transpose(s1_t, (3, 0, 1, 2))


def _out_kernel(ys_ref, z_ref, wg_ref, bg_ref, or_ref, gs_ref, gr_ref, x_ref, gt_ref, w1_ref, w2_ref,
                fg_ref, y_ref, *, paired_blocks):
    ys = ys_ref[...].astype(F32)
    if paired_blocks:
        nb = ys.shape[1] // LANES
        pos = [2 * j if j < nb // 2 else 2 * (j - nb // 2) + 1 for j in range(nb)]
        ys = jnp.concatenate([ys[:, p * LANES:(p + 1) * LANES] for p in pos], axis=1)
    ys = jax.nn.gelu(ys, approximate=True)
    o_s = (ys * _sigmoid(_dot(ys, wg_ref[...]) + bg_ref[...]) * _silu(z_ref[...])).astype(BF16)
    mixed = (gs_ref[...].astype(F32) * jnp.dot(o_s, w1_ref[...], preferred_element_type=F32)
             + gr_ref[...].astype(F32) * jnp.dot(or_ref[...], w2_ref[...], preferred_element_type=F32))
    x = x_ref[...] + gt_ref[...] * mixed
    y_ref[...] = x * lax.rsqrt(jnp.mean(x * x, axis=-1, keepdims=True) + RMS_EPS) * fg_ref[...]


def _out_proj(y_s5, p_main, o_r, p_gate, col, x, mod, w_glu, b_glu, w_out, final_g, tt, paired_blocks):
    b, t, d = x.shape
    dh = o_r.shape[-1]
    tt = min(tt, t)
    tm = 1 if mod.shape[1] == 1 else tt
    gt_map = (lambda i, j: (i, 0, 2)) if tm == 1 else (lambda i, j: (i, j, 2))
    const = lambda shp, r=0: pl.BlockSpec(shp, lambda i, j: (r, 0), pipeline_mode=pl.Buffered(1))
    return pl.pallas_call(
        functools.partial(_out_kernel, paired_blocks=paired_blocks),
        grid=(b, t // tt),
        in_specs=[pl.BlockSpec((None, tt, dh), lambda i, j: (i, j, 0)),
                  pl.BlockSpec((None, tt, dh), lambda i, j: (i, j, col["z_s5"])),
                  const((dh, dh)), const((1, dh)),
                  pl.BlockSpec((None, tt, dh), lambda i, j: (i, j, 0)),
                  pl.BlockSpec((None, tt, d), lambda i, j: (i, j, col["g_s5"])),
                  pl.BlockSpec((None, tt, d), lambda i, j: (i, j, col["g_rw"])),
                  pl.BlockSpec((None, tt, d), lambda i, j: (i, j, 0)),
                  pl.BlockSpec((None, tm, d), gt_map),
                  const((dh, d), 0), const((dh, d), 1), const((1, d))],
        out_specs=pl.BlockSpec((None, tt, d), lambda i, j: (i, j, 0)),
        out_shape=jax.ShapeDtypeStruct((b, t, d), F32),
        compiler_params=_params("parallel", "parallel"),
        name="out_proj",
    )(y_s5, p_main, w_glu, b_glu.reshape(1, dh), o_r, p_gate, p_gate, x, mod, w_out, w_out,
      final_g.reshape(1, d))


def kernel(x_prompt, x_sample, c_prompt, c_sample, state_s5_re, state_s5_im, state_wkv, state_shift, norm_g, w_ada, b_ada, w_in, mu_rw, A_re, A_im, log_step, B_re, B_im, C_re, C_im, D_skip, w_glu, b_glu, w0, w2, a0, a2, k_k, k_a, r_k, gn_w, gn_b, w_out, final_g):
    depth = norm_g.shape[0]
    assert depth == 1
    bp, tp, d = x_prompt.shape
    bs = x_sample.shape[0]
    assert x_sample.shape[1] == 1
    dh = d // 2
    l = 0

    w_main, w_lz, w_gate = 5 * dh, 2 * LORA + dh, 2 * d
    col = {"u": 0, "z_s5": 1, "r": 2, "k": 3, "v": 4, "g_s5": 0, "g_rw": 1}
    w = w_in[l]

    def project(rows, rows2):
        return (_in_proj(rows, rows2, w, 0, w_main, w_main // IN_PROJ_COL_TILES, IN_PROJ_ROWS),
                _in_proj(rows, rows2, w, w_main, w_lz, w_lz, IN_PROJ_ROWS),
                _in_proj(rows, rows2, w, w_main + w_lz, w_gate, w_gate // IN_PROJ_COL_TILES, IN_PROJ_ROWS,
                         gate=True))

    mu = mu_rw[l]
    mu_parts = (mu[None, :dh], mu[None, dh:2 * dh], mu[None, 2 * dh:3 * dh], mu[None, 3 * dh:])
    zpad = jnp.zeros((LORA, dh), F32)
    w2p = jnp.concatenate([w2[l], zpad], axis=0).astype(BF16)
    a2p = jnp.concatenate([zpad, a2[l]], axis=0).astype(BF16)
    row = lambda x: x.reshape(1, -1)
    rw_params = (row(w0[l]), w2p, row(a0[l]), a2p, row(k_k[l]), row(k_a[l]), row(r_k[l]),
                 row(gn_w[l]), row(gn_b[l]))
    w_out_bf = w_out[l].astype(BF16)
    w_glu_bf = w_glu[l].astype(BF16)

    bw2, cw2, a_blk_re, a_blk_im, a_slot_re, a_slot_im = _s5_weights(
        A_re[l], A_im[l], log_step[l], B_re[l], B_im[l], C_re[l], C_im[l], bp)
    d_skip = D_skip[l].reshape(1, -1)

    mod_p, mod_s = _mod(c_prompt, c_sample, w_ada[l], b_ada[l])
    mod_p = mod_p.reshape(bp, 1, 3 * d)
    mod_s = mod_s.reshape(1, bs, 3 * d)

    h_p = _modulated_norm(x_prompt, norm_g[l], mod_p, BF16, NORM_ROWS)
    shift_p = _modulated_norm(x_prompt[:, tp - 1:, :], norm_g[l], mod_p, F32, 1)[:, 0]
    xs = x_sample.reshape(1, bs, d)
    h_s = _modulated_norm(xs, norm_g[l], mod_s, F32, bs)[0]
    a_s = jnp.concatenate([h_s, state_shift[l]], axis=0).astype(BF16)
    (pm, ps), (plz, pslz), (pg, psg) = project(h_p.reshape(bp * tp, d), a_s)

    pm3 = pm.reshape(bp, tp, -1)
    y_s5, xre_p, xim_p = _s5_sequence(pm3, bw2, cw2, a_slot_re, a_slot_im, d_skip)
    o_r, hs_p = _rwkv_sequence(pm3, plz.reshape(bp, tp, -1), col, mu_parts, *rw_params)
    y_prompt = _out_proj(y_s5, pm3, o_r, pg.reshape(bp, tp, -1), col, x_prompt, mod_p,
                         w_glu_bf, b_glu[l], w_out_bf, final_g, OUT_ROWS, True)
    g_s5 = A_re.shape[1]
    s5_shape = (1, bp, g_s5, P_S5)
    wkv_p = hs_p[None]

    y_s5s, xre_s, xim_s = _s5_step(ps, bw2, cw2, a_blk_re, a_blk_im, d_skip,
                                   state_s5_re[l].reshape(bs, -1), state_s5_im[l].reshape(bs, -1))
    o_rs, wkv_s = _rwkv_step(ps, pslz, col, mu_parts, *rw_params, state_wkv[l])
    y_sample = _out_proj(y_s5s[None], ps.reshape(2, bs, -1), o_rs[None], psg.reshape(2, bs, -1), col, xs, mod_s,
                         w_glu_bf, b_glu[l], w_out_bf, final_g, bs, False)
    y_sample = y_sample.reshape(bs, 1, d)

    return (y_prompt, y_sample,
            xre_p.reshape(s5_shape), xim_p.reshape(s5_shape), wkv_p, shift_p[None],
            xre_s.reshape(1, bs, g_s5, P_S5), xim_s.reshape(1, bs, g_s5, P_S5), wkv_s[None], h_s[None])
```

```python
import functools

import jax
import jax.numpy as jnp
from jax import lax
from jax.experimental import pallas as pl
from jax.experimental.pallas import tpu as pltpu

F32 = jnp.float32
BF16 = jnp.bfloat16

RMS_EPS = 1e-6
GN_EPS = 64e-5
S5_GROUP = 16
P_S5 = 64
RW_HEAD = 64
LORA = 64
LANES = 128
GROUPS_PER_BLOCK = LANES // S5_GROUP
STATES_PER_BLOCK = GROUPS_PER_BLOCK * P_S5
S5_TC = 512
S5_SUB = 8
RW_CHUNK = 64
INV_BASE = 16
RW_SEQS = 4
RW_SEQ_LAG = 1
RW_STEP_HEADS = 2

V7X_VMEM_BYTES = 64 * 1024 * 1024
VMEM_LIMIT_BYTES = V7X_VMEM_BYTES // 8 * 7
MOD_TN = 1024
NORM_ROWS = 1024
IN_PROJ_ROWS = 1024
IN_PROJ_COL_TILES = 4
OUT_ROWS = 512

_NT = (((1,), (1,)), ((), ()))
_TN = (((0,), (0,)), ((), ()))


def _dot(a, b):
    return jnp.dot(a.astype(BF16), b.astype(BF16), preferred_element_type=F32)


def _sigmoid(x):
    return 0.5 * jnp.tanh(0.5 * x) + 0.5


def _silu(x):
    return x * _sigmoid(x)


def _params(*sem):
    return pltpu.CompilerParams(dimension_semantics=sem, vmem_limit_bytes=VMEM_LIMIT_BYTES)


def _mod_kernel(c1_ref, c2_ref, w_ref, b_ref, o1_ref, o2_ref):
    w = w_ref[...].astype(BF16)
    for c_ref, o_ref in ((c1_ref, o1_ref), (c2_ref, o2_ref)):
        o_ref[...] = _dot(_silu(c_ref[...]), w) + b_ref[...]


def _mod(c1, c2, w, b):
    d, n = w.shape
    rows = lambda c: pl.BlockSpec((c.shape[0], d), lambda j: (0, 0))
    out = lambda c: pl.BlockSpec((c.shape[0], MOD_TN), lambda j: (0, j))
    return pl.pallas_call(
        _mod_kernel,
        grid=(n // MOD_TN,),
        in_specs=[rows(c1), rows(c2),
                  pl.BlockSpec((d, MOD_TN), lambda j: (0, j)),
                  pl.BlockSpec((1, MOD_TN), lambda j: (0, j))],
        out_specs=[out(c1), out(c2)],
        out_shape=(jax.ShapeDtypeStruct((c1.shape[0], n), F32), jax.ShapeDtypeStruct((c2.shape[0], n), F32)),
        compiler_params=_params("parallel"),
        name="adaln_mod",
    )(c1, c2, w, b.reshape(1, n))


def _h_kernel(x_ref, g_ref, sh_ref, sc_ref, h_ref):
    x = x_ref[...]
    y = x * lax.rsqrt(jnp.mean(x * x, axis=-1, keepdims=True) + RMS_EPS) * g_ref[...]
    h_ref[...] = (y * (1.0 + sc_ref[...]) + sh_ref[...]).astype(h_ref.dtype)


def _modulated_norm(x, g, mod, out_dtype, tt):
    b, t, d = x.shape
    tt = min(tt, t)
    tm = 1 if mod.shape[1] == 1 else tt
    mod_map = (lambda i, j: (i, 0, 0)) if tm == 1 else (lambda i, j: (i, j, 0))
    mod_map1 = (lambda i, j: (i, 0, 1)) if tm == 1 else (lambda i, j: (i, j, 1))
    return pl.pallas_call(
        _h_kernel,
        grid=(b, t // tt),
        in_specs=[pl.BlockSpec((None, tt, d), lambda i, j: (i, j, 0)),
                  pl.BlockSpec((1, d), lambda i, j: (0, 0)),
                  pl.BlockSpec((None, tm, d), mod_map),
                  pl.BlockSpec((None, tm, d), mod_map1)],
        out_specs=pl.BlockSpec((None, tt, d), lambda i, j: (i, j, 0)),
        out_shape=jax.ShapeDtypeStruct((b, t, d), out_dtype),
        compiler_params=_params("parallel", "parallel"),
        name="modulated_norm",
    )(x, g.reshape(1, d), mod, mod)


def _in_proj_kernel(a_ref, a2_ref, w_ref, o_ref, o2_ref, wbf_ref, *, gate):
    i = pl.program_id(1)

    def project(x_ref, out_ref):
        p = jnp.dot(x_ref[...], wbf_ref[...], preferred_element_type=F32)
        out_ref[...] = (_sigmoid(p) if gate else p).astype(out_ref.dtype)

    @pl.when(i == 0)
    def _():
        wbf_ref[...] = w_ref[...].astype(BF16)
        project(a2_ref, o2_ref)

    @pl.when(i > 0)
    def _():
        project(a_ref, o_ref)


def _in_proj(a, a2, w, col0, width, tn, tm, gate=False):
    m, k = a.shape
    m2 = a2.shape[0]
    tm = min(tm, m)
    assert width % tn == 0 and m % tm == 0 and col0 % LANES == 0
    main = lambda i: jnp.maximum(i - 1, 0)
    dt = BF16 if gate else F32
    return pl.pallas_call(
        functools.partial(_in_proj_kernel, gate=gate),
        grid=(width // tn, m // tm + 1),
        in_specs=[pl.BlockSpec((tm, k), lambda j, i: (main(i), 0)),
                  pl.BlockSpec((m2, k), lambda j, i: (0, 0)),
                  pl.BlockSpec((pl.Element(k), pl.Element(tn)), lambda j, i: (0, pl.multiple_of(col0 + j * tn, LANES)))],
        out_specs=[pl.BlockSpec((tm, tn), lambda j, i: (main(i), j)),
                   pl.BlockSpec((m2, tn), lambda j, i: (0, j))],
        out_shape=(jax.ShapeDtypeStruct((m, width), dt), jax.ShapeDtypeStruct((m2, width), dt)),
        scratch_shapes=[pltpu.VMEM((k, tn), BF16)],
        compiler_params=_params("parallel", "arbitrary"),
        name="in_proj",
    )(a, a2, w)


def _s5_weights_kernel(*refs, n_b):
    (are0, are1, aim0, aim1, ls0, ls1, bre0, bre1, bim0, bim1, cre0, cre1, cim0, cim1,
     bw_ref, cw_ref, abr_ref, abi_ref, apr_ref, api_ref) = refs
    n_gl = GROUPS_PER_BLOCK
    n_lt = STATES_PER_BLOCK // LANES

    def discretise(are_ref, aim_ref, ls_ref, bre_ref, bim_ref):
        step = jnp.exp(ls_ref[...])
        lam_re = jnp.minimum(are_ref[...], -1e-4)
        lam_im = aim_ref[...]
        mag = jnp.exp(lam_re * step)
        ab_re = mag * jnp.cos(lam_im * step)
        ab_im = mag * jnp.sin(lam_im * step)
        den = lam_re * lam_re + lam_im * lam_im
        f_re = ((ab_re - 1.0) * lam_re + ab_im * lam_im) / den
        f_im = (ab_im * lam_re - (ab_re - 1.0) * lam_im) / den
        br, bi = bre_ref[...], bim_ref[...]
        return ab_re, ab_im, f_re * br - f_im * bi, f_re * bi + f_im * br

    halves = [discretise(are0, aim0, ls0, bre0, bim0), discretise(are1, aim1, ls1, bre1, bim1)]
    zero = jnp.zeros((S5_GROUP, P_S5), F32)

    def band(t_re, t_im, gl):
        pieces = []
        for lt in range(n_lt):
            for tile in (t_re, t_im):
                for half in range(2):
                    pieces.append(tile if 2 * lt + half == gl else zero)
        return jnp.concatenate(pieces, axis=1)

    def block_rows(tiles_re, tiles_im):
        return jnp.concatenate([band(tiles_re[h][gl], tiles_im[h][gl], gl)
                                for h in range(2) for gl in range(n_gl)], axis=0)

    bw_ref[...] = block_rows([h[2] for h in halves], [h[3] for h in halves]).astype(BF16)
    cw_ref[...] = block_rows([cre0[...], cre1[...]], [-cim0[...], -cim1[...]]).T.astype(BF16)

    for h in range(2):
        for a_ref, a in ((abr_ref, halves[h][0]), (abi_ref, halves[h][1])):
            a_ref[h] = jnp.concatenate([a[gl] for gl in range(n_gl)], axis=1)
    for p_ref, a_ref in ((apr_ref, abr_ref), (api_ref, abi_ref)):
        p_ref[...] = jnp.concatenate([a_ref[0], a_ref[1]] * n_b, axis=0)


def _s5_weights(a_re, a_im, log_step, b_re, b_im, c_re, c_im, n_b):
    g, p = a_re.shape
    c = b_re.shape[-1]
    nq = g // GROUPS_PER_BLOCK // 2
    ns = STATES_PER_BLOCK
    args, specs = [], []
    for x, shp in ((a_re.reshape(g, 1, p), (1, p)), (a_im.reshape(g, 1, p), (1, p)),
                   (log_step.reshape(g, 1, 1), (1, 1)),
                   (jnp.swapaxes(b_re, 1, 2), (c, p)), (jnp.swapaxes(b_im, 1, 2), (c, p)),
                   (c_re, (c, p)), (c_im, (c, p))):
        for half in range(2):
            args.append(x)
            specs.append(pl.BlockSpec((GROUPS_PER_BLOCK,) + shp, lambda q, half=half: (q + half * nq, 0, 0)))
    return pl.pallas_call(
        functools.partial(_s5_weights_kernel, n_b=n_b),
        grid=(nq,),
        in_specs=specs,
        out_specs=[pl.BlockSpec((None, 2 * LANES, 2 * ns), lambda q: (q, 0, 0)),
                   pl.BlockSpec((None, 2 * ns, 2 * LANES), lambda q: (q, 0, 0)),
                   pl.BlockSpec((None, 2, 1, ns), lambda q: (q, 0, 0, 0)),
                   pl.BlockSpec((None, 2, 1, ns), lambda q: (q, 0, 0, 0)),
                   pl.BlockSpec((None, 2 * n_b, ns), lambda q: (q, 0, 0)),
                   pl.BlockSpec((None, 2 * n_b, ns), lambda q: (q, 0, 0))],
        out_shape=(jax.ShapeDtypeStruct((nq, 2 * LANES, 2 * ns), BF16),
                   jax.ShapeDtypeStruct((nq, 2 * ns, 2 * LANES), BF16),
                   jax.ShapeDtypeStruct((nq, 2, 1, ns), F32), jax.ShapeDtypeStruct((nq, 2, 1, ns), F32),
                   jax.ShapeDtypeStruct((nq, 2 * n_b, ns), F32), jax.ShapeDtypeStruct((nq, 2 * n_b, ns), F32)),
        compiler_params=_params("parallel"),
        name="s5_weights",
    )(*args)


def _s5_seq_kernel(u0_ref, u1_ref, bw_ref, cw_ref, are_ref, aim_ref, d_ref,
                   y_ref, xre_ref, xim_ref, lhs_ref, bu_ref, y_scr, x_scr, *, tc, n_b):
    n_lt = STATES_PER_BLOCK // LANES
    slots = 2 * n_b
    re_l = lambda lt: slice(2 * lt * LANES, (2 * lt + 1) * LANES)
    im_l = lambda lt: slice((2 * lt + 1) * LANES, (2 * lt + 2) * LANES)
    nat = lambda lt: slice(lt * LANES, (lt + 1) * LANES)
    c = pl.program_id(1)

    @pl.when(c == 0)
    def _():
        x_scr[...] = jnp.zeros_like(x_scr)
        lhs_ref[...] = jnp.zeros_like(lhs_ref)

    for b in range(n_b):
        lhs_ref.at[0][pl.ds(2 * b, tc, stride=slots), :] = u0_ref[b]
        lhs_ref.at[1][pl.ds(2 * b + 1, tc, stride=slots), :] = u1_ref[b]
    a_re = [are_ref[:, nat(lt)] for lt in range(n_lt)]
    a_im = [aim_ref[:, nat(lt)] for lt in range(n_lt)]
    a2_re = [a_re[lt] * a_re[lt] - a_im[lt] * a_im[lt] for lt in range(n_lt)]
    a2_im = [2.0 * a_re[lt] * a_im[lt] for lt in range(n_lt)]
    x = [(x_scr[:, re_l(lt)], x_scr[:, im_l(lt)]) for lt in range(n_lt)]
    sub = tc // S5_SUB
    sub_rows = lambda k: slice(k * sub * slots, (k + 1) * sub * slots)
    first_half = (lax.broadcasted_iota(jnp.int32, (sub * slots, LANES), 0) & 1) == 0

    def project_in(k):
        rows = sub_rows(k)
        lhs = jnp.concatenate([lhs_ref[0, rows, :], lhs_ref[1, rows, :]], axis=1)
        bu_ref[rows, :] = _dot(lhs, bw_ref[...])

    def scan(k):
        for t in range(k * sub, (k + 1) * sub, 2):
            r0 = slice(t * slots, (t + 1) * slots)
            r1 = slice((t + 1) * slots, (t + 2) * slots)
            for lt in range(n_lt):
                xr, xi = x[lt]
                ar, ai = a_re[lt], a_im[lt]
                b0r, b0i = bu_ref[r0, re_l(lt)], bu_ref[r0, im_l(lt)]
                b1r, b1i = bu_ref[r1, re_l(lt)], bu_ref[r1, im_l(lt)]
                cr = ar * b0r - ai * b0i + b1r
                ci = ar * b0i + ai * b0r + b1i
                bu_ref[r0, re_l(lt)] = ar * xr - ai * xi + b0r
                bu_ref[r0, im_l(lt)] = ar * xi + ai * xr + b0i
                nr = a2_re[lt] * xr - a2_im[lt] * xi + cr
                ni = a2_re[lt] * xi + a2_im[lt] * xr + ci
                bu_ref[r1, re_l(lt)] = nr
                bu_ref[r1, im_l(lt)] = ni
                x[lt] = (nr, ni)

    def project_out(k):
        rows = sub_rows(k)
        yf = _dot(bu_ref[rows, :], cw_ref[...])
        y = jnp.where(first_half, yf[:, :LANES], yf[:, LANES:])
        u_rows = lhs_ref[0, rows, :] + lhs_ref[1, rows, :]
        skip = (u_rows.reshape(sub, slots, LANES) * d_ref[...][None]).reshape(sub * slots, LANES)
        y_scr[rows, :] = y + skip

    project_in(0)
    for k in range(S5_SUB):
        if k + 1 < S5_SUB:
            project_in(k + 1)
        scan(k)
        project_out(k)
    for lt in range(n_lt):
        x_scr[:, re_l(lt)] = x[lt][0]
        x_scr[:, im_l(lt)] = x[lt][1]

    for b in range(n_b):
        y_ref[b, :, :LANES] = y_scr[pl.ds(2 * b, tc, stride=slots), :].astype(y_ref.dtype)
        y_ref[b, :, LANES:] = y_scr[pl.ds(2 * b + 1, tc, stride=slots), :].astype(y_ref.dtype)

    @pl.when(c == pl.num_programs(1) - 1)
    def _():
        for lt in range(n_lt):
            xre_ref[:, nat(lt)] = x[lt][0]
            xim_ref[:, nat(lt)] = x[lt][1]


def _s5_sequence(p_main, bw2, cw2, a_slot_re, a_slot_im, d_skip):
    b, t, _ = p_main.shape
    nq = bw2.shape[0]
    nb = 2 * nq
    ns = STATES_PER_BLOCK
    slots = 2 * b
    assert slots == 8, "one 8-row tile must hold every (sequence, half) slot"
    tc = min(S5_TC, t)
    d_blk = d_skip.reshape(nb, LANES)
    d_slot = jnp.tile(jnp.stack([d_blk[:nq], d_blk[nq:]], axis=1), (1, b, 1))
    slot_vec = lambda n: pl.BlockSpec((None, slots, n), lambda q, c: (q, 0, 0))
    y, xre, xim = pl.pallas_call(
        functools.partial(_s5_seq_kernel, tc=tc, n_b=b),
        grid=(nq, t // tc),
        in_specs=[pl.BlockSpec((b, tc, LANES), lambda q, c: (0, c, q)),
                  pl.BlockSpec((b, tc, LANES), lambda q, c: (0, c, q + nq)),
                  pl.BlockSpec((None, 2 * LANES, 2 * ns), lambda q, c: (q, 0, 0)),
                  pl.BlockSpec((None, 2 * ns, 2 * LANES), lambda q, c: (q, 0, 0)),
                  slot_vec(ns), slot_vec(ns), slot_vec(LANES)],
        out_specs=[pl.BlockSpec((b, tc, 2 * LANES), lambda q, c: (0, c, q)),
                   slot_vec(ns), slot_vec(ns)],
        out_shape=(jax.ShapeDtypeStruct((b, t, nb * LANES), BF16),
                   jax.ShapeDtypeStruct((nq, slots, ns), F32),
                   jax.ShapeDtypeStruct((nq, slots, ns), F32)),
        scratch_shapes=[pltpu.VMEM((2, tc * slots, LANES), F32),
                        pltpu.VMEM((tc * slots, 2 * ns), F32),
                        pltpu.VMEM((tc * slots, LANES), F32),
                        pltpu.VMEM((slots, 2 * ns), F32)],
        compiler_params=_params("parallel", "arbitrary"),
        name="s5_sequence",
    )(p_main, p_main, bw2, cw2, a_slot_re, a_slot_im, d_slot)

    def state(x):
        x = x.reshape(nq, b, 2, ns)
        return jnp.transpose(x, (1, 2, 0, 3)).reshape(b, nb * ns)

    return y, state(xre), state(xim)


def _s5_step_kernel(u_ref, bw_ref, cw_ref, abr_ref, abi_ref, d_ref, x0r_ref, x0i_ref,
                    y_ref, x1r_ref, x1i_ref):
    n_lt = STATES_PER_BLOCK // LANES
    u = u_ref[...]
    bu = _dot(u, bw_ref[...])
    tiles = []
    for lt in range(n_lt):
        nat = slice(lt * LANES, (lt + 1) * LANES)
        ar, ai = abr_ref[:, nat], abi_ref[:, nat]
        x0r, x0i = x0r_ref[:, nat], x0i_ref[:, nat]
        xr = ar * x0r - ai * x0i + bu[:, 2 * lt * LANES:(2 * lt + 1) * LANES]
        xi = ar * x0i + ai * x0r + bu[:, (2 * lt + 1) * LANES:(2 * lt + 2) * LANES]
        x1r_ref[:, nat] = xr
        x1i_ref[:, nat] = xi
        tiles += [xr, xi]
    y_ref[...] = _dot(jnp.concatenate(tiles, axis=1), cw_ref[...]) + d_ref[...] * u


def _s5_step(p_rows, bw2, cw2, a_blk_re, a_blk_im, d_skip, x0_re, x0_im):
    rows = x0_re.shape[0]
    nq = bw2.shape[0]
    nb = 2 * nq
    ns = STATES_PER_BLOCK
    vec = lambda: pl.BlockSpec((None, None, 1, ns), lambda j: (j % nq, j // nq, 0, 0))
    st = lambda: pl.BlockSpec((rows, ns), lambda j: (0, j))
    return pl.pallas_call(
        _s5_step_kernel,
        grid=(nb,),
        in_specs=[pl.BlockSpec((rows, LANES), lambda j: (0, j)),
                  pl.BlockSpec((None, LANES, 2 * ns), lambda j: (j % nq, j // nq, 0)),
                  pl.BlockSpec((None, 2 * ns, LANES), lambda j: (j % nq, 0, j // nq)),
                  vec(), vec(),
                  pl.BlockSpec((1, LANES), lambda j: (0, j)),
                  st(), st()],
        out_specs=[pl.BlockSpec((rows, LANES), lambda j: (0, j)), st(), st()],
        out_shape=(jax.ShapeDtypeStruct((rows, nb * LANES), F32),
                   jax.ShapeDtypeStruct((rows, nb * ns), F32),
                   jax.ShapeDtypeStruct((rows, nb * ns), F32)),
        compiler_params=_params("parallel"),
        name="s5_step",
    )(p_rows, bw2, cw2, a_blk_re, a_blk_im, d_skip, x0_re, x0_im)


def _softplus(x):
    return jnp.maximum(x, 0.0) + jnp.log(1.0 + jnp.exp(-jnp.abs(x)))


def _rwkv_token_terms(r, k, lo, w0, w2p, a0, a2p, k_a):
    w = -_softplus(-(w0 + _dot(jnp.tanh(lo), w2p))) - 0.5
    logd = -jnp.exp(w)
    a = _sigmoid(a0 + _dot(lo, a2p))
    k2 = k * (1.0 + (a - 1.0) * k_a)
    return logd, a, k2


def _rwkv_chunk_kernel(r_ref, k_ref, v_ref, lo_ref, z_ref,
                       mur_ref, muk_ref, muv_ref, mulo_ref,
                       w0_ref, w2p_ref, a0_ref, a2p_ref, kk_ref, ka_ref, rk_ref, gw_ref, gb_ref,
                       o_ref, hs_ref,
                       h_scr, pr_scr, pk_scr, pv_scr, plo_scr, *, n_heads, n_seqs):
    L = RW_CHUNK
    W = 2 * RW_HEAD
    n_pairs = n_heads // 2
    c = pl.program_id(1)

    @pl.when(c == 0)
    def _():
        h_scr[...] = jnp.zeros_like(h_scr)
        pr_scr[...] = jnp.zeros_like(pr_scr)
        pk_scr[...] = jnp.zeros_like(pk_scr)
        pv_scr[...] = jnp.zeros_like(pv_scr)
        plo_scr[...] = jnp.zeros_like(plo_scr)

    row1 = lax.broadcasted_iota(jnp.int32, (L, 1), 0)

    def token_shift(cur, prev_scr, mu_ref):
        prev = jnp.where(row1 == 0, prev_scr[...], pltpu.roll(cur, 1, 0))
        prev_scr[...] = cur[L - 1:L, :]
        return cur + (prev - cur) * mu_ref[...]

    tri = (lax.broadcasted_iota(jnp.int32, (L, L), 0)
           >= lax.broadcasted_iota(jnp.int32, (L, L), 1)).astype(BF16)
    first_head = lax.broadcasted_iota(jnp.int32, (L, W), 1) < RW_HEAD

    def head_sum(x):
        s0 = jnp.sum(jnp.where(first_head, x, 0.0), axis=-1, keepdims=True)
        s1 = jnp.sum(jnp.where(first_head, 0.0, x), axis=-1, keepdims=True)
        return jnp.where(first_head, s0, s1)

    def stack_heads(x):
        return jnp.concatenate([jnp.where(first_head, x, 0.0), jnp.where(first_head, 0.0, x)], axis=0)

    ri = lax.broadcasted_iota(jnp.int32, (2 * L, 2 * L), 0)
    ci = lax.broadcasted_iota(jnp.int32, (2 * L, 2 * L), 1)
    t_row, t_col = ri & (L - 1), ci & (L - 1)
    same_head = (ri // L) == (ci // L)
    strict = same_head & (t_row > t_col)
    incl = same_head & (t_row >= t_col)
    eye = (ri == ci).astype(F32)
    blk_masks = []
    s = INV_BASE
    while s <= L:
        blk_masks.append((ri // s) == (ci // s))
        s *= 2

    pairs = range(n_pairs)
    loc = [slice(p * W, (p + 1) * W) for p in pairs]

    def seq_phases(si):
        r = token_shift(r_ref[si], pr_scr.at[si], mur_ref)
        k = token_shift(k_ref[si], pk_scr.at[si], muk_ref)
        v = token_shift(v_ref[si], pv_scr.at[si], muv_ref)
        lo = token_shift(lo_ref[si], plo_scr.at[si], mulo_ref)
        logd, a, k2 = _rwkv_token_terms(r, k, lo, w0_ref[...], w2p_ref[...], a0_ref[...], a2p_ref[...],
                                        ka_ref[...])
        kk = k * kk_ref[...]
        hi = logd.astype(BF16)
        rem = logd - hi.astype(F32)
        mid = rem.astype(BF16)
        low = (rem - mid.astype(F32)).astype(BF16)
        lp = (jnp.dot(tri, hi, preferred_element_type=F32)
              + jnp.dot(tri, mid, preferred_element_type=F32)
              + jnp.dot(tri, low, preferred_element_type=F32))
        p_inc = jnp.exp(lp)
        p_exc = jnp.exp(lp - logd)
        p_inv = jnp.exp(-lp)
        yield

        lhs, nm, mm, ab, vs, kb, p_end = [], [], [], [], [], [], []
        for sl in loc:
            kk_p = kk[:, sl]
            kkn = kk_p * lax.rsqrt(jnp.maximum(head_sum(kk_p * kk_p), 1e-24))
            pinc, pinv = p_inc[:, sl], p_inv[:, sl]
            kd = k2[:, sl] * pinv
            bd = kkn * a[:, sl] * pinv
            pe = pinc[L - 1:L, :]
            lhs_p = jnp.concatenate([stack_heads(kkn * p_exc[:, sl]), stack_heads(r[:, sl] * pinc)],
                                    axis=0).astype(BF16)
            rhs_p = jnp.concatenate([bd, bd, kd, kd], axis=0).astype(BF16)
            amat = lax.dot_general(lhs_p, rhs_p, _NT, preferred_element_type=F32)
            lhs.append(lhs_p)
            nm.append(jnp.where(strict, amat[:2 * L, :2 * L], 0.0))
            mm.append(jnp.where(strict, amat[:2 * L, 2 * L:], 0.0).astype(BF16))
            ab.append(jnp.concatenate([jnp.where(incl, amat[2 * L:, 2 * L:], 0.0),
                                       -jnp.where(incl, amat[2 * L:, :2 * L], 0.0)], axis=1).astype(BF16))
            vs.append(stack_heads(v[:, sl]).astype(BF16))
            kb.append(jnp.concatenate([stack_heads(kd * pe), stack_heads(bd * pe)], axis=0).astype(BF16))
            p_end.append(pe)
            yield

        d = [jnp.where(blk_masks[0], n_p, 0.0).astype(BF16) for n_p in nm]
        x = [eye - d_p.astype(F32) for d_p in d]
        pw = [jnp.dot(d_p, d_p, preferred_element_type=F32) for d_p in d]
        yield
        s = 2
        while s < INV_BASE:
            x = [x_p + _dot(x_p, pw_p) for x_p, pw_p in zip(x, pw)]
            s *= 2
            if s < INV_BASE:
                pw = [_dot(pw_p, pw_p) for pw_p in pw]
            yield
        for lvl in range(1, len(blk_masks)):
            off = blk_masks[lvl] & ~blk_masks[lvl - 1]
            xc = [_dot(x_p, jnp.where(off, n_p, 0.0)) for x_p, n_p in zip(x, nm)]
            yield
            x = [x_p - _dot(xc_p, x_p) for x_p, xc_p in zip(x, xc)]
            yield

        hs = [h_scr[si, p] for p in pairs]
        lh = [jnp.dot(lhs_p, hs_p.astype(BF16), preferred_element_type=F32) for lhs_p, hs_p in zip(lhs, hs)]
        mv = [jnp.dot(mm_p, vs_p, preferred_element_type=F32) for mm_p, vs_p in zip(mm, vs)]
        yield
        u = [_dot(x_p, lh_p[:2 * L] + mv_p).astype(BF16) for x_p, lh_p, mv_p in zip(x, lh, mv)]
        yield
        o_st = [lh_p[2 * L:] + jnp.dot(ab_p, jnp.concatenate([vs_p, u_p], axis=0), preferred_element_type=F32)
                for lh_p, ab_p, vs_p, u_p in zip(lh, ab, vs, u)]
        for p in pairs:
            p_end_col = jnp.sum(eye * p_end[p], axis=1, keepdims=True)
            h_scr[si, p] = p_end_col * hs[p] + lax.dot_general(
                kb[p], jnp.concatenate([vs[p], -u[p]], axis=0), _TN, preferred_element_type=F32)
        yield

        for p, sl in enumerate(loc):
            o = o_st[p][:L] + o_st[p][L:]
            mu = head_sum(o) * (1.0 / RW_HEAD)
            var = head_sum((o - mu) ** 2) * (1.0 / RW_HEAD)
            o = (o - mu) * lax.rsqrt(var + GN_EPS) * gw_ref[:, sl] + gb_ref[:, sl]
            o = o + head_sum(r[:, sl] * k2[:, sl] * rk_ref[:, sl]) * v[:, sl]
            z = z_ref[si, :, LANES + p * W:LANES + (p + 1) * W]
            o_ref[si, :, sl] = (o * _silu(z)).astype(o_ref.dtype)
        yield

    gens = [seq_phases(si) for si in range(n_seqs)]
    started = 0
    live = []
    step = 0
    while started < n_seqs or live:
        if started < n_seqs and step % RW_SEQ_LAG == 0:
            live.append(gens[started])
            started += 1
        for g in list(live):
            if next(g, "done") == "done":
                live.remove(g)
        step += 1

    @pl.when(c == pl.num_programs(1) - 1)
    def _():
        for si in range(n_seqs):
            for p in pairs:
                st = h_scr[si, p].T
                hs_ref[si, 2 * p] = st[:RW_HEAD, :RW_HEAD]
                hs_ref[si, 2 * p + 1] = st[RW_HEAD:, RW_HEAD:]


def _rwkv_sequence(p_main, p_lz, col, mu, w0, w2p, a0, a2p, k_k, k_a, r_k, gn_w, gn_b):
    b, t, _ = p_main.shape
    wlz = p_lz.shape[-1]
    d = w0.shape[-1]
    n_heads = d // RW_HEAD
    n_pairs, pw = n_heads // 2, 2 * RW_HEAD
    L = RW_CHUNK
    ns = RW_SEQS if b % RW_SEQS == 0 else 1
    blk = lambda cb: pl.BlockSpec((ns, L, d), lambda i, j, cb=cb: (i, j, cb))
    vec = lambda n: pl.BlockSpec((1, n), lambda i, j: (0, 0))
    full = lambda shp: pl.BlockSpec(shp, lambda i, j: (0,) * len(shp))
    mu_r, mu_k, mu_v, mu_lo = mu
    o, hs = pl.pallas_call(
        functools.partial(_rwkv_chunk_kernel, n_heads=n_heads, n_seqs=ns),
        grid=(b // ns, t // L),
        in_specs=[blk(col["r"]), blk(col["k"]), blk(col["v"]),
                  pl.BlockSpec((ns, L, LANES), lambda i, j: (i, j, 0)),
                  pl.BlockSpec((ns, L, wlz), lambda i, j: (i, j, 0)),
                  vec(d), vec(d), vec(d), vec(LANES),
                  vec(d), full((LANES, d)), vec(d), full((LANES, d)),
                  vec(d), vec(d), vec(d), vec(d), vec(d)],
        out_specs=[pl.BlockSpec((ns, L, d), lambda i, j: (i, j, 0)),
                   pl.BlockSpec((ns, n_heads, RW_HEAD, RW_HEAD), lambda i, j: (i, 0, 0, 0))],
        out_shape=(jax.ShapeDtypeStruct((b, t, d), BF16),
                   jax.ShapeDtypeStruct((b, n_heads, RW_HEAD, RW_HEAD), F32)),
        scratch_shapes=[pltpu.VMEM((ns, n_pairs, pw, pw), F32),
                        pltpu.VMEM((ns, 1, d), F32), pltpu.VMEM((ns, 1, d), F32), pltpu.VMEM((ns, 1, d), F32),
                        pltpu.VMEM((ns, 1, LANES), F32)],
        compiler_params=_params("parallel", "arbitrary"),
        name="rwkv_sequence",
    )(p_main, p_main, p_main, p_lz, p_lz,
      mu_r, mu_k, mu_v, mu_lo, w0, w2p, a0, a2p, k_k, k_a, r_k, gn_w, gn_b)
    return o, hs


def _rwkv_step_prep_kernel(cr_ref, ck_ref, cv_ref, clo_ref, cz_ref, pr_ref, pk_ref, pv_ref, plo_ref,
                           mur_ref, muk_ref, muv_ref, mulo_ref,
                           w0_ref, w2p_ref, a0_ref, a2p_ref, kk_ref, ka_ref,
                           r_o, k2_o, v_o, kk_o, a_o, d_o, z_o):
    def lerp(c_ref, p_ref, mu_ref):
        cur = c_ref[...]
        return cur + (p_ref[...] - cur) * mu_ref[...]

    r = lerp(cr_ref, pr_ref, mur_ref)
    k = lerp(ck_ref, pk_ref, muk_ref)
    v = lerp(cv_ref, pv_ref, muv_ref)
    lo = lerp(clo_ref, plo_ref, mulo_ref)
    logd, a, k2 = _rwkv_token_terms(r, k, lo, w0_ref[...], w2p_ref[...], a0_ref[...], a2p_ref[...],
                                    ka_ref[...])
    r_o[...] = r.T
    k2_o[...] = k2.T
    v_o[...] = v.T
    kk_o[...] = (k * kk_ref[...]).T
    a_o[...] = a.T
    d_o[...] = jnp.exp(logd).T
    z_o[...] = cz_ref[:, LANES:].T


def _rwkv_step_kernel(s_ref, r_ref, k2_ref, v_ref, kk_ref, a_ref, d_ref, z_ref,
                      rk_ref, gw_ref, gb_ref, o_ref, s1_ref, o_scr):
    for hh in range(RW_STEP_HEADS):
        hs = slice(hh * RW_HEAD, (hh + 1) * RW_HEAD)
        r, k2, v, dec = r_ref[hs, :], k2_ref[hs, :], v_ref[hs, :], d_ref[hs, :]
        kk = kk_ref[hs, :]
        kkn = kk / jnp.maximum(jnp.sqrt(jnp.sum(kk * kk, axis=0, keepdims=True)), 1e-12)
        bvec = kkn * a_ref[hs, :]
        for i in range(RW_HEAD):
            s = s_ref[hh, i]
            sa = jnp.sum(s * kkn, axis=0, keepdims=True)
            s1 = s * dec - sa * bvec + v[i:i + 1, :] * k2
            s1_ref[hh, i] = s1
            o_scr[pl.ds(hh * RW_HEAD + i, 1), :] = jnp.sum(s1 * r, axis=0, keepdims=True)
        o = o_scr[hs, :]
        mu = jnp.mean(o, axis=0, keepdims=True)
        var = jnp.mean((o - mu) ** 2, axis=0, keepdims=True)
        o = (o - mu) * lax.rsqrt(var + GN_EPS) * gw_ref[hs, :] + gb_ref[hs, :]
        o = o + jnp.sum(r * k2 * rk_ref[hs, :], axis=0, keepdims=True) * v
        o_ref[hs, :] = o * _silu(z_ref[hs, :])


def _rwkv_step(p_rows, p_lz_rows, col, mu, w0, w2p, a0, a2p, k_k, k_a, r_k, gn_w, gn_b, s0):
    rows = s0.shape[0]
    wlz = p_lz_rows.shape[-1]
    d = w0.shape[-1]
    n_heads = d // RW_HEAD
    cur = lambda cb: pl.BlockSpec((rows, d), lambda i, cb=cb: (0, cb))
    prv = lambda cb: pl.BlockSpec((rows, d), lambda i, cb=cb: (1, cb))
    vec = lambda n: pl.BlockSpec((1, n), lambda i: (0, 0))
    mu_r, mu_k, mu_v, mu_lo = mu
    out = jax.ShapeDtypeStruct((d, rows), F32)
    terms = pl.pallas_call(
        _rwkv_step_prep_kernel,
        grid=(1,),
        in_specs=[cur(col["r"]), cur(col["k"]), cur(col["v"]),
                  pl.BlockSpec((rows, LANES), lambda i: (0, 0)),
                  pl.BlockSpec((rows, wlz), lambda i: (0, 0)),
                  prv(col["r"]), prv(col["k"]), prv(col["v"]),
                  pl.BlockSpec((rows, LANES), lambda i: (1, 0)),
                  vec(d), vec(d), vec(d), vec(LANES),
                  vec(d), pl.BlockSpec((LANES, d), lambda i: (0, 0)),
                  vec(d), pl.BlockSpec((LANES, d), lambda i: (0, 0)),
                  vec(d), vec(d)],
        out_specs=[pl.BlockSpec((d, rows), lambda i: (0, 0))] * 7,
        out_shape=(out,) * 7,
        compiler_params=_params("arbitrary"),
        name="rwkv_step_prep",
    )(p_rows, p_rows, p_rows, p_lz_rows, p_lz_rows, p_rows, p_rows, p_rows, p_lz_rows,
      mu_r, mu_k, mu_v, mu_lo, w0, w2p, a0, a2p, k_k, k_a)
    hb = RW_STEP_HEADS
    per_h = lambda: pl.BlockSpec((hb * RW_HEAD, rows), lambda h: (h, 0))
    par = lambda: pl.BlockSpec((hb * RW_HEAD, 1), lambda h: (h, 0))
    st = lambda: pl.BlockSpec((hb, RW_HEAD, RW_HEAD, rows), lambda h: (h, 0, 0, 0))
    o_t, s1_t = pl.pallas_call(
        _rwkv_step_kernel,
        grid=(n_heads // hb,),
        in_specs=[st()] + [per_h()] * 7 + [par()] * 3,
        out_specs=[per_h(), st()],
        out_shape=(jax.ShapeDtypeStruct((d, rows), F32),
                   jax.ShapeDtypeStruct((n_heads, RW_HEAD, RW_HEAD, rows), F32)),
        scratch_shapes=[pltpu.VMEM((hb * RW_HEAD, rows), F32)],
        compiler_params=_params("parallel"),
        name="rwkv_step",
    )(jnp.transpose(s0, (1, 2, 3, 0)), *terms,
      r_k.reshape(d, 1), gn_w.reshape(d, 1), gn_b.reshape(d, 1))
    return o_t.T.astype(BF16), jnp.transpose(s1_t, (3, 0, 1, 2))


def _out_kernel(ys_ref, z_ref, wg_ref, bg_ref, or_ref, gs_ref, gr_ref, x_ref, gt_ref, w1_ref, w2_ref,
                fg_ref, y_ref, *, paired_blocks):
    ys = ys_ref[...].astype(F32)
    if paired_blocks:
        nb = ys.shape[1] // LANES
        pos = [2 * j if j < nb // 2 else 2 * (j - nb // 2) + 1 for j in range(nb)]
        ys = jnp.concatenate([ys[:, p * LANES:(p + 1) * LANES] for p in pos], axis=1)
    ys = jax.nn.gelu(ys, approximate=True)
    o_s = (ys * _sigmoid(_dot(ys, wg_ref[...]) + bg_ref[...]) * _silu(z_ref[...])).astype(BF16)
    mixed = (gs_ref[...].astype(F32) * jnp.dot(o_s, w1_ref[...], preferred_element_type=F32)
             + gr_ref[...].astype(F32) * jnp.dot(or_ref[...], w2_ref[...], preferred_element_type=F32))
    x = x_ref[...] + gt_ref[...] * mixed
    y_ref[...] = x * lax.rsqrt(jnp.mean(x * x, axis=-1, keepdims=True) + RMS_EPS) * fg_ref[...]


def _out_proj(y_s5, p_main, o_r, p_gate, col, x, mod, w_glu, b_glu, w_out, final_g, tt, paired_blocks):
    b, t, d = x.shape
    dh = o_r.shape[-1]
    tt = min(tt, t)
    tm = 1 if mod.shape[1] == 1 else tt
    gt_map = (lambda i, j: (i, 0, 2)) if tm == 1 else (lambda i, j: (i, j, 2))
    const = lambda shp, r=0: pl.BlockSpec(shp, lambda i, j: (r, 0), pipeline_mode=pl.Buffered(1))
    return pl.pallas_call(
        functools.partial(_out_kernel, paired_blocks=paired_blocks),
        grid=(b, t // tt),
        in_specs=[pl.BlockSpec((None, tt, dh), lambda i, j: (i, j, 0)),
                  pl.BlockSpec((None, tt, dh), lambda i, j: (i, j, col["z_s5"])),
                  const((dh, dh)), const((1, dh)),
                  pl.BlockSpec((None, tt, dh), lambda i, j: (i, j, 0)),
                  pl.BlockSpec((None, tt, d), lambda i, j: (i, j, col["g_s5"])),
                  pl.BlockSpec((None, tt, d), lambda i, j: (i, j, col["g_rw"])),
                  pl.BlockSpec((None, tt, d), lambda i, j: (i, j, 0)),
                  pl.BlockSpec((None, tm, d), gt_map),
                  const((dh, d), 0), const((dh, d), 1), const((1, d))],
        out_specs=pl.BlockSpec((None, tt, d), lambda i, j: (i, j, 0)),
        out_shape=jax.ShapeDtypeStruct((b, t, d), F32),
        compiler_params=_params("parallel", "parallel"),
        name="out_proj",
    )(y_s5, p_main, w_glu, b_glu.reshape(1, dh), o_r, p_gate, p_gate, x, mod, w_out, w_out,
      final_g.reshape(1, d))


def kernel(x_prompt, x_sample, c_prompt, c_sample, state_s5_re, state_s5_im, state_wkv, state_shift, norm_g, w_ada, b_ada, w_in, mu_rw, A_re, A_im, log_step, B_re, B_im, C_re, C_im, D_skip, w_glu, b_glu, w0, w2, a0, a2, k_k, k_a, r_k, gn_w, gn_b, w_out, final_g):
    depth = norm_g.shape[0]
    assert depth == 1
    bp, tp, d = x_prompt.shape
    bs = x_sample.shape[0]
    assert x_sample.shape[1] == 1
    dh = d // 2
    l = 0

    w_main, w_lz, w_gate = 5 * dh, 2 * LORA + dh, 2 * d
    col = {"u": 0, "z_s5": 1, "r": 2, "k": 3, "v": 4, "g_s5": 0, "g_rw": 1}
    w = w_in[l]

    def project(rows, rows2):
        return (_in_proj(rows, rows2, w, 0, w_main, w_main // IN_PROJ_COL_TILES, IN_PROJ_ROWS),
                _in_proj(rows, rows2, w, w_main, w_lz, w_lz, IN_PROJ_ROWS),
                _in_proj(rows, rows2, w, w_main + w_lz, w_gate, w_gate // IN_PROJ_COL_TILES, IN_PROJ_ROWS,
                         gate=True))

    mu = mu_rw[l]
    mu_parts = (mu[None, :dh], mu[None, dh:2 * dh], mu[None, 2 * dh:3 * dh], mu[None, 3 * dh:])
    zpad = jnp.zeros((LORA, dh), F32)
    w2p = jnp.concatenate([w2[l], zpad], axis=0).astype(BF16)
    a2p = jnp.concatenate([zpad, a2[l]], axis=0).astype(BF16)
    row = lambda x: x.reshape(1, -1)
    rw_params = (row(w0[l]), w2p, row(a0[l]), a2p, row(k_k[l]), row(k_a[l]), row(r_k[l]),
                 row(gn_w[l]), row(gn_b[l]))
    w_out_bf = w_out[l].astype(BF16)
    w_glu_bf = w_glu[l].astype(BF16)

    bw2, cw2, a_blk_re, a_blk_im, a_slot_re, a_slot_im = _s5_weights(
        A_re[l], A_im[l], log_step[l], B_re[l], B_im[l], C_re[l], C_im[l], bp)
    d_skip = D_skip[l].reshape(1, -1)

    mod_p, mod_s = _mod(c_prompt, c_sample, w_ada[l], b_ada[l])
    mod_p = mod_p.reshape(bp, 1, 3 * d)
    mod_s = mod_s.reshape(1, bs, 3 * d)

    h_p = _modulated_norm(x_prompt, norm_g[l], mod_p, BF16, NORM_ROWS)
    shift_p = _modulated_norm(x_prompt[:, tp - 1:, :], norm_g[l], mod_p, F32, 1)[:, 0]
    xs = x_sample.reshape(1, bs, d)
    h_s = _modulated_norm(xs, norm_g[l], mod_s, F32, bs)[0]
    a_s = jnp.concatenate([h_s, state_shift[l]], axis=0).astype(BF16)
    (pm, ps), (plz, pslz), (pg, psg) = project(h_p.reshape(bp * tp, d), a_s)

    pm3 = pm.reshape(bp, tp, -1)
    y_s5, xre_p, xim_p = _s5_sequence(pm3, bw2, cw2, a_slot_re, a_slot_im, d_skip)
    o_r, hs_p = _rwkv_sequence(pm3, plz.reshape(bp, tp, -1), col, mu_parts, *rw_params)
    y_prompt = _out_proj(y_s5, pm3, o_r, pg.reshape(bp, tp, -1), col, x_prompt, mod_p,
                         w_glu_bf, b_glu[l], w_out_bf, final_g, OUT_ROWS, True)
    g_s5 = A_re.shape[1]
    s5_shape = (1, bp, g_s5, P_S5)
    wkv_p = hs_p[None]

    y_s5s, xre_s, xim_s = _s5_step(ps, bw2, cw2, a_blk_re, a_blk_im, d_skip,
                                   state_s5_re[l].reshape(bs, -1), state_s5_im[l].reshape(bs, -1))
    o_rs, wkv_s = _rwkv_step(ps, pslz, col, mu_parts, *rw_params, state_wkv[l])
    y_sample = _out_proj(y_s5s[None], ps.reshape(2, bs, -1), o_rs[None], psg.reshape(2, bs, -1), col, xs, mod_s,
                         w_glu_bf, b_glu[l], w_out_bf, final_g, bs, False)
    y_sample = y_sample.reshape(bs, 1, d)

    return (y_prompt, y_sample,
            xre_p.reshape(s5_shape), xim_p.reshape(s5_shape), wkv_p, shift_p[None],
            xre_s.reshape(1, bs, g_s5, P_S5), xim_s.reshape(1, bs, g_s5, P_S5), wkv_s[None], h_s[None])
```
